```python
import math
import jax
import jax.numpy as jnp
from jax import lax
import numpy as np

D_MODEL = 2048
BATCH = 4
SEQ = 4096
DEPTH = 4

GRID_W = 64
CTX_LEN = 256
N_EVEN = (DEPTH + 1) // 2
N_ODD = DEPTH // 2
N_MOD = 6
NORM_EPS = 1e-6
NEG_INF = -1e30

DIFF_HEADS = 8
DIFF_QK_DIM = 64
DIFF_V_DIM = 128
DIFF_QK_W = 2 * DIFF_HEADS * DIFF_QK_DIM
DIFF_V_W = DIFF_HEADS * DIFF_V_DIM
NA_HEADS = 8
NA_DIM = 128
NA_W = NA_HEADS * NA_DIM
NA_KH = 8
NA_KW = 16
IN_SPLITS = (DIFF_QK_W, 2 * DIFF_QK_W, 2 * DIFF_QK_W + DIFF_V_W,
             2 * DIFF_QK_W + DIFF_V_W + NA_W, 2 * DIFF_QK_W + DIFF_V_W + 2 * NA_W)
IN_COLS = 2 * DIFF_QK_W + DIFF_V_W + 3 * NA_W
MIX_W = DIFF_V_W + NA_W
Q_BLOCK = 128
ROPE_THETA = 10000.0
SUBLN_EPS = 1e-5

RWKV_HEAD = 64
RWKV_HEADS = D_MODEL // RWKV_HEAD
DECAY_LORA = 96
ICLR_LORA = 96
GATE_LORA = 256
N_DIRS = 2
GN_EPS = 64e-5

N_GROUPS = 4
EXPERTS_PER_GROUP = 8
TOP_K = 2
EXPERT_FF = 512

kernel_name = 'hybrid_diffna_rwkv7_hmoe_dit'


def rmsnorm(x, g, eps=NORM_EPS):
    xf = x.astype(jnp.float32)
    y = xf * lax.rsqrt(jnp.mean(xf * xf, axis=-1, keepdims=True) + eps)
    return (y * g.astype(jnp.float32)).astype(x.dtype)


def modulate(h, shift, scale):
    return h * (1 + scale) + shift


def lambda_init(layer_idx):
    return 0.8 - 0.6 * math.exp(-0.3 * layer_idx)


def split_heads(t, n_heads, dim):
    b, s, _ = t.shape
    return t.reshape(b, s, n_heads, dim).transpose(0, 2, 1, 3)


def merge_heads(t):
    b, h, s, d = t.shape
    return t.transpose(0, 2, 1, 3).reshape(b, s, h * d)


def axial_rope_tables(n_tokens, dim):
    n_freq = dim // 4
    inv_freq = ROPE_THETA ** (-jnp.arange(n_freq, dtype=jnp.float32) / n_freq)
    t = jnp.arange(n_tokens, dtype=jnp.int32)
    row = (t // GRID_W).astype(jnp.float32)
    col = (t % GRID_W).astype(jnp.float32)
    ang = jnp.concatenate([row[:, None] * inv_freq, col[:, None] * inv_freq], axis=-1)
    return jnp.cos(ang), jnp.sin(ang)


def apply_rope(x, cos, sin):
    x1, x2 = jnp.split(x, 2, axis=-1)
    cos = cos.astype(x.dtype)
    sin = sin.astype(x.dtype)
    return jnp.concatenate([x1 * cos - x2 * sin, x1 * sin + x2 * cos], axis=-1)


def softmax_attention(q, k, v):
    s = jnp.einsum('bhqd,bhkd->bhqk', q, k).astype(jnp.float32) * (q.shape[-1] ** -0.5)
    return jnp.einsum('bhqk,bhkd->bhqd', jax.nn.softmax(s, axis=-1).astype(v.dtype), v)


def diff_maps(q1, q2, k1, k2, v, lam):
    scale = q1.shape[-1] ** -0.5
    s1 = jnp.einsum('bhqd,bhkd->bhqk', q1, k1).astype(jnp.float32) * scale
    s2 = jnp.einsum('bhqd,bhkd->bhqk', q2, k2).astype(jnp.float32) * scale
    p = jax.nn.softmax(s1, axis=-1) - lam * jax.nn.softmax(s2, axis=-1)
    return jnp.einsum('bhqk,bhkd->bhqd', p.astype(v.dtype), v)


def diff_attention_latent(q1, q2, k1, k2, v, lam):
    b, h, n, dq = q1.shape
    nblk = n // Q_BLOCK

    def to_blocks(t):
        return t.reshape(b, h, nblk, Q_BLOCK, dq).transpose(2, 0, 1, 3, 4)

    out = lax.map(lambda qs: diff_maps(qs[0], qs[1], k1, k2, v, lam), (to_blocks(q1), to_blocks(q2)))
    return out.transpose(1, 2, 0, 3, 4).reshape(b, h, n, v.shape[-1])


def neighbourhood_attention_latent(q, k, v, k_ctx, v_ctx, rpb):
    b, h, n, dh = q.shape
    rows = n // GRID_W
    kh = min(NA_KH, rows)
    kw = NA_KW
    scale = dh ** -0.5
    n_ctx = k_ctx.shape[2]
    qg = q.reshape(b, h, rows, GRID_W, dh).transpose(2, 0, 1, 3, 4)
    kg = k.reshape(b, h, rows, GRID_W, dh)
    vg = v.reshape(b, h, rows, GRID_W, dh)
    cols = jnp.arange(GRID_W, dtype=jnp.int32)
    col_start = jnp.clip(cols - kw // 2, 0, GRID_W - kw)
    col_mask = (cols[None, :] >= col_start[:, None]) & (cols[None, :] < col_start[:, None] + kw)
    c_idx = jnp.clip(cols[None, :] - cols[:, None] + NA_KW - 1, 0, 2 * NA_KW - 2)
    row_ids = jnp.arange(rows, dtype=jnp.int32)
    row_start = jnp.clip(row_ids - kh // 2, 0, rows - kh)

    def one_row(args):
        q_r, r, r0 = args
        k_r = lax.dynamic_slice_in_dim(kg, r0, kh, axis=2).reshape(b, h, kh * GRID_W, dh)
        v_r = lax.dynamic_slice_in_dim(vg, r0, kh, axis=2).reshape(b, h, kh * GRID_W, dh)
        r_idx = r0 + jnp.arange(kh, dtype=jnp.int32) - r + NA_KH - 1
        bias = rpb[:, r_idx[None, :, None], c_idx[:, None, :]].astype(jnp.float32)
        bias = jnp.where(col_mask[:, None, :], bias, NEG_INF).reshape(h, GRID_W, kh * GRID_W)
        s_win = jnp.einsum('bhqd,bhkd->bhqk', q_r, k_r).astype(jnp.float32) * scale + bias[None]
        s_ctx = jnp.einsum('bhqd,bhkd->bhqk', q_r, k_ctx).astype(jnp.float32) * scale
        p = jax.nn.softmax(jnp.concatenate([s_ctx, s_win], axis=-1), axis=-1).astype(v.dtype)
        return (jnp.einsum('bhqk,bhkd->bhqd', p[..., :n_ctx], v_ctx)
                + jnp.einsum('bhqk,bhkd->bhqd', p[..., n_ctx:], v_r))

    out = lax.map(one_row, (qg, row_ids, row_start))
    return out.transpose(1, 2, 0, 3, 4).reshape(b, h, n, dh)


def even_project(h, w_in):
    aq, ak, av, bq, bk, bv = jnp.split(h @ w_in, IN_SPLITS, axis=-1)
    aq = split_heads(aq, 2 * DIFF_HEADS, DIFF_QK_DIM)
    ak = split_heads(ak, 2 * DIFF_HEADS, DIFF_QK_DIM)
    return (aq[:, 0::2], aq[:, 1::2], ak[:, 0::2], ak[:, 1::2],
            split_heads(av, DIFF_HEADS, DIFF_V_DIM),
            split_heads(bq, NA_HEADS, NA_DIM), split_heads(bk, NA_HEADS, NA_DIM),
            split_heads(bv, NA_HEADS, NA_DIM))


def even_mixer(hc, hl, w_in, w_out, diff_lambda, subln_g, rpb, lam_init, cos, sin, need_ctx_out):
    q1c, q2c, k1c, k2c, vac, qbc, kbc, vbc = even_project(hc, w_in)
    q1l, q2l, k1l, k2l, val, qbl, kbl, vbl = even_project(hl, w_in)
    q1l, q2l, k1l, k2l = (apply_rope(t, cos, sin) for t in (q1l, q2l, k1l, k2l))
    lf = diff_lambda.astype(jnp.float32)
    lam = jnp.exp(jnp.sum(lf[0] * lf[1])) - jnp.exp(jnp.sum(lf[2] * lf[3])) + lam_init

    def combine(a_heads, b_heads):
        a_heads = rmsnorm(a_heads, subln_g, SUBLN_EPS) * (1.0 - lam_init)
        return jnp.concatenate([merge_heads(a_heads), merge_heads(b_heads)], axis=-1) @ w_out

    k1_all = jnp.concatenate([k1c, k1l], axis=2)
    k2_all = jnp.concatenate([k2c, k2l], axis=2)
    va_all = jnp.concatenate([vac, val], axis=2)
    a_l = diff_attention_latent(q1l, q2l, k1_all, k2_all, va_all, lam)
    b_l = neighbourhood_attention_latent(qbl, kbl, vbl, kbc, vbc, rpb)
    out_l = combine(a_l, b_l)
    out_c = None
    if need_ctx_out:
        out_c = combine(diff_maps(q1c, q2c, k1c, k2c, vac, lam), softmax_attention(qbc, kbc, vbc))
    return out_c, out_l


def token_shift_centred(x):
    prev = jnp.pad(x[:, :-1], ((0, 0), (1, 0), (0, 0)))
    nxt = jnp.pad(x[:, 1:], ((0, 0), (0, 1), (0, 0)))
    return 0.5 * (prev + nxt) - x


def rwkv7_features(h, x_mix, w_rkv, dec_w0, dec_w1, dec_w2, iclr_a0, iclr_a1, iclr_a2,
                   gate_g1, gate_g2, k_k, k_a):
    f32 = jnp.float32
    xx = token_shift_centred(h)

    def mix(j):
        return h + xx * x_mix[j]

    def heads(z):
        return z.astype(f32).reshape(*z.shape[:-1], RWKV_HEADS, RWKV_HEAD)

    r = heads(mix(0) @ w_rkv[0])
    k = heads(mix(1) @ w_rkv[1])
    v = heads(mix(2) @ w_rkv[2])
    xw, xa, xg = mix(3), mix(4), mix(5)
    w_lora = jnp.einsum('nbtr,nrd->nbtd', jnp.tanh(jnp.einsum('btd,ndr->nbtr', xw, dec_w1)), dec_w2)
    log_w = -jax.nn.softplus(-(dec_w0[:, None, None, :] + w_lora).astype(f32)) - 0.5
    decay = heads(jnp.exp(-jnp.exp(log_w)))
    a_lora = jnp.einsum('nbtr,nrd->nbtd', jnp.einsum('btd,ndr->nbtr', xa, iclr_a1), iclr_a2)
    a = heads(jax.nn.sigmoid((iclr_a0[:, None, None, :] + a_lora).astype(f32)))
    g = jax.nn.sigmoid(xg @ gate_g1) @ gate_g2
    kk = k * heads(k_k)
    kk = kk * lax.rsqrt(jnp.maximum(jnp.sum(kk * kk, axis=-1, keepdims=True), 1e-24))
    k_dir = k[None] * (1.0 + (a - 1.0) * heads(k_a))
    return r, k_dir, v, kk, decay, a, g


def rwkv7_scan(s0, r, decay, k, v, kk, a, reverse):
    def step(state, inp):
        r_t, w_t, k_t, v_t, kk_t, a_t = inp
        removed = jnp.einsum('bhvk,bhk->bhv', state, kk_t)
        state = (state * w_t[:, :, None, :]
                 - removed[..., None] * (kk_t * a_t)[:, :, None, :]
                 + v_t[..., None] * k_t[:, :, None, :])
        return state, jnp.einsum('bhvk,bhk->bhv', state, r_t)

    xs = tuple(jnp.moveaxis(t, 1, 0) for t in (r, decay, k, v, kk, a))
    s_final, ys = lax.scan(step, s0, xs, reverse=reverse)
    return s_final, jnp.moveaxis(ys, 0, 1)


def rwkv7_output(y_sum, r, k_dir, v, g, r_k, gn_g, gn_b, w_out, dtype):
    f32 = jnp.float32
    mu = jnp.mean(y_sum, axis=-1, keepdims=True)
    var = jnp.mean(jnp.square(y_sum - mu), axis=-1, keepdims=True)
    b, t, nh, n = y_sum.shape
    yn = ((y_sum - mu) * lax.rsqrt(var + GN_EPS)).reshape(b, t, nh * n)
    yn = yn * gn_g.astype(f32) + gn_b.astype(f32)
    bonus = jnp.sum(r[None] * k_dir * r_k.astype(f32), axis=(0, -1))[..., None] * v
    y = (yn + bonus.reshape(b, t, nh * n)) * g.astype(f32)
    return y.astype(dtype) @ w_out


def rwkv7_mixer(hc, hl, x_mix, w_rkv, w_out, dec_w0, dec_w1, dec_w2, iclr_a0, iclr_a1, iclr_a2,
                gate_g1, gate_g2, k_k, k_a, r_k, gn_g, gn_b, need_ctx_out):
    def feats(h):
        return rwkv7_features(h, x_mix, w_rkv, dec_w0, dec_w1, dec_w2, iclr_a0, iclr_a1, iclr_a2,
                              gate_g1, gate_g2, k_k, k_a)

    r_c, kd_c, v_c, kk_c, w_c, a_c, g_c = feats(hc)
    r_l, kd_l, v_l, kk_l, w_l, a_l, g_l = feats(hl)
    s0 = jnp.zeros((hc.shape[0], RWKV_HEADS, RWKV_HEAD, RWKV_HEAD), jnp.float32)
    y_c_sum = jnp.zeros_like(r_c)
    y_l_sum = jnp.zeros_like(r_l)
    for d, rev in enumerate((False, True)):
        s_ctx, y_c = rwkv7_scan(s0, r_c, w_c[d], kd_c[d], v_c, kk_c, a_c[d], rev)
        _, y_l = rwkv7_scan(s_ctx, r_l, w_l[d], kd_l[d], v_l, kk_l, a_l[d], rev)
        y_c_sum = y_c_sum + y_c
        y_l_sum = y_l_sum + y_l
    out_l = rwkv7_output(y_l_sum, r_l, kd_l, v_l, g_l, r_k, gn_g, gn_b, w_out, hl.dtype)
    out_c = None
    if need_ctx_out:
        out_c = rwkv7_output(y_c_sum, r_c, kd_c, v_c, g_c, r_k, gn_g, gn_b, w_out, hc.dtype)
    return out_c, out_l


def hierarchical_moe(h, router_g, router_g_b, router_e, router_e_b, w1, w3, w2):
    shp = h.shape
    x = h.reshape(-1, shp[-1])
    lg = (x @ router_g + router_g_b).astype(jnp.float32)
    pg = jax.nn.softmax(lg, axis=-1)
    _, g_sel = lax.top_k(lg, 1)
    p_sel = jnp.take_along_axis(pg, g_sel, axis=-1)
    le = (x @ router_e + router_e_b).astype(jnp.float32).reshape(-1, N_GROUPS, EXPERTS_PER_GROUP)
    le_sel = jnp.take_along_axis(le, g_sel[:, :, None], axis=1)[:, 0]
    top_v, top_i = lax.top_k(le_sel, TOP_K)
    top_p = jax.nn.softmax(top_v, axis=-1) * p_sel
    gate_e = jnp.sum(jax.nn.one_hot(top_i, EXPERTS_PER_GROUP, dtype=jnp.float32) * top_p[..., None], axis=1)
    gate = jax.nn.one_hot(g_sel[:, 0], N_GROUPS, dtype=jnp.float32)[:, :, None] * gate_e[:, None, :]
    y = jnp.zeros_like(x)
    for gi in range(N_GROUPS):
        hid = (jax.nn.silu(jnp.einsum('td,edf->tef', x, w1[gi]))
               * jnp.einsum('td,edf->tef', x, w3[gi]) * gate[:, gi, :, None].astype(x.dtype))
        y = y + jnp.einsum('tef,efd->td', hid, w2[gi])
    return y.reshape(shp)


def setup_inputs(seed: int = 0) -> dict:
    key = jax.random.key(seed)
    ks = iter(jax.random.split(key, 48))
    D = D_MODEL
    E_ALL = N_GROUPS * EXPERTS_PER_GROUP

    def nrm(shape, scale):
        return jax.random.normal(next(ks), shape, jnp.float32) * scale

    def uni(shape, lo, hi):
        return jax.random.uniform(next(ks), shape, jnp.float32, minval=lo, maxval=hi)

    return {
        'x': nrm((BATCH, SEQ, D), 1.0),
        'c': nrm((BATCH, D), 1.0),
        'ctx': nrm((BATCH, CTX_LEN, D), 1.0),
        'c_ctx': nrm((D,), 1.0),
        'ada_w': nrm((DEPTH, D, N_MOD * D), 0.5 * D ** -0.5),
        'ada_b': nrm((DEPTH, N_MOD * D), 0.01),
        'norm_g': 1.0 + nrm((DEPTH, 2, D), 0.02),
        'final_g': 1.0 + nrm((D,), 0.02),
        'even_w_in': nrm((N_EVEN, D, IN_COLS), D ** -0.5),
        'even_w_out': nrm((N_EVEN, MIX_W, D), MIX_W ** -0.5),
        'diff_lambda': nrm((N_EVEN, 4, DIFF_QK_DIM), 0.1),
        'diff_subln_g': 1.0 + nrm((N_EVEN, DIFF_V_DIM), 0.02),
        'na_rpb': nrm((N_EVEN, NA_HEADS, 2 * NA_KH - 1, 2 * NA_KW - 1), 0.1),
        'rwkv_x_mix': uni((N_ODD, 6, D), 0.0, 1.0),
        'rwkv_w_rkv': nrm((N_ODD, 3, D, D), D ** -0.5),
        'rwkv_w_out': nrm((N_ODD, D, D), D ** -0.5),
        'rwkv_dec_w0': uni((N_ODD, N_DIRS, D), -6.0, -1.0),
        'rwkv_dec_w1': nrm((N_ODD, N_DIRS, D, DECAY_LORA), D ** -0.5),
        'rwkv_dec_w2': nrm((N_ODD, N_DIRS, DECAY_LORA, D), 0.1 * DECAY_LORA ** -0.5),
        'rwkv_iclr_a0': nrm((N_ODD, N_DIRS, D), 0.1),
        'rwkv_iclr_a1': nrm((N_ODD, N_DIRS, D, ICLR_LORA), D ** -0.5),
        'rwkv_iclr_a2': nrm((N_ODD, N_DIRS, ICLR_LORA, D), 0.1 * ICLR_LORA ** -0.5),
        'rwkv_gate_g1': nrm((N_ODD, D, GATE_LORA), D ** -0.5),
        'rwkv_gate_g2': nrm((N_ODD, GATE_LORA, D), GATE_LORA ** -0.5),
        'rwkv_k_k': 0.85 + nrm((N_ODD, D), 0.02),
        'rwkv_k_a': 1.0 + nrm((N_ODD, D), 0.02),
        'rwkv_r_k': nrm((N_ODD, RWKV_HEADS, RWKV_HEAD), 0.1),
        'rwkv_gn_g': 1.0 + nrm((N_ODD, D), 0.02),
        'rwkv_gn_b': nrm((N_ODD, D), 0.01),
        'moe_router_g': nrm((DEPTH, D, N_GROUPS), D ** -0.5),
        'moe_router_g_b': nrm((DEPTH, N_GROUPS), 0.01),
        'moe_router_e': nrm((DEPTH, D, E_ALL), D ** -0.5),
        'moe_router_e_b': nrm((DEPTH, E_ALL), 0.01),
        'moe_w1': nrm((DEPTH, N_GROUPS, EXPERTS_PER_GROUP, D, EXPERT_FF), D ** -0.5),
        'moe_w3': nrm((DEPTH, N_GROUPS, EXPERTS_PER_GROUP, D, EXPERT_FF), D ** -0.5),
        'moe_w2': nrm((DEPTH, N_GROUPS, EXPERTS_PER_GROUP, EXPERT_FF, D), EXPERT_FF ** -0.5),
    }


def reference(x, c, ctx, c_ctx, ada_w, ada_b, norm_g, final_g, even_w_in, even_w_out, diff_lambda,
              diff_subln_g, na_rpb, rwkv_x_mix, rwkv_w_rkv, rwkv_w_out, rwkv_dec_w0, rwkv_dec_w1,
              rwkv_dec_w2, rwkv_iclr_a0, rwkv_iclr_a1, rwkv_iclr_a2, rwkv_gate_g1, rwkv_gate_g2,
              rwkv_k_k, rwkv_k_a, rwkv_r_k, rwkv_gn_g, rwkv_gn_b, moe_router_g, moe_router_g_b,
              moe_router_e, moe_router_e_b, moe_w1, moe_w3, moe_w2):
    xl, xc = x, ctx
    cos, sin = axial_rope_tables(xl.shape[1], DIFF_QK_DIM)
    for i in range(DEPTH):
        last = i == DEPTH - 1
        j = i // 2
        mod_l = (jax.nn.silu(c) @ ada_w[i] + ada_b[i])[:, None, :]
        mod_c = (jax.nn.silu(c_ctx) @ ada_w[i] + ada_b[i])[None, None, :]
        sh1_l, sc1_l, gt1_l, sh2_l, sc2_l, gt2_l = jnp.split(mod_l, N_MOD, axis=-1)
        sh1_c, sc1_c, gt1_c, sh2_c, sc2_c, gt2_c = jnp.split(mod_c, N_MOD, axis=-1)
        hl = modulate(rmsnorm(xl, norm_g[i, 0]), sh1_l, sc1_l)
        hc = modulate(rmsnorm(xc, norm_g[i, 0]), sh1_c, sc1_c)
        if i % 2 == 0:
            out_c, out_l = even_mixer(hc, hl, even_w_in[j], even_w_out[j], diff_lambda[j],
                                      diff_subln_g[j], na_rpb[j], lambda_init(i), cos, sin, not last)
        else:
            out_c, out_l = rwkv7_mixer(hc, hl, rwkv_x_mix[j], rwkv_w_rkv[j], rwkv_w_out[j],
                                       rwkv_dec_w0[j], rwkv_dec_w1[j], rwkv_dec_w2[j],
                                       rwkv_iclr_a0[j], rwkv_iclr_a1[j], rwkv_iclr_a2[j],
                                       rwkv_gate_g1[j], rwkv_gate_g2[j], rwkv_k_k[j], rwkv_k_a[j],
                                       rwkv_r_k[j], rwkv_gn_g[j], rwkv_gn_b[j], not last)
        xl = xl + gt1_l * out_l
        xl = xl + gt2_l * hierarchical_moe(modulate(rmsnorm(xl, norm_g[i, 1]), sh2_l, sc2_l),
                                           moe_router_g[i], moe_router_g_b[i], moe_router_e[i],
                                           moe_router_e_b[i], moe_w1[i], moe_w3[i], moe_w2[i])
        if not last:
            xc = xc + gt1_c * out_c
            xc = xc + gt2_c * hierarchical_moe(modulate(rmsnorm(xc, norm_g[i, 1]), sh2_c, sc2_c),
                                               moe_router_g[i], moe_router_g_b[i], moe_router_e[i],
                                               moe_router_e_b[i], moe_w1[i], moe_w3[i], moe_w2[i])
    return rmsnorm(xl, final_g)
```

```python
import functools
import math

import numpy as np
import jax
import jax.numpy as jnp
from jax import lax
from jax.experimental import pallas as pl
from jax.experimental.pallas import tpu as pltpu

F32 = jnp.float32
BF16 = jnp.bfloat16

D_MODEL = 2048
DEPTH = 4
GRID_W = 64
CTX_LEN = 256
SEQ = 4096
S_ALL = CTX_LEN + SEQ
N_MOD = 6
NORM_EPS = 1e-6
NEG_INF = -1e30

DIFF_HEADS = 8
DIFF_QK_DIM = 64
NA_HEADS = 8
NA_DIM = 128
NA_KH = 8
NA_KW = 16
ROPE_THETA = 10000.0
SUBLN_EPS = 1e-5
HEAD_W = 128
ATT_TILE = 256
NA_ROWS_PER_STEP = 4
NA_WIN_ROWS = 12

RWKV_HEAD = 64
RWKV_HEADS = D_MODEL // RWKV_HEAD
GN_EPS = 64e-5
SCAN_CHUNK = 64
SCAN_PAIR_W = 2 * RWKV_HEAD

N_GROUPS = 4
EXPERTS_PER_GROUP = 8
N_EXPERTS = N_GROUPS * EXPERTS_PER_GROUP
TOP_K = 2
EXPERT_FF = 512

VMEM_LIMIT = 52 * 1024 * 1024


def _params(sem):
    return pltpu.CompilerParams(dimension_semantics=sem, vmem_limit_bytes=VMEM_LIMIT)


def _dot(a, b):
    return jnp.dot(a, b, preferred_element_type=F32)


def _dot_nt(a, b):
    return lax.dot_general(a, b, (((1,), (1,)), ((), ())), preferred_element_type=F32)


def _dot_tn(a, b):
    return lax.dot_general(a, b, (((0,), (0,)), ((), ())), preferred_element_type=F32)


def _mm_kernel(a_ref, w_ref, o_ref):
    o_ref[...] = _dot(a_ref[...], w_ref[...]).astype(o_ref.dtype)


def matmul(a, w, out_dtype, tm=512, tn=512):
    m, k = a.shape
    n = w.shape[1]
    tm = min(tm, m)
    tn = min(tn, n)
    assert m % tm == 0 and n % tn == 0
    return pl.pallas_call(
        _mm_kernel,
        grid=(m // tm, n // tn),
        in_specs=[pl.BlockSpec((tm, k), lambda i, j: (i, 0)),
                  pl.BlockSpec((k, tn), lambda i, j: (0, j))],
        out_specs=pl.BlockSpec((tm, tn), lambda i, j: (i, j)),
        out_shape=jax.ShapeDtypeStruct((m, n), out_dtype),
        compiler_params=_params(("parallel", "parallel")),
        name="matmul",
    )(a, w)


def _ada_kernel(s_ref, w_ref, b_ref, o_ref):
    o_ref[...] = _dot(s_ref[...], w_ref[...].astype(BF16)) + b_ref[...]


def ada_modulation(svec, ada_w, ada_b, tn=1024):
    nl, d, n = ada_w.shape
    rows = svec.shape[0]
    return pl.pallas_call(
        _ada_kernel,
        grid=(nl, n // tn),
        in_specs=[pl.BlockSpec((rows, d), lambda l, j: (0, 0)),
                  pl.BlockSpec((None, d, tn), lambda l, j: (l, 0, j)),
                  pl.BlockSpec((None, 1, tn), lambda l, j: (l, 0, j))],
        out_specs=pl.BlockSpec((None, rows, tn), lambda l, j: (l, 0, j)),
        out_shape=jax.ShapeDtypeStruct((nl, rows, n), F32),
        compiler_params=_params(("parallel", "parallel")),
        name="ada_modulation",
    )(svec, ada_w, ada_b.reshape(nl, 1, n))


def _rope(x, cos, sin_signed):
    lane = lax.broadcasted_iota(jnp.int32, x.shape, 1)
    first_half = (lane & 63) < 32
    partner = jnp.where(first_half, pltpu.roll(x, HEAD_W - 32, 1), pltpu.roll(x, 32, 1))
    return x * cos + partner * sin_signed


def _diff_attn_kernel(lam_ref, q_ref, k_ref, v_ref, cos_ref, sin_ref, g_ref, o_ref, kr_ref, *, post_scale):
    j = pl.program_id(2)
    n_chunks = S_ALL // ATT_TILE

    @pl.when(j == 0)
    def _():
        def rope_chunk(c, carry):
            rows = pl.ds(pl.multiple_of(c * ATT_TILE, ATT_TILE), ATT_TILE)
            kr_ref[rows, :] = _rope(k_ref[rows, :].astype(F32), cos_ref[rows, :], sin_ref[rows, :]).astype(BF16)
            return carry
        lax.fori_loop(0, n_chunks, rope_chunk, 0)

    qrows = pl.ds(pl.multiple_of(j * ATT_TILE, ATT_TILE), ATT_TILE)
    q = _rope(q_ref[...].astype(F32), cos_ref[qrows, :], sin_ref[qrows, :]) * (DIFF_QK_DIM ** -0.5)
    lane = lax.broadcasted_iota(jnp.int32, q.shape, 1)
    q1 = jnp.where(lane < DIFF_QK_DIM, q, 0.0).astype(BF16)
    q2 = jnp.where(lane < DIFF_QK_DIM, 0.0, q).astype(BF16)

    def kv_chunk(c, carry):
        m1, l1, a1, m2, l2, a2 = carry
        rows = pl.ds(pl.multiple_of(c * ATT_TILE, ATT_TILE), ATT_TILE)
        kc = kr_ref[rows, :]
        vc = v_ref[rows, :]

        def update(qm, m, l, acc):
            s = _dot_nt(qm, kc)
            m_new = jnp.maximum(m, jnp.max(s, axis=-1, keepdims=True))
            alpha = jnp.exp(m - m_new)
            p = jnp.exp(s - m_new)
            l_new = alpha * l + jnp.sum(p, axis=-1, keepdims=True)
            acc_new = alpha * acc + _dot(p.astype(BF16), vc)
            return m_new, l_new, acc_new

        m1, l1, a1 = update(q1, m1, l1, a1)
        m2, l2, a2 = update(q2, m2, l2, a2)
        return m1, l1, a1, m2, l2, a2

    col = jnp.full((ATT_TILE, 1), NEG_INF, F32)
    zcol = jnp.zeros((ATT_TILE, 1), F32)
    zacc = jnp.zeros((ATT_TILE, HEAD_W), F32)
    n_kv = jnp.where(j == 0, CTX_LEN // ATT_TILE, n_chunks)
    m1, l1, a1, m2, l2, a2 = lax.fori_loop(0, n_kv, kv_chunk, (col, zcol, zacc, col, zcol, zacc))
    out = a1 / l1 - lam_ref[0] * (a2 / l2)
    ms = jnp.mean(out * out, axis=-1, keepdims=True)
    y = out * lax.rsqrt(ms + SUBLN_EPS) * g_ref[...] * post_scale
    o_ref[...] = y.astype(o_ref.dtype)


def diff_attention(proj, lam, cos_t, sin_t, subln_g, lam_init):
    b = proj.shape[0]
    kernel = functools.partial(_diff_attn_kernel, post_scale=1.0 - lam_init)
    return pl.pallas_call(
        kernel,
        grid=(b, DIFF_HEADS, S_ALL // ATT_TILE),
        in_specs=[pl.BlockSpec(memory_space=pltpu.SMEM),
                  pl.BlockSpec((None, ATT_TILE, HEAD_W), lambda bi, h, j: (bi, j, h)),
                  pl.BlockSpec((None, S_ALL, HEAD_W), lambda bi, h, j: (bi, 0, DIFF_HEADS + h)),
                  pl.BlockSpec((None, S_ALL, HEAD_W), lambda bi, h, j: (bi, 0, 2 * DIFF_HEADS + h)),
                  pl.BlockSpec((S_ALL, HEAD_W), lambda bi, h, j: (0, 0)),
                  pl.BlockSpec((S_ALL, HEAD_W), lambda bi, h, j: (0, 0)),
                  pl.BlockSpec((1, HEAD_W), lambda bi, h, j: (0, 0))],
        out_specs=pl.BlockSpec((None, ATT_TILE, HEAD_W), lambda bi, h, j: (bi, j, h)),
        out_shape=jax.ShapeDtypeStruct((b, S_ALL, DIFF_HEADS * HEAD_W), BF16),
        scratch_shapes=[pltpu.VMEM((S_ALL, HEAD_W), BF16)],
        compiler_params=_params(("parallel", "parallel", "arbitrary")),
        name="diff_attention",
    )(lam, proj, proj, proj, cos_t, sin_t, subln_g.reshape(1, HEAD_W))


def rope_tables():
    n_freq = DIFF_QK_DIM // 4
    inv_freq = ROPE_THETA ** (-jnp.arange(n_freq, dtype=F32) / n_freq)
    t = jnp.arange(SEQ, dtype=jnp.int32)
    row = (t // GRID_W).astype(F32)
    col = (t % GRID_W).astype(F32)
    ang = jnp.concatenate([row[:, None] * inv_freq, col[:, None] * inv_freq], axis=-1)
    cos, sin = jnp.cos(ang), jnp.sin(ang)
    cos_l = jnp.concatenate([cos, cos, cos, cos], axis=-1)
    sin_l = jnp.concatenate([-sin, sin, -sin, sin], axis=-1)
    cos_all = jnp.concatenate([jnp.ones((CTX_LEN, HEAD_W), F32), cos_l], axis=0)
    sin_all = jnp.concatenate([jnp.zeros((CTX_LEN, HEAD_W), F32), sin_l], axis=0)
    return cos_all, sin_all


def _na_window_start(j):
    g = j - 1
    return jnp.clip(NA_ROWS_PER_STEP * g - NA_KH // 2, 0, SEQ // GRID_W - NA_WIN_ROWS)


def _na_kernel(q_ref, k_ref, v_ref, bias_ref, o_ref):
    j = pl.program_id(2)
    win = NA_WIN_ROWS * GRID_W
    start = pl.multiple_of(CTX_LEN + _na_window_start(j) * GRID_W, GRID_W)
    scale = NA_DIM ** -0.5
    q = q_ref[...]
    s_c = _dot_nt(q, k_ref[pl.ds(0, CTX_LEN), :]) * scale
    s_w = _dot_nt(q, k_ref[pl.ds(start, win), :]) * scale + bias_ref[...]
    m = jnp.maximum(jnp.max(s_c, axis=-1, keepdims=True), jnp.max(s_w, axis=-1, keepdims=True))
    p_c = jnp.exp(s_c - m)
    p_w = jnp.exp(s_w - m)
    l = jnp.sum(p_c, axis=-1, keepdims=True) + jnp.sum(p_w, axis=-1, keepdims=True)
    o = _dot(p_c.astype(BF16), v_ref[pl.ds(0, CTX_LEN), :]) + _dot(p_w.astype(BF16), v_ref[pl.ds(start, win), :])
    o_ref[...] = (o / l).astype(o_ref.dtype)


def _na_bias_pattern(j):
    n_groups = SEQ // (GRID_W * NA_ROWS_PER_STEP)
    g = j - 1
    return jnp.where(j == 0, 3, jnp.where(g == 0, 0, jnp.where(g == n_groups - 1, 2, 1)))


def na_bias_table(rpb):
    rows = SEQ // GRID_W
    n_groups = rows // NA_ROWS_PER_STEP
    cols = np.arange(GRID_W)
    col_start = np.clip(cols - NA_KW // 2, 0, GRID_W - NA_KW)
    col_mask = (cols[None, :] >= col_start[:, None]) & (cols[None, :] < col_start[:, None] + NA_KW)
    c_idx = np.clip(cols[None, :] - cols[:, None] + NA_KW - 1, 0, 2 * NA_KW - 2)
    pats = []
    for g in (0, 1, n_groups - 1):
        u0 = int(np.clip(NA_ROWS_PER_STEP * g - NA_KH // 2, 0, rows - NA_WIN_ROWS))
        r = NA_ROWS_PER_STEP * g + np.arange(NA_ROWS_PER_STEP)
        r0 = np.clip(r - NA_KH // 2, 0, rows - NA_KH)
        kr = u0 + np.arange(NA_WIN_ROWS)
        valid_r = (kr[None, :] >= r0[:, None]) & (kr[None, :] < r0[:, None] + NA_KH)
        r_idx = np.clip(kr[None, :] - r[:, None] + NA_KH - 1, 0, 2 * NA_KH - 2)
        valid = valid_r[:, None, :, None] & col_mask[None, :, None, :]
        gathered = rpb[:, r_idx[:, None, :, None], c_idx[None, :, None, :]].astype(F32)
        pats.append(jnp.where(valid[None], gathered, NEG_INF))
    pats.append(jnp.full_like(pats[0], NEG_INF))
    tab = jnp.stack(pats, axis=1)
    return tab.reshape(NA_HEADS, 4, NA_ROWS_PER_STEP * GRID_W, NA_WIN_ROWS * GRID_W)


def neighbourhood_attention(proj, bias_tab):
    b = proj.shape[0]
    tq = NA_ROWS_PER_STEP * GRID_W
    assert tq == CTX_LEN
    win = NA_WIN_ROWS * GRID_W
    base = 3 * DIFF_HEADS
    return pl.pallas_call(
        _na_kernel,
        grid=(b, NA_HEADS, S_ALL // tq),
        in_specs=[pl.BlockSpec((None, tq, HEAD_W), lambda bi, h, j: (bi, j, base + h)),
                  pl.BlockSpec((None, S_ALL, HEAD_W), lambda bi, h, j: (bi, 0, base + NA_HEADS + h)),
                  pl.BlockSpec((None, S_ALL, HEAD_W), lambda bi, h, j: (bi, 0, base + 2 * NA_HEADS + h)),
                  pl.BlockSpec((None, None, tq, win), lambda bi, h, j: (h, _na_bias_pattern(j), 0, 0))],
        out_specs=pl.BlockSpec((None, tq, HEAD_W), lambda bi, h, j: (bi, j, h)),
        out_shape=jax.ShapeDtypeStruct((b, S_ALL, NA_HEADS * HEAD_W), BF16),
        compiler_params=_params(("parallel", "parallel", "arbitrary")),
        name="neighbourhood_attention",
    )(proj, proj, proj, bias_tab)


def _scan_prepare_kernel(r_ref, kk_ref, v_ref, ld_ref, a_ref, kd_ref, y0_ref, rm_ref, d0_ref, *, reverse, n_sub):
    c = SCAN_CHUNK
    w = SCAN_PAIR_W
    hw = RWKV_HEAD
    t_idx = lax.broadcasted_iota(jnp.int32, (c, w), 0)
    lane = lax.broadcasted_iota(jnp.int32, (c, w), 1)
    s_idx = lane & (hw - 1)
    head0 = lane < hw
    if reverse:
        strict = s_idx > t_idx
        incl = s_idx >= t_idx
    else:
        strict = s_idx < t_idx
        incl = s_idx <= t_idx
    eye = s_idx == t_idx
    tt = lax.broadcasted_iota(jnp.int32, (c, c), 0)
    ss = lax.broadcasted_iota(jnp.int32, (c, c), 1)
    tri = jnp.where((ss >= tt) if reverse else (ss <= tt), 1.0, 0.0).astype(BF16)
    row2 = lax.broadcasted_iota(jnp.int32, (w, w), 0)
    lane2 = lax.broadcasted_iota(jnp.int32, (w, w), 1)
    bdmask = (row2 < hw) == (lane2 < hw)
    eye2 = row2 == lane2

    def bd(y):
        return jnp.where(bdmask, jnp.concatenate([y, y], axis=0), 0.0).astype(BF16)

    def pm(x, ybd):
        return _dot(x.astype(BF16), ybd)

    for sub in range(n_sub):
        rows = slice(sub * c, (sub + 1) * c)
        r = r_ref[rows, :]
        kk = kk_ref[rows, :]
        v = v_ref[rows, :]
        ld = ld_ref[rows, :]
        a = a_ref[rows, :]
        kd = kd_ref[rows, :]

        p_hi = ld.astype(BF16)
        rem = ld - p_hi.astype(F32)
        p_mid = rem.astype(BF16)
        p_lo = (rem - p_mid.astype(F32)).astype(BF16)
        cs = _dot(tri, jnp.concatenate([p_hi, p_mid, p_lo], axis=1))
        lam = cs[:, :w] + cs[:, w:2 * w] + cs[:, 2 * w:]
        lam_end = lam[0:1, :] if reverse else lam[c - 1:c, :]
        g_cum = jnp.exp(lam)
        g_inv = jnp.exp(-lam)
        g_prev = jnp.exp(lam - ld)
        g_rel = jnp.exp(lam_end - lam)
        g_end = jnp.exp(lam_end)

        qk = kk * g_prev
        rt = r * g_cum
        beta = kk * a
        bt = beta * g_inv
        kt = kd * g_inv
        bh = beta * g_rel
        kh = kd * g_rel

        rhs = jnp.concatenate([jnp.where(head0, bt, 0.0), jnp.where(head0, 0.0, bt),
                               jnp.where(head0, kt, 0.0), jnp.where(head0, 0.0, kt)], axis=0).astype(BF16)
        big = _dot_nt(jnp.concatenate([qk, rt], axis=0).astype(BF16), rhs)
        n_pow = jnp.where(strict, -big[:c, :w], 0.0)
        a_k = jnp.where(strict, big[:c, w:], 0.0)
        g_b = jnp.where(incl, big[c:, :w], 0.0)
        g_k = jnp.where(incl, big[c:, w:], 0.0)

        t_inv = jnp.where(eye, 1.0, 0.0) + n_pow
        for _ in range(int(math.log2(c)) - 1):
            n_pow = pm(n_pow, bd(n_pow))
            t_inv = t_inv + pm(t_inv, bd(n_pow))

        v_bd = bd(v)
        x1 = pm(a_k, v_bd)
        uw = _dot(t_inv.astype(BF16), jnp.concatenate([bd(x1), bd(qk)], axis=1))
        u0 = -uw[:, :w]
        wm = uw[:, w:]
        y0 = pm(g_k, v_bd) + pm(g_b, bd(u0))
        rm = rt - pm(g_b, bd(wm))
        d0 = _dot_tn(jnp.concatenate([kh, bh], axis=0).astype(BF16), jnp.concatenate([v, u0], axis=0).astype(BF16))
        d0 = jnp.where(bdmask, d0, 0.0)
        bw = _dot_tn(bh.astype(BF16), wm.astype(BF16))
        mm = jnp.where(eye2, g_end, 0.0) - jnp.where(bdmask, bw, 0.0)
        mm_hi = mm.astype(BF16)
        mm_lo = (mm - mm_hi.astype(F32)).astype(BF16)

        y0_ref[rows, :] = y0
        rm_ref[sub, 0, 0:c, :] = rm.astype(BF16)
        rm_ref[sub, 0, c:c + w, :] = mm_hi
        rm_ref[sub, 0, c + w:c + 2 * w, :] = mm_lo
        d0_ref[sub, 0, :, :] = d0


def scan_prepare(r, kk, v, ld, a, kd, reverse, n_sub=4):
    rows, d = r.shape
    c, w = SCAN_CHUNK, SCAN_PAIR_W
    n_pairs = d // w
    n_chunks = rows // c
    blk = n_sub * c
    kernel = functools.partial(_scan_prepare_kernel, reverse=reverse, n_sub=n_sub)
    in_spec = pl.BlockSpec((blk, w), lambda i, p: (i, p))
    return pl.pallas_call(
        kernel,
        grid=(rows // blk, n_pairs),
        in_specs=[in_spec] * 6,
        out_specs=[pl.BlockSpec((blk, w), lambda i, p: (i, p)),
                   pl.BlockSpec((n_sub, 1, c + 2 * w, w), lambda i, p: (i, p, 0, 0)),
                   pl.BlockSpec((n_sub, 1, w, w), lambda i, p: (i, p, 0, 0))],
        out_shape=[jax.ShapeDtypeStruct((rows, d), F32),
                   jax.ShapeDtypeStruct((n_chunks, n_pairs, c + 2 * w, w), BF16),
                   jax.ShapeDtypeStruct((n_chunks, n_pairs, w, w), F32)],
        compiler_params=_params(("parallel", "parallel")),
        name="scan_prepare_rev" if reverse else "scan_prepare_fwd",
    )(r, kk, v, ld, a, kd)


def _scan_apply_kernel(y0_ref, rm_ref, d0_ref, y_ref, z_ref, *, n_pairs):
    c, w = SCAN_CHUNK, SCAN_PAIR_W
    i = pl.program_id(1)

    @pl.when(i == 0)
    def _():
        z_ref[...] = jnp.zeros_like(z_ref)

    for p in range(n_pairs):
        z = z_ref[p]
        z_hi = z.astype(BF16)
        z_lo = (z - z_hi.astype(F32)).astype(BF16)
        res = _dot(rm_ref[0, p], jnp.concatenate([z_hi, z_lo], axis=1))
        res = res[:, :w] + res[:, w:]
        cols = slice(p * w, (p + 1) * w)
        y_ref[:, cols] = y0_ref[:, cols] + res[:c]
        z_ref[p] = d0_ref[0, p] + res[c:c + w] + res[c + w:]


def _scan_chunk_index(i, reverse):
    n_ctx = CTX_LEN // SCAN_CHUNK
    n_all = S_ALL // SCAN_CHUNK
    if not reverse:
        return i
    return jnp.where(i < n_ctx, n_ctx - 1 - i, n_all + n_ctx - 1 - i)


def scan_apply(y0, rm, d0, batch, reverse):
    rows, d = y0.shape
    c, w = SCAN_CHUNK, SCAN_PAIR_W
    n_pairs = d // w
    per_b = rows // batch // c

    def idx(b, i):
        return b * per_b + _scan_chunk_index(i, reverse)

    kernel = functools.partial(_scan_apply_kernel, n_pairs=n_pairs)
    return pl.pallas_call(
        kernel,
        grid=(batch, per_b),
        in_specs=[pl.BlockSpec((c, d), lambda b, i: (idx(b, i), 0)),
                  pl.BlockSpec((1, n_pairs, c + 2 * w, w), lambda b, i: (idx(b, i), 0, 0, 0)),
                  pl.BlockSpec((1, n_pairs, w, w), lambda b, i: (idx(b, i), 0, 0, 0))],
        out_specs=pl.BlockSpec((c, d), lambda b, i: (idx(b, i), 0)),
        out_shape=jax.ShapeDtypeStruct((rows, d), F32),
        scratch_shapes=[pltpu.VMEM((n_pairs, w, w), F32)],
        compiler_params=_params(("parallel", "arbitrary")),
        name="scan_apply_rev" if reverse else "scan_apply_fwd",
    )(y0, rm, d0)


def rwkv7_scan_bidir(r, kk, v, ld, a, kd, batch):
    y = None
    for d, rev in enumerate((False, True)):
        y0, rm, d0 = scan_prepare(r, kk, v, ld[d], a[d], kd[d], rev)
        yd = scan_apply(y0, rm, d0, batch, rev)
        y = yd if y is None else y + yd
    return y


def _moe_kernel(h_ref, gate_ref, w1_ref, w3_ref, w2_ref, o_ref, acc_ref):
    e = pl.program_id(1)

    @pl.when(e == 0)
    def _():
        acc_ref[...] = jnp.zeros_like(acc_ref)

    h = h_ref[...]
    lane = lax.broadcasted_iota(jnp.int32, gate_ref.shape, 1)
    g = jnp.sum(jnp.where(lane == e, gate_ref[...], 0.0), axis=-1, keepdims=True)
    up = _dot(h, w1_ref[...])
    hid = (up * jax.nn.sigmoid(up)) * _dot(h, w3_ref[...]) * g
    acc_ref[...] += _dot(hid.astype(BF16), w2_ref[...])

    @pl.when(e == pl.num_programs(1) - 1)
    def _():
        o_ref[...] = acc_ref[...]


def moe_experts(h, gate, w1, w3, w2, tm=512):
    t, d = h.shape
    ne, _, f = w1.shape
    return pl.pallas_call(
        _moe_kernel,
        grid=(t // tm, ne),
        in_specs=[pl.BlockSpec((tm, d), lambda i, e: (i, 0)),
                  pl.BlockSpec((tm, ne), lambda i, e: (i, 0)),
                  pl.BlockSpec((None, d, f), lambda i, e: (e, 0, 0)),
                  pl.BlockSpec((None, d, f), lambda i, e: (e, 0, 0)),
                  pl.BlockSpec((None, f, d), lambda i, e: (e, 0, 0))],
        out_specs=pl.BlockSpec((tm, d), lambda i, e: (i, 0)),
        out_shape=jax.ShapeDtypeStruct((t, d), F32),
        scratch_shapes=[pltpu.VMEM((tm, d), F32)],
        compiler_params=_params(("parallel", "arbitrary")),
        name="moe_experts",
    )(h, gate, w1, w3, w2)


def moe_gates(h, router_g, router_g_b, router_e, router_e_b):
    hp = lax.Precision.HIGHEST
    lg = jnp.dot(h, router_g, precision=hp) + router_g_b
    pg = jax.nn.softmax(lg, axis=-1)
    _, g_sel = lax.top_k(lg, 1)
    p_sel = jnp.take_along_axis(pg, g_sel, axis=-1)
    le = (jnp.dot(h, router_e, precision=hp) + router_e_b).reshape(-1, N_GROUPS, EXPERTS_PER_GROUP)
    le_sel = jnp.take_along_axis(le, g_sel[:, :, None], axis=1)[:, 0]
    top_v, top_i = lax.top_k(le_sel, TOP_K)
    top_p = jax.nn.softmax(top_v, axis=-1) * p_sel
    gate_e = jnp.sum(jax.nn.one_hot(top_i, EXPERTS_PER_GROUP, dtype=F32) * top_p[..., None], axis=1)
    gate = jax.nn.one_hot(g_sel[:, 0], N_GROUPS, dtype=F32)[:, :, None] * gate_e[:, None, :]
    return gate.reshape(-1, N_EXPERTS)


def _rmsnorm(x, g, eps=NORM_EPS):
    return x * lax.rsqrt(jnp.mean(x * x, axis=-1, keepdims=True) + eps) * g


def _seg(mod, k):
    is_latent = (jnp.arange(S_ALL) >= CTX_LEN)[None, :, None]
    return jnp.where(is_latent, mod[:, 1, k][:, None, :], mod[:, 0, k][:, None, :])


def _modulated(x, mod, k_shift, k_scale, g):
    return _rmsnorm(x, g) * (1.0 + _seg(mod, k_scale)) + _seg(mod, k_shift)


def _gated_add(x, mod, k_gate, upd):
    return x + _seg(mod, k_gate) * upd


def _token_shift(h):
    def one(x):
        prev = jnp.pad(x[:, :-1], ((0, 0), (1, 0), (0, 0)))
        nxt = jnp.pad(x[:, 1:], ((0, 0), (0, 1), (0, 0)))
        return 0.5 * (prev + nxt) - x
    return jnp.concatenate([one(h[:, :CTX_LEN]), one(h[:, CTX_LEN:])], axis=1)


def _pad_cols(w, n):
    return jnp.pad(w, ((0, 0), (0, n - w.shape[1])))


def _pad_rows(w, n):
    return jnp.pad(w, ((0, n - w.shape[0]), (0, 0)))


def _two_dir_lora(x, w1, w2):
    r = w1.shape[-1]
    w1c = _pad_cols(jnp.concatenate([w1[0], w1[1]], axis=1), 256).astype(BF16)
    d = w2.shape[-1]
    w2bd = jnp.zeros((256, 2 * d), F32).at[:r, :d].set(w2[0]).at[r:2 * r, d:].set(w2[1]).astype(BF16)
    return matmul(x, w1c, F32, tn=256), w2bd


def _heads(z):
    return z.reshape(*z.shape[:-1], RWKV_HEADS, RWKV_HEAD)


def rwkv_mixer(h, p, batch):
    rows = batch * S_ALL
    xx = _token_shift(h)
    mixes = [(h + xx * p['x_mix'][j]).reshape(rows, D_MODEL).astype(BF16) for j in range(6)]
    r = matmul(mixes[0], p['w_rkv'][0].astype(BF16), F32)
    k = matmul(mixes[1], p['w_rkv'][1].astype(BF16), F32)
    v = matmul(mixes[2], p['w_rkv'][2].astype(BF16), F32)
    hid_w, w2bd_w = _two_dir_lora(mixes[3], p['dec_w1'], p['dec_w2'])
    w_lora = matmul(jnp.tanh(hid_w).astype(BF16), w2bd_w, F32)
    hid_a, w2bd_a = _two_dir_lora(mixes[4], p['iclr_a1'], p['iclr_a2'])
    a_lora = matmul(hid_a.astype(BF16), w2bd_a, F32)
    g_hid = matmul(mixes[5], p['gate_g1'].astype(BF16), F32, tn=256)
    g = matmul(jax.nn.sigmoid(g_hid).astype(BF16), p['gate_g2'].astype(BF16), F32)

    w_lora = jnp.stack([w_lora[:, :D_MODEL], w_lora[:, D_MODEL:]])
    a_lora = jnp.stack([a_lora[:, :D_MODEL], a_lora[:, D_MODEL:]])
    log_w = -jax.nn.softplus(-(p['dec_w0'][:, None, :] + w_lora)) - 0.5
    ld = -jnp.exp(log_w)
    a = jax.nn.sigmoid(p['iclr_a0'][:, None, :] + a_lora)
    kk = _heads(k * p['k_k'])
    kk = (kk * lax.rsqrt(jnp.maximum(jnp.sum(kk * kk, axis=-1, keepdims=True), 1e-24))).reshape(rows, D_MODEL)
    kd = k[None] * (1.0 + (a - 1.0) * p['k_a'])

    y = rwkv7_scan_bidir(r, kk, v, ld, a, kd, batch)

    yh = _heads(y)
    mu = jnp.mean(yh, axis=-1, keepdims=True)
    var = jnp.mean(jnp.square(yh - mu), axis=-1, keepdims=True)
    yn = ((yh - mu) * lax.rsqrt(var + GN_EPS)).reshape(rows, D_MODEL) * p['gn_g'] + p['gn_b']
    bonus = jnp.sum(_heads(r)[None] * _heads(kd) * p['r_k'], axis=(0, -1))[..., None] * _heads(v)
    yo = (yn + bonus.reshape(rows, D_MODEL)) * g
    return matmul(yo.astype(BF16), p['w_out'].astype(BF16), F32)


def even_mixer(h, p, layer_idx, cos_t, sin_t, batch):
    rows = batch * S_ALL
    lam_init = 0.8 - 0.6 * math.exp(-0.3 * layer_idx)
    proj = matmul(h.reshape(rows, D_MODEL).astype(BF16), p['w_in'].astype(BF16), BF16, tn=1024)
    proj = proj.reshape(batch, S_ALL, -1)
    lf = p['diff_lambda']
    lam = (jnp.exp(jnp.sum(lf[0] * lf[1])) - jnp.exp(jnp.sum(lf[2] * lf[3])) + lam_init).reshape(1)
    a_out = diff_attention(proj, lam, cos_t, sin_t, p['subln_g'], lam_init)
    b_out = neighbourhood_attention(proj, na_bias_table(p['rpb']))
    mixed = jnp.concatenate([a_out, b_out], axis=-1).reshape(rows, -1)
    return matmul(mixed, p['w_out'].astype(BF16), F32)


def moe_block(h, p):
    rows = h.shape[0] * S_ALL
    hf = h.reshape(rows, D_MODEL)
    gate = moe_gates(hf, p['router_g'], p['router_g_b'], p['router_e'], p['router_e_b'])
    d, f = D_MODEL, EXPERT_FF
    return moe_experts(hf.astype(BF16), gate,
                       p['w1'].reshape(N_EXPERTS, d, f).astype(BF16),
                       p['w3'].reshape(N_EXPERTS, d, f).astype(BF16),
                       p['w2'].reshape(N_EXPERTS, f, d).astype(BF16))


def kernel(x, c, ctx, c_ctx, ada_w, ada_b, norm_g, final_g, even_w_in, even_w_out, diff_lambda, diff_subln_g, na_rpb, rwkv_x_mix, rwkv_w_rkv, rwkv_w_out, rwkv_dec_w0, rwkv_dec_w1, rwkv_dec_w2, rwkv_iclr_a0, rwkv_iclr_a1, rwkv_iclr_a2, rwkv_gate_g1, rwkv_gate_g2, rwkv_k_k, rwkv_k_a, rwkv_r_k, rwkv_gn_g, rwkv_gn_b, moe_router_g, moe_router_g_b, moe_router_e, moe_router_e_b, moe_w1, moe_w3, moe_w2):
    batch = x.shape[0]
    xs = jnp.concatenate([ctx, x], axis=1)
    cos_t, sin_t = rope_tables()

    cvec = jnp.concatenate([c, c_ctx[None], jnp.zeros((8 - batch - 1, D_MODEL), F32)], axis=0)
    mods = ada_modulation(jax.nn.silu(cvec).astype(BF16), ada_w, ada_b)
    mods = mods.reshape(DEPTH, 8, N_MOD, D_MODEL)

    for i in range(DEPTH):
        j = i // 2
        mod_l = mods[i, :batch]
        mod_c = jnp.broadcast_to(mods[i, batch][None], mod_l.shape)
        mod = jnp.stack([mod_c, mod_l], axis=1)
        h = _modulated(xs, mod, 0, 1, norm_g[i, 0])
        if i % 2 == 0:
            p = dict(w_in=even_w_in[j], w_out=even_w_out[j], diff_lambda=diff_lambda[j],
                     subln_g=diff_subln_g[j], rpb=na_rpb[j])
            out = even_mixer(h, p, i, cos_t, sin_t, batch)
        else:
            p = dict(x_mix=rwkv_x_mix[j], w_rkv=rwkv_w_rkv[j], w_out=rwkv_w_out[j], dec_w0=rwkv_dec_w0[j],
                     dec_w1=rwkv_dec_w1[j], dec_w2=rwkv_dec_w2[j], iclr_a0=rwkv_iclr_a0[j],
                     iclr_a1=rwkv_iclr_a1[j], iclr_a2=rwkv_iclr_a2[j], gate_g1=rwkv_gate_g1[j],
                     gate_g2=rwkv_gate_g2[j], k_k=rwkv_k_k[j], k_a=rwkv_k_a[j], r_k=rwkv_r_k[j],
                     gn_g=rwkv_gn_g[j], gn_b=rwkv_gn_b[j])
            out = rwkv_mixer(h, p, batch)
        xs = _gated_add(xs, mod, 2, out.reshape(batch, S_ALL, D_MODEL))
        h2 = _modulated(xs, mod, 3, 4, norm_g[i, 1])
        pm = dict(router_g=moe_router_g[i], router_g_b=moe_router_g_b[i], router_e=moe_router_e[i],
                  router_e_b=moe_router_e_b[i], w1=moe_w1[i], w3=moe_w3[i], w2=moe_w2[i])
        xs = _gated_add(xs, mod, 5, moe_block(h2, pm).reshape(batch, S_ALL, D_MODEL))
    return _rmsnorm(xs[:, CTX_LEN:], final_g)
```

```python
import functools
import math

import numpy as np
import jax
import jax.numpy as jnp
from jax import lax
from jax.experimental import pallas as pl
from jax.experimental.pallas import tpu as pltpu

F32 = jnp.float32
BF16 = jnp.bfloat16

D_MODEL = 2048
DEPTH = 4
GRID_W = 64
CTX_LEN = 256
SEQ = 4096
S_ALL = CTX_LEN + SEQ
N_MOD = 6
NORM_EPS = 1e-6
NEG_INF = -1e30

DIFF_HEADS = 8
DIFF_QK_DIM = 64
NA_HEADS = 8
NA_DIM = 128
NA_KH = 8
NA_KW = 16
ROPE_THETA = 10000.0
SUBLN_EPS = 1e-5
HEAD_W = 128
ATT_TILE = 256
DIFF_KV_CHUNK = 1024
NA_ROWS_PER_STEP = 4
NA_WIN_ROWS = 12

RWKV_HEAD = 64
RWKV_HEADS = D_MODEL // RWKV_HEAD
GN_EPS = 64e-5
SCAN_CHUNK = 64
SCAN_PAIR_W = 2 * RWKV_HEAD
SCAN_SUB = 4

N_GROUPS = 4
EXPERTS_PER_GROUP = 8
N_EXPERTS = N_GROUPS * EXPERTS_PER_GROUP
TOP_K = 2
EXPERT_FF = 512
MOE_TILE = 256

VMEM_LIMIT = 52 * 1024 * 1024


def _params(sem):
    return pltpu.CompilerParams(dimension_semantics=sem, vmem_limit_bytes=VMEM_LIMIT)


def _dot(a, b):
    return jnp.dot(a, b, preferred_element_type=F32)


def _dot_nt(a, b):
    return lax.dot_general(a, b, (((1,), (1,)), ((), ())), preferred_element_type=F32)


def _dot_tn(a, b):
    return lax.dot_general(a, b, (((0,), (0,)), ((), ())), preferred_element_type=F32)


def _mm_kernel(a_ref, w_ref, o_ref):
    o_ref[...] = _dot(a_ref[...], w_ref[...]).astype(o_ref.dtype)


def matmul(a, w, out_dtype, tm=512, tn=512):
    m, k = a.shape
    n = w.shape[1]
    tm = min(tm, m)
    tn = min(tn, n)
    assert m % tm == 0 and n % tn == 0
    return pl.pallas_call(
        _mm_kernel,
        grid=(m // tm, n // tn),
        in_specs=[pl.BlockSpec((tm, k), lambda i, j: (i, 0)),
                  pl.BlockSpec((k, tn), lambda i, j: (0, j))],
        out_specs=pl.BlockSpec((tm, tn), lambda i, j: (i, j)),
        out_shape=jax.ShapeDtypeStruct((m, n), out_dtype),
        compiler_params=_params(("parallel", "parallel")),
        name="matmul",
    )(a, w)


def _ada_kernel(s_ref, w_ref, b_ref, o_ref):
    o_ref[...] = _dot(s_ref[...], w_ref[...].astype(BF16)) + b_ref[...]


def ada_modulation(svec, ada_w, ada_b, tn=1024):
    nl, d, n = ada_w.shape
    rows = svec.shape[0]
    return pl.pallas_call(
        _ada_kernel,
        grid=(nl, n // tn),
        in_specs=[pl.BlockSpec((rows, d), lambda l, j: (0, 0)),
                  pl.BlockSpec((None, d, tn), lambda l, j: (l, 0, j)),
                  pl.BlockSpec((None, 1, tn), lambda l, j: (l, 0, j))],
        out_specs=pl.BlockSpec((None, rows, tn), lambda l, j: (l, 0, j)),
        out_shape=jax.ShapeDtypeStruct((nl, rows, n), F32),
        compiler_params=_params(("parallel", "parallel")),
        name="ada_modulation",
    )(svec, ada_w, ada_b.reshape(nl, 1, n))


def _rope(x, cos, sin_signed):
    lane = lax.broadcasted_iota(jnp.int32, x.shape, 1)
    first_half = (lane & 63) < 32
    partner = jnp.where(first_half, pltpu.roll(x, HEAD_W - 32, 1), pltpu.roll(x, 32, 1))
    return x * cos + partner * sin_signed


def _diff_attn_kernel(lam_ref, q_ref, k_ref, v_ref, cos_ref, sin_ref, g_ref, o_ref, kr_ref, vt_ref, *, post_scale):
    j = pl.program_id(2)
    tq = ATT_TILE

    @pl.when(j == 0)
    def _():
        def prep_chunk(c, carry):
            rows = pl.ds(pl.multiple_of(c * ATT_TILE, ATT_TILE), ATT_TILE)
            kr_ref[rows, :] = _rope(k_ref[rows, :].astype(F32), cos_ref[rows, :], sin_ref[rows, :]).astype(BF16)
            vt_ref[:, rows] = v_ref[rows, :].astype(F32).T.astype(BF16)
            return carry
        lax.fori_loop(0, S_ALL // ATT_TILE, prep_chunk, 0)

    qrows = pl.ds(pl.multiple_of(j * tq, tq), tq)
    q = _rope(q_ref[...].astype(F32), cos_ref[qrows, :], sin_ref[qrows, :]) * (DIFF_QK_DIM ** -0.5)
    lane = lax.broadcasted_iota(jnp.int32, q.shape, 1)
    q1 = jnp.where(lane < DIFF_QK_DIM, q, 0.0).astype(BF16)
    q2 = jnp.where(lane < DIFF_QK_DIM, 0.0, q).astype(BF16)

    def attend(start, size, state):
        kc = kr_ref[pl.ds(start, size), :]
        vt = vt_ref[:, pl.ds(start, size)]

        def update(qm, m, l, acc):
            s = _dot_nt(kc, qm)
            m_new = jnp.maximum(m, jnp.max(s, axis=0, keepdims=True))
            alpha = jnp.exp(m - m_new)
            p = jnp.exp(s - m_new)
            l_new = alpha * l + jnp.sum(p, axis=0, keepdims=True)
            acc_new = alpha * acc + _dot(vt, p.astype(BF16))
            return m_new, l_new, acc_new

        m1, l1, a1, m2, l2, a2 = state
        return update(q1, m1, l1, a1) + update(q2, m2, l2, a2)

    row = jnp.full((1, tq), NEG_INF, F32)
    zrow = jnp.zeros((1, tq), F32)
    zacc = jnp.zeros((HEAD_W, tq), F32)
    state = attend(0, CTX_LEN, (row, zrow, zacc, row, zrow, zacc))

    def latent_chunk(c, st):
        return attend(pl.multiple_of(CTX_LEN + c * DIFF_KV_CHUNK, CTX_LEN), DIFF_KV_CHUNK, st)

    n_latent = jnp.where(j == 0, 0, SEQ // DIFF_KV_CHUNK)
    m1, l1, a1, m2, l2, a2 = lax.fori_loop(0, n_latent, latent_chunk, state)
    out = a1 / l1 - lam_ref[0] * (a2 / l2)
    ms = jnp.mean(out * out, axis=0, keepdims=True)
    y = out * lax.rsqrt(ms + SUBLN_EPS) * (g_ref[...] * post_scale)
    o_ref[...] = y.T.astype(o_ref.dtype)


def diff_attention(proj, lam, cos_t, sin_t, subln_g, lam_init):
    b = proj.shape[0]
    kernel = functools.partial(_diff_attn_kernel, post_scale=1.0 - lam_init)
    return pl.pallas_call(
        kernel,
        grid=(b, DIFF_HEADS, S_ALL // ATT_TILE),
        in_specs=[pl.BlockSpec(memory_space=pltpu.SMEM),
                  pl.BlockSpec((None, ATT_TILE, HEAD_W), lambda bi, h, j: (bi, j, h)),
                  pl.BlockSpec((None, S_ALL, HEAD_W), lambda bi, h, j: (bi, 0, DIFF_HEADS + h)),
                  pl.BlockSpec((None, S_ALL, HEAD_W), lambda bi, h, j: (bi, 0, 2 * DIFF_HEADS + h)),
                  pl.BlockSpec((S_ALL, HEAD_W), lambda bi, h, j: (0, 0)),
                  pl.BlockSpec((S_ALL, HEAD_W), lambda bi, h, j: (0, 0)),
                  pl.BlockSpec((HEAD_W, 1), lambda bi, h, j: (0, 0))],
        out_specs=pl.BlockSpec((None, ATT_TILE, HEAD_W), lambda bi, h, j: (bi, j, h)),
        out_shape=jax.ShapeDtypeStruct((b, S_ALL, DIFF_HEADS * HEAD_W), BF16),
        scratch_shapes=[pltpu.VMEM((S_ALL, HEAD_W), BF16), pltpu.VMEM((HEAD_W, S_ALL), BF16)],
        compiler_params=_params(("parallel", "parallel", "arbitrary")),
        name="diff_attention",
    )(lam, proj, proj, proj, cos_t, sin_t, subln_g.reshape(HEAD_W, 1))


def rope_tables():
    n_freq = DIFF_QK_DIM // 4
    inv_freq = ROPE_THETA ** (-jnp.arange(n_freq, dtype=F32) / n_freq)
    t = jnp.arange(SEQ, dtype=jnp.int32)
    row = (t // GRID_W).astype(F32)
    col = (t % GRID_W).astype(F32)
    ang = jnp.concatenate([row[:, None] * inv_freq, col[:, None] * inv_freq], axis=-1)
    cos, sin = jnp.cos(ang), jnp.sin(ang)
    cos_l = jnp.concatenate([cos, cos, cos, cos], axis=-1)
    sin_l = jnp.concatenate([-sin, sin, -sin, sin], axis=-1)
    cos_all = jnp.concatenate([jnp.ones((CTX_LEN, HEAD_W), F32), cos_l], axis=0)
    sin_all = jnp.concatenate([jnp.zeros((CTX_LEN, HEAD_W), F32), sin_l], axis=0)
    return cos_all, sin_all


def _na_window_start(j):
    g = j - 1
    return jnp.clip(NA_ROWS_PER_STEP * g - NA_KH // 2, 0, SEQ // GRID_W - NA_WIN_ROWS)


def _na_kernel(q_ref, k_ref, v_ref, bias_ref, o_ref):
    j = pl.program_id(2)
    win = NA_WIN_ROWS * GRID_W
    start = pl.multiple_of(CTX_LEN + _na_window_start(j) * GRID_W, GRID_W)
    scale = NA_DIM ** -0.5
    q = q_ref[...]
    s_c = _dot_nt(q, k_ref[pl.ds(0, CTX_LEN), :]) * scale
    s_w = _dot_nt(q, k_ref[pl.ds(start, win), :]) * scale + bias_ref[...]
    m = jnp.maximum(jnp.max(s_c, axis=-1, keepdims=True), jnp.max(s_w, axis=-1, keepdims=True))
    p_c = jnp.exp(s_c - m)
    p_w = jnp.exp(s_w - m)
    l = jnp.sum(p_c, axis=-1, keepdims=True) + jnp.sum(p_w, axis=-1, keepdims=True)
    o = _dot(p_c.astype(BF16), v_ref[pl.ds(0, CTX_LEN), :]) + _dot(p_w.astype(BF16), v_ref[pl.ds(start, win), :])
    o_ref[...] = (o / l).astype(o_ref.dtype)


def _na_bias_pattern(j):
    n_groups = SEQ // (GRID_W * NA_ROWS_PER_STEP)
    g = j - 1
    return jnp.where(j == 0, 3, jnp.where(g == 0, 0, jnp.where(g == n_groups - 1, 2, 1)))


def na_bias_table(rpb):
    rows = SEQ // GRID_W
    n_groups = rows // NA_ROWS_PER_STEP
    cols = np.arange(GRID_W)
    col_start = np.clip(cols - NA_KW // 2, 0, GRID_W - NA_KW)
    col_mask = (cols[None, :] >= col_start[:, None]) & (cols[None, :] < col_start[:, None] + NA_KW)
    c_idx = np.clip(cols[None, :] - cols[:, None] + NA_KW - 1, 0, 2 * NA_KW - 2)
    pats = []
    for g in (0, 1, n_groups - 1):
        u0 = int(np.clip(NA_ROWS_PER_STEP * g - NA_KH // 2, 0, rows - NA_WIN_ROWS))
        r = NA_ROWS_PER_STEP * g + np.arange(NA_ROWS_PER_STEP)
        r0 = np.clip(r - NA_KH // 2, 0, rows - NA_KH)
        kr = u0 + np.arange(NA_WIN_ROWS)
        valid_r = (kr[None, :] >= r0[:, None]) & (kr[None, :] < r0[:, None] + NA_KH)
        r_idx = np.clip(kr[None, :] - r[:, None] + NA_KH - 1, 0, 2 * NA_KH - 2)
        valid = valid_r[:, None, :, None] & col_mask[None, :, None, :]
        r_sel = np.eye(2 * NA_KH - 1, dtype=np.float32)[r_idx]
        c_sel = np.eye(2 * NA_KW - 1, dtype=np.float32)[c_idx]
        rows_sel = jnp.einsum('qkr,hrc->hqkc', r_sel, rpb.astype(F32), precision=lax.Precision.HIGHEST)
        gathered = jnp.einsum('hqkc,abc->hqakb', rows_sel, c_sel, precision=lax.Precision.HIGHEST)
        pats.append(jnp.where(valid[None], gathered, NEG_INF))
    pats.append(jnp.full_like(pats[0], NEG_INF))
    tab = jnp.stack(pats, axis=1)
    return tab.reshape(NA_HEADS, 4, NA_ROWS_PER_STEP * GRID_W, NA_WIN_ROWS * GRID_W)


def neighbourhood_attention(proj, bias_tab):
    b = proj.shape[0]
    tq = NA_ROWS_PER_STEP * GRID_W
    assert tq == CTX_LEN
    win = NA_WIN_ROWS * GRID_W
    base = 3 * DIFF_HEADS
    return pl.pallas_call(
        _na_kernel,
        grid=(b, NA_HEADS, S_ALL // tq),
        in_specs=[pl.BlockSpec((None, tq, HEAD_W), lambda bi, h, j: (bi, j, base + h)),
                  pl.BlockSpec((None, S_ALL, HEAD_W), lambda bi, h, j: (bi, 0, base + NA_HEADS + h)),
                  pl.BlockSpec((None, S_ALL, HEAD_W), lambda bi, h, j: (bi, 0, base + 2 * NA_HEADS + h)),
                  pl.BlockSpec((None, None, tq, win), lambda bi, h, j: (h, _na_bias_pattern(j), 0, 0))],
        out_specs=pl.BlockSpec((None, tq, HEAD_W), lambda bi, h, j: (bi, j, h)),
        out_shape=jax.ShapeDtypeStruct((b, S_ALL, NA_HEADS * HEAD_W), BF16),
        compiler_params=_params(("parallel", "parallel", "arbitrary")),
        name="neighbourhood_attention",
    )(proj, proj, proj, bias_tab)


def _each(fn, *lists):
    return [fn(*args) for args in zip(*lists)]


def _scan_prepare_kernel(r_ref, kk_ref, v_ref, ld_ref, a_ref, kd_ref, y0_ref, rm_ref, d0_ref):
    c = SCAN_CHUNK
    w = SCAN_PAIR_W
    hw = RWKV_HEAD
    t_idx = lax.broadcasted_iota(jnp.int32, (c, w), 0)
    lane = lax.broadcasted_iota(jnp.int32, (c, w), 1)
    s_idx = lane & (hw - 1)
    head0 = lane < hw
    eye = s_idx == t_idx
    tt = lax.broadcasted_iota(jnp.int32, (c, c), 0)
    ss = lax.broadcasted_iota(jnp.int32, (c, c), 1)
    strict = (s_idx < t_idx, s_idx > t_idx)
    incl = (s_idx <= t_idx, s_idx >= t_idx)
    tri = (jnp.where(ss <= tt, 1.0, 0.0).astype(BF16), jnp.where(ss >= tt, 1.0, 0.0).astype(BF16))
    row2 = lax.broadcasted_iota(jnp.int32, (w, w), 0)
    lane2 = lax.broadcasted_iota(jnp.int32, (w, w), 1)
    bdmask = (row2 < hw) == (lane2 < hw)
    eye2 = row2 == lane2

    def bd(y):
        return jnp.where(bdmask, jnp.concatenate([y, y], axis=0), 0.0).astype(BF16)

    def pm(x, ybd):
        return _dot(x.astype(BF16), ybd)

    inst = [(d, sub) for d in range(2) for sub in range(SCAN_SUB)]
    dirs = [d for d, _ in inst]
    rows = [slice(sub * c, (sub + 1) * c) for _, sub in inst]
    r = [r_ref[rw, :] for rw in rows]
    kk = [kk_ref[rw, :] for rw in rows]
    v = [v_ref[rw, :] for rw in rows]
    ld = [ld_ref[d, rw, :] for d, rw in zip(dirs, rows)]
    a = [a_ref[d, rw, :] for d, rw in zip(dirs, rows)]
    kd = [kd_ref[d, rw, :] for d, rw in zip(dirs, rows)]

    def cumulative(ld_i, d):
        p_hi = ld_i.astype(BF16)
        rem = ld_i - p_hi.astype(F32)
        p_mid = rem.astype(BF16)
        p_lo = (rem - p_mid.astype(F32)).astype(BF16)
        cs = _dot(tri[d], jnp.concatenate([p_hi, p_mid, p_lo], axis=1))
        return cs[:, :w] + cs[:, w:2 * w] + cs[:, 2 * w:]

    lam = _each(cumulative, ld, dirs)
    lam_end = _each(lambda l, d: l[0:1, :] if d else l[c - 1:c, :], lam, dirs)
    g_cum = _each(jnp.exp, lam)
    g_inv = _each(lambda l: jnp.exp(-l), lam)
    g_prev = _each(lambda l, x: jnp.exp(l - x), lam, ld)
    g_rel = _each(lambda le, l: jnp.exp(le - l), lam_end, lam)
    g_end = _each(jnp.exp, lam_end)

    qk = _each(jnp.multiply, kk, g_prev)
    rt = _each(jnp.multiply, r, g_cum)
    beta = _each(jnp.multiply, kk, a)
    bt = _each(jnp.multiply, beta, g_inv)
    kt = _each(jnp.multiply, kd, g_inv)
    bh = _each(jnp.multiply, beta, g_rel)
    kh = _each(jnp.multiply, kd, g_rel)

    def big_product(qk_i, rt_i, bt_i, kt_i):
        rhs = jnp.concatenate([jnp.where(head0, bt_i, 0.0), jnp.where(head0, 0.0, bt_i),
                               jnp.where(head0, kt_i, 0.0), jnp.where(head0, 0.0, kt_i)], axis=0).astype(BF16)
        return _dot_nt(jnp.concatenate([qk_i, rt_i], axis=0).astype(BF16), rhs)

    big = _each(big_product, qk, rt, bt, kt)
    n_pow = _each(lambda b, d: jnp.where(strict[d], -b[:c, :w], 0.0), big, dirs)
    a_k = _each(lambda b, d: jnp.where(strict[d], b[:c, w:], 0.0), big, dirs)
    g_b = _each(lambda b, d: jnp.where(incl[d], b[c:, :w], 0.0), big, dirs)
    g_k = _each(lambda b, d: jnp.where(incl[d], b[c:, w:], 0.0), big, dirs)

    t_inv = _each(lambda n: jnp.where(eye, 1.0, 0.0) + n, n_pow)
    for _ in range(int(math.log2(c)) - 1):
        n_pow = _each(lambda n: pm(n, bd(n)), n_pow)
        t_inv = _each(lambda t, n: t + pm(t, bd(n)), t_inv, n_pow)

    v_bd = _each(bd, v)
    x1 = _each(pm, a_k, v_bd)
    uw = _each(lambda t, x, q: _dot(t.astype(BF16), jnp.concatenate([bd(x), bd(q)], axis=1)), t_inv, x1, qk)
    u0 = _each(lambda x: -x[:, :w], uw)
    wm = _each(lambda x: x[:, w:], uw)
    y0 = _each(lambda gk, vb, gb, u: pm(gk, vb) + pm(gb, bd(u)), g_k, v_bd, g_b, u0)
    rm = _each(lambda rt_i, gb, wm_i: rt_i - pm(gb, bd(wm_i)), rt, g_b, wm)
    d0 = _each(lambda kh_i, bh_i, v_i, u: jnp.where(bdmask, _dot_tn(
        jnp.concatenate([kh_i, bh_i], axis=0).astype(BF16), jnp.concatenate([v_i, u], axis=0).astype(BF16)), 0.0),
        kh, bh, v, u0)
    mm = _each(lambda bh_i, wm_i, ge: jnp.where(eye2, ge, 0.0) - jnp.where(
        bdmask, _dot_tn(bh_i.astype(BF16), wm_i.astype(BF16)), 0.0), bh, wm, g_end)
    mm_hi = _each(lambda m: m.astype(BF16), mm)
    mm_lo = _each(lambda m, mh: (m - mh.astype(F32)).astype(BF16), mm, mm_hi)

    for i, (d, sub) in enumerate(inst):
        y0_ref[d, rows[i], :] = y0[i]
        rm_ref[d, sub, 0, 0:c, :] = rm[i].astype(BF16)
        rm_ref[d, sub, 0, c:c + w, :] = mm_hi[i]
        rm_ref[d, sub, 0, c + w:c + 2 * w, :] = mm_lo[i]
        d0_ref[d, sub, 0, :, :] = d0[i]


def scan_prepare(r, kk, v, ld, a, kd):
    rows, d = r.shape
    c, w = SCAN_CHUNK, SCAN_PAIR_W
    n_pairs = d // w
    n_chunks = rows // c
    blk = SCAN_SUB * c
    shared = pl.BlockSpec((blk, w), lambda i, p: (i, p))
    per_dir = pl.BlockSpec((2, blk, w), lambda i, p: (0, i, p))
    return pl.pallas_call(
        _scan_prepare_kernel,
        grid=(rows // blk, n_pairs),
        in_specs=[shared, shared, shared, per_dir, per_dir, per_dir],
        out_specs=[pl.BlockSpec((2, blk, w), lambda i, p: (0, i, p)),
                   pl.BlockSpec((2, SCAN_SUB, 1, c + 2 * w, w), lambda i, p: (0, i, p, 0, 0)),
                   pl.BlockSpec((2, SCAN_SUB, 1, w, w), lambda i, p: (0, i, p, 0, 0))],
        out_shape=[jax.ShapeDtypeStruct((2, rows, d), F32),
                   jax.ShapeDtypeStruct((2, n_chunks, n_pairs, c + 2 * w, w), BF16),
                   jax.ShapeDtypeStruct((2, n_chunks, n_pairs, w, w), F32)],
        compiler_params=_params(("parallel", "parallel")),
        name="scan_prepare",
    )(r, kk, v, ld, a, kd)


def _scan_apply_kernel(y0_ref, rm_ref, d0_ref, y_ref, z_ref, *, n_pairs):
    c, w = SCAN_CHUNK, SCAN_PAIR_W
    i = pl.program_id(2)

    @pl.when(i == 0)
    def _():
        z_ref[...] = jnp.zeros_like(z_ref)

    z = [z_ref[p] for p in range(n_pairs)]
    z_hi = _each(lambda x: x.astype(BF16), z)
    z_lo = _each(lambda x, xh: (x - xh.astype(F32)).astype(BF16), z, z_hi)
    res = [_dot(rm_ref[p], jnp.concatenate([z_hi[p], z_lo[p]], axis=1)) for p in range(n_pairs)]
    res = _each(lambda x: x[:, :w] + x[:, w:], res)
    y = [y0_ref[:, p * w:(p + 1) * w] + res[p][:c] for p in range(n_pairs)]
    z_new = [d0_ref[p] + res[p][c:c + w] + res[p][c + w:] for p in range(n_pairs)]
    for p in range(n_pairs):
        y_ref[:, p * w:(p + 1) * w] = y[p]
        z_ref[p] = z_new[p]


def _scan_chunk_index(i, d):
    n_ctx = CTX_LEN // SCAN_CHUNK
    n_all = S_ALL // SCAN_CHUNK
    return jnp.where(d == 0, i, jnp.where(i < n_ctx, n_ctx - 1 - i, n_all + n_ctx - 1 - i))


def scan_apply(y0, rm, d0, batch):
    _, rows, d = y0.shape
    c, w = SCAN_CHUNK, SCAN_PAIR_W
    n_pairs = d // w
    per_b = rows // batch // c

    def idx(b, i, dr):
        return b * per_b + _scan_chunk_index(i, dr)

    kernel = functools.partial(_scan_apply_kernel, n_pairs=n_pairs)
    return pl.pallas_call(
        kernel,
        grid=(2, batch, per_b),
        in_specs=[pl.BlockSpec((None, c, d), lambda dr, b, i: (dr, idx(b, i, dr), 0)),
                  pl.BlockSpec((None, None, n_pairs, c + 2 * w, w), lambda dr, b, i: (dr, idx(b, i, dr), 0, 0, 0)),
                  pl.BlockSpec((None, None, n_pairs, w, w), lambda dr, b, i: (dr, idx(b, i, dr), 0, 0, 0))],
        out_specs=pl.BlockSpec((None, c, d), lambda dr, b, i: (dr, idx(b, i, dr), 0)),
        out_shape=jax.ShapeDtypeStruct((2, rows, d), F32),
        scratch_shapes=[pltpu.VMEM((n_pairs, w, w), F32)],
        compiler_params=_params(("parallel", "parallel", "arbitrary")),
        name="scan_apply",
    )(y0, rm, d0)


def rwkv7_scan_bidir(r, kk, v, ld, a, kd, batch):
    y0, rm, d0 = scan_prepare(r, kk, v, ld, a, kd)
    y = scan_apply(y0, rm, d0, batch)
    return y[0] + y[1]


def moe_route(h, router_g, router_g_b, router_e, router_e_b):
    hp = lax.Precision.HIGHEST
    lg = jnp.dot(h, router_g, precision=hp) + router_g_b
    pg = jax.nn.softmax(lg, axis=-1)
    _, g_sel = lax.top_k(lg, 1)
    grp = (jnp.arange(N_GROUPS)[None, :] == g_sel).astype(F32)
    p_sel = jnp.sum(pg * grp, axis=-1, keepdims=True)
    le = (jnp.dot(h, router_e, precision=hp) + router_e_b).reshape(-1, N_GROUPS, EXPERTS_PER_GROUP)
    le_sel = jnp.sum(le * grp[:, :, None], axis=1)
    top_v, top_i = lax.top_k(le_sel, TOP_K)
    top_p = jax.nn.softmax(top_v, axis=-1) * p_sel
    return g_sel * EXPERTS_PER_GROUP + top_i, top_p


def moe_dispatch(eid, wts):
    t = eid.shape[0]
    n_pairs = TOP_K * t
    tm = MOE_TILE
    n_slots = n_pairs + N_EXPERTS * tm
    n_tiles = n_slots // tm
    e_flat = eid.reshape(-1)
    onehot = (e_flat[:, None] == jnp.arange(N_EXPERTS, dtype=jnp.int32)[None, :]).astype(jnp.int32)
    csum = jnp.cumsum(onehot, axis=0)
    counts = csum[-1]
    padded = ((counts + tm - 1) // tm) * tm
    pend = jnp.cumsum(padded)
    pstart = pend - padded
    pos = jnp.sum(onehot * (pstart[None, :] + csum - onehot), axis=1)
    pair = jnp.arange(n_pairs, dtype=jnp.int32)
    tok = jnp.zeros((n_slots,), jnp.int32).at[pos].set(pair // TOP_K)
    dst = jnp.zeros((n_slots,), jnp.int32).at[pos].set(pair)
    gate = jnp.zeros((n_slots,), F32).at[pos].set(wts.reshape(-1))
    tile_start = jnp.arange(n_tiles, dtype=jnp.int32) * tm
    tile_e = jnp.sum((tile_start[:, None] >= pend[None, :]).astype(jnp.int32), axis=1)
    tile_e = jnp.minimum(tile_e, N_EXPERTS - 1)
    tile_oh = (tile_e[:, None] == jnp.arange(N_EXPERTS, dtype=jnp.int32)[None, :]).astype(jnp.int32)
    n_valid = jnp.clip(jnp.sum(tile_oh * (pstart + counts)[None, :], axis=1) - tile_start, 0, tm)
    n_live = (pend[-1] // tm).reshape(1)
    return (tok.reshape(n_tiles, 1, tm), dst.reshape(n_tiles, 1, tm), gate.reshape(n_slots, 1),
            tile_e.astype(jnp.int32), n_valid.astype(jnp.int32), n_live.astype(jnp.int32))


def _moe_kernel(te_ref, nv_ref, nl_ref, tok_ref, tokn_ref, dst_ref, gate_ref, h_hbm, w1_ref, w3_ref, w2_ref,
                y_hbm, xbuf, obuf, w1b, w3b, w2b, sem_in, sem_out):
    i = pl.program_id(0)
    n_live = nl_ref[0]
    tm = MOE_TILE
    slot = i % 2

    def gather_copy(src_row, r, s):
        return pltpu.make_async_copy(h_hbm.at[pl.ds(src_row, 1), :], xbuf.at[s, pl.ds(r, 1), :], sem_in.at[s])

    def scatter_copy(r, dst_row):
        return pltpu.make_async_copy(obuf.at[pl.ds(r, 1), :], y_hbm.at[pl.ds(dst_row, 1), :], sem_out.at[0])

    def start_gather(idx_ref, s):
        def body(r, carry):
            gather_copy(idx_ref[0, r], r, s).start()
            return carry
        lax.fori_loop(0, tm, body, 0)

    def wait_scatter(n):
        def body(r, carry):
            scatter_copy(r, 0).wait()
            return carry
        lax.fori_loop(0, n, body, 0)

    @pl.when(i == 0)
    def _():
        start_gather(tok_ref, 0)

    @pl.when(i + 1 < n_live)
    def _():
        start_gather(tokn_ref, (i + 1) % 2)

    @pl.when(i < n_live)
    def _():
        def wait_body(r, carry):
            gather_copy(0, r, slot).wait()
            return carry
        lax.fori_loop(0, tm, wait_body, 0)

        prev_e = te_ref[jnp.maximum(i - 1, 0)]

        @pl.when((i == 0) | (te_ref[i] != prev_e))
        def _():
            w1b[...] = w1_ref[...].astype(BF16)
            w3b[...] = w3_ref[...].astype(BF16)
            w2b[...] = w2_ref[...].astype(BF16)

        x = xbuf[slot].astype(BF16)
        up = _dot(x, w1b[...])
        hid = (up * jax.nn.sigmoid(up)) * _dot(x, w3b[...]) * gate_ref[...]
        out = _dot(hid.astype(BF16), w2b[...])

        @pl.when(i > 0)
        def _():
            wait_scatter(nv_ref[jnp.maximum(i - 1, 0)])

        obuf[...] = out

        def scatter_body(r, carry):
            scatter_copy(r, dst_ref[0, r]).start()
            return carry
        lax.fori_loop(0, nv_ref[i], scatter_body, 0)

        @pl.when(i == n_live - 1)
        def _():
            wait_scatter(nv_ref[i])


def moe_experts(h, tok, dst, gate, tile_e, n_valid, n_live, w1, w3, w2, layer):
    t, d = h.shape
    f = w1.shape[-1]
    tm = MOE_TILE
    n_tiles = tok.shape[0]

    def w_index(i, te, nv, nl):
        return (layer, te[i] // EXPERTS_PER_GROUP, te[i] % EXPERTS_PER_GROUP, 0, 0)

    grid_spec = pltpu.PrefetchScalarGridSpec(
        num_scalar_prefetch=3,
        grid=(n_tiles,),
        in_specs=[pl.BlockSpec((None, 1, tm), lambda i, te, nv, nl: (i, 0, 0), memory_space=pltpu.SMEM),
                  pl.BlockSpec((None, 1, tm), lambda i, te, nv, nl: (jnp.minimum(i + 1, n_tiles - 1), 0, 0),
                               memory_space=pltpu.SMEM),
                  pl.BlockSpec((None, 1, tm), lambda i, te, nv, nl: (i, 0, 0), memory_space=pltpu.SMEM),
                  pl.BlockSpec((tm, 1), lambda i, te, nv, nl: (i, 0)),
                  pl.BlockSpec(memory_space=pl.ANY),
                  pl.BlockSpec((None, None, None, d, f), w_index),
                  pl.BlockSpec((None, None, None, d, f), w_index),
                  pl.BlockSpec((None, None, None, f, d), w_index)],
        out_specs=pl.BlockSpec(memory_space=pl.ANY),
        scratch_shapes=[pltpu.VMEM((2, tm, d), F32), pltpu.VMEM((tm, d), F32),
                        pltpu.VMEM((d, f), BF16), pltpu.VMEM((d, f), BF16), pltpu.VMEM((f, d), BF16),
                        pltpu.SemaphoreType.DMA((2,)), pltpu.SemaphoreType.DMA((1,))])
    return pl.pallas_call(
        _moe_kernel,
        grid_spec=grid_spec,
        out_shape=jax.ShapeDtypeStruct((TOP_K * t, d), F32),
        compiler_params=_params(("arbitrary",)),
        name="moe_experts",
    )(tile_e, n_valid, n_live, tok, tok, dst, gate, h, w1, w3, w2)


def _rmsnorm(x, g, eps=NORM_EPS):
    return x * lax.rsqrt(jnp.mean(x * x, axis=-1, keepdims=True) + eps) * g


def _seg(mod, k):
    is_latent = (jnp.arange(S_ALL) >= CTX_LEN)[None, :, None]
    return jnp.where(is_latent, mod[:, 1, k][:, None, :], mod[:, 0, k][:, None, :])


def _modulated(x, mod, k_shift, k_scale, g):
    return _rmsnorm(x, g) * (1.0 + _seg(mod, k_scale)) + _seg(mod, k_shift)


def _gated_add(x, mod, k_gate, upd):
    return x + _seg(mod, k_gate) * upd


def _token_shift(h):
    def one(x):
        prev = jnp.pad(x[:, :-1], ((0, 0), (1, 0), (0, 0)))
        nxt = jnp.pad(x[:, 1:], ((0, 0), (0, 1), (0, 0)))
        return 0.5 * (prev + nxt) - x
    return jnp.concatenate([one(h[:, :CTX_LEN]), one(h[:, CTX_LEN:])], axis=1)


def _pad_cols(w, n):
    return jnp.pad(w, ((0, 0), (0, n - w.shape[1])))


def _two_dir_lora(x, w1, w2):
    r = w1.shape[-1]
    w1c = _pad_cols(jnp.concatenate([w1[0], w1[1]], axis=1), 256).astype(BF16)
    d = w2.shape[-1]
    w2bd = jnp.zeros((256, 2 * d), F32).at[:r, :d].set(w2[0]).at[r:2 * r, d:].set(w2[1]).astype(BF16)
    return matmul(x, w1c, F32, tn=256), w2bd


def _heads(z):
    return z.reshape(*z.shape[:-1], RWKV_HEADS, RWKV_HEAD)


def rwkv_mixer(h, p, batch):
    rows = batch * S_ALL
    xx = _token_shift(h)
    mixes = [(h + xx * p['x_mix'][j]).reshape(rows, D_MODEL).astype(BF16) for j in range(6)]
    r = matmul(mixes[0], p['w_rkv'][0].astype(BF16), F32)
    k = matmul(mixes[1], p['w_rkv'][1].astype(BF16), F32)
    v = matmul(mixes[2], p['w_rkv'][2].astype(BF16), F32)
    hid_w, w2bd_w = _two_dir_lora(mixes[3], p['dec_w1'], p['dec_w2'])
    w_lora = matmul(jnp.tanh(hid_w).astype(BF16), w2bd_w, F32)
    hid_a, w2bd_a = _two_dir_lora(mixes[4], p['iclr_a1'], p['iclr_a2'])
    a_lora = matmul(hid_a.astype(BF16), w2bd_a, F32)
    g_hid = matmul(mixes[5], p['gate_g1'].astype(BF16), F32, tn=256)
    g = matmul(jax.nn.sigmoid(g_hid).astype(BF16), p['gate_g2'].astype(BF16), F32)

    w_lora = jnp.stack([w_lora[:, :D_MODEL], w_lora[:, D_MODEL:]])
    a_lora = jnp.stack([a_lora[:, :D_MODEL], a_lora[:, D_MODEL:]])
    log_w = -jax.nn.softplus(-(p['dec_w0'][:, None, :] + w_lora)) - 0.5
    ld = -jnp.exp(log_w)
    a = jax.nn.sigmoid(p['iclr_a0'][:, None, :] + a_lora)
    kk = _heads(k * p['k_k'])
    kk = (kk * lax.rsqrt(jnp.maximum(jnp.sum(kk * kk, axis=-1, keepdims=True), 1e-24))).reshape(rows, D_MODEL)
    kd = k[None] * (1.0 + (a - 1.0) * p['k_a'])

    y = rwkv7_scan_bidir(r, kk, v, ld, a, kd, batch)

    yh = _heads(y)
    mu = jnp.mean(yh, axis=-1, keepdims=True)
    var = jnp.mean(jnp.square(yh - mu), axis=-1, keepdims=True)
    yn = ((yh - mu) * lax.rsqrt(var + GN_EPS)).reshape(rows, D_MODEL) * p['gn_g'] + p['gn_b']
    bonus = jnp.sum(_heads(r)[None] * _heads(kd) * p['r_k'], axis=(0, -1))[..., None] * _heads(v)
    yo = (yn + bonus.reshape(rows, D_MODEL)) * g
    return matmul(yo.astype(BF16), p['w_out'].astype(BF16), F32)


def even_mixer(h, p, layer_idx, cos_t, sin_t, batch):
    rows = batch * S_ALL
    lam_init = 0.8 - 0.6 * math.exp(-0.3 * layer_idx)
    proj = matmul(h.reshape(rows, D_MODEL).astype(BF16), p['w_in'].astype(BF16), BF16, tn=1024)
    proj = proj.reshape(batch, S_ALL, -1)
    lf = p['diff_lambda']
    lam = (jnp.exp(jnp.sum(lf[0] * lf[1])) - jnp.exp(jnp.sum(lf[2] * lf[3])) + lam_init).reshape(1)
    a_out = diff_attention(proj, lam, cos_t, sin_t, p['subln_g'], lam_init)
    b_out = neighbourhood_attention(proj, na_bias_table(p['rpb']))
    mixed = jnp.concatenate([a_out, b_out], axis=-1).reshape(rows, -1)
    return matmul(mixed, p['w_out'].astype(BF16), F32)


def moe_block(h, p, layer):
    rows = h.shape[0] * S_ALL
    hf = h.reshape(rows, D_MODEL)
    eid, wts = moe_route(hf, p['router_g'], p['router_g_b'], p['router_e'], p['router_e_b'])
    tok, dst, gate, tile_e, n_valid, n_live = moe_dispatch(eid, wts)
    y2 = moe_experts(hf, tok, dst, gate, tile_e, n_valid, n_live, p['w1'], p['w3'], p['w2'], layer)
    return y2.reshape(rows, TOP_K, D_MODEL).sum(axis=1)


def kernel(x, c, ctx, c_ctx, ada_w, ada_b, norm_g, final_g, even_w_in, even_w_out, diff_lambda, diff_subln_g, na_rpb, rwkv_x_mix, rwkv_w_rkv, rwkv_w_out, rwkv_dec_w0, rwkv_dec_w1, rwkv_dec_w2, rwkv_iclr_a0, rwkv_iclr_a1, rwkv_iclr_a2, rwkv_gate_g1, rwkv_gate_g2, rwkv_k_k, rwkv_k_a, rwkv_r_k, rwkv_gn_g, rwkv_gn_b, moe_router_g, moe_router_g_b, moe_router_e, moe_router_e_b, moe_w1, moe_w3, moe_w2):
    batch = x.shape[0]
    xs = jnp.concatenate([ctx, x], axis=1)
    cos_t, sin_t = rope_tables()

    cvec = jnp.concatenate([c, c_ctx[None], jnp.zeros((8 - batch - 1, D_MODEL), F32)], axis=0)
    mods = ada_modulation(jax.nn.silu(cvec).astype(BF16), ada_w, ada_b)
    mods = mods.reshape(DEPTH, 8, N_MOD, D_MODEL)

    for i in range(DEPTH):
        j = i // 2
        mod_l = mods[i, :batch]
        mod_c = jnp.broadcast_to(mods[i, batch][None], mod_l.shape)
        mod = jnp.stack([mod_c, mod_l], axis=1)
        h = _modulated(xs, mod, 0, 1, norm_g[i, 0])
        if i % 2 == 0:
            p = dict(w_in=even_w_in[j], w_out=even_w_out[j], diff_lambda=diff_lambda[j],
                     subln_g=diff_subln_g[j], rpb=na_rpb[j])
            out = even_mixer(h, p, i, cos_t, sin_t, batch)
        else:
            p = dict(x_mix=rwkv_x_mix[j], w_rkv=rwkv_w_rkv[j], w_out=rwkv_w_out[j], dec_w0=rwkv_dec_w0[j],
                     dec_w1=rwkv_dec_w1[j], dec_w2=rwkv_dec_w2[j], iclr_a0=rwkv_iclr_a0[j],
                     iclr_a1=rwkv_iclr_a1[j], iclr_a2=rwkv_iclr_a2[j], gate_g1=rwkv_gate_g1[j],
                     gate_g2=rwkv_gate_g2[j], k_k=rwkv_k_k[j], k_a=rwkv_k_a[j], r_k=rwkv_r_k[j],
                     gn_g=rwkv_gn_g[j], gn_b=rwkv_gn_b[j])
            out = rwkv_mixer(h, p, batch)
        xs = _gated_add(xs, mod, 2, out.reshape(batch, S_ALL, D_MODEL))
        h2 = _modulated(xs, mod, 3, 4, norm_g[i, 1])
        pm = dict(router_g=moe_router_g[i], router_g_b=moe_router_g_b[i], router_e=moe_router_e[i],
                  router_e_b=moe_router_e_b[i], w1=moe_w1, w3=moe_w3, w2=moe_w2)
        xs = _gated_add(xs, mod, 5, moe_block(h2, pm, i).reshape(batch, S_ALL, D_MODEL))
    return _rmsnorm(xs[:, CTX_LEN:], final_g)
```

```python
import functools
import math

import numpy as np
import jax
import jax.numpy as jnp
from jax import lax
from jax.experimental import pallas as pl
from jax.experimental.pallas import tpu as pltpu

F32 = jnp.float32
BF16 = jnp.bfloat16

D_MODEL = 2048
DEPTH = 4
GRID_W = 64
CTX_LEN = 256
SEQ = 4096
S_ALL = CTX_LEN + SEQ
N_MOD = 6
NORM_EPS = 1e-6
NEG_INF = -1e30

DIFF_HEADS = 8
DIFF_QK_DIM = 64
NA_HEADS = 8
NA_DIM = 128
NA_KH = 8
NA_KW = 16
ROPE_THETA = 10000.0
SUBLN_EPS = 1e-5
HEAD_W = 128
ATT_TILE = 256
DIFF_KV_CHUNK = 4096
DIFF_KV_UNROLL = 1
NA_ROWS_PER_STEP = 4
NA_WIN_ROWS = 12

RWKV_HEAD = 64
RWKV_HEADS = D_MODEL // RWKV_HEAD
GN_EPS = 64e-5
SCAN_CHUNK = 64
SCAN_PAIR_W = 2 * RWKV_HEAD
SCAN_SUB = 4

N_GROUPS = 4
EXPERTS_PER_GROUP = 8
N_EXPERTS = N_GROUPS * EXPERTS_PER_GROUP
TOP_K = 2
EXPERT_FF = 512
MOE_TILE = 256
MOE_DMA_UNROLL = 8
ROW_TILE = 256

VMEM_LIMIT = 52 * 1024 * 1024


def _params(sem):
    return pltpu.CompilerParams(dimension_semantics=sem, vmem_limit_bytes=VMEM_LIMIT)


def _dot(a, b):
    return jnp.dot(a, b, preferred_element_type=F32)


def _dot_nt(a, b):
    return lax.dot_general(a, b, (((1,), (1,)), ((), ())), preferred_element_type=F32)


def _dot_tn(a, b):
    return lax.dot_general(a, b, (((0,), (0,)), ((), ())), preferred_element_type=F32)


def _mm_kernel(a_ref, w_ref, o_ref):
    o_ref[...] = _dot(a_ref[...], w_ref[...]).astype(o_ref.dtype)


def matmul(a, w, out_dtype, tm=512, tn=512):
    m, k = a.shape
    n = w.shape[1]
    tm = min(tm, m)
    tn = min(tn, n)
    assert m % tm == 0 and n % tn == 0
    return pl.pallas_call(
        _mm_kernel,
        grid=(m // tm, n // tn),
        in_specs=[pl.BlockSpec((tm, k), lambda i, j: (i, 0)),
                  pl.BlockSpec((k, tn), lambda i, j: (0, j))],
        out_specs=pl.BlockSpec((tm, tn), lambda i, j: (i, j)),
        out_shape=jax.ShapeDtypeStruct((m, n), out_dtype),
        compiler_params=_params(("parallel", "parallel")),
        name="matmul",
    )(a, w)


def _ada_kernel(s_ref, w_ref, b_ref, o_ref):
    o_ref[...] = _dot(s_ref[...], w_ref[...].astype(BF16)) + b_ref[...]


def ada_modulation(svec, ada_w, ada_b, tn=1024):
    nl, d, n = ada_w.shape
    rows = svec.shape[0]
    return pl.pallas_call(
        _ada_kernel,
        grid=(nl, n // tn),
        in_specs=[pl.BlockSpec((rows, d), lambda l, j: (0, 0)),
                  pl.BlockSpec((None, d, tn), lambda l, j: (l, 0, j)),
                  pl.BlockSpec((None, 1, tn), lambda l, j: (l, 0, j))],
        out_specs=pl.BlockSpec((None, rows, tn), lambda l, j: (l, 0, j)),
        out_shape=jax.ShapeDtypeStruct((nl, rows, n), F32),
        compiler_params=_params(("parallel", "parallel")),
        name="ada_modulation",
    )(svec, ada_w, ada_b.reshape(nl, 1, n))


def _rope(x, cos, sin_signed):
    lane = lax.broadcasted_iota(jnp.int32, x.shape, 1)
    first_half = (lane & 63) < 32
    partner = jnp.where(first_half, pltpu.roll(x, HEAD_W - 32, 1), pltpu.roll(x, 32, 1))
    return x * cos + partner * sin_signed


def _diff_attn_kernel(lam_ref, q_ref, k_ref, v_ref, cos_ref, sin_ref, g_ref, o_ref, kr_ref, vt_ref, *, post_scale):
    j = pl.program_id(2)
    tq = ATT_TILE

    @pl.when(j == 0)
    def _():
        def prep_chunk(c, carry):
            rows = pl.ds(pl.multiple_of(c * ATT_TILE, ATT_TILE), ATT_TILE)
            kr_ref[rows, :] = _rope(k_ref[rows, :].astype(F32), cos_ref[rows, :], sin_ref[rows, :]).astype(BF16)
            vt_ref[:, rows] = v_ref[rows, :].astype(F32).T.astype(BF16)
            return carry
        lax.fori_loop(0, S_ALL // ATT_TILE, prep_chunk, 0)

    qrows = pl.ds(pl.multiple_of(j * tq, tq), tq)
    q = _rope(q_ref[...].astype(F32), cos_ref[qrows, :], sin_ref[qrows, :]) * (DIFF_QK_DIM ** -0.5)
    lane = lax.broadcasted_iota(jnp.int32, q.shape, 1)
    q1 = jnp.where(lane < DIFF_QK_DIM, q, 0.0).astype(BF16)
    q2 = jnp.where(lane < DIFF_QK_DIM, 0.0, q).astype(BF16)

    def attend(start, size, state):
        kc = kr_ref[pl.ds(start, size), :]
        vt = vt_ref[:, pl.ds(start, size)]

        def update(qm, m, l, acc):
            s = _dot_nt(kc, qm)
            m_new = jnp.maximum(m, jnp.max(s, axis=0, keepdims=True))
            alpha = jnp.exp(m - m_new)
            p = jnp.exp(s - m_new)
            l_new = alpha * l + jnp.sum(p, axis=0, keepdims=True)
            acc_new = alpha * acc + _dot(vt, p.astype(BF16))
            return m_new, l_new, acc_new

        m1, l1, a1, m2, l2, a2 = state
        return update(q1, m1, l1, a1) + update(q2, m2, l2, a2)

    row = jnp.full((1, tq), NEG_INF, F32)
    zrow = jnp.zeros((1, tq), F32)
    zacc = jnp.zeros((HEAD_W, tq), F32)
    state = attend(0, CTX_LEN, (row, zrow, zacc, row, zrow, zacc))

    def latent_group(c, st):
        for u in range(DIFF_KV_UNROLL):
            start = CTX_LEN + (c * DIFF_KV_UNROLL + u) * DIFF_KV_CHUNK
            st = attend(pl.multiple_of(start, CTX_LEN), DIFF_KV_CHUNK, st)
        return st

    n_latent = jnp.where(j == 0, 0, SEQ // (DIFF_KV_CHUNK * DIFF_KV_UNROLL))
    m1, l1, a1, m2, l2, a2 = lax.fori_loop(0, n_latent, latent_group, state)
    out = a1 / l1 - lam_ref[0] * (a2 / l2)
    ms = jnp.mean(out * out, axis=0, keepdims=True)
    y = out * lax.rsqrt(ms + SUBLN_EPS) * (g_ref[...] * post_scale)
    o_ref[...] = y.T.astype(o_ref.dtype)


def diff_attention(proj, lam, cos_t, sin_t, subln_g, lam_init):
    b = proj.shape[0]
    kernel = functools.partial(_diff_attn_kernel, post_scale=1.0 - lam_init)
    return pl.pallas_call(
        kernel,
        grid=(b, DIFF_HEADS, S_ALL // ATT_TILE),
        in_specs=[pl.BlockSpec(memory_space=pltpu.SMEM),
                  pl.BlockSpec((None, ATT_TILE, HEAD_W), lambda bi, h, j: (bi, j, h)),
                  pl.BlockSpec((None, S_ALL, HEAD_W), lambda bi, h, j: (bi, 0, DIFF_HEADS + h)),
                  pl.BlockSpec((None, S_ALL, HEAD_W), lambda bi, h, j: (bi, 0, 2 * DIFF_HEADS + h)),
                  pl.BlockSpec((S_ALL, HEAD_W), lambda bi, h, j: (0, 0)),
                  pl.BlockSpec((S_ALL, HEAD_W), lambda bi, h, j: (0, 0)),
                  pl.BlockSpec((HEAD_W, 1), lambda bi, h, j: (0, 0))],
        out_specs=pl.BlockSpec((None, ATT_TILE, HEAD_W), lambda bi, h, j: (bi, j, h)),
        out_shape=jax.ShapeDtypeStruct((b, S_ALL, DIFF_HEADS * HEAD_W), BF16),
        scratch_shapes=[pltpu.VMEM((S_ALL, HEAD_W), BF16), pltpu.VMEM((HEAD_W, S_ALL), BF16)],
        compiler_params=_params(("parallel", "parallel", "arbitrary")),
        name="diff_attention",
    )(lam, proj, proj, proj, cos_t, sin_t, subln_g.reshape(HEAD_W, 1))


def rope_tables():
    n_freq = DIFF_QK_DIM // 4
    inv_freq = ROPE_THETA ** (-jnp.arange(n_freq, dtype=F32) / n_freq)
    t = jnp.arange(SEQ, dtype=jnp.int32)
    row = (t // GRID_W).astype(F32)
    col = (t % GRID_W).astype(F32)
    ang = jnp.concatenate([row[:, None] * inv_freq, col[:, None] * inv_freq], axis=-1)
    cos, sin = jnp.cos(ang), jnp.sin(ang)
    cos_l = jnp.concatenate([cos, cos, cos, cos], axis=-1)
    sin_l = jnp.concatenate([-sin, sin, -sin, sin], axis=-1)
    cos_all = jnp.concatenate([jnp.ones((CTX_LEN, HEAD_W), F32), cos_l], axis=0)
    sin_all = jnp.concatenate([jnp.zeros((CTX_LEN, HEAD_W), F32), sin_l], axis=0)
    return cos_all, sin_all


def _na_window_start(j):
    g = j - 1
    return jnp.clip(NA_ROWS_PER_STEP * g - NA_KH // 2, 0, SEQ // GRID_W - NA_WIN_ROWS)


def _na_kernel(q_ref, k_ref, v_ref, bias_ref, o_ref):
    j = pl.program_id(2)
    win = NA_WIN_ROWS * GRID_W
    start = pl.multiple_of(CTX_LEN + _na_window_start(j) * GRID_W, GRID_W)
    scale = NA_DIM ** -0.5
    q = q_ref[...]
    s_c = _dot_nt(q, k_ref[pl.ds(0, CTX_LEN), :]) * scale
    s_w = _dot_nt(q, k_ref[pl.ds(start, win), :]) * scale + bias_ref[...]
    m = jnp.maximum(jnp.max(s_c, axis=-1, keepdims=True), jnp.max(s_w, axis=-1, keepdims=True))
    p_c = jnp.exp(s_c - m)
    p_w = jnp.exp(s_w - m)
    l = jnp.sum(p_c, axis=-1, keepdims=True) + jnp.sum(p_w, axis=-1, keepdims=True)
    o = _dot(p_c.astype(BF16), v_ref[pl.ds(0, CTX_LEN), :]) + _dot(p_w.astype(BF16), v_ref[pl.ds(start, win), :])
    o_ref[...] = (o / l).astype(o_ref.dtype)


def _na_bias_pattern(j):
    n_groups = SEQ // (GRID_W * NA_ROWS_PER_STEP)
    g = j - 1
    return jnp.where(j == 0, 3, jnp.where(g == 0, 0, jnp.where(g == n_groups - 1, 2, 1)))


def na_bias_table(rpb):
    rows = SEQ // GRID_W
    n_groups = rows // NA_ROWS_PER_STEP
    cols = np.arange(GRID_W)
    col_start = np.clip(cols - NA_KW // 2, 0, GRID_W - NA_KW)
    col_mask = (cols[None, :] >= col_start[:, None]) & (cols[None, :] < col_start[:, None] + NA_KW)
    c_idx = np.clip(cols[None, :] - cols[:, None] + NA_KW - 1, 0, 2 * NA_KW - 2)
    pats = []
    for g in (0, 1, n_groups - 1):
        u0 = int(np.clip(NA_ROWS_PER_STEP * g - NA_KH // 2, 0, rows - NA_WIN_ROWS))
        r = NA_ROWS_PER_STEP * g + np.arange(NA_ROWS_PER_STEP)
        r0 = np.clip(r - NA_KH // 2, 0, rows - NA_KH)
        kr = u0 + np.arange(NA_WIN_ROWS)
        valid_r = (kr[None, :] >= r0[:, None]) & (kr[None, :] < r0[:, None] + NA_KH)
        r_idx = np.clip(kr[None, :] - r[:, None] + NA_KH - 1, 0, 2 * NA_KH - 2)
        valid = valid_r[:, None, :, None] & col_mask[None, :, None, :]
        r_sel = np.eye(2 * NA_KH - 1, dtype=np.float32)[r_idx]
        c_sel = np.eye(2 * NA_KW - 1, dtype=np.float32)[c_idx]
        rows_sel = jnp.einsum('qkr,hrc->hqkc', r_sel, rpb.astype(F32), precision=lax.Precision.HIGHEST)
        gathered = jnp.einsum('hqkc,abc->hqakb', rows_sel, c_sel, precision=lax.Precision.HIGHEST)
        pats.append(jnp.where(valid[None], gathered, NEG_INF))
    pats.append(jnp.full_like(pats[0], NEG_INF))
    tab = jnp.stack(pats, axis=1)
    return tab.reshape(NA_HEADS, 4, NA_ROWS_PER_STEP * GRID_W, NA_WIN_ROWS * GRID_W)


def neighbourhood_attention(proj, bias_tab):
    b = proj.shape[0]
    tq = NA_ROWS_PER_STEP * GRID_W
    assert tq == CTX_LEN
    win = NA_WIN_ROWS * GRID_W
    base = 3 * DIFF_HEADS
    return pl.pallas_call(
        _na_kernel,
        grid=(b, NA_HEADS, S_ALL // tq),
        in_specs=[pl.BlockSpec((None, tq, HEAD_W), lambda bi, h, j: (bi, j, base + h)),
                  pl.BlockSpec((None, S_ALL, HEAD_W), lambda bi, h, j: (bi, 0, base + NA_HEADS + h)),
                  pl.BlockSpec((None, S_ALL, HEAD_W), lambda bi, h, j: (bi, 0, base + 2 * NA_HEADS + h)),
                  pl.BlockSpec((None, None, tq, win), lambda bi, h, j: (h, _na_bias_pattern(j), 0, 0))],
        out_specs=pl.BlockSpec((None, tq, HEAD_W), lambda bi, h, j: (bi, j, h)),
        out_shape=jax.ShapeDtypeStruct((b, S_ALL, NA_HEADS * HEAD_W), BF16),
        compiler_params=_params(("parallel", "parallel", "arbitrary")),
        name="neighbourhood_attention",
    )(proj, proj, proj, bias_tab)


def _each(fn, *lists):
    return [fn(*args) for args in zip(*lists)]


def _head_sum(x):
    rows, w = x.shape
    r_i = lax.broadcasted_iota(jnp.int32, (w, w), 0)
    c_i = lax.broadcasted_iota(jnp.int32, (w, w), 1)
    ones_bd = jnp.where((r_i < RWKV_HEAD) == (c_i < RWKV_HEAD), 1.0, 0.0).astype(BF16)
    hi = x.astype(BF16)
    lo = (x - hi.astype(F32)).astype(BF16)
    s = _dot(jnp.concatenate([hi, lo], axis=0), ones_bd)
    return s[:rows] + s[rows:]


def _scan_prepare_kernel(r_ref, k_ref, v_ref, wl0_ref, wl1_ref, al0_ref, al1_ref, par_ref,
                         y0_ref, rm_ref, d0_ref, bonus_ref):
    c = SCAN_CHUNK
    w = SCAN_PAIR_W
    hw = RWKV_HEAD
    t_idx = lax.broadcasted_iota(jnp.int32, (c, w), 0)
    lane = lax.broadcasted_iota(jnp.int32, (c, w), 1)
    s_idx = lane & (hw - 1)
    head0 = lane < hw
    eye = s_idx == t_idx
    tt = lax.broadcasted_iota(jnp.int32, (c, c), 0)
    ss = lax.broadcasted_iota(jnp.int32, (c, c), 1)
    strict = (s_idx < t_idx, s_idx > t_idx)
    incl = (s_idx <= t_idx, s_idx >= t_idx)
    tri = (jnp.where(ss <= tt, 1.0, 0.0).astype(BF16), jnp.where(ss >= tt, 1.0, 0.0).astype(BF16))
    row2 = lax.broadcasted_iota(jnp.int32, (w, w), 0)
    lane2 = lax.broadcasted_iota(jnp.int32, (w, w), 1)
    bdmask = (row2 < hw) == (lane2 < hw)
    eye2 = row2 == lane2

    def bd(y):
        return jnp.where(bdmask, jnp.concatenate([y, y], axis=0), 0.0).astype(BF16)

    def pm(x, ybd):
        return _dot(x.astype(BF16), ybd)

    inst = [(d, sub) for d in range(2) for sub in range(SCAN_SUB)]
    dirs = [d for d, _ in inst]
    rows = [slice(sub * c, (sub + 1) * c) for _, sub in inst]
    par = par_ref[...]
    k_k, k_a, r_k = par[4:5, :], par[5:6, :], par[6:7, :]
    sub_rows = [slice(sub * c, (sub + 1) * c) for sub in range(SCAN_SUB)]
    r_s = [r_ref[rw, :] for rw in sub_rows]
    k_s = [k_ref[rw, :] for rw in sub_rows]
    v_s = [v_ref[rw, :] for rw in sub_rows]
    wl_s = [[wl0_ref[rw, :] for rw in sub_rows], [wl1_ref[rw, :] for rw in sub_rows]]
    al_s = [[al0_ref[rw, :] for rw in sub_rows], [al1_ref[rw, :] for rw in sub_rows]]

    def unit_key(k_i):
        kk_i = k_i * k_k
        return kk_i * lax.rsqrt(jnp.maximum(_head_sum(kk_i * kk_i), 1e-24))

    def log_decay(wl_i, d):
        z = -(par[d:d + 1, :] + wl_i)
        softplus = jnp.maximum(z, 0.0) + jnp.log(1.0 + jnp.exp(-jnp.abs(z)))
        return -jnp.exp(-softplus - 0.5)

    kk_s = _each(unit_key, k_s)
    a_s = [[jax.nn.sigmoid(par[2 + d:3 + d, :] + x) for x in al_s[d]] for d in range(2)]
    kd_s = [[k_i * (1.0 + (a_i - 1.0) * k_a) for k_i, a_i in zip(k_s, a_s[d])] for d in range(2)]
    bonus = [_head_sum(r_i * r_k * (kd0 + kd1)) * v_i for r_i, kd0, kd1, v_i in zip(r_s, kd_s[0], kd_s[1], v_s)]

    r = [r_s[sub] for _, sub in inst]
    kk = [kk_s[sub] for _, sub in inst]
    v = [v_s[sub] for _, sub in inst]
    ld = [log_decay(wl_s[d][sub], d) for d, sub in inst]
    a = [a_s[d][sub] for d, sub in inst]
    kd = [kd_s[d][sub] for d, sub in inst]

    def cumulative(ld_i, d):
        p_hi = ld_i.astype(BF16)
        rem = ld_i - p_hi.astype(F32)
        p_mid = rem.astype(BF16)
        p_lo = (rem - p_mid.astype(F32)).astype(BF16)
        cs = _dot(tri[d], jnp.concatenate([p_hi, p_mid, p_lo], axis=1))
        return cs[:, :w] + cs[:, w:2 * w] + cs[:, 2 * w:]

    lam = _each(cumulative, ld, dirs)
    lam_end = _each(lambda l, d: l[0:1, :] if d else l[c - 1:c, :], lam, dirs)
    g_cum = _each(jnp.exp, lam)
    g_inv = _each(lambda l: jnp.exp(-l), lam)
    g_prev = _each(lambda l, x: jnp.exp(l - x), lam, ld)
    g_rel = _each(lambda le, l: jnp.exp(le - l), lam_end, lam)
    g_end = _each(jnp.exp, lam_end)

    qk = _each(jnp.multiply, kk, g_prev)
    rt = _each(jnp.multiply, r, g_cum)
    beta = _each(jnp.multiply, kk, a)
    bt = _each(jnp.multiply, beta, g_inv)
    kt = _each(jnp.multiply, kd, g_inv)
    bh = _each(jnp.multiply, beta, g_rel)
    kh = _each(jnp.multiply, kd, g_rel)

    def big_product(qk_i, rt_i, bt_i, kt_i):
        rhs = jnp.concatenate([jnp.where(head0, bt_i, 0.0), jnp.where(head0, 0.0, bt_i),
                               jnp.where(head0, kt_i, 0.0), jnp.where(head0, 0.0, kt_i)], axis=0).astype(BF16)
        return _dot_nt(jnp.concatenate([qk_i, rt_i], axis=0).astype(BF16), rhs)

    big = _each(big_product, qk, rt, bt, kt)
    n_pow = _each(lambda b, d: jnp.where(strict[d], -b[:c, :w], 0.0), big, dirs)
    a_k = _each(lambda b, d: jnp.where(strict[d], b[:c, w:], 0.0), big, dirs)
    g_b = _each(lambda b, d: jnp.where(incl[d], b[c:, :w], 0.0), big, dirs)
    g_k = _each(lambda b, d: jnp.where(incl[d], b[c:, w:], 0.0), big, dirs)

    t_inv = _each(lambda n: jnp.where(eye, 1.0, 0.0) + n, n_pow)
    for _ in range(int(math.log2(c)) - 1):
        n_pow = _each(lambda n: pm(n, bd(n)), n_pow)
        t_inv = _each(lambda t, n: t + pm(t, bd(n)), t_inv, n_pow)

    v_bd = _each(bd, v)
    x1 = _each(pm, a_k, v_bd)
    uw = _each(lambda t, x, q: _dot(t.astype(BF16), jnp.concatenate([bd(x), bd(q)], axis=1)), t_inv, x1, qk)
    u0 = _each(lambda x: -x[:, :w], uw)
    wm = _each(lambda x: x[:, w:], uw)
    y0 = _each(lambda gk, vb, gb, u: pm(gk, vb) + pm(gb, bd(u)), g_k, v_bd, g_b, u0)
    rm = _each(lambda rt_i, gb, wm_i: rt_i - pm(gb, bd(wm_i)), rt, g_b, wm)
    d0 = _each(lambda kh_i, bh_i, v_i, u: jnp.where(bdmask, _dot_tn(
        jnp.concatenate([kh_i, bh_i], axis=0).astype(BF16), jnp.concatenate([v_i, u], axis=0).astype(BF16)), 0.0),
        kh, bh, v, u0)
    mm = _each(lambda bh_i, wm_i, ge: jnp.where(eye2, ge, 0.0) - jnp.where(
        bdmask, _dot_tn(bh_i.astype(BF16), wm_i.astype(BF16)), 0.0), bh, wm, g_end)
    mm_hi = _each(lambda m: m.astype(BF16), mm)
    mm_lo = _each(lambda m, mh: (m - mh.astype(F32)).astype(BF16), mm, mm_hi)

    for sub in range(SCAN_SUB):
        bonus_ref[sub_rows[sub], :] = bonus[sub]
    for i, (d, sub) in enumerate(inst):
        y0_ref[d, rows[i], :] = y0[i]
        rm_ref[d, sub, 0, 0:c, :] = rm[i].astype(BF16)
        rm_ref[d, sub, 0, c:c + w, :] = mm_hi[i]
        rm_ref[d, sub, 0, c + w:c + 2 * w, :] = mm_lo[i]
        d0_ref[d, sub, 0, :, :] = d0[i]


def scan_prepare(r, k, v, w_lora, a_lora, par):
    rows, d = r.shape
    c, w = SCAN_CHUNK, SCAN_PAIR_W
    n_pairs = d // w
    n_chunks = rows // c
    blk = SCAN_SUB * c
    shared = pl.BlockSpec((blk, w), lambda i, p: (i, p))
    rev_half = pl.BlockSpec((blk, w), lambda i, p: (i, n_pairs + p))
    return pl.pallas_call(
        _scan_prepare_kernel,
        grid=(rows // blk, n_pairs),
        in_specs=[shared, shared, shared, shared, rev_half, shared, rev_half,
                  pl.BlockSpec((8, w), lambda i, p: (0, p))],
        out_specs=[pl.BlockSpec((2, blk, w), lambda i, p: (0, i, p)),
                   pl.BlockSpec((2, SCAN_SUB, 1, c + 2 * w, w), lambda i, p: (0, i, p, 0, 0)),
                   pl.BlockSpec((2, SCAN_SUB, 1, w, w), lambda i, p: (0, i, p, 0, 0)),
                   shared],
        out_shape=[jax.ShapeDtypeStruct((2, rows, d), F32),
                   jax.ShapeDtypeStruct((2, n_chunks, n_pairs, c + 2 * w, w), BF16),
                   jax.ShapeDtypeStruct((2, n_chunks, n_pairs, w, w), F32),
                   jax.ShapeDtypeStruct((rows, d), F32)],
        compiler_params=_params(("parallel", "parallel")),
        name="scan_prepare",
    )(r, k, v, w_lora, w_lora, a_lora, a_lora, par)


def _scan_apply_kernel(y0_ref, rm_ref, d0_ref, y_ref, z_ref, *, n_pairs):
    c, w = SCAN_CHUNK, SCAN_PAIR_W
    i = pl.program_id(2)

    @pl.when(i == 0)
    def _():
        z_ref[...] = jnp.zeros_like(z_ref)

    z = [z_ref[p] for p in range(n_pairs)]
    z_hi = _each(lambda x: x.astype(BF16), z)
    z_lo = _each(lambda x, xh: (x - xh.astype(F32)).astype(BF16), z, z_hi)
    res = [_dot(rm_ref[p], jnp.concatenate([z_hi[p], z_lo[p]], axis=1)) for p in range(n_pairs)]
    res = _each(lambda x: x[:, :w] + x[:, w:], res)
    y = [y0_ref[:, p * w:(p + 1) * w] + res[p][:c] for p in range(n_pairs)]
    z_new = [d0_ref[p] + res[p][c:c + w] + res[p][c + w:] for p in range(n_pairs)]
    for p in range(n_pairs):
        y_ref[:, p * w:(p + 1) * w] = y[p]
        z_ref[p] = z_new[p]


def _scan_chunk_index(i, d):
    n_ctx = CTX_LEN // SCAN_CHUNK
    n_all = S_ALL // SCAN_CHUNK
    return jnp.where(d == 0, i, jnp.where(i < n_ctx, n_ctx - 1 - i, n_all + n_ctx - 1 - i))


def scan_apply(y0, rm, d0, batch):
    _, rows, d = y0.shape
    c, w = SCAN_CHUNK, SCAN_PAIR_W
    n_pairs = d // w
    per_b = rows // batch // c

    def idx(b, i, dr):
        return b * per_b + _scan_chunk_index(i, dr)

    kernel = functools.partial(_scan_apply_kernel, n_pairs=n_pairs)
    return pl.pallas_call(
        kernel,
        grid=(2, batch, per_b),
        in_specs=[pl.BlockSpec((None, c, d), lambda dr, b, i: (dr, idx(b, i, dr), 0)),
                  pl.BlockSpec((None, None, n_pairs, c + 2 * w, w), lambda dr, b, i: (dr, idx(b, i, dr), 0, 0, 0)),
                  pl.BlockSpec((None, None, n_pairs, w, w), lambda dr, b, i: (dr, idx(b, i, dr), 0, 0, 0))],
        out_specs=pl.BlockSpec((None, c, d), lambda dr, b, i: (dr, idx(b, i, dr), 0)),
        out_shape=jax.ShapeDtypeStruct((2, rows, d), F32),
        scratch_shapes=[pltpu.VMEM((n_pairs, w, w), F32)],
        compiler_params=_params(("parallel", "parallel", "arbitrary")),
        name="scan_apply",
    )(y0, rm, d0)


def _shift_mix_kernel(h_ref, hp_ref, hn_ref, mix_ref, *out_refs):
    per_b = S_ALL // ROW_TILE
    pos = pl.program_id(0) % per_b
    h = h_ref[...]
    row = lax.broadcasted_iota(jnp.int32, h.shape, 0)
    starts_seq = (pos == 0) | (pos == 1)
    ends_seq = (pos == 0) | (pos == per_b - 1)
    prev_row = jnp.where(starts_seq, 0.0, hp_ref[7:8, :])
    next_row = jnp.where(ends_seq, 0.0, hn_ref[0:1, :])
    prev = jnp.where(row == 0, prev_row, pltpu.roll(h, 1, 0))
    nxt = jnp.where(row == ROW_TILE - 1, next_row, pltpu.roll(h, ROW_TILE - 1, 0))
    xx = 0.5 * (prev + nxt) - h
    for j, o_ref in enumerate(out_refs):
        o_ref[...] = (h + xx * mix_ref[j:j + 1, :]).astype(o_ref.dtype)


def shift_mix(h, x_mix):
    rows, d = h.shape
    n_mix = x_mix.shape[0]
    sub = ROW_TILE // 8
    tile = pl.BlockSpec((ROW_TILE, d), lambda i: (i, 0))
    return pl.pallas_call(
        _shift_mix_kernel,
        grid=(rows // ROW_TILE,),
        in_specs=[tile,
                  pl.BlockSpec((8, d), lambda i: (jnp.maximum(i * sub - 1, 0), 0)),
                  pl.BlockSpec((8, d), lambda i: (jnp.minimum((i + 1) * sub, rows // 8 - 1), 0)),
                  pl.BlockSpec((n_mix, d), lambda i: (0, 0))],
        out_specs=[tile] * n_mix,
        out_shape=[jax.ShapeDtypeStruct((rows, d), BF16)] * n_mix,
        compiler_params=_params(("parallel",)),
        name="shift_mix",
    )(h, h, h, x_mix)


def _rwkv_out_kernel(y_ref, bonus_ref, g_ref, gn_ref, o_ref):
    w = SCAN_PAIR_W
    for s in range(o_ref.shape[-1] // w):
        cols = slice(s * w, (s + 1) * w)
        y = y_ref[0, :, cols] + y_ref[1, :, cols]
        mu = _head_sum(y) * (1.0 / RWKV_HEAD)
        dev = y - mu
        var = _head_sum(dev * dev) * (1.0 / RWKV_HEAD)
        yn = dev * lax.rsqrt(var + GN_EPS) * gn_ref[0:1, cols] + gn_ref[1:2, cols]
        o_ref[:, cols] = ((yn + bonus_ref[:, cols]) * g_ref[:, cols]).astype(o_ref.dtype)


def rwkv_out(y, bonus, g, gn_g, gn_b, tn=512):
    _, rows, d = y.shape
    tile = pl.BlockSpec((ROW_TILE, tn), lambda i, j: (i, j))
    return pl.pallas_call(
        _rwkv_out_kernel,
        grid=(rows // ROW_TILE, d // tn),
        in_specs=[pl.BlockSpec((2, ROW_TILE, tn), lambda i, j: (0, i, j)), tile, tile,
                  pl.BlockSpec((2, tn), lambda i, j: (0, j))],
        out_specs=tile,
        out_shape=jax.ShapeDtypeStruct((rows, d), BF16),
        compiler_params=_params(("parallel", "parallel")),
        name="rwkv_out",
    )(y, bonus, g, jnp.stack([gn_g, gn_b]))


def _route_topk(logits):
    lane = lax.broadcasted_iota(jnp.int32, logits.shape, 1)
    far = 4 * HEAD_W

    def first_max(vals):
        top = jnp.max(vals, axis=-1, keepdims=True)
        return top, jnp.min(jnp.where(vals == top, lane, far), axis=-1, keepdims=True)

    is_group = lane < N_GROUPS
    g_top, g_sel = first_max(jnp.where(is_group, logits, NEG_INF))
    p_sel = 1.0 / jnp.sum(jnp.where(is_group, jnp.exp(logits - g_top), 0.0), axis=-1, keepdims=True)
    lo = N_GROUPS + g_sel * EXPERTS_PER_GROUP
    le = jnp.where((lane >= lo) & (lane < lo + EXPERTS_PER_GROUP), logits, NEG_INF)
    v1, i1 = first_max(le)
    v2, i2 = first_max(jnp.where(lane == i1, NEG_INF, le))
    e2 = jnp.exp(v2 - v1)
    w1 = p_sel / (1.0 + e2)
    w2 = p_sel * e2 / (1.0 + e2)
    out = jnp.where(lane == 0, (i1 - N_GROUPS).astype(F32), 0.0)
    out = jnp.where(lane == 1, (i2 - N_GROUPS).astype(F32), out)
    out = jnp.where(lane == 2, w1, out)
    return jnp.where(lane == 3, w2, out)


def moe_dispatch(route):
    t = route.shape[0]
    n_pairs = TOP_K * t
    tm = MOE_TILE
    n_slots = n_pairs + N_EXPERTS * tm
    n_tiles = n_slots // tm
    e_flat = route[:, :TOP_K].astype(jnp.int32).reshape(-1)
    gate_bits = lax.bitcast_convert_type(route[:, TOP_K:2 * TOP_K], jnp.int32).reshape(-1)
    onehot = (e_flat[:, None] == jnp.arange(N_EXPERTS, dtype=jnp.int32)[None, :]).astype(jnp.int32)
    csum = jnp.cumsum(onehot, axis=0)
    counts = csum[-1]
    padded = ((counts + tm - 1) // tm) * tm
    pend = jnp.cumsum(padded)
    pstart = pend - padded
    pos = jnp.sum(onehot * (pstart[None, :] + csum - onehot), axis=1)
    pair = jnp.arange(n_pairs, dtype=jnp.int32)
    slots = jnp.full((n_slots, 2), -1, jnp.int32).at[pos].set(jnp.stack([pair, gate_bits], axis=1))
    valid = slots[:, 0] >= 0
    tok = jnp.where(valid, slots[:, 0] // TOP_K, 0)
    dst = jnp.where(valid, slots[:, 0], n_pairs + jnp.arange(n_slots, dtype=jnp.int32) % tm)
    gate = jnp.where(valid, lax.bitcast_convert_type(slots[:, 1], F32), 0.0)
    tile_start = jnp.arange(n_tiles, dtype=jnp.int32) * tm
    tile_e = jnp.sum((tile_start[:, None] >= pend[None, :]).astype(jnp.int32), axis=1)
    tile_e = jnp.minimum(tile_e, N_EXPERTS - 1)
    n_live = (pend[-1] // tm).reshape(1)
    return (tok.reshape(n_tiles, 1, tm), dst.reshape(n_tiles, 1, tm), gate.reshape(n_slots, 1),
            tile_e.astype(jnp.int32), n_live.astype(jnp.int32))


def _moe_kernel(te_ref, nl_ref, tok_ref, tokn_ref, dst_ref, gate_ref, h_hbm, w1_ref, w3_ref, w2_ref,
                y_hbm, xbuf, obuf, w1b, w3b, w2b, sem_in, sem_out):
    i = pl.program_id(0)
    n_live = nl_ref[0]
    tm = MOE_TILE
    slot = i % 2

    def gather_copy(src_row, r, s):
        return pltpu.make_async_copy(h_hbm.at[pl.ds(src_row, 1), :], xbuf.at[s, pl.ds(r, 1), :], sem_in.at[s])

    def scatter_copy(r, dst_row):
        return pltpu.make_async_copy(obuf.at[pl.ds(r, 1), :], y_hbm.at[pl.ds(dst_row, 1), :], sem_out.at[0])

    def per_row(fn):
        def body(r, carry):
            fn(r)
            return carry
        lax.fori_loop(0, tm, body, 0, unroll=MOE_DMA_UNROLL)

    @pl.when(i == 0)
    def _():
        per_row(lambda r: gather_copy(tok_ref[0, r], r, 0).start())

    @pl.when(i + 1 < n_live)
    def _():
        per_row(lambda r: gather_copy(tokn_ref[0, r], r, (i + 1) % 2).start())

    @pl.when(i < n_live)
    def _():
        per_row(lambda r: gather_copy(0, r, slot).wait())
        prev_e = te_ref[jnp.maximum(i - 1, 0)]

        @pl.when((i == 0) | (te_ref[i] != prev_e))
        def _():
            w1b[...] = w1_ref[...].astype(BF16)
            w3b[...] = w3_ref[...].astype(BF16)
            w2b[...] = w2_ref[...].astype(BF16)

        x = xbuf[slot].astype(BF16)
        up = _dot(x, w1b[...])
        hid = (up * jax.nn.sigmoid(up)) * _dot(x, w3b[...]) * gate_ref[...]
        out = _dot(hid.astype(BF16), w2b[...])

        @pl.when(i > 0)
        def _():
            per_row(lambda r: scatter_copy(r, 0).wait())

        obuf[...] = out

        @pl.when(i == 0)
        def _():
            spare = y_hbm.shape[0] - tm
            per_row(lambda r: scatter_copy(r, spare + r).start())
            per_row(lambda r: scatter_copy(r, 0).wait())

        per_row(lambda r: scatter_copy(r, dst_ref[0, r]).start())

        @pl.when(i == n_live - 1)
        def _():
            per_row(lambda r: scatter_copy(r, 0).wait())


def moe_experts(h, tok, dst, gate, tile_e, n_live, w1, w3, w2, layer):
    t, d = h.shape
    f = w1.shape[-1]
    tm = MOE_TILE
    n_tiles = tok.shape[0]

    def w_index(i, te, nl):
        return (layer, te[i] // EXPERTS_PER_GROUP, te[i] % EXPERTS_PER_GROUP, 0, 0)

    grid_spec = pltpu.PrefetchScalarGridSpec(
        num_scalar_prefetch=2,
        grid=(n_tiles,),
        in_specs=[pl.BlockSpec((None, 1, tm), lambda i, te, nl: (i, 0, 0), memory_space=pltpu.SMEM),
                  pl.BlockSpec((None, 1, tm), lambda i, te, nl: (jnp.minimum(i + 1, n_tiles - 1), 0, 0),
                               memory_space=pltpu.SMEM),
                  pl.BlockSpec((None, 1, tm), lambda i, te, nl: (i, 0, 0), memory_space=pltpu.SMEM),
                  pl.BlockSpec((tm, 1), lambda i, te, nl: (i, 0)),
                  pl.BlockSpec(memory_space=pl.ANY),
                  pl.BlockSpec((None, None, None, d, f), w_index),
                  pl.BlockSpec((None, None, None, d, f), w_index),
                  pl.BlockSpec((None, None, None, f, d), w_index)],
        out_specs=pl.BlockSpec(memory_space=pl.ANY),
        scratch_shapes=[pltpu.VMEM((2, tm, d), F32), pltpu.VMEM((tm, d), F32),
                        pltpu.VMEM((d, f), BF16), pltpu.VMEM((d, f), BF16), pltpu.VMEM((f, d), BF16),
                        pltpu.SemaphoreType.DMA((2,)), pltpu.SemaphoreType.DMA((1,))])
    return pl.pallas_call(
        _moe_kernel,
        grid_spec=grid_spec,
        out_shape=jax.ShapeDtypeStruct((TOP_K * t + tm, d), F32),
        compiler_params=_params(("arbitrary",)),
        name="moe_experts",
    )(tile_e, n_live, tok, tok, dst, gate, h, w1, w3, w2)


def _mod_spec(k):
    per_b = S_ALL // ROW_TILE
    return pl.BlockSpec((None, 1, D_MODEL),
                        lambda i: ((i // per_b * 2 + jnp.minimum(i % per_b, 1)) * N_MOD + k, 0, 0))


def _norm_mod(x, g_ref, sh_ref, sc_ref):
    h = x * lax.rsqrt(jnp.mean(x * x, axis=-1, keepdims=True) + NORM_EPS) * g_ref[...]
    if sh_ref is None:
        return h
    return h * (1.0 + sc_ref[...]) + sh_ref[...]


def _first_norm_kernel(x_ref, g_ref, sh_ref, sc_ref, h_ref):
    h_ref[...] = _norm_mod(x_ref[...], g_ref, sh_ref, sc_ref).astype(h_ref.dtype)


def first_norm(xs, g, mod, h_dtype):
    rows, d = xs.shape
    tile = pl.BlockSpec((ROW_TILE, d), lambda i: (i, 0))
    return pl.pallas_call(
        _first_norm_kernel,
        grid=(rows // ROW_TILE,),
        in_specs=[tile, pl.BlockSpec((1, d), lambda i: (0, 0)), _mod_spec(0), _mod_spec(1)],
        out_specs=tile,
        out_shape=jax.ShapeDtypeStruct((rows, d), h_dtype),
        compiler_params=_params(("parallel",)),
        name="first_norm",
    )(xs, g.reshape(1, d), mod, mod)


def _post_mixer_kernel(x_ref, u_ref, gate_ref, g_ref, sh_ref, sc_ref, wr_ref, br_ref, xo_ref, h_ref, route_ref):
    x = x_ref[...] + gate_ref[...] * u_ref[...]
    xo_ref[...] = x
    h = _norm_mod(x, g_ref, sh_ref, sc_ref)
    h_ref[...] = h
    logits = jnp.dot(h, wr_ref[...], precision=lax.Precision.HIGHEST, preferred_element_type=F32) + br_ref[...]
    route_ref[...] = _route_topk(logits)


def post_mixer(xs, upd, g, mod, router_w, router_b):
    rows, d = xs.shape
    tile = pl.BlockSpec((ROW_TILE, d), lambda i: (i, 0))
    rtile = pl.BlockSpec((ROW_TILE, HEAD_W), lambda i: (i, 0))
    return pl.pallas_call(
        _post_mixer_kernel,
        grid=(rows // ROW_TILE,),
        in_specs=[tile, tile, _mod_spec(2), pl.BlockSpec((1, d), lambda i: (0, 0)), _mod_spec(3), _mod_spec(4),
                  pl.BlockSpec((d, HEAD_W), lambda i: (0, 0)), pl.BlockSpec((1, HEAD_W), lambda i: (0, 0))],
        out_specs=[tile, tile, rtile],
        out_shape=[jax.ShapeDtypeStruct((rows, d), F32), jax.ShapeDtypeStruct((rows, d), F32),
                   jax.ShapeDtypeStruct((rows, HEAD_W), F32)],
        compiler_params=_params(("parallel",)),
        name="post_mixer",
    )(xs, upd, mod, g.reshape(1, d), mod, mod, router_w, router_b)


def _post_moe_kernel(x_ref, y2_ref, gate_ref, g_ref, *rest, modulate):
    if modulate:
        sh_ref, sc_ref, xo_ref, h_ref = rest
    else:
        sh_ref = sc_ref = None
        xo_ref, h_ref = rest
    d = x_ref.shape[-1]
    x = x_ref[...] + gate_ref[...] * (y2_ref[:, :d] + y2_ref[:, d:])
    xo_ref[...] = x
    h_ref[...] = _norm_mod(x, g_ref, sh_ref, sc_ref).astype(h_ref.dtype)


def post_moe(xs, y2, g, mod, next_mod, h_dtype):
    rows, d = xs.shape
    tile = pl.BlockSpec((ROW_TILE, d), lambda i: (i, 0))
    pair = pl.BlockSpec((ROW_TILE, TOP_K * d), lambda i: (i, 0))
    gspec = pl.BlockSpec((1, d), lambda i: (0, 0))
    modulate = next_mod is not None
    in_specs = [tile, pair, _mod_spec(5), gspec] + ([_mod_spec(0), _mod_spec(1)] if modulate else [])
    args = (xs, y2.reshape(-1, TOP_K * d), mod, g.reshape(1, d)) + ((next_mod, next_mod) if modulate else ())
    return pl.pallas_call(
        functools.partial(_post_moe_kernel, modulate=modulate),
        grid=(rows // ROW_TILE,),
        in_specs=in_specs,
        out_specs=[tile, tile],
        out_shape=[jax.ShapeDtypeStruct((rows, d), F32), jax.ShapeDtypeStruct((rows, d), h_dtype)],
        compiler_params=_params(("parallel",)),
        name="post_moe",
    )(*args)


def _pad_cols(w, n):
    return jnp.pad(w, ((0, 0), (0, n - w.shape[1])))


def _two_dir_lora(x, w1, w2):
    r = w1.shape[-1]
    w1c = _pad_cols(jnp.concatenate([w1[0], w1[1]], axis=1), 256).astype(BF16)
    d = w2.shape[-1]
    w2bd = jnp.zeros((256, 2 * d), F32).at[:r, :d].set(w2[0]).at[r:2 * r, d:].set(w2[1]).astype(BF16)
    return matmul(x, w1c, F32, tn=256), w2bd


def rwkv_mixer(h, p, batch):
    mixes = shift_mix(h, p['x_mix'])
    r = matmul(mixes[0], p['w_rkv'][0].astype(BF16), F32)
    k = matmul(mixes[1], p['w_rkv'][1].astype(BF16), F32)
    v = matmul(mixes[2], p['w_rkv'][2].astype(BF16), F32)
    hid_w, w2bd_w = _two_dir_lora(mixes[3], p['dec_w1'], p['dec_w2'])
    w_lora = matmul(jnp.tanh(hid_w).astype(BF16), w2bd_w, F32)
    hid_a, w2bd_a = _two_dir_lora(mixes[4], p['iclr_a1'], p['iclr_a2'])
    a_lora = matmul(hid_a.astype(BF16), w2bd_a, F32)
    g_hid = matmul(mixes[5], p['gate_g1'].astype(BF16), F32, tn=256)
    g = matmul(jax.nn.sigmoid(g_hid).astype(BF16), p['gate_g2'].astype(BF16), F32)

    par = jnp.concatenate([p['dec_w0'], p['iclr_a0'], p['k_k'][None], p['k_a'][None], p['r_k'].reshape(1, -1),
                           jnp.zeros((1, D_MODEL), F32)], axis=0)
    y0, rm, d0, bonus = scan_prepare(r, k, v, w_lora, a_lora, par)
    y = scan_apply(y0, rm, d0, batch)
    yo = rwkv_out(y, bonus, g, p['gn_g'], p['gn_b'])
    return matmul(yo, p['w_out'].astype(BF16), F32)


def even_mixer(h, p, layer_idx, cos_t, sin_t, batch):
    rows = batch * S_ALL
    lam_init = 0.8 - 0.6 * math.exp(-0.3 * layer_idx)
    proj = matmul(h, p['w_in'].astype(BF16), BF16, tn=1024)
    proj = proj.reshape(batch, S_ALL, -1)
    lf = p['diff_lambda']
    lam = (jnp.exp(jnp.sum(lf[0] * lf[1])) - jnp.exp(jnp.sum(lf[2] * lf[3])) + lam_init).reshape(1)
    a_out = diff_attention(proj, lam, cos_t, sin_t, p['subln_g'], lam_init)
    b_out = neighbourhood_attention(proj, na_bias_table(p['rpb']))
    mixed = jnp.concatenate([a_out, b_out], axis=-1).reshape(rows, -1)
    return matmul(mixed, p['w_out'].astype(BF16), F32)


def router_table(router_g, router_g_b, router_e, router_e_b):
    w = _pad_cols(jnp.concatenate([router_g, router_e], axis=1), HEAD_W)
    b = _pad_cols(jnp.concatenate([router_g_b, router_e_b])[None, :], HEAD_W)
    return w, b


def kernel(x, c, ctx, c_ctx, ada_w, ada_b, norm_g, final_g, even_w_in, even_w_out, diff_lambda, diff_subln_g, na_rpb, rwkv_x_mix, rwkv_w_rkv, rwkv_w_out, rwkv_dec_w0, rwkv_dec_w1, rwkv_dec_w2, rwkv_iclr_a0, rwkv_iclr_a1, rwkv_iclr_a2, rwkv_gate_g1, rwkv_gate_g2, rwkv_k_k, rwkv_k_a, rwkv_r_k, rwkv_gn_g, rwkv_gn_b, moe_router_g, moe_router_g_b, moe_router_e, moe_router_e_b, moe_w1, moe_w3, moe_w2):
    batch = x.shape[0]
    rows = batch * S_ALL
    xs = jnp.concatenate([ctx, x], axis=1).reshape(rows, D_MODEL)
    cos_t, sin_t = rope_tables()

    cvec = jnp.concatenate([c, c_ctx[None], jnp.zeros((8 - batch - 1, D_MODEL), F32)], axis=0)
    mods = ada_modulation(jax.nn.silu(cvec).astype(BF16), ada_w, ada_b)
    mods = mods.reshape(DEPTH, 8, N_MOD, D_MODEL)

    def mod_table(i):
        mod_l = mods[i, :batch]
        mod_c = jnp.broadcast_to(mods[i, batch][None], mod_l.shape)
        return jnp.stack([mod_c, mod_l], axis=1).reshape(batch * 2 * N_MOD, 1, D_MODEL)

    mod = mod_table(0)
    h = first_norm(xs, norm_g[0, 0], mod, BF16)
    for i in range(DEPTH):
        j = i // 2
        if i % 2 == 0:
            p = dict(w_in=even_w_in[j], w_out=even_w_out[j], diff_lambda=diff_lambda[j],
                     subln_g=diff_subln_g[j], rpb=na_rpb[j])
            out = even_mixer(h, p, i, cos_t, sin_t, batch)
        else:
            p = dict(x_mix=rwkv_x_mix[j], w_rkv=rwkv_w_rkv[j], w_out=rwkv_w_out[j], dec_w0=rwkv_dec_w0[j],
                     dec_w1=rwkv_dec_w1[j], dec_w2=rwkv_dec_w2[j], iclr_a0=rwkv_iclr_a0[j],
                     iclr_a1=rwkv_iclr_a1[j], iclr_a2=rwkv_iclr_a2[j], gate_g1=rwkv_gate_g1[j],
                     gate_g2=rwkv_gate_g2[j], k_k=rwkv_k_k[j], k_a=rwkv_k_a[j], r_k=rwkv_r_k[j],
                     gn_g=rwkv_gn_g[j], gn_b=rwkv_gn_b[j])
            out = rwkv_mixer(h, p, batch)
        rw, rb = router_table(moe_router_g[i], moe_router_g_b[i], moe_router_e[i], moe_router_e_b[i])
        xs, h2, route = post_mixer(xs, out, norm_g[i, 1], mod, rw, rb)
        tok, dst, gate, tile_e, n_live = moe_dispatch(route)
        y2 = moe_experts(h2, tok, dst, gate, tile_e, n_live, moe_w1, moe_w3, moe_w2, i)
        if i + 1 < DEPTH:
            next_mod = mod_table(i + 1)
            xs, h = post_moe(xs, y2, norm_g[i + 1, 0], mod, next_mod, F32 if (i + 1) % 2 else BF16)
            mod = next_mod
        else:
            _, h = post_moe(xs, y2, final_g, mod, None, F32)
    return h.reshape(batch, S_ALL, D_MODEL)[:, CTX_LEN:]
```

```python
import functools
import math

import numpy as np
import jax
import jax.numpy as jnp
from jax import lax
from jax.experimental import pallas as pl
from jax.experimental.pallas import tpu as pltpu

F32 = jnp.float32
BF16 = jnp.bfloat16

D_MODEL = 2048
DEPTH = 4
GRID_W = 64
CTX_LEN = 256
SEQ = 4096
S_ALL = CTX_LEN + SEQ
N_MOD = 6
NORM_EPS = 1e-6
NEG_INF = -1e30

DIFF_HEADS = 8
DIFF_QK_DIM = 64
NA_HEADS = 8
NA_DIM = 128
NA_KH = 8
NA_KW = 16
ROPE_THETA = 10000.0
SUBLN_EPS = 1e-5
HEAD_W = 128
ATT_TILE = 256
DIFF_KV_BLOCK = 1024
NA_ROWS_PER_STEP = 4
NA_WIN_ROWS = 12

RWKV_HEAD = 64
RWKV_HEADS = D_MODEL // RWKV_HEAD
GN_EPS = 64e-5
SCAN_CHUNK = 64
SCAN_PAIR_W = 2 * RWKV_HEAD
SCAN_SUB = 4

N_GROUPS = 4
EXPERTS_PER_GROUP = 8
N_EXPERTS = N_GROUPS * EXPERTS_PER_GROUP
TOP_K = 2
EXPERT_FF = 512
MOE_TILE = 256
MOE_DMA_UNROLL = 8
ROW_TILE = 256

VMEM_LIMIT = 52 * 1024 * 1024


def _params(sem):
    return pltpu.CompilerParams(dimension_semantics=sem, vmem_limit_bytes=VMEM_LIMIT)


def _dot(a, b):
    return jnp.dot(a, b, preferred_element_type=F32)


def _dot_nt(a, b):
    return lax.dot_general(a, b, (((1,), (1,)), ((), ())), preferred_element_type=F32)


def _dot_tn(a, b):
    return lax.dot_general(a, b, (((0,), (0,)), ((), ())), preferred_element_type=F32)


def _mm_kernel(a_ref, w_ref, o_ref):
    o_ref[...] = _dot(a_ref[...], w_ref[...]).astype(o_ref.dtype)


def matmul(a, w, out_dtype, tm=1024, tn=512):
    m, k = a.shape
    n = w.shape[1]
    while m % tm:
        tm //= 2
    tn = min(tn, n)
    assert n % tn == 0
    return pl.pallas_call(
        _mm_kernel,
        grid=(m // tm, n // tn),
        in_specs=[pl.BlockSpec((tm, k), lambda i, j: (i, 0)),
                  pl.BlockSpec((k, tn), lambda i, j: (0, j))],
        out_specs=pl.BlockSpec((tm, tn), lambda i, j: (i, j)),
        out_shape=jax.ShapeDtypeStruct((m, n), out_dtype),
        compiler_params=_params(("parallel", "parallel")),
        name="matmul",
    )(a, w)


def _ada_kernel(s_ref, w_ref, b_ref, o_ref):
    o_ref[...] = _dot(s_ref[...], w_ref[...].astype(BF16)) + b_ref[...]


def ada_modulation(svec, ada_w, ada_b, tn=1024):
    nl, d, n = ada_w.shape
    rows = svec.shape[0]
    return pl.pallas_call(
        _ada_kernel,
        grid=(nl, n // tn),
        in_specs=[pl.BlockSpec((rows, d), lambda l, j: (0, 0)),
                  pl.BlockSpec((None, d, tn), lambda l, j: (l, 0, j)),
                  pl.BlockSpec((None, 1, tn), lambda l, j: (l, 0, j))],
        out_specs=pl.BlockSpec((None, rows, tn), lambda l, j: (l, 0, j)),
        out_shape=jax.ShapeDtypeStruct((nl, rows, n), F32),
        compiler_params=_params(("parallel", "parallel")),
        name="ada_modulation",
    )(svec, ada_w, ada_b.reshape(nl, 1, n))


def _rope(x, cos, sin_signed):
    lane = lax.broadcasted_iota(jnp.int32, x.shape, 1)
    first_half = (lane & 63) < 32
    partner = jnp.where(first_half, pltpu.roll(x, HEAD_W - 32, 1), pltpu.roll(x, 32, 1))
    return x * cos + partner * sin_signed


def _diff_attn_kernel(lam_ref, q_ref, k_ref, v_ref, cos_ref, sin_ref, g_ref, o_ref, kr_ref, vt_ref, sa_ref, sb_ref, *,
                      post_scale):
    j = pl.program_id(2)
    tq = ATT_TILE

    @pl.when(j == 0)
    def _():
        def prep_chunk(c, carry):
            rows = pl.ds(pl.multiple_of(c * ATT_TILE, ATT_TILE), ATT_TILE)
            kr_ref[rows, :] = _rope(k_ref[rows, :].astype(F32), cos_ref[rows, :], sin_ref[rows, :]).astype(BF16)
            vt_ref[:, rows] = v_ref[rows, :].astype(F32).T.astype(BF16)
            return carry
        lax.fori_loop(0, S_ALL // ATT_TILE, prep_chunk, 0)

    qrows = pl.ds(pl.multiple_of(j * tq, tq), tq)
    q = _rope(q_ref[...].astype(F32), cos_ref[qrows, :], sin_ref[qrows, :]) * (DIFF_QK_DIM ** -0.5)
    lane = lax.broadcasted_iota(jnp.int32, q.shape, 1)
    q1 = jnp.where(lane < DIFF_QK_DIM, q, 0.0).astype(BF16)
    q2 = jnp.where(lane < DIFF_QK_DIM, 0.0, q).astype(BF16)

    kb = DIFF_KV_BLOCK
    n_latent = SEQ // kb
    maps = (q1, q2)

    def score_stage(start, size, sbuf):
        kc = kr_ref[pl.ds(start, size), :]
        tops = []
        for i, qm in enumerate(maps):
            s = _dot_nt(kc, qm)
            sbuf[i, 0:size, :] = s
            tops.append(jnp.max(s, axis=0, keepdims=True))
        return tuple(tops)

    def softmax_stage(start, size, sbuf, tops, state):
        vt = vt_ref[:, pl.ds(start, size)]
        new = []
        for i in range(2):
            m, l, acc = state[3 * i:3 * i + 3]
            m_new = jnp.maximum(m, tops[i])
            alpha = jnp.exp(m - m_new)
            p = jnp.exp(sbuf[i, 0:size, :] - m_new)
            new += [m_new, alpha * l + jnp.sum(p, axis=0, keepdims=True),
                    alpha * acc + _dot(vt, p.astype(BF16))]
        return tuple(new)

    def latent_start(n):
        return pl.multiple_of(CTX_LEN + jnp.minimum(n, n_latent - 1) * kb, CTX_LEN)

    row = jnp.full((1, tq), NEG_INF, F32)
    zrow = jnp.zeros((1, tq), F32)
    zacc = jnp.zeros((HEAD_W, tq), F32)
    state = softmax_stage(0, CTX_LEN, sa_ref, score_stage(0, CTX_LEN, sa_ref), (row, zrow, zacc, row, zrow, zacc))
    tops = score_stage(latent_start(0), kb, sb_ref)

    def block_pair(t, carry):
        st, tp = carry[:6], carry[6:]
        n = 2 * t
        tp_next = score_stage(latent_start(n + 1), kb, sa_ref)
        st = softmax_stage(latent_start(n), kb, sb_ref, tp, st)
        tp_last = score_stage(latent_start(n + 2), kb, sb_ref)
        st = softmax_stage(latent_start(n + 1), kb, sa_ref, tp_next, st)
        return st + tp_last

    n_pairs = jnp.where(j == 0, 0, n_latent // 2)
    m1, l1, a1, m2, l2, a2 = lax.fori_loop(0, n_pairs, block_pair, state + tops)[:6]
    out = a1 / l1 - lam_ref[0] * (a2 / l2)
    ms = jnp.mean(out * out, axis=0, keepdims=True)
    y = out * lax.rsqrt(ms + SUBLN_EPS) * (g_ref[...] * post_scale)
    o_ref[...] = y.T.astype(o_ref.dtype)


def diff_attention(proj, lam, cos_t, sin_t, subln_g, lam_init):
    b = proj.shape[0]
    kernel = functools.partial(_diff_attn_kernel, post_scale=1.0 - lam_init)
    return pl.pallas_call(
        kernel,
        grid=(b, DIFF_HEADS, S_ALL // ATT_TILE),
        in_specs=[pl.BlockSpec(memory_space=pltpu.SMEM),
                  pl.BlockSpec((None, ATT_TILE, HEAD_W), lambda bi, h, j: (bi, j, h)),
                  pl.BlockSpec((None, S_ALL, HEAD_W), lambda bi, h, j: (bi, 0, DIFF_HEADS + h)),
                  pl.BlockSpec((None, S_ALL, HEAD_W), lambda bi, h, j: (bi, 0, 2 * DIFF_HEADS + h)),
                  pl.BlockSpec((S_ALL, HEAD_W), lambda bi, h, j: (0, 0)),
                  pl.BlockSpec((S_ALL, HEAD_W), lambda bi, h, j: (0, 0)),
                  pl.BlockSpec((HEAD_W, 1), lambda bi, h, j: (0, 0))],
        out_specs=pl.BlockSpec((None, ATT_TILE, HEAD_W), lambda bi, h, j: (bi, j, h)),
        out_shape=jax.ShapeDtypeStruct((b, S_ALL, DIFF_HEADS * HEAD_W), BF16),
        scratch_shapes=[pltpu.VMEM((S_ALL, HEAD_W), BF16), pltpu.VMEM((HEAD_W, S_ALL), BF16),
                        pltpu.VMEM((2, DIFF_KV_BLOCK, ATT_TILE), F32), pltpu.VMEM((2, DIFF_KV_BLOCK, ATT_TILE), F32)],
        compiler_params=_params(("parallel", "parallel", "arbitrary")),
        name="diff_attention",
    )(lam, proj, proj, proj, cos_t, sin_t, subln_g.reshape(HEAD_W, 1))


def rope_tables():
    n_freq = DIFF_QK_DIM // 4
    inv_freq = ROPE_THETA ** (-jnp.arange(n_freq, dtype=F32) / n_freq)
    t = jnp.arange(SEQ, dtype=jnp.int32)
    row = (t // GRID_W).astype(F32)
    col = (t % GRID_W).astype(F32)
    ang = jnp.concatenate([row[:, None] * inv_freq, col[:, None] * inv_freq], axis=-1)
    cos, sin = jnp.cos(ang), jnp.sin(ang)
    cos_l = jnp.concatenate([cos, cos, cos, cos], axis=-1)
    sin_l = jnp.concatenate([-sin, sin, -sin, sin], axis=-1)
    cos_all = jnp.concatenate([jnp.ones((CTX_LEN, HEAD_W), F32), cos_l], axis=0)
    sin_all = jnp.concatenate([jnp.zeros((CTX_LEN, HEAD_W), F32), sin_l], axis=0)
    return cos_all, sin_all


def _na_window_start(j):
    g = j - 1
    return jnp.clip(NA_ROWS_PER_STEP * g - NA_KH // 2, 0, SEQ // GRID_W - NA_WIN_ROWS)


def _na_kernel(q_ref, k_ref, v_ref, bias_ref, o_ref):
    j = pl.program_id(2)
    win = NA_WIN_ROWS * GRID_W
    start = pl.multiple_of(CTX_LEN + _na_window_start(j) * GRID_W, GRID_W)
    scale = NA_DIM ** -0.5
    q = q_ref[...]
    s_c = _dot_nt(q, k_ref[pl.ds(0, CTX_LEN), :]) * scale
    s_w = _dot_nt(q, k_ref[pl.ds(start, win), :]) * scale + bias_ref[...]
    m = jnp.maximum(jnp.max(s_c, axis=-1, keepdims=True), jnp.max(s_w, axis=-1, keepdims=True))
    p_c = jnp.exp(s_c - m)
    p_w = jnp.exp(s_w - m)
    l = jnp.sum(p_c, axis=-1, keepdims=True) + jnp.sum(p_w, axis=-1, keepdims=True)
    o = _dot(p_c.astype(BF16), v_ref[pl.ds(0, CTX_LEN), :]) + _dot(p_w.astype(BF16), v_ref[pl.ds(start, win), :])
    o_ref[...] = (o / l).astype(o_ref.dtype)


def _na_bias_pattern(j):
    n_groups = SEQ // (GRID_W * NA_ROWS_PER_STEP)
    g = j - 1
    return jnp.where(j == 0, 3, jnp.where(g == 0, 0, jnp.where(g == n_groups - 1, 2, 1)))


def na_bias_table(rpb):
    rows = SEQ // GRID_W
    n_groups = rows // NA_ROWS_PER_STEP
    cols = np.arange(GRID_W)
    col_start = np.clip(cols - NA_KW // 2, 0, GRID_W - NA_KW)
    col_mask = (cols[None, :] >= col_start[:, None]) & (cols[None, :] < col_start[:, None] + NA_KW)
    c_idx = np.clip(cols[None, :] - cols[:, None] + NA_KW - 1, 0, 2 * NA_KW - 2)
    pats = []
    for g in (0, 1, n_groups - 1):
        u0 = int(np.clip(NA_ROWS_PER_STEP * g - NA_KH // 2, 0, rows - NA_WIN_ROWS))
        r = NA_ROWS_PER_STEP * g + np.arange(NA_ROWS_PER_STEP)
        r0 = np.clip(r - NA_KH // 2, 0, rows - NA_KH)
        kr = u0 + np.arange(NA_WIN_ROWS)
        valid_r = (kr[None, :] >= r0[:, None]) & (kr[None, :] < r0[:, None] + NA_KH)
        r_idx = np.clip(kr[None, :] - r[:, None] + NA_KH - 1, 0, 2 * NA_KH - 2)
        valid = valid_r[:, None, :, None] & col_mask[None, :, None, :]
        r_sel = np.eye(2 * NA_KH - 1, dtype=np.float32)[r_idx]
        c_sel = np.eye(2 * NA_KW - 1, dtype=np.float32)[c_idx]
        rows_sel = jnp.einsum('qkr,hrc->hqkc', r_sel, rpb.astype(F32), precision=lax.Precision.HIGHEST)
        gathered = jnp.einsum('hqkc,abc->hqakb', rows_sel, c_sel, precision=lax.Precision.HIGHEST)
        pats.append(jnp.where(valid[None], gathered, NEG_INF))
    pats.append(jnp.full_like(pats[0], NEG_INF))
    tab = jnp.stack(pats, axis=1)
    return tab.reshape(NA_HEADS, 4, NA_ROWS_PER_STEP * GRID_W, NA_WIN_ROWS * GRID_W)


def neighbourhood_attention(proj, bias_tab):
    b = proj.shape[0]
    tq = NA_ROWS_PER_STEP * GRID_W
    assert tq == CTX_LEN
    win = NA_WIN_ROWS * GRID_W
    base = 3 * DIFF_HEADS
    return pl.pallas_call(
        _na_kernel,
        grid=(b, NA_HEADS, S_ALL // tq),
        in_specs=[pl.BlockSpec((None, tq, HEAD_W), lambda bi, h, j: (bi, j, base + h)),
                  pl.BlockSpec((None, S_ALL, HEAD_W), lambda bi, h, j: (bi, 0, base + NA_HEADS + h)),
                  pl.BlockSpec((None, S_ALL, HEAD_W), lambda bi, h, j: (bi, 0, base + 2 * NA_HEADS + h)),
                  pl.BlockSpec((None, None, tq, win), lambda bi, h, j: (h, _na_bias_pattern(j), 0, 0))],
        out_specs=pl.BlockSpec((None, tq, HEAD_W), lambda bi, h, j: (bi, j, h)),
        out_shape=jax.ShapeDtypeStruct((b, S_ALL, NA_HEADS * HEAD_W), BF16),
        compiler_params=_params(("parallel", "parallel", "arbitrary")),
        name="neighbourhood_attention",
    )(proj, proj, proj, bias_tab)


def _each(fn, *lists):
    return [fn(*args) for args in zip(*lists)]


def _head_sum(x):
    rows, w = x.shape
    r_i = lax.broadcasted_iota(jnp.int32, (w, w), 0)
    c_i = lax.broadcasted_iota(jnp.int32, (w, w), 1)
    ones_bd = jnp.where((r_i < RWKV_HEAD) == (c_i < RWKV_HEAD), 1.0, 0.0).astype(BF16)
    hi = x.astype(BF16)
    lo = (x - hi.astype(F32)).astype(BF16)
    s = _dot(jnp.concatenate([hi, lo], axis=0), ones_bd)
    return s[:rows] + s[rows:]


def _scan_prepare_kernel(r_ref, k_ref, v_ref, wl0_ref, wl1_ref, al0_ref, al1_ref, par_ref,
                         y0_ref, rm_ref, d0_ref, bonus_ref):
    c = SCAN_CHUNK
    w = SCAN_PAIR_W
    hw = RWKV_HEAD
    t_idx = lax.broadcasted_iota(jnp.int32, (c, w), 0)
    lane = lax.broadcasted_iota(jnp.int32, (c, w), 1)
    s_idx = lane & (hw - 1)
    head0 = lane < hw
    eye = s_idx == t_idx
    tt = lax.broadcasted_iota(jnp.int32, (c, c), 0)
    ss = lax.broadcasted_iota(jnp.int32, (c, c), 1)
    strict = (s_idx < t_idx, s_idx > t_idx)
    incl = (s_idx <= t_idx, s_idx >= t_idx)
    tri = (jnp.where(ss <= tt, 1.0, 0.0).astype(BF16), jnp.where(ss >= tt, 1.0, 0.0).astype(BF16))
    row2 = lax.broadcasted_iota(jnp.int32, (w, w), 0)
    lane2 = lax.broadcasted_iota(jnp.int32, (w, w), 1)
    bdmask = (row2 < hw) == (lane2 < hw)
    eye2 = row2 == lane2

    def bd(y):
        return jnp.where(bdmask, jnp.concatenate([y, y], axis=0), 0.0).astype(BF16)

    def pm(x, ybd):
        return _dot(x.astype(BF16), ybd)

    inst = [(d, sub) for d in range(2) for sub in range(SCAN_SUB)]
    dirs = [d for d, _ in inst]
    rows = [slice(sub * c, (sub + 1) * c) for _, sub in inst]
    par = par_ref[...]
    k_k, k_a, r_k = par[4:5, :], par[5:6, :], par[6:7, :]
    sub_rows = [slice(sub * c, (sub + 1) * c) for sub in range(SCAN_SUB)]
    r_s = [r_ref[rw, :] for rw in sub_rows]
    k_s = [k_ref[rw, :] for rw in sub_rows]
    v_s = [v_ref[rw, :] for rw in sub_rows]
    wl_s = [[wl0_ref[rw, :] for rw in sub_rows], [wl1_ref[rw, :] for rw in sub_rows]]
    al_s = [[al0_ref[rw, :] for rw in sub_rows], [al1_ref[rw, :] for rw in sub_rows]]

    def unit_key(k_i):
        kk_i = k_i * k_k
        return kk_i * lax.rsqrt(jnp.maximum(_head_sum(kk_i * kk_i), 1e-24))

    def log_decay(wl_i, d):
        z = -(par[d:d + 1, :] + wl_i)
        softplus = jnp.maximum(z, 0.0) + jnp.log(1.0 + jnp.exp(-jnp.abs(z)))
        return -jnp.exp(-softplus - 0.5)

    kk_s = _each(unit_key, k_s)
    a_s = [[jax.nn.sigmoid(par[2 + d:3 + d, :] + x) for x in al_s[d]] for d in range(2)]
    kd_s = [[k_i * (1.0 + (a_i - 1.0) * k_a) for k_i, a_i in zip(k_s, a_s[d])] for d in range(2)]
    bonus = [_head_sum(r_i * r_k * (kd0 + kd1)) * v_i for r_i, kd0, kd1, v_i in zip(r_s, kd_s[0], kd_s[1], v_s)]

    r = [r_s[sub] for _, sub in inst]
    kk = [kk_s[sub] for _, sub in inst]
    v = [v_s[sub] for _, sub in inst]
    ld = [log_decay(wl_s[d][sub], d) for d, sub in inst]
    a = [a_s[d][sub] for d, sub in inst]
    kd = [kd_s[d][sub] for d, sub in inst]

    def cumulative(ld_i, d):
        p_hi = ld_i.astype(BF16)
        rem = ld_i - p_hi.astype(F32)
        p_mid = rem.astype(BF16)
        p_lo = (rem - p_mid.astype(F32)).astype(BF16)
        cs = _dot(tri[d], jnp.concatenate([p_hi, p_mid, p_lo], axis=1))
        return cs[:, :w] + cs[:, w:2 * w] + cs[:, 2 * w:]

    lam = _each(cumulative, ld, dirs)
    lam_end = _each(lambda l, d: l[0:1, :] if d else l[c - 1:c, :], lam, dirs)
    g_cum = _each(jnp.exp, lam)
    g_inv = _each(lambda l: jnp.exp(-l), lam)
    g_prev = _each(lambda l, x: jnp.exp(l - x), lam, ld)
    g_rel = _each(lambda le, l: jnp.exp(le - l), lam_end, lam)
    g_end = _each(jnp.exp, lam_end)

    qk = _each(jnp.multiply, kk, g_prev)
    rt = _each(jnp.multiply, r, g_cum)
    beta = _each(jnp.multiply, kk, a)
    bt = _each(jnp.multiply, beta, g_inv)
    kt = _each(jnp.multiply, kd, g_inv)
    bh = _each(jnp.multiply, beta, g_rel)
    kh = _each(jnp.multiply, kd, g_rel)

    def big_product(qk_i, rt_i, bt_i, kt_i):
        rhs = jnp.concatenate([jnp.where(head0, bt_i, 0.0), jnp.where(head0, 0.0, bt_i),
                               jnp.where(head0, kt_i, 0.0), jnp.where(head0, 0.0, kt_i)], axis=0).astype(BF16)
        return _dot_nt(jnp.concatenate([qk_i, rt_i], axis=0).astype(BF16), rhs)

    big = _each(big_product, qk, rt, bt, kt)
    n_pow = _each(lambda b, d: jnp.where(strict[d], -b[:c, :w], 0.0), big, dirs)
    a_k = _each(lambda b, d: jnp.where(strict[d], b[:c, w:], 0.0), big, dirs)
    g_b = _each(lambda b, d: jnp.where(incl[d], b[c:, :w], 0.0), big, dirs)
    g_k = _each(lambda b, d: jnp.where(incl[d], b[c:, w:], 0.0), big, dirs)

    t_inv = _each(lambda n: jnp.where(eye, 1.0, 0.0) + n, n_pow)
    for _ in range(int(math.log2(c)) - 1):
        n_pow = _each(lambda n: pm(n, bd(n)), n_pow)
        t_inv = _each(lambda t, n: t + pm(t, bd(n)), t_inv, n_pow)

    v_bd = _each(bd, v)
    x1 = _each(pm, a_k, v_bd)
    uw = _each(lambda t, x, q: _dot(t.astype(BF16), jnp.concatenate([bd(x), bd(q)], axis=1)), t_inv, x1, qk)
    u0 = _each(lambda x: -x[:, :w], uw)
    wm = _each(lambda x: x[:, w:], uw)
    y0 = _each(lambda gk, vb, gb, u: pm(gk, vb) + pm(gb, bd(u)), g_k, v_bd, g_b, u0)
    rm = _each(lambda rt_i, gb, wm_i: rt_i - pm(gb, bd(wm_i)), rt, g_b, wm)
    d0 = _each(lambda kh_i, bh_i, v_i, u: jnp.where(bdmask, _dot_tn(
        jnp.concatenate([kh_i, bh_i], axis=0).astype(BF16), jnp.concatenate([v_i, u], axis=0).astype(BF16)), 0.0),
        kh, bh, v, u0)
    mm = _each(lambda bh_i, wm_i, ge: jnp.where(eye2, ge, 0.0) - jnp.where(
        bdmask, _dot_tn(bh_i.astype(BF16), wm_i.astype(BF16)), 0.0), bh, wm, g_end)
    mm_hi = _each(lambda m: m.astype(BF16), mm)
    mm_lo = _each(lambda m, mh: (m - mh.astype(F32)).astype(BF16), mm, mm_hi)

    for sub in range(SCAN_SUB):
        bonus_ref[sub_rows[sub], :] = bonus[sub]
    for i, (d, sub) in enumerate(inst):
        y0_ref[d, rows[i], :] = y0[i]
        rm_ref[d, sub, 0, 0:c, :] = rm[i].astype(BF16)
        rm_ref[d, sub, 0, c:c + w, :] = mm_hi[i]
        rm_ref[d, sub, 0, c + w:c + 2 * w, :] = mm_lo[i]
        d0_ref[d, sub, 0, :, :] = d0[i]


def scan_prepare(r, k, v, w_lora, a_lora, par):
    rows, d = r.shape
    c, w = SCAN_CHUNK, SCAN_PAIR_W
    n_pairs = d // w
    n_chunks = rows // c
    blk = SCAN_SUB * c
    shared = pl.BlockSpec((blk, w), lambda i, p: (i, p))
    rev_half = pl.BlockSpec((blk, w), lambda i, p: (i, n_pairs + p))
    return pl.pallas_call(
        _scan_prepare_kernel,
        grid=(rows // blk, n_pairs),
        in_specs=[shared, shared, shared, shared, rev_half, shared, rev_half,
                  pl.BlockSpec((8, w), lambda i, p: (0, p))],
        out_specs=[pl.BlockSpec((2, blk, w), lambda i, p: (0, i, p)),
                   pl.BlockSpec((2, SCAN_SUB, 1, c + 2 * w, w), lambda i, p: (0, i, p, 0, 0)),
                   pl.BlockSpec((2, SCAN_SUB, 1, w, w), lambda i, p: (0, i, p, 0, 0)),
                   shared],
        out_shape=[jax.ShapeDtypeStruct((2, rows, d), F32),
                   jax.ShapeDtypeStruct((2, n_chunks, n_pairs, c + 2 * w, w), BF16),
                   jax.ShapeDtypeStruct((2, n_chunks, n_pairs, w, w), F32),
                   jax.ShapeDtypeStruct((rows, d), F32)],
        compiler_params=_params(("parallel", "parallel")),
        name="scan_prepare",
    )(r, k, v, w_lora, w_lora, a_lora, a_lora, par)


def _scan_apply_kernel(y0_ref, rm_ref, d0_ref, y_ref, z_ref, *, n_pairs):
    c, w = SCAN_CHUNK, SCAN_PAIR_W
    i = pl.program_id(2)

    @pl.when(i == 0)
    def _():
        z_ref[...] = jnp.zeros_like(z_ref)

    z = [z_ref[p] for p in range(n_pairs)]
    z_hi = _each(lambda x: x.astype(BF16), z)
    z_lo = _each(lambda x, xh: (x - xh.astype(F32)).astype(BF16), z, z_hi)
    res = [_dot(rm_ref[p], jnp.concatenate([z_hi[p], z_lo[p]], axis=1)) for p in range(n_pairs)]
    res = _each(lambda x: x[:, :w] + x[:, w:], res)
    y = [y0_ref[:, p * w:(p + 1) * w] + res[p][:c] for p in range(n_pairs)]
    z_new = [d0_ref[p] + res[p][c:c + w] + res[p][c + w:] for p in range(n_pairs)]
    for p in range(n_pairs):
        y_ref[:, p * w:(p + 1) * w] = y[p]
        z_ref[p] = z_new[p]


def _scan_chunk_index(i, d):
    n_ctx = CTX_LEN // SCAN_CHUNK
    n_all = S_ALL // SCAN_CHUNK
    return jnp.where(d == 0, i, jnp.where(i < n_ctx, n_ctx - 1 - i, n_all + n_ctx - 1 - i))


def scan_apply(y0, rm, d0, batch):
    _, rows, d = y0.shape
    c, w = SCAN_CHUNK, SCAN_PAIR_W
    n_pairs = d // w
    per_b = rows // batch // c

    def idx(b, i, dr):
        return b * per_b + _scan_chunk_index(i, dr)

    kernel = functools.partial(_scan_apply_kernel, n_pairs=n_pairs)
    return pl.pallas_call(
        kernel,
        grid=(2, batch, per_b),
        in_specs=[pl.BlockSpec((None, c, d), lambda dr, b, i: (dr, idx(b, i, dr), 0)),
                  pl.BlockSpec((None, None, n_pairs, c + 2 * w, w), lambda dr, b, i: (dr, idx(b, i, dr), 0, 0, 0)),
                  pl.BlockSpec((None, None, n_pairs, w, w), lambda dr, b, i: (dr, idx(b, i, dr), 0, 0, 0))],
        out_specs=pl.BlockSpec((None, c, d), lambda dr, b, i: (dr, idx(b, i, dr), 0)),
        out_shape=jax.ShapeDtypeStruct((2, rows, d), F32),
        scratch_shapes=[pltpu.VMEM((n_pairs, w, w), F32)],
        compiler_params=_params(("parallel", "parallel", "arbitrary")),
        name="scan_apply",
    )(y0, rm, d0)


def _shift_mix_kernel(h_ref, hp_ref, hn_ref, mix_ref, *out_refs):
    per_b = S_ALL // ROW_TILE
    pos = pl.program_id(0) % per_b
    h = h_ref[...]
    row = lax.broadcasted_iota(jnp.int32, h.shape, 0)
    starts_seq = (pos == 0) | (pos == 1)
    ends_seq = (pos == 0) | (pos == per_b - 1)
    prev_row = jnp.where(starts_seq, 0.0, hp_ref[7:8, :])
    next_row = jnp.where(ends_seq, 0.0, hn_ref[0:1, :])
    prev = jnp.where(row == 0, prev_row, pltpu.roll(h, 1, 0))
    nxt = jnp.where(row == ROW_TILE - 1, next_row, pltpu.roll(h, ROW_TILE - 1, 0))
    xx = 0.5 * (prev + nxt) - h
    for j, o_ref in enumerate(out_refs):
        o_ref[...] = (h + xx * mix_ref[j:j + 1, :]).astype(o_ref.dtype)


def shift_mix(h, x_mix):
    rows, d = h.shape
    n_mix = x_mix.shape[0]
    sub = ROW_TILE // 8
    tile = pl.BlockSpec((ROW_TILE, d), lambda i: (i, 0))
    return pl.pallas_call(
        _shift_mix_kernel,
        grid=(rows // ROW_TILE,),
        in_specs=[tile,
                  pl.BlockSpec((8, d), lambda i: (jnp.maximum(i * sub - 1, 0), 0)),
                  pl.BlockSpec((8, d), lambda i: (jnp.minimum((i + 1) * sub, rows // 8 - 1), 0)),
                  pl.BlockSpec((n_mix, d), lambda i: (0, 0))],
        out_specs=[tile] * n_mix,
        out_shape=[jax.ShapeDtypeStruct((rows, d), BF16)] * n_mix,
        compiler_params=_params(("parallel",)),
        name="shift_mix",
    )(h, h, h, x_mix)


def _rwkv_out_kernel(y_ref, bonus_ref, g_ref, gn_ref, o_ref):
    w = SCAN_PAIR_W
    for s in range(o_ref.shape[-1] // w):
        cols = slice(s * w, (s + 1) * w)
        y = y_ref[0, :, cols] + y_ref[1, :, cols]
        mu = _head_sum(y) * (1.0 / RWKV_HEAD)
        dev = y - mu
        var = _head_sum(dev * dev) * (1.0 / RWKV_HEAD)
        yn = dev * lax.rsqrt(var + GN_EPS) * gn_ref[0:1, cols] + gn_ref[1:2, cols]
        o_ref[:, cols] = ((yn + bonus_ref[:, cols]) * g_ref[:, cols]).astype(o_ref.dtype)


def rwkv_out(y, bonus, g, gn_g, gn_b, tn=512):
    _, rows, d = y.shape
    tile = pl.BlockSpec((ROW_TILE, tn), lambda i, j: (i, j))
    return pl.pallas_call(
        _rwkv_out_kernel,
        grid=(rows // ROW_TILE, d // tn),
        in_specs=[pl.BlockSpec((2, ROW_TILE, tn), lambda i, j: (0, i, j)), tile, tile,
                  pl.BlockSpec((2, tn), lambda i, j: (0, j))],
        out_specs=tile,
        out_shape=jax.ShapeDtypeStruct((rows, d), BF16),
        compiler_params=_params(("parallel", "parallel")),
        name="rwkv_out",
    )(y, bonus, g, jnp.stack([gn_g, gn_b]))


def _route_topk(logits):
    lane = lax.broadcasted_iota(jnp.int32, logits.shape, 1)
    far = 4 * HEAD_W

    def first_max(vals):
        top = jnp.max(vals, axis=-1, keepdims=True)
        return top, jnp.min(jnp.where(vals == top, lane, far), axis=-1, keepdims=True)

    is_group = lane < N_GROUPS
    g_top, g_sel = first_max(jnp.where(is_group, logits, NEG_INF))
    p_sel = 1.0 / jnp.sum(jnp.where(is_group, jnp.exp(logits - g_top), 0.0), axis=-1, keepdims=True)
    lo = N_GROUPS + g_sel * EXPERTS_PER_GROUP
    le = jnp.where((lane >= lo) & (lane < lo + EXPERTS_PER_GROUP), logits, NEG_INF)
    v1, i1 = first_max(le)
    v2, i2 = first_max(jnp.where(lane == i1, NEG_INF, le))
    e2 = jnp.exp(v2 - v1)
    w1 = p_sel / (1.0 + e2)
    w2 = p_sel * e2 / (1.0 + e2)
    out = jnp.where(lane == 0, (i1 - N_GROUPS).astype(F32), 0.0)
    out = jnp.where(lane == 1, (i2 - N_GROUPS).astype(F32), out)
    out = jnp.where(lane == 2, w1, out)
    return jnp.where(lane == 3, w2, out)


def moe_dispatch(route):
    t = route.shape[0]
    n_pairs = TOP_K * t
    tm = MOE_TILE
    n_slots = n_pairs + N_EXPERTS * tm
    n_tiles = n_slots // tm
    e_flat = route[:, :TOP_K].astype(jnp.int32).reshape(-1)
    gate_bits = lax.bitcast_convert_type(route[:, TOP_K:2 * TOP_K], jnp.int32).reshape(-1)
    onehot = (e_flat[:, None] == jnp.arange(N_EXPERTS, dtype=jnp.int32)[None, :]).astype(jnp.int32)
    csum = jnp.cumsum(onehot, axis=0)
    counts = csum[-1]
    padded = ((counts + tm - 1) // tm) * tm
    pend = jnp.cumsum(padded)
    pstart = pend - padded
    pos = jnp.sum(onehot * (pstart[None, :] + csum - onehot), axis=1)
    pair = jnp.arange(n_pairs, dtype=jnp.int32)
    slots = jnp.full((n_slots, 2), -1, jnp.int32).at[pos].set(jnp.stack([pair, gate_bits], axis=1))
    valid = slots[:, 0] >= 0
    tok = jnp.where(valid, slots[:, 0] // TOP_K, 0)
    dst = jnp.where(valid, slots[:, 0], n_pairs + jnp.arange(n_slots, dtype=jnp.int32) % tm)
    gate = jnp.where(valid, lax.bitcast_convert_type(slots[:, 1], F32), 0.0)
    tile_start = jnp.arange(n_tiles, dtype=jnp.int32) * tm
    tile_e = jnp.sum((tile_start[:, None] >= pend[None, :]).astype(jnp.int32), axis=1)
    tile_e = jnp.minimum(tile_e, N_EXPERTS - 1)
    n_live = (pend[-1] // tm).reshape(1)
    return (tok.reshape(n_tiles, 1, tm), dst.reshape(n_tiles, 1, tm), gate.reshape(n_slots, 1),
            tile_e.astype(jnp.int32), n_live.astype(jnp.int32))


def _moe_kernel(te_ref, nl_ref, tok_ref, tokn_ref, dst_ref, gate_ref, h_hbm, w1_ref, w3_ref, w2_ref,
                y_hbm, xbuf, obuf, w1b, w3b, w2b, sem_in, sem_out):
    i = pl.program_id(0)
    n_live = nl_ref[0]
    tm = MOE_TILE
    slot = i % 2

    def gather_copy(src_row, r, s):
        return pltpu.make_async_copy(h_hbm.at[pl.ds(src_row, 1), :], xbuf.at[s, pl.ds(r, 1), :], sem_in.at[s])

    def scatter_copy(r, pair):
        d = obuf.shape[-1]
        col = pl.multiple_of((pair % TOP_K) * d, d)
        return pltpu.make_async_copy(obuf.at[pl.ds(r, 1), :], y_hbm.at[pl.ds(pair // TOP_K, 1), pl.ds(col, d)],
                                     sem_out.at[0])

    def wait_gather(s):
        pltpu.make_async_copy(h_hbm.at[pl.ds(0, tm), :], xbuf.at[s], sem_in.at[s]).wait()

    def wait_scatter():
        pltpu.make_async_copy(obuf, y_hbm.at[pl.ds(0, tm), pl.ds(0, obuf.shape[-1])], sem_out.at[0]).wait()

    def per_row(fn):
        def body(r, carry):
            fn(r)
            return carry
        lax.fori_loop(0, tm, body, 0, unroll=MOE_DMA_UNROLL)

    @pl.when(i == 0)
    def _():
        per_row(lambda r: gather_copy(tok_ref[0, r], r, 0).start())

    @pl.when(i + 1 < n_live)
    def _():
        per_row(lambda r: gather_copy(tokn_ref[0, r], r, (i + 1) % 2).start())

    @pl.when(i < n_live)
    def _():
        wait_gather(slot)
        prev_e = te_ref[jnp.maximum(i - 1, 0)]

        @pl.when((i == 0) | (te_ref[i] != prev_e))
        def _():
            w1b[...] = w1_ref[...].astype(BF16)
            w3b[...] = w3_ref[...].astype(BF16)
            w2b[...] = w2_ref[...].astype(BF16)

        x = xbuf[slot].astype(BF16)
        up = _dot(x, w1b[...])
        hid = (up * jax.nn.sigmoid(up)) * _dot(x, w3b[...]) * gate_ref[...]
        out = _dot(hid.astype(BF16), w2b[...])

        @pl.when(i > 0)
        def _():
            wait_scatter()

        obuf[...] = out

        @pl.when(i == 0)
        def _():
            first_spare = TOP_K * (y_hbm.shape[0] - tm // TOP_K)
            per_row(lambda r: scatter_copy(r, first_spare + r).start())
            wait_scatter()

        per_row(lambda r: scatter_copy(r, dst_ref[0, r]).start())

        @pl.when(i == n_live - 1)
        def _():
            wait_scatter()


def moe_experts(h, tok, dst, gate, tile_e, n_live, w1, w3, w2, layer):
    t, d = h.shape
    f = w1.shape[-1]
    tm = MOE_TILE
    n_tiles = tok.shape[0]

    def w_index(i, te, nl):
        return (layer, te[i] // EXPERTS_PER_GROUP, te[i] % EXPERTS_PER_GROUP, 0, 0)

    grid_spec = pltpu.PrefetchScalarGridSpec(
        num_scalar_prefetch=2,
        grid=(n_tiles,),
        in_specs=[pl.BlockSpec((None, 1, tm), lambda i, te, nl: (i, 0, 0), memory_space=pltpu.SMEM),
                  pl.BlockSpec((None, 1, tm), lambda i, te, nl: (jnp.minimum(i + 1, n_tiles - 1), 0, 0),
                               memory_space=pltpu.SMEM),
                  pl.BlockSpec((None, 1, tm), lambda i, te, nl: (i, 0, 0), memory_space=pltpu.SMEM),
                  pl.BlockSpec((tm, 1), lambda i, te, nl: (i, 0)),
                  pl.BlockSpec(memory_space=pl.ANY),
                  pl.BlockSpec((None, None, None, d, f), w_index),
                  pl.BlockSpec((None, None, None, d, f), w_index),
                  pl.BlockSpec((None, None, None, f, d), w_index)],
        out_specs=pl.BlockSpec(memory_space=pl.ANY),
        scratch_shapes=[pltpu.VMEM((2, tm, d), F32), pltpu.VMEM((tm, d), F32),
                        pltpu.VMEM((d, f), BF16), pltpu.VMEM((d, f), BF16), pltpu.VMEM((f, d), BF16),
                        pltpu.SemaphoreType.DMA((2,)), pltpu.SemaphoreType.DMA((1,))])
    return pl.pallas_call(
        _moe_kernel,
        grid_spec=grid_spec,
        out_shape=jax.ShapeDtypeStruct((t + tm // TOP_K, TOP_K * d), F32),
        compiler_params=_params(("arbitrary",)),
        name="moe_experts",
    )(tile_e, n_live, tok, tok, dst, gate, h, w1, w3, w2)


def _mod_spec(k):
    per_b = S_ALL // ROW_TILE
    return pl.BlockSpec((None, 1, D_MODEL),
                        lambda i: ((i // per_b * 2 + jnp.minimum(i % per_b, 1)) * N_MOD + k, 0, 0))


def _norm_mod(x, g_ref, sh_ref, sc_ref):
    h = x * lax.rsqrt(jnp.mean(x * x, axis=-1, keepdims=True) + NORM_EPS) * g_ref[...]
    if sh_ref is None:
        return h
    return h * (1.0 + sc_ref[...]) + sh_ref[...]


def _first_norm_kernel(x_ref, g_ref, sh_ref, sc_ref, h_ref):
    h_ref[...] = _norm_mod(x_ref[...], g_ref, sh_ref, sc_ref).astype(h_ref.dtype)


def first_norm(xs, g, mod, h_dtype):
    rows, d = xs.shape
    tile = pl.BlockSpec((ROW_TILE, d), lambda i: (i, 0))
    return pl.pallas_call(
        _first_norm_kernel,
        grid=(rows // ROW_TILE,),
        in_specs=[tile, pl.BlockSpec((1, d), lambda i: (0, 0)), _mod_spec(0), _mod_spec(1)],
        out_specs=tile,
        out_shape=jax.ShapeDtypeStruct((rows, d), h_dtype),
        compiler_params=_params(("parallel",)),
        name="first_norm",
    )(xs, g.reshape(1, d), mod, mod)


def _post_mixer_kernel(x_ref, u_ref, gate_ref, g_ref, sh_ref, sc_ref, wr_ref, br_ref, xo_ref, h_ref, route_ref):
    x = x_ref[...] + gate_ref[...] * u_ref[...]
    xo_ref[...] = x
    h = _norm_mod(x, g_ref, sh_ref, sc_ref)
    h_ref[...] = h
    logits = jnp.dot(h, wr_ref[...], precision=lax.Precision.HIGHEST, preferred_element_type=F32) + br_ref[...]
    route_ref[...] = _route_topk(logits)


def post_mixer(xs, upd, g, mod, router_w, router_b):
    rows, d = xs.shape
    tile = pl.BlockSpec((ROW_TILE, d), lambda i: (i, 0))
    rtile = pl.BlockSpec((ROW_TILE, HEAD_W), lambda i: (i, 0))
    return pl.pallas_call(
        _post_mixer_kernel,
        grid=(rows // ROW_TILE,),
        in_specs=[tile, tile, _mod_spec(2), pl.BlockSpec((1, d), lambda i: (0, 0)), _mod_spec(3), _mod_spec(4),
                  pl.BlockSpec((d, HEAD_W), lambda i: (0, 0)), pl.BlockSpec((1, HEAD_W), lambda i: (0, 0))],
        out_specs=[tile, tile, rtile],
        out_shape=[jax.ShapeDtypeStruct((rows, d), F32), jax.ShapeDtypeStruct((rows, d), F32),
                   jax.ShapeDtypeStruct((rows, HEAD_W), F32)],
        compiler_params=_params(("parallel",)),
        name="post_mixer",
    )(xs, upd, mod, g.reshape(1, d), mod, mod, router_w, router_b)


def _post_moe_kernel(x_ref, y2_ref, gate_ref, g_ref, *rest, modulate):
    if modulate:
        sh_ref, sc_ref, xo_ref, h_ref = rest
    else:
        sh_ref = sc_ref = None
        xo_ref, h_ref = rest
    d = x_ref.shape[-1]
    x = x_ref[...] + gate_ref[...] * (y2_ref[:, :d] + y2_ref[:, d:])
    xo_ref[...] = x
    h_ref[...] = _norm_mod(x, g_ref, sh_ref, sc_ref).astype(h_ref.dtype)


def post_moe(xs, y2, g, mod, next_mod, h_dtype):
    rows, d = xs.shape
    tile = pl.BlockSpec((ROW_TILE, d), lambda i: (i, 0))
    pair = pl.BlockSpec((ROW_TILE, TOP_K * d), lambda i: (i, 0))
    gspec = pl.BlockSpec((1, d), lambda i: (0, 0))
    modulate = next_mod is not None
    in_specs = [tile, pair, _mod_spec(5), gspec] + ([_mod_spec(0), _mod_spec(1)] if modulate else [])
    args = (xs, y2, mod, g.reshape(1, d)) + ((next_mod, next_mod) if modulate else ())
    return pl.pallas_call(
        functools.partial(_post_moe_kernel, modulate=modulate),
        grid=(rows // ROW_TILE,),
        in_specs=in_specs,
        out_specs=[tile, tile],
        out_shape=[jax.ShapeDtypeStruct((rows, d), F32), jax.ShapeDtypeStruct((rows, d), h_dtype)],
        compiler_params=_params(("parallel",)),
        name="post_moe",
    )(*args)


def _pad_cols(w, n):
    return jnp.pad(w, ((0, 0), (0, n - w.shape[1])))


def _two_dir_lora(x, w1, w2):
    r = w1.shape[-1]
    w1c = _pad_cols(jnp.concatenate([w1[0], w1[1]], axis=1), 256).astype(BF16)
    d = w2.shape[-1]
    w2bd = jnp.zeros((256, 2 * d), F32).at[:r, :d].set(w2[0]).at[r:2 * r, d:].set(w2[1]).astype(BF16)
    return matmul(x, w1c, F32, tn=256), w2bd


def rwkv_mixer(h, p, batch):
    mixes = shift_mix(h, p['x_mix'])
    r = matmul(mixes[0], p['w_rkv'][0].astype(BF16), F32)
    k = matmul(mixes[1], p['w_rkv'][1].astype(BF16), F32)
    v = matmul(mixes[2], p['w_rkv'][2].astype(BF16), F32)
    hid_w, w2bd_w = _two_dir_lora(mixes[3], p['dec_w1'], p['dec_w2'])
    w_lora = matmul(jnp.tanh(hid_w).astype(BF16), w2bd_w, F32)
    hid_a, w2bd_a = _two_dir_lora(mixes[4], p['iclr_a1'], p['iclr_a2'])
    a_lora = matmul(hid_a.astype(BF16), w2bd_a, F32)
    g_hid = matmul(mixes[5], p['gate_g1'].astype(BF16), F32, tn=256)
    g = matmul(jax.nn.sigmoid(g_hid).astype(BF16), p['gate_g2'].astype(BF16), F32)

    par = jnp.concatenate([p['dec_w0'], p['iclr_a0'], p['k_k'][None], p['k_a'][None], p['r_k'].reshape(1, -1),
                           jnp.zeros((1, D_MODEL), F32)], axis=0)
    y0, rm, d0, bonus = scan_prepare(r, k, v, w_lora, a_lora, par)
    y = scan_apply(y0, rm, d0, batch)
    yo = rwkv_out(y, bonus, g, p['gn_g'], p['gn_b'])
    return matmul(yo, p['w_out'].astype(BF16), F32)


def even_mixer(h, p, layer_idx, cos_t, sin_t, batch):
    rows = batch * S_ALL
    lam_init = 0.8 - 0.6 * math.exp(-0.3 * layer_idx)
    proj = matmul(h, p['w_in'].astype(BF16), BF16, tn=1024)
    proj = proj.reshape(batch, S_ALL, -1)
    lf = p['diff_lambda']
    lam = (jnp.exp(jnp.sum(lf[0] * lf[1])) - jnp.exp(jnp.sum(lf[2] * lf[3])) + lam_init).reshape(1)
    a_out = diff_attention(proj, lam, cos_t, sin_t, p['subln_g'], lam_init)
    b_out = neighbourhood_attention(proj, na_bias_table(p['rpb']))
    mixed = jnp.concatenate([a_out, b_out], axis=-1).reshape(rows, -1)
    return matmul(mixed, p['w_out'].astype(BF16), F32)


def router_table(router_g, router_g_b, router_e, router_e_b):
    w = _pad_cols(jnp.concatenate([router_g, router_e], axis=1), HEAD_W)
    b = _pad_cols(jnp.concatenate([router_g_b, router_e_b])[None, :], HEAD_W)
    return w, b


def kernel(x, c, ctx, c_ctx, ada_w, ada_b, norm_g, final_g, even_w_in, even_w_out, diff_lambda, diff_subln_g, na_rpb, rwkv_x_mix, rwkv_w_rkv, rwkv_w_out, rwkv_dec_w0, rwkv_dec_w1, rwkv_dec_w2, rwkv_iclr_a0, rwkv_iclr_a1, rwkv_iclr_a2, rwkv_gate_g1, rwkv_gate_g2, rwkv_k_k, rwkv_k_a, rwkv_r_k, rwkv_gn_g, rwkv_gn_b, moe_router_g, moe_router_g_b, moe_router_e, moe_router_e_b, moe_w1, moe_w3, moe_w2):
    batch = x.shape[0]
    rows = batch * S_ALL
    xs = jnp.concatenate([ctx, x], axis=1).reshape(rows, D_MODEL)
    cos_t, sin_t = rope_tables()

    cvec = jnp.concatenate([c, c_ctx[None], jnp.zeros((8 - batch - 1, D_MODEL), F32)], axis=0)
    mods = ada_modulation(jax.nn.silu(cvec).astype(BF16), ada_w, ada_b)
    mods = mods.reshape(DEPTH, 8, N_MOD, D_MODEL)

    def mod_table(i):
        mod_l = mods[i, :batch]
        mod_c = jnp.broadcast_to(mods[i, batch][None], mod_l.shape)
        return jnp.stack([mod_c, mod_l], axis=1).reshape(batch * 2 * N_MOD, 1, D_MODEL)

    mod = mod_table(0)
    h = first_norm(xs, norm_g[0, 0], mod, BF16)
    for i in range(DEPTH):
        j = i // 2
        if i % 2 == 0:
            p = dict(w_in=even_w_in[j], w_out=even_w_out[j], diff_lambda=diff_lambda[j],
                     subln_g=diff_subln_g[j], rpb=na_rpb[j])
            out = even_mixer(h, p, i, cos_t, sin_t, batch)
        else:
            p = dict(x_mix=rwkv_x_mix[j], w_rkv=rwkv_w_rkv[j], w_out=rwkv_w_out[j], dec_w0=rwkv_dec_w0[j],
                     dec_w1=rwkv_dec_w1[j], dec_w2=rwkv_dec_w2[j], iclr_a0=rwkv_iclr_a0[j],
                     iclr_a1=rwkv_iclr_a1[j], iclr_a2=rwkv_iclr_a2[j], gate_g1=rwkv_gate_g1[j],
                     gate_g2=rwkv_gate_g2[j], k_k=rwkv_k_k[j], k_a=rwkv_k_a[j], r_k=rwkv_r_k[j],
                     gn_g=rwkv_gn_g[j], gn_b=rwkv_gn_b[j])
            out = rwkv_mixer(h, p, batch)
        rw, rb = router_table(moe_router_g[i], moe_router_g_b[i], moe_router_e[i], moe_router_e_b[i])
        xs, h2, route = post_mixer(xs, out, norm_g[i, 1], mod, rw, rb)
        tok, dst, gate, tile_e, n_live = moe_dispatch(route)
        y2 = moe_experts(h2, tok, dst, gate, tile_e, n_live, moe_w1, moe_w3, moe_w2, i)
        if i + 1 < DEPTH:
            next_mod = mod_table(i + 1)
            xs, h = post_moe(xs, y2, norm_g[i + 1, 0], mod, next_mod, F32 if (i + 1) % 2 else BF16)
            mod = next_mod
        else:
            _, h = post_moe(xs, y2, final_g, mod, None, F32)
    return h.reshape(batch, S_ALL, D_MODEL)[:, CTX_LEN:]
```

```python
import functools
import math

import numpy as np
import jax
import jax.numpy as jnp
from jax import lax
from jax.experimental import pallas as pl
from jax.experimental.pallas import tpu as pltpu

F32 = jnp.float32
BF16 = jnp.bfloat16

D_MODEL = 2048
DEPTH = 4
GRID_W = 64
CTX_LEN = 256
SEQ = 4096
S_ALL = CTX_LEN + SEQ
N_MOD = 6
NORM_EPS = 1e-6
NEG_INF = -1e30

DIFF_HEADS = 8
DIFF_QK_DIM = 64
NA_HEADS = 8
NA_DIM = 128
NA_KH = 8
NA_KW = 16
ROPE_THETA = 10000.0
SUBLN_EPS = 1e-5
HEAD_W = 128
ATT_TILE = 256
DIFF_KV_BLOCK = 1024
NA_ROWS_PER_STEP = 4
NA_WIN_ROWS = 12

RWKV_HEAD = 64
RWKV_HEADS = D_MODEL // RWKV_HEAD
GN_EPS = 64e-5
SCAN_CHUNK = 64
SCAN_PAIR_W = 2 * RWKV_HEAD
SCAN_SUB = 8

N_GROUPS = 4
EXPERTS_PER_GROUP = 8
N_EXPERTS = N_GROUPS * EXPERTS_PER_GROUP
TOP_K = 2
EXPERT_FF = 512
MOE_TILE = 256
MOE_DMA_UNROLL = 8
ROW_TILE = 256

VMEM_LIMIT = 52 * 1024 * 1024


def _params(sem):
    return pltpu.CompilerParams(dimension_semantics=sem, vmem_limit_bytes=VMEM_LIMIT)


def _dot(a, b):
    return jnp.dot(a, b, preferred_element_type=F32)


def _dot_nt(a, b):
    return lax.dot_general(a, b, (((1,), (1,)), ((), ())), preferred_element_type=F32)


def _dot_tn(a, b):
    return lax.dot_general(a, b, (((0,), (0,)), ((), ())), preferred_element_type=F32)


def _mm_kernel(a_ref, w_ref, o_ref):
    o_ref[...] = _dot(a_ref[...], w_ref[...]).astype(o_ref.dtype)


def matmul(a, w, out_dtype, tm=1024, tn=512):
    m, k = a.shape
    n = w.shape[1]
    while m % tm:
        tm //= 2
    tn = min(tn, n)
    assert n % tn == 0
    return pl.pallas_call(
        _mm_kernel,
        grid=(m // tm, n // tn),
        in_specs=[pl.BlockSpec((tm, k), lambda i, j: (i, 0)),
                  pl.BlockSpec((k, tn), lambda i, j: (0, j))],
        out_specs=pl.BlockSpec((tm, tn), lambda i, j: (i, j)),
        out_shape=jax.ShapeDtypeStruct((m, n), out_dtype),
        compiler_params=_params(("parallel", "parallel")),
        name="matmul",
    )(a, w)


def _ada_kernel(s_ref, w_ref, b_ref, o_ref):
    o_ref[...] = _dot(s_ref[...], w_ref[...].astype(BF16)) + b_ref[...]


def ada_modulation(svec, ada_w, ada_b, tn=1024):
    nl, d, n = ada_w.shape
    rows = svec.shape[0]
    return pl.pallas_call(
        _ada_kernel,
        grid=(nl, n // tn),
        in_specs=[pl.BlockSpec((rows, d), lambda l, j: (0, 0)),
                  pl.BlockSpec((None, d, tn), lambda l, j: (l, 0, j)),
                  pl.BlockSpec((None, 1, tn), lambda l, j: (l, 0, j))],
        out_specs=pl.BlockSpec((None, rows, tn), lambda l, j: (l, 0, j)),
        out_shape=jax.ShapeDtypeStruct((nl, rows, n), F32),
        compiler_params=_params(("parallel", "parallel")),
        name="ada_modulation",
    )(svec, ada_w, ada_b.reshape(nl, 1, n))


def _rope(x, cos, sin_signed):
    lane = lax.broadcasted_iota(jnp.int32, x.shape, 1)
    first_half = (lane & 63) < 32
    partner = jnp.where(first_half, pltpu.roll(x, HEAD_W - 32, 1), pltpu.roll(x, 32, 1))
    return x * cos + partner * sin_signed


def _diff_attn_kernel(lam_ref, q_ref, k_ref, v_ref, cos_ref, sin_ref, g_ref, o_ref, kr_ref, vt_ref, sa_ref, sb_ref, *,
                      post_scale):
    j = pl.program_id(2)
    tq = ATT_TILE

    @pl.when(j == 0)
    def _():
        def prep_chunk(c, carry):
            rows = pl.ds(pl.multiple_of(c * ATT_TILE, ATT_TILE), ATT_TILE)
            kr_ref[rows, :] = _rope(k_ref[rows, :].astype(F32), cos_ref[rows, :], sin_ref[rows, :]).astype(BF16)
            vt_ref[:, rows] = v_ref[rows, :].astype(F32).T.astype(BF16)
            return carry
        lax.fori_loop(0, S_ALL // ATT_TILE, prep_chunk, 0)

    qrows = pl.ds(pl.multiple_of(j * tq, tq), tq)
    q = _rope(q_ref[...].astype(F32), cos_ref[qrows, :], sin_ref[qrows, :]) * (DIFF_QK_DIM ** -0.5)
    lane = lax.broadcasted_iota(jnp.int32, q.shape, 1)
    q1 = jnp.where(lane < DIFF_QK_DIM, q, 0.0).astype(BF16)
    q2 = jnp.where(lane < DIFF_QK_DIM, 0.0, q).astype(BF16)

    kb = DIFF_KV_BLOCK
    n_latent = SEQ // kb
    maps = (q1, q2)

    def score_stage(start, size, sbuf):
        kc = kr_ref[pl.ds(start, size), :]
        tops = []
        for i, qm in enumerate(maps):
            s = _dot_nt(kc, qm)
            sbuf[i, 0:size, :] = s
            tops.append(jnp.max(s, axis=0, keepdims=True))
        return tuple(tops)

    def softmax_stage(start, size, sbuf, tops, state):
        vt = vt_ref[:, pl.ds(start, size)]
        new = []
        for i in range(2):
            m, l, acc = state[3 * i:3 * i + 3]
            m_new = jnp.maximum(m, tops[i])
            alpha = jnp.exp(m - m_new)
            p = jnp.exp(sbuf[i, 0:size, :] - m_new)
            new += [m_new, alpha * l + jnp.sum(p, axis=0, keepdims=True),
                    alpha * acc + _dot(vt, p.astype(BF16))]
        return tuple(new)

    def latent_start(n):
        return pl.multiple_of(CTX_LEN + jnp.minimum(n, n_latent - 1) * kb, CTX_LEN)

    row = jnp.full((1, tq), NEG_INF, F32)
    zrow = jnp.zeros((1, tq), F32)
    zacc = jnp.zeros((HEAD_W, tq), F32)
    state = softmax_stage(0, CTX_LEN, sa_ref, score_stage(0, CTX_LEN, sa_ref), (row, zrow, zacc, row, zrow, zacc))
    tops = score_stage(latent_start(0), kb, sb_ref)

    def block_pair(t, carry):
        st, tp = carry[:6], carry[6:]
        n = 2 * t
        tp_next = score_stage(latent_start(n + 1), kb, sa_ref)
        st = softmax_stage(latent_start(n), kb, sb_ref, tp, st)
        tp_last = score_stage(latent_start(n + 2), kb, sb_ref)
        st = softmax_stage(latent_start(n + 1), kb, sa_ref, tp_next, st)
        return st + tp_last

    n_pairs = jnp.where(j == 0, 0, n_latent // 2)
    m1, l1, a1, m2, l2, a2 = lax.fori_loop(0, n_pairs, block_pair, state + tops)[:6]
    out = a1 / l1 - lam_ref[0] * (a2 / l2)
    ms = jnp.mean(out * out, axis=0, keepdims=True)
    y = out * lax.rsqrt(ms + SUBLN_EPS) * (g_ref[...] * post_scale)
    o_ref[...] = y.T.astype(o_ref.dtype)


def diff_attention(proj, lam, cos_t, sin_t, subln_g, lam_init):
    b = proj.shape[0]
    kernel = functools.partial(_diff_attn_kernel, post_scale=1.0 - lam_init)
    return pl.pallas_call(
        kernel,
        grid=(b, DIFF_HEADS, S_ALL // ATT_TILE),
        in_specs=[pl.BlockSpec(memory_space=pltpu.SMEM),
                  pl.BlockSpec((None, ATT_TILE, HEAD_W), lambda bi, h, j: (bi, j, h)),
                  pl.BlockSpec((None, S_ALL, HEAD_W), lambda bi, h, j: (bi, 0, DIFF_HEADS + h)),
                  pl.BlockSpec((None, S_ALL, HEAD_W), lambda bi, h, j: (bi, 0, 2 * DIFF_HEADS + h)),
                  pl.BlockSpec((S_ALL, HEAD_W), lambda bi, h, j: (0, 0)),
                  pl.BlockSpec((S_ALL, HEAD_W), lambda bi, h, j: (0, 0)),
                  pl.BlockSpec((HEAD_W, 1), lambda bi, h, j: (0, 0))],
        out_specs=pl.BlockSpec((None, ATT_TILE, HEAD_W), lambda bi, h, j: (bi, j, h)),
        out_shape=jax.ShapeDtypeStruct((b, S_ALL, DIFF_HEADS * HEAD_W), BF16),
        scratch_shapes=[pltpu.VMEM((S_ALL, HEAD_W), BF16), pltpu.VMEM((HEAD_W, S_ALL), BF16),
                        pltpu.VMEM((2, DIFF_KV_BLOCK, ATT_TILE), F32), pltpu.VMEM((2, DIFF_KV_BLOCK, ATT_TILE), F32)],
        compiler_params=_params(("parallel", "parallel", "arbitrary")),
        name="diff_attention",
    )(lam, proj, proj, proj, cos_t, sin_t, subln_g.reshape(HEAD_W, 1))


def rope_tables():
    n_freq = DIFF_QK_DIM // 4
    inv_freq = ROPE_THETA ** (-jnp.arange(n_freq, dtype=F32) / n_freq)
    t = jnp.arange(SEQ, dtype=jnp.int32)
    row = (t // GRID_W).astype(F32)
    col = (t % GRID_W).astype(F32)
    ang = jnp.concatenate([row[:, None] * inv_freq, col[:, None] * inv_freq], axis=-1)
    cos, sin = jnp.cos(ang), jnp.sin(ang)
    cos_l = jnp.concatenate([cos, cos, cos, cos], axis=-1)
    sin_l = jnp.concatenate([-sin, sin, -sin, sin], axis=-1)
    cos_all = jnp.concatenate([jnp.ones((CTX_LEN, HEAD_W), F32), cos_l], axis=0)
    sin_all = jnp.concatenate([jnp.zeros((CTX_LEN, HEAD_W), F32), sin_l], axis=0)
    return cos_all, sin_all


def _na_window_start(j):
    g = j - 1
    return jnp.clip(NA_ROWS_PER_STEP * g - NA_KH // 2, 0, SEQ // GRID_W - NA_WIN_ROWS)


def _na_kernel(q_ref, k_ref, v_ref, bias_ref, o_ref):
    j = pl.program_id(2)
    win = NA_WIN_ROWS * GRID_W
    start = pl.multiple_of(CTX_LEN + _na_window_start(j) * GRID_W, GRID_W)
    scale = NA_DIM ** -0.5
    q = q_ref[...]
    s_c = _dot_nt(q, k_ref[pl.ds(0, CTX_LEN), :]) * scale
    s_w = _dot_nt(q, k_ref[pl.ds(start, win), :]) * scale + bias_ref[...]
    m = jnp.maximum(jnp.max(s_c, axis=-1, keepdims=True), jnp.max(s_w, axis=-1, keepdims=True))
    p_c = jnp.exp(s_c - m)
    p_w = jnp.exp(s_w - m)
    l = jnp.sum(p_c, axis=-1, keepdims=True) + jnp.sum(p_w, axis=-1, keepdims=True)
    o = _dot(p_c.astype(BF16), v_ref[pl.ds(0, CTX_LEN), :]) + _dot(p_w.astype(BF16), v_ref[pl.ds(start, win), :])
    o_ref[...] = (o / l).astype(o_ref.dtype)


def _na_bias_pattern(j):
    n_groups = SEQ // (GRID_W * NA_ROWS_PER_STEP)
    g = j - 1
    return jnp.where(j == 0, 3, jnp.where(g == 0, 0, jnp.where(g == n_groups - 1, 2, 1)))


def na_bias_table(rpb):
    rows = SEQ // GRID_W
    n_groups = rows // NA_ROWS_PER_STEP
    cols = np.arange(GRID_W)
    col_start = np.clip(cols - NA_KW // 2, 0, GRID_W - NA_KW)
    col_mask = (cols[None, :] >= col_start[:, None]) & (cols[None, :] < col_start[:, None] + NA_KW)
    c_idx = np.clip(cols[None, :] - cols[:, None] + NA_KW - 1, 0, 2 * NA_KW - 2)
    pats = []
    for g in (0, 1, n_groups - 1):
        u0 = int(np.clip(NA_ROWS_PER_STEP * g - NA_KH // 2, 0, rows - NA_WIN_ROWS))
        r = NA_ROWS_PER_STEP * g + np.arange(NA_ROWS_PER_STEP)
        r0 = np.clip(r - NA_KH // 2, 0, rows - NA_KH)
        kr = u0 + np.arange(NA_WIN_ROWS)
        valid_r = (kr[None, :] >= r0[:, None]) & (kr[None, :] < r0[:, None] + NA_KH)
        r_idx = np.clip(kr[None, :] - r[:, None] + NA_KH - 1, 0, 2 * NA_KH - 2)
        valid = valid_r[:, None, :, None] & col_mask[None, :, None, :]
        r_sel = np.eye(2 * NA_KH - 1, dtype=np.float32)[r_idx]
        c_sel = np.eye(2 * NA_KW - 1, dtype=np.float32)[c_idx]
        rows_sel = jnp.einsum('qkr,hrc->hqkc', r_sel, rpb.astype(F32), precision=lax.Precision.HIGHEST)
        gathered = jnp.einsum('hqkc,abc->hqakb', rows_sel, c_sel, precision=lax.Precision.HIGHEST)
        pats.append(jnp.where(valid[None], gathered, NEG_INF))
    pats.append(jnp.full_like(pats[0], NEG_INF))
    tab = jnp.stack(pats, axis=1)
    return tab.reshape(NA_HEADS, 4, NA_ROWS_PER_STEP * GRID_W, NA_WIN_ROWS * GRID_W)


def neighbourhood_attention(proj, bias_tab):
    b = proj.shape[0]
    tq = NA_ROWS_PER_STEP * GRID_W
    assert tq == CTX_LEN
    win = NA_WIN_ROWS * GRID_W
    base = 3 * DIFF_HEADS
    return pl.pallas_call(
        _na_kernel,
        grid=(b, NA_HEADS, S_ALL // tq),
        in_specs=[pl.BlockSpec((None, tq, HEAD_W), lambda bi, h, j: (bi, j, base + h)),
                  pl.BlockSpec((None, S_ALL, HEAD_W), lambda bi, h, j: (bi, 0, base + NA_HEADS + h)),
                  pl.BlockSpec((None, S_ALL, HEAD_W), lambda bi, h, j: (bi, 0, base + 2 * NA_HEADS + h)),
                  pl.BlockSpec((None, None, tq, win), lambda bi, h, j: (h, _na_bias_pattern(j), 0, 0))],
        out_specs=pl.BlockSpec((None, tq, HEAD_W), lambda bi, h, j: (bi, j, h)),
        out_shape=jax.ShapeDtypeStruct((b, S_ALL, NA_HEADS * HEAD_W), BF16),
        compiler_params=_params(("parallel", "parallel", "arbitrary")),
        name="neighbourhood_attention",
    )(proj, proj, proj, bias_tab)


def _each(fn, *lists):
    return [fn(*args) for args in zip(*lists)]


def _head_sum(x):
    rows, w = x.shape
    r_i = lax.broadcasted_iota(jnp.int32, (w, w), 0)
    c_i = lax.broadcasted_iota(jnp.int32, (w, w), 1)
    ones_bd = jnp.where((r_i < RWKV_HEAD) == (c_i < RWKV_HEAD), 1.0, 0.0).astype(BF16)
    hi = x.astype(BF16)
    lo = (x - hi.astype(F32)).astype(BF16)
    s = _dot(jnp.concatenate([hi, lo], axis=0), ones_bd)
    return s[:rows] + s[rows:]


def _scan_prepare_kernel(r_ref, k_ref, v_ref, wl0_ref, wl1_ref, al0_ref, al1_ref, par_ref,
                         y0_ref, rm_ref, d0_ref, bonus_ref):
    c = SCAN_CHUNK
    w = SCAN_PAIR_W
    hw = RWKV_HEAD
    t_idx = lax.broadcasted_iota(jnp.int32, (c, w), 0)
    lane = lax.broadcasted_iota(jnp.int32, (c, w), 1)
    s_idx = lane & (hw - 1)
    head0 = lane < hw
    eye = s_idx == t_idx
    tt = lax.broadcasted_iota(jnp.int32, (c, c), 0)
    ss = lax.broadcasted_iota(jnp.int32, (c, c), 1)
    strict = (s_idx < t_idx, s_idx > t_idx)
    incl = (s_idx <= t_idx, s_idx >= t_idx)
    tri = (jnp.where(ss <= tt, 1.0, 0.0).astype(BF16), jnp.where(ss >= tt, 1.0, 0.0).astype(BF16))
    row2 = lax.broadcasted_iota(jnp.int32, (w, w), 0)
    lane2 = lax.broadcasted_iota(jnp.int32, (w, w), 1)
    bdmask = (row2 < hw) == (lane2 < hw)
    eye2 = row2 == lane2

    def bd(y):
        return jnp.where(bdmask, jnp.concatenate([y, y], axis=0), 0.0).astype(BF16)

    def pm(x, ybd):
        return _dot(x.astype(BF16), ybd)

    inst = [(d, sub) for d in range(2) for sub in range(SCAN_SUB)]
    dirs = [d for d, _ in inst]
    rows = [slice(sub * c, (sub + 1) * c) for _, sub in inst]
    par = par_ref[...]
    k_k, k_a, r_k = par[4:5, :], par[5:6, :], par[6:7, :]
    sub_rows = [slice(sub * c, (sub + 1) * c) for sub in range(SCAN_SUB)]
    r_s = [r_ref[rw, :] for rw in sub_rows]
    k_s = [k_ref[rw, :] for rw in sub_rows]
    v_s = [v_ref[rw, :] for rw in sub_rows]
    wl_s = [[wl0_ref[rw, :] for rw in sub_rows], [wl1_ref[rw, :] for rw in sub_rows]]
    al_s = [[al0_ref[rw, :] for rw in sub_rows], [al1_ref[rw, :] for rw in sub_rows]]

    def unit_key(k_i):
        kk_i = k_i * k_k
        return kk_i * lax.rsqrt(jnp.maximum(_head_sum(kk_i * kk_i), 1e-24))

    def log_decay(wl_i, d):
        z = -(par[d:d + 1, :] + wl_i)
        softplus = jnp.maximum(z, 0.0) + jnp.log(1.0 + jnp.exp(-jnp.abs(z)))
        return -jnp.exp(-softplus - 0.5)

    kk_s = _each(unit_key, k_s)
    a_s = [[jax.nn.sigmoid(par[2 + d:3 + d, :] + x) for x in al_s[d]] for d in range(2)]
    kd_s = [[k_i * (1.0 + (a_i - 1.0) * k_a) for k_i, a_i in zip(k_s, a_s[d])] for d in range(2)]
    bonus = [_head_sum(r_i * r_k * (kd0 + kd1)) * v_i for r_i, kd0, kd1, v_i in zip(r_s, kd_s[0], kd_s[1], v_s)]

    r = [r_s[sub] for _, sub in inst]
    kk = [kk_s[sub] for _, sub in inst]
    v = [v_s[sub] for _, sub in inst]
    ld = [log_decay(wl_s[d][sub], d) for d, sub in inst]
    a = [a_s[d][sub] for d, sub in inst]
    kd = [kd_s[d][sub] for d, sub in inst]

    def cumulative(ld_i, d):
        p_hi = ld_i.astype(BF16)
        rem = ld_i - p_hi.astype(F32)
        p_mid = rem.astype(BF16)
        p_lo = (rem - p_mid.astype(F32)).astype(BF16)
        cs = _dot(tri[d], jnp.concatenate([p_hi, p_mid, p_lo], axis=1))
        return cs[:, :w] + cs[:, w:2 * w] + cs[:, 2 * w:]

    lam = _each(cumulative, ld, dirs)
    lam_end = _each(lambda l, d: l[0:1, :] if d else l[c - 1:c, :], lam, dirs)
    g_cum = _each(jnp.exp, lam)
    g_inv = _each(lambda l: jnp.exp(-l), lam)
    g_prev = _each(lambda l, x: jnp.exp(l - x), lam, ld)
    g_rel = _each(lambda le, l: jnp.exp(le - l), lam_end, lam)
    g_end = _each(jnp.exp, lam_end)

    qk = _each(jnp.multiply, kk, g_prev)
    rt = _each(jnp.multiply, r, g_cum)
    beta = _each(jnp.multiply, kk, a)
    bt = _each(jnp.multiply, beta, g_inv)
    kt = _each(jnp.multiply, kd, g_inv)
    bh = _each(jnp.multiply, beta, g_rel)
    kh = _each(jnp.multiply, kd, g_rel)

    def big_product(qk_i, rt_i, bt_i, kt_i):
        rhs = jnp.concatenate([jnp.where(head0, bt_i, 0.0), jnp.where(head0, 0.0, bt_i),
                               jnp.where(head0, kt_i, 0.0), jnp.where(head0, 0.0, kt_i)], axis=0).astype(BF16)
        return _dot_nt(jnp.concatenate([qk_i, rt_i], axis=0).astype(BF16), rhs)

    big = _each(big_product, qk, rt, bt, kt)
    n_pow = _each(lambda b, d: jnp.where(strict[d], -b[:c, :w], 0.0), big, dirs)
    a_k = _each(lambda b, d: jnp.where(strict[d], b[:c, w:], 0.0), big, dirs)
    g_b = _each(lambda b, d: jnp.where(incl[d], b[c:, :w], 0.0), big, dirs)
    g_k = _each(lambda b, d: jnp.where(incl[d], b[c:, w:], 0.0), big, dirs)

    t_inv = _each(lambda n: jnp.where(eye, 1.0, 0.0) + n, n_pow)
    for _ in range(int(math.log2(c)) - 1):
        n_pow = _each(lambda n: pm(n, bd(n)), n_pow)
        t_inv = _each(lambda t, n: t + pm(t, bd(n)), t_inv, n_pow)

    v_bd = _each(bd, v)
    x1 = _each(pm, a_k, v_bd)
    uw = _each(lambda t, x, q: _dot(t.astype(BF16), jnp.concatenate([bd(x), bd(q)], axis=1)), t_inv, x1, qk)
    u0 = _each(lambda x: -x[:, :w], uw)
    wm = _each(lambda x: x[:, w:], uw)
    y0 = _each(lambda gk, vb, gb, u: pm(gk, vb) + pm(gb, bd(u)), g_k, v_bd, g_b, u0)
    rm = _each(lambda rt_i, gb, wm_i: rt_i - pm(gb, bd(wm_i)), rt, g_b, wm)
    d0 = _each(lambda kh_i, bh_i, v_i, u: jnp.where(bdmask, _dot_tn(
        jnp.concatenate([kh_i, bh_i], axis=0).astype(BF16), jnp.concatenate([v_i, u], axis=0).astype(BF16)), 0.0),
        kh, bh, v, u0)
    mm = _each(lambda bh_i, wm_i, ge: jnp.where(eye2, ge, 0.0) - jnp.where(
        bdmask, _dot_tn(bh_i.astype(BF16), wm_i.astype(BF16)), 0.0), bh, wm, g_end)
    d0 = _each(lambda x: x[:hw] + x[hw:], d0)
    mm = _each(lambda x: x[:hw] + x[hw:], mm)
    mm_hi = _each(lambda m: m.astype(BF16), mm)
    mm_lo = _each(lambda m, mh: (m - mh.astype(F32)).astype(BF16), mm, mm_hi)

    for sub in range(SCAN_SUB):
        bonus_ref[sub_rows[sub], :] = bonus[sub]
    for i, (d, sub) in enumerate(inst):
        y0_ref[d, rows[i], :] = y0[i]
        rm_ref[d, sub, 0, 0:c, :] = rm[i].astype(BF16)
        rm_ref[d, sub, 0, c:c + hw, :] = mm_hi[i]
        rm_ref[d, sub, 0, c + hw:c + 2 * hw, :] = mm_lo[i]
        d0_ref[d, sub, 0, :, :] = d0[i]


def scan_prepare(r, k, v, w_lora, a_lora, par):
    rows, d = r.shape
    c, w = SCAN_CHUNK, SCAN_PAIR_W
    n_pairs = d // w
    n_chunks = rows // c
    blk = SCAN_SUB * c
    shared = pl.BlockSpec((blk, w), lambda i, p: (i, p))
    rev_half = pl.BlockSpec((blk, w), lambda i, p: (i, n_pairs + p))
    return pl.pallas_call(
        _scan_prepare_kernel,
        grid=(rows // blk, n_pairs),
        in_specs=[shared, shared, shared, shared, rev_half, shared, rev_half,
                  pl.BlockSpec((8, w), lambda i, p: (0, p))],
        out_specs=[pl.BlockSpec((2, blk, w), lambda i, p: (0, i, p)),
                   pl.BlockSpec((2, SCAN_SUB, 1, c + w, w), lambda i, p: (0, i, p, 0, 0)),
                   pl.BlockSpec((2, SCAN_SUB, 1, w // 2, w), lambda i, p: (0, i, p, 0, 0)),
                   shared],
        out_shape=[jax.ShapeDtypeStruct((2, rows, d), F32),
                   jax.ShapeDtypeStruct((2, n_chunks, n_pairs, c + w, w), BF16),
                   jax.ShapeDtypeStruct((2, n_chunks, n_pairs, w // 2, w), F32),
                   jax.ShapeDtypeStruct((rows, d), F32)],
        compiler_params=_params(("parallel", "parallel")),
        name="scan_prepare",
    )(r, k, v, w_lora, w_lora, a_lora, a_lora, par)


def _scan_apply_kernel(y0_ref, rm_ref, d0_ref, y_ref, z_ref, *, n_pairs):
    c, w = SCAN_CHUNK, SCAN_PAIR_W
    i = pl.program_id(2)

    @pl.when(i == 0)
    def _():
        z_ref[...] = jnp.zeros_like(z_ref)

    hw = RWKV_HEAD
    row2 = lax.broadcasted_iota(jnp.int32, (w, w), 0)
    lane2 = lax.broadcasted_iota(jnp.int32, (w, w), 1)
    bdmask = (row2 < hw) == (lane2 < hw)

    def bd(x):
        return jnp.where(bdmask, jnp.concatenate([x, x], axis=0), 0.0)

    z = [bd(z_ref[p]) for p in range(n_pairs)]
    z_hi = _each(lambda x: x.astype(BF16), z)
    z_lo = _each(lambda x, xh: (x - xh.astype(F32)).astype(BF16), z, z_hi)
    res = [_dot(rm_ref[p], jnp.concatenate([z_hi[p], z_lo[p]], axis=1)) for p in range(n_pairs)]
    res = _each(lambda x: x[:, :w] + x[:, w:], res)
    y = [y0_ref[:, p * w:(p + 1) * w] + res[p][:c] for p in range(n_pairs)]
    z_new = [d0_ref[p] + res[p][c:c + hw] + res[p][c + hw:] for p in range(n_pairs)]
    for p in range(n_pairs):
        y_ref[:, p * w:(p + 1) * w] = y[p]
        z_ref[p] = z_new[p]


def _scan_chunk_index(i, d):
    n_ctx = CTX_LEN // SCAN_CHUNK
    n_all = S_ALL // SCAN_CHUNK
    return jnp.where(d == 0, i, jnp.where(i < n_ctx, n_ctx - 1 - i, n_all + n_ctx - 1 - i))


def scan_apply(y0, rm, d0, batch):
    _, rows, d = y0.shape
    c, w = SCAN_CHUNK, SCAN_PAIR_W
    n_pairs = d // w
    per_b = rows // batch // c

    def idx(b, i, dr):
        return b * per_b + _scan_chunk_index(i, dr)

    kernel = functools.partial(_scan_apply_kernel, n_pairs=n_pairs)
    return pl.pallas_call(
        kernel,
        grid=(2, batch, per_b),
        in_specs=[pl.BlockSpec((None, c, d), lambda dr, b, i: (dr, idx(b, i, dr), 0)),
                  pl.BlockSpec((None, None, n_pairs, c + w, w), lambda dr, b, i: (dr, idx(b, i, dr), 0, 0, 0)),
                  pl.BlockSpec((None, None, n_pairs, w // 2, w), lambda dr, b, i: (dr, idx(b, i, dr), 0, 0, 0))],
        out_specs=pl.BlockSpec((None, c, d), lambda dr, b, i: (dr, idx(b, i, dr), 0)),
        out_shape=jax.ShapeDtypeStruct((2, rows, d), F32),
        scratch_shapes=[pltpu.VMEM((n_pairs, w // 2, w), F32)],
        compiler_params=_params(("parallel", "parallel", "arbitrary")),
        name="scan_apply",
    )(y0, rm, d0)


def _shift_mix_kernel(h_ref, hp_ref, hn_ref, mix_ref, *out_refs):
    per_b = S_ALL // ROW_TILE
    pos = pl.program_id(0) % per_b
    h = h_ref[...]
    row = lax.broadcasted_iota(jnp.int32, h.shape, 0)
    starts_seq = (pos == 0) | (pos == 1)
    ends_seq = (pos == 0) | (pos == per_b - 1)
    prev_row = jnp.where(starts_seq, 0.0, hp_ref[7:8, :])
    next_row = jnp.where(ends_seq, 0.0, hn_ref[0:1, :])
    prev = jnp.where(row == 0, prev_row, pltpu.roll(h, 1, 0))
    nxt = jnp.where(row == ROW_TILE - 1, next_row, pltpu.roll(h, ROW_TILE - 1, 0))
    xx = 0.5 * (prev + nxt) - h
    for j, o_ref in enumerate(out_refs):
        o_ref[...] = (h + xx * mix_ref[j:j + 1, :]).astype(o_ref.dtype)


def shift_mix(h, x_mix):
    rows, d = h.shape
    n_mix = x_mix.shape[0]
    sub = ROW_TILE // 8
    tile = pl.BlockSpec((ROW_TILE, d), lambda i: (i, 0))
    return pl.pallas_call(
        _shift_mix_kernel,
        grid=(rows // ROW_TILE,),
        in_specs=[tile,
                  pl.BlockSpec((8, d), lambda i: (jnp.maximum(i * sub - 1, 0), 0)),
                  pl.BlockSpec((8, d), lambda i: (jnp.minimum((i + 1) * sub, rows // 8 - 1), 0)),
                  pl.BlockSpec((n_mix, d), lambda i: (0, 0))],
        out_specs=[tile] * n_mix,
        out_shape=[jax.ShapeDtypeStruct((rows, d), BF16)] * n_mix,
        compiler_params=_params(("parallel",)),
        name="shift_mix",
    )(h, h, h, x_mix)


def _rwkv_out_kernel(y_ref, bonus_ref, g_ref, gn_ref, o_ref):
    w = SCAN_PAIR_W
    for s in range(o_ref.shape[-1] // w):
        cols = slice(s * w, (s + 1) * w)
        y = y_ref[0, :, cols] + y_ref[1, :, cols]
        mu = _head_sum(y) * (1.0 / RWKV_HEAD)
        dev = y - mu
        var = _head_sum(dev * dev) * (1.0 / RWKV_HEAD)
        yn = dev * lax.rsqrt(var + GN_EPS) * gn_ref[0:1, cols] + gn_ref[1:2, cols]
        o_ref[:, cols] = ((yn + bonus_ref[:, cols]) * g_ref[:, cols]).astype(o_ref.dtype)


def rwkv_out(y, bonus, g, gn_g, gn_b, tn=512):
    _, rows, d = y.shape
    tile = pl.BlockSpec((ROW_TILE, tn), lambda i, j: (i, j))
    return pl.pallas_call(
        _rwkv_out_kernel,
        grid=(rows // ROW_TILE, d // tn),
        in_specs=[pl.BlockSpec((2, ROW_TILE, tn), lambda i, j: (0, i, j)), tile, tile,
                  pl.BlockSpec((2, tn), lambda i, j: (0, j))],
        out_specs=tile,
        out_shape=jax.ShapeDtypeStruct((rows, d), BF16),
        compiler_params=_params(("parallel", "parallel")),
        name="rwkv_out",
    )(y, bonus, g, jnp.stack([gn_g, gn_b]))


def _route_topk(logits):
    lane = lax.broadcasted_iota(jnp.int32, logits.shape, 1)
    far = 4 * HEAD_W

    def first_max(vals):
        top = jnp.max(vals, axis=-1, keepdims=True)
        return top, jnp.min(jnp.where(vals == top, lane, far), axis=-1, keepdims=True)

    is_group = lane < N_GROUPS
    g_top, g_sel = first_max(jnp.where(is_group, logits, NEG_INF))
    p_sel = 1.0 / jnp.sum(jnp.where(is_group, jnp.exp(logits - g_top), 0.0), axis=-1, keepdims=True)
    lo = N_GROUPS + g_sel * EXPERTS_PER_GROUP
    le = jnp.where((lane >= lo) & (lane < lo + EXPERTS_PER_GROUP), logits, NEG_INF)
    v1, i1 = first_max(le)
    v2, i2 = first_max(jnp.where(lane == i1, NEG_INF, le))
    e2 = jnp.exp(v2 - v1)
    w1 = p_sel / (1.0 + e2)
    w2 = p_sel * e2 / (1.0 + e2)
    out = jnp.where(lane == 0, (i1 - N_GROUPS).astype(F32), 0.0)
    out = jnp.where(lane == 1, (i2 - N_GROUPS).astype(F32), out)
    out = jnp.where(lane == 2, w1, out)
    return jnp.where(lane == 3, w2, out)


def moe_dispatch(route):
    t = route.shape[0]
    n_pairs = TOP_K * t
    tm = MOE_TILE
    n_slots = n_pairs + N_EXPERTS * tm
    n_tiles = n_slots // tm
    e_flat = route[:, :TOP_K].astype(jnp.int32).reshape(-1)
    gate_bits = lax.bitcast_convert_type(route[:, TOP_K:2 * TOP_K], jnp.int32).reshape(-1)
    onehot = (e_flat[:, None] == jnp.arange(N_EXPERTS, dtype=jnp.int32)[None, :]).astype(jnp.int32)
    csum = jnp.cumsum(onehot, axis=0)
    counts = csum[-1]
    padded = ((counts + tm - 1) // tm) * tm
    pend = jnp.cumsum(padded)
    pstart = pend - padded
    pos = jnp.sum(onehot * (pstart[None, :] + csum - onehot), axis=1)
    pair = jnp.arange(n_pairs, dtype=jnp.int32)
    slots = jnp.full((n_slots, 2), -1, jnp.int32).at[pos].set(jnp.stack([pair, gate_bits], axis=1))
    valid = slots[:, 0] >= 0
    tok = jnp.where(valid, slots[:, 0] // TOP_K, 0)
    dst = jnp.where(valid, (slots[:, 0] % TOP_K) * t + slots[:, 0] // TOP_K,
                    n_pairs + jnp.arange(n_slots, dtype=jnp.int32) % tm)
    gate = jnp.where(valid, lax.bitcast_convert_type(slots[:, 1], F32), 0.0)
    tile_start = jnp.arange(n_tiles, dtype=jnp.int32) * tm
    tile_e = jnp.sum((tile_start[:, None] >= pend[None, :]).astype(jnp.int32), axis=1)
    tile_e = jnp.minimum(tile_e, N_EXPERTS - 1)
    n_live = (pend[-1] // tm).reshape(1)
    return (tok.reshape(n_tiles, 1, tm), dst.reshape(n_tiles, 1, tm), gate.reshape(n_slots, 1),
            tile_e.astype(jnp.int32), n_live.astype(jnp.int32))


def _moe_kernel(te_ref, nl_ref, tok_ref, tokn_ref, dst_ref, gate_ref, h_hbm, w1_ref, w3_ref, w2_ref,
                y_hbm, xbuf, obuf, w1b, w3b, w2b, sem_in, sem_out):
    i = pl.program_id(0)
    n_live = nl_ref[0]
    tm = MOE_TILE
    slot = i % 2

    def gather_copy(src_row, r, s):
        return pltpu.make_async_copy(h_hbm.at[pl.ds(src_row, 1), :], xbuf.at[s, pl.ds(r, 1), :], sem_in.at[s])

    def scatter_copy(r, dst_row):
        return pltpu.make_async_copy(obuf.at[pl.ds(r, 1), :], y_hbm.at[pl.ds(dst_row, 1), :], sem_out.at[0])

    def wait_gather(s):
        pltpu.make_async_copy(h_hbm.at[pl.ds(0, tm), :], xbuf.at[s], sem_in.at[s]).wait()

    def wait_scatter():
        pltpu.make_async_copy(obuf, y_hbm.at[pl.ds(0, tm), :], sem_out.at[0]).wait()

    def per_row(fn):
        def body(r, carry):
            fn(r)
            return carry
        lax.fori_loop(0, tm, body, 0, unroll=MOE_DMA_UNROLL)

    @pl.when(i == 0)
    def _():
        per_row(lambda r: gather_copy(tok_ref[0, r], r, 0).start())

    @pl.when(i < n_live)
    def _():
        wait_gather(slot)
        prev_e = te_ref[jnp.maximum(i - 1, 0)]

        @pl.when((i == 0) | (te_ref[i] != prev_e))
        def _():
            w1b[...] = w1_ref[...].astype(BF16)
            w3b[...] = w3_ref[...].astype(BF16)
            w2b[...] = w2_ref[...].astype(BF16)

        for r in range(tm):
            gather_copy(tokn_ref[0, r], r, 1 - slot).start()

        x = xbuf[slot].astype(BF16)
        up = _dot(x, w1b[...])
        hid = (up * jax.nn.sigmoid(up)) * _dot(x, w3b[...]) * gate_ref[...]
        out = _dot(hid.astype(BF16), w2b[...])

        @pl.when(i > 0)
        def _():
            wait_scatter()

        obuf[...] = out

        @pl.when(i == 0)
        def _():
            first_spare = y_hbm.shape[0] - tm
            per_row(lambda r: scatter_copy(r, first_spare + r).start())
            wait_scatter()

        per_row(lambda r: scatter_copy(r, dst_ref[0, r]).start())

        @pl.when(i == n_live - 1)
        def _():
            wait_gather(1 - slot)
            wait_scatter()


def moe_experts(h, tok, dst, gate, tile_e, n_live, w1, w3, w2, layer):
    t, d = h.shape
    f = w1.shape[-1]
    tm = MOE_TILE
    n_tiles = tok.shape[0]

    def w_index(i, te, nl):
        return (layer, te[i] // EXPERTS_PER_GROUP, te[i] % EXPERTS_PER_GROUP, 0, 0)

    grid_spec = pltpu.PrefetchScalarGridSpec(
        num_scalar_prefetch=2,
        grid=(n_tiles,),
        in_specs=[pl.BlockSpec((None, 1, tm), lambda i, te, nl: (i, 0, 0), memory_space=pltpu.SMEM),
                  pl.BlockSpec((None, 1, tm), lambda i, te, nl: (jnp.minimum(i + 1, n_tiles - 1), 0, 0),
                               memory_space=pltpu.SMEM),
                  pl.BlockSpec((None, 1, tm), lambda i, te, nl: (i, 0, 0), memory_space=pltpu.SMEM),
                  pl.BlockSpec((tm, 1), lambda i, te, nl: (i, 0)),
                  pl.BlockSpec(memory_space=pl.ANY),
                  pl.BlockSpec((None, None, None, d, f), w_index),
                  pl.BlockSpec((None, None, None, d, f), w_index),
                  pl.BlockSpec((None, None, None, f, d), w_index)],
        out_specs=pl.BlockSpec(memory_space=pl.ANY),
        scratch_shapes=[pltpu.VMEM((2, tm, d), F32), pltpu.VMEM((tm, d), F32),
                        pltpu.VMEM((d, f), BF16), pltpu.VMEM((d, f), BF16), pltpu.VMEM((f, d), BF16),
                        pltpu.SemaphoreType.DMA((2,)), pltpu.SemaphoreType.DMA((1,))])
    return pl.pallas_call(
        _moe_kernel,
        grid_spec=grid_spec,
        out_shape=jax.ShapeDtypeStruct((TOP_K * t + tm, d), F32),
        compiler_params=_params(("arbitrary",)),
        name="moe_experts",
    )(tile_e, n_live, tok, tok, dst, gate, h, w1, w3, w2)


def _mod_spec(k):
    per_b = S_ALL // ROW_TILE
    return pl.BlockSpec((None, 1, D_MODEL),
                        lambda i: ((i // per_b * 2 + jnp.minimum(i % per_b, 1)) * N_MOD + k, 0, 0))


def _norm_mod(x, g_ref, sh_ref, sc_ref):
    h = x * lax.rsqrt(jnp.mean(x * x, axis=-1, keepdims=True) + NORM_EPS) * g_ref[...]
    if sh_ref is None:
        return h
    return h * (1.0 + sc_ref[...]) + sh_ref[...]


def _first_norm_kernel(x_ref, g_ref, sh_ref, sc_ref, h_ref):
    h_ref[...] = _norm_mod(x_ref[...], g_ref, sh_ref, sc_ref).astype(h_ref.dtype)


def first_norm(xs, g, mod, h_dtype):
    rows, d = xs.shape
    tile = pl.BlockSpec((ROW_TILE, d), lambda i: (i, 0))
    return pl.pallas_call(
        _first_norm_kernel,
        grid=(rows // ROW_TILE,),
        in_specs=[tile, pl.BlockSpec((1, d), lambda i: (0, 0)), _mod_spec(0), _mod_spec(1)],
        out_specs=tile,
        out_shape=jax.ShapeDtypeStruct((rows, d), h_dtype),
        compiler_params=_params(("parallel",)),
        name="first_norm",
    )(xs, g.reshape(1, d), mod, mod)


def _post_mixer_kernel(x_ref, u_ref, gate_ref, g_ref, sh_ref, sc_ref, wr_ref, br_ref, xo_ref, h_ref, route_ref):
    x = x_ref[...] + gate_ref[...] * u_ref[...]
    xo_ref[...] = x
    h = _norm_mod(x, g_ref, sh_ref, sc_ref)
    h_ref[...] = h
    logits = jnp.dot(h, wr_ref[...], precision=lax.Precision.HIGHEST, preferred_element_type=F32) + br_ref[...]
    route_ref[...] = _route_topk(logits)


def post_mixer(xs, upd, g, mod, router_w, router_b):
    rows, d = xs.shape
    tile = pl.BlockSpec((ROW_TILE, d), lambda i: (i, 0))
    rtile = pl.BlockSpec((ROW_TILE, HEAD_W), lambda i: (i, 0))
    return pl.pallas_call(
        _post_mixer_kernel,
        grid=(rows // ROW_TILE,),
        in_specs=[tile, tile, _mod_spec(2), pl.BlockSpec((1, d), lambda i: (0, 0)), _mod_spec(3), _mod_spec(4),
                  pl.BlockSpec((d, HEAD_W), lambda i: (0, 0)), pl.BlockSpec((1, HEAD_W), lambda i: (0, 0))],
        out_specs=[tile, tile, rtile],
        out_shape=[jax.ShapeDtypeStruct((rows, d), F32), jax.ShapeDtypeStruct((rows, d), F32),
                   jax.ShapeDtypeStruct((rows, HEAD_W), F32)],
        compiler_params=_params(("parallel",)),
        name="post_mixer",
    )(xs, upd, mod, g.reshape(1, d), mod, mod, router_w, router_b)


def _post_moe_kernel(x_ref, ya_ref, yb_ref, gate_ref, g_ref, *rest, modulate):
    if modulate:
        sh_ref, sc_ref, xo_ref, h_ref = rest
    else:
        sh_ref = sc_ref = None
        xo_ref, h_ref = rest
    x = x_ref[...] + gate_ref[...] * (ya_ref[...] + yb_ref[...])
    xo_ref[...] = x
    h_ref[...] = _norm_mod(x, g_ref, sh_ref, sc_ref).astype(h_ref.dtype)


def post_moe(xs, y2, g, mod, next_mod, h_dtype):
    rows, d = xs.shape
    tile = pl.BlockSpec((ROW_TILE, d), lambda i: (i, 0))
    second = pl.BlockSpec((ROW_TILE, d), lambda i: (rows // ROW_TILE + i, 0))
    gspec = pl.BlockSpec((1, d), lambda i: (0, 0))
    modulate = next_mod is not None
    in_specs = [tile, tile, second, _mod_spec(5), gspec] + ([_mod_spec(0), _mod_spec(1)] if modulate else [])
    args = (xs, y2, y2, mod, g.reshape(1, d)) + ((next_mod, next_mod) if modulate else ())
    return pl.pallas_call(
        functools.partial(_post_moe_kernel, modulate=modulate),
        grid=(rows // ROW_TILE,),
        in_specs=in_specs,
        out_specs=[tile, tile],
        out_shape=[jax.ShapeDtypeStruct((rows, d), F32), jax.ShapeDtypeStruct((rows, d), h_dtype)],
        compiler_params=_params(("parallel",)),
        name="post_moe",
    )(*args)


def _pad_cols(w, n):
    return jnp.pad(w, ((0, 0), (0, n - w.shape[1])))


def _two_dir_lora(x, w1, w2):
    r = w1.shape[-1]
    w1c = _pad_cols(jnp.concatenate([w1[0], w1[1]], axis=1), 256).astype(BF16)
    d = w2.shape[-1]
    w2bd = jnp.zeros((256, 2 * d), F32).at[:r, :d].set(w2[0]).at[r:2 * r, d:].set(w2[1]).astype(BF16)
    return matmul(x, w1c, F32, tn=256), w2bd


def rwkv_mixer(h, p, batch):
    mixes = shift_mix(h, p['x_mix'])
    r = matmul(mixes[0], p['w_rkv'][0].astype(BF16), F32)
    k = matmul(mixes[1], p['w_rkv'][1].astype(BF16), F32)
    v = matmul(mixes[2], p['w_rkv'][2].astype(BF16), F32)
    hid_w, w2bd_w = _two_dir_lora(mixes[3], p['dec_w1'], p['dec_w2'])
    w_lora = matmul(jnp.tanh(hid_w).astype(BF16), w2bd_w, F32)
    hid_a, w2bd_a = _two_dir_lora(mixes[4], p['iclr_a1'], p['iclr_a2'])
    a_lora = matmul(hid_a.astype(BF16), w2bd_a, F32)
    g_hid = matmul(mixes[5], p['gate_g1'].astype(BF16), F32, tn=256)
    g = matmul(jax.nn.sigmoid(g_hid).astype(BF16), p['gate_g2'].astype(BF16), F32)

    par = jnp.concatenate([p['dec_w0'], p['iclr_a0'], p['k_k'][None], p['k_a'][None], p['r_k'].reshape(1, -1),
                           jnp.zeros((1, D_MODEL), F32)], axis=0)
    y0, rm, d0, bonus = scan_prepare(r, k, v, w_lora, a_lora, par)
    y = scan_apply(y0, rm, d0, batch)
    yo = rwkv_out(y, bonus, g, p['gn_g'], p['gn_b'])
    return matmul(yo, p['w_out'].astype(BF16), F32)


def even_mixer(h, p, layer_idx, cos_t, sin_t, batch):
    rows = batch * S_ALL
    lam_init = 0.8 - 0.6 * math.exp(-0.3 * layer_idx)
    proj = matmul(h, p['w_in'].astype(BF16), BF16, tn=1024)
    proj = proj.reshape(batch, S_ALL, -1)
    lf = p['diff_lambda']
    lam = (jnp.exp(jnp.sum(lf[0] * lf[1])) - jnp.exp(jnp.sum(lf[2] * lf[3])) + lam_init).reshape(1)
    a_out = diff_attention(proj, lam, cos_t, sin_t, p['subln_g'], lam_init)
    b_out = neighbourhood_attention(proj, na_bias_table(p['rpb']))
    mixed = jnp.concatenate([a_out, b_out], axis=-1).reshape(rows, -1)
    return matmul(mixed, p['w_out'].astype(BF16), F32)


def router_table(router_g, router_g_b, router_e, router_e_b):
    w = _pad_cols(jnp.concatenate([router_g, router_e], axis=1), HEAD_W)
    b = _pad_cols(jnp.concatenate([router_g_b, router_e_b])[None, :], HEAD_W)
    return w, b


def kernel(x, c, ctx, c_ctx, ada_w, ada_b, norm_g, final_g, even_w_in, even_w_out, diff_lambda, diff_subln_g, na_rpb, rwkv_x_mix, rwkv_w_rkv, rwkv_w_out, rwkv_dec_w0, rwkv_dec_w1, rwkv_dec_w2, rwkv_iclr_a0, rwkv_iclr_a1, rwkv_iclr_a2, rwkv_gate_g1, rwkv_gate_g2, rwkv_k_k, rwkv_k_a, rwkv_r_k, rwkv_gn_g, rwkv_gn_b, moe_router_g, moe_router_g_b, moe_router_e, moe_router_e_b, moe_w1, moe_w3, moe_w2):
    batch = x.shape[0]
    rows = batch * S_ALL
    xs = jnp.concatenate([ctx, x], axis=1).reshape(rows, D_MODEL)
    cos_t, sin_t = rope_tables()

    cvec = jnp.concatenate([c, c_ctx[None], jnp.zeros((8 - batch - 1, D_MODEL), F32)], axis=0)
    mods = ada_modulation(jax.nn.silu(cvec).astype(BF16), ada_w, ada_b)
    mods = mods.reshape(DEPTH, 8, N_MOD, D_MODEL)

    def mod_table(i):
        mod_l = mods[i, :batch]
        mod_c = jnp.broadcast_to(mods[i, batch][None], mod_l.shape)
        return jnp.stack([mod_c, mod_l], axis=1).reshape(batch * 2 * N_MOD, 1, D_MODEL)

    mod = mod_table(0)
    h = first_norm(xs, norm_g[0, 0], mod, BF16)
    for i in range(DEPTH):
        j = i // 2
        if i % 2 == 0:
            p = dict(w_in=even_w_in[j], w_out=even_w_out[j], diff_lambda=diff_lambda[j],
                     subln_g=diff_subln_g[j], rpb=na_rpb[j])
            out = even_mixer(h, p, i, cos_t, sin_t, batch)
        else:
            p = dict(x_mix=rwkv_x_mix[j], w_rkv=rwkv_w_rkv[j], w_out=rwkv_w_out[j], dec_w0=rwkv_dec_w0[j],
                     dec_w1=rwkv_dec_w1[j], dec_w2=rwkv_dec_w2[j], iclr_a0=rwkv_iclr_a0[j],
                     iclr_a1=rwkv_iclr_a1[j], iclr_a2=rwkv_iclr_a2[j], gate_g1=rwkv_gate_g1[j],
                     gate_g2=rwkv_gate_g2[j], k_k=rwkv_k_k[j], k_a=rwkv_k_a[j], r_k=rwkv_r_k[j],
                     gn_g=rwkv_gn_g[j], gn_b=rwkv_gn_b[j])
            out = rwkv_mixer(h, p, batch)
        rw, rb = router_table(moe_router_g[i], moe_router_g_b[i], moe_router_e[i], moe_router_e_b[i])
        xs, h2, route = post_mixer(xs, out, norm_g[i, 1], mod, rw, rb)
        tok, dst, gate, tile_e, n_live = moe_dispatch(route)
        y2 = moe_experts(h2, tok, dst, gate, tile_e, n_live, moe_w1, moe_w3, moe_w2, i)
        if i + 1 < DEPTH:
            next_mod = mod_table(i + 1)
            xs, h = post_moe(xs, y2, norm_g[i + 1, 0], mod, next_mod, F32 if (i + 1) % 2 else BF16)
            mod = next_mod
        else:
            _, h = post_moe(xs, y2, final_g, mod, None, F32)
    return h.reshape(batch, S_ALL, D_MODEL)[:, CTX_LEN:]
```

```python
import functools
import math

import numpy as np
import jax
import jax.numpy as jnp
from jax import lax
from jax.experimental import pallas as pl
from jax.experimental.pallas import tpu as pltpu

F32 = jnp.float32
BF16 = jnp.bfloat16

D_MODEL = 2048
DEPTH = 4
GRID_W = 64
CTX_LEN = 256
SEQ = 4096
S_ALL = CTX_LEN + SEQ
N_MOD = 6
NORM_EPS = 1e-6
NEG_INF = -1e30

DIFF_HEADS = 8
DIFF_QK_DIM = 64
NA_HEADS = 8
NA_DIM = 128
NA_KH = 8
NA_KW = 16
ROPE_THETA = 10000.0
SUBLN_EPS = 1e-5
HEAD_W = 128
ATT_TILE = 256
DIFF_KV_BLOCK = 1024
NA_ROWS_PER_STEP = 4
NA_WIN_ROWS = 12

RWKV_HEAD = 64
RWKV_HEADS = D_MODEL // RWKV_HEAD
GN_EPS = 64e-5
SCAN_CHUNK = 64
SCAN_PAIR_W = 2 * RWKV_HEAD
SCAN_SUB = 8

N_GROUPS = 4
EXPERTS_PER_GROUP = 8
N_EXPERTS = N_GROUPS * EXPERTS_PER_GROUP
TOP_K = 2
EXPERT_FF = 512
MOE_TILE = 256
MOE_DMA_UNROLL = 8
ROW_TILE = 256

VMEM_LIMIT = 52 * 1024 * 1024


def _params(sem):
    return pltpu.CompilerParams(dimension_semantics=sem, vmem_limit_bytes=VMEM_LIMIT)


def _dot(a, b):
    return jnp.dot(a, b, preferred_element_type=F32)


def _dot_nt(a, b):
    return lax.dot_general(a, b, (((1,), (1,)), ((), ())), preferred_element_type=F32)


def _dot_tn(a, b):
    return lax.dot_general(a, b, (((0,), (0,)), ((), ())), preferred_element_type=F32)


def _mm_kernel(a_ref, w_ref, o_ref):
    o_ref[...] = _dot(a_ref[...], w_ref[...]).astype(o_ref.dtype)


def matmul(a, w, out_dtype, tm=1024, tn=512):
    m, k = a.shape
    n = w.shape[1]
    while m % tm:
        tm //= 2
    tn = min(tn, n)
    assert n % tn == 0
    return pl.pallas_call(
        _mm_kernel,
        grid=(m // tm, n // tn),
        in_specs=[pl.BlockSpec((tm, k), lambda i, j: (i, 0)),
                  pl.BlockSpec((k, tn), lambda i, j: (0, j))],
        out_specs=pl.BlockSpec((tm, tn), lambda i, j: (i, j)),
        out_shape=jax.ShapeDtypeStruct((m, n), out_dtype),
        compiler_params=_params(("parallel", "parallel")),
        name="matmul",
    )(a, w)


def _ada_kernel(s_ref, w_ref, b_ref, o_ref):
    o_ref[...] = _dot(s_ref[...], w_ref[...].astype(BF16)) + b_ref[...]


def ada_modulation(svec, ada_w, ada_b, tn=1024):
    nl, d, n = ada_w.shape
    rows = svec.shape[0]
    return pl.pallas_call(
        _ada_kernel,
        grid=(nl, n // tn),
        in_specs=[pl.BlockSpec((rows, d), lambda l, j: (0, 0)),
                  pl.BlockSpec((None, d, tn), lambda l, j: (l, 0, j)),
                  pl.BlockSpec((None, 1, tn), lambda l, j: (l, 0, j))],
        out_specs=pl.BlockSpec((None, rows, tn), lambda l, j: (l, 0, j)),
        out_shape=jax.ShapeDtypeStruct((nl, rows, n), F32),
        compiler_params=_params(("parallel", "parallel")),
        name="ada_modulation",
    )(svec, ada_w, ada_b.reshape(nl, 1, n))


def _rope(x, cos, sin_signed):
    lane = lax.broadcasted_iota(jnp.int32, x.shape, 1)
    first_half = (lane & 63) < 32
    partner = jnp.where(first_half, pltpu.roll(x, HEAD_W - 32, 1), pltpu.roll(x, 32, 1))
    return x * cos + partner * sin_signed


def _diff_attn_kernel(lam_ref, q_ref, k_ref, v_ref, cos_ref, sin_ref, g_ref, o_ref, kr_ref, vt_ref, sa_ref, sb_ref, *,
                      post_scale):
    j = pl.program_id(2)
    tq = ATT_TILE

    @pl.when(j == 0)
    def _():
        def prep_chunk(c, carry):
            rows = pl.ds(pl.multiple_of(c * ATT_TILE, ATT_TILE), ATT_TILE)
            kr_ref[rows, :] = _rope(k_ref[rows, :].astype(F32), cos_ref[rows, :], sin_ref[rows, :]).astype(BF16)
            vt_ref[:, rows] = v_ref[rows, :].astype(F32).T.astype(BF16)
            return carry
        lax.fori_loop(0, S_ALL // ATT_TILE, prep_chunk, 0)

    qrows = pl.ds(pl.multiple_of(j * tq, tq), tq)
    q = _rope(q_ref[...].astype(F32), cos_ref[qrows, :], sin_ref[qrows, :]) * (DIFF_QK_DIM ** -0.5)
    lane = lax.broadcasted_iota(jnp.int32, q.shape, 1)
    q1 = jnp.where(lane < DIFF_QK_DIM, q, 0.0).astype(BF16)
    q2 = jnp.where(lane < DIFF_QK_DIM, 0.0, q).astype(BF16)

    kb = DIFF_KV_BLOCK
    n_latent = SEQ // kb
    maps = (q1, q2)

    def score_stage(start, size, sbuf):
        kc = kr_ref[pl.ds(start, size), :]
        tops = []
        for i, qm in enumerate(maps):
            s = _dot_nt(kc, qm)
            sbuf[i, 0:size, :] = s
            tops.append(jnp.max(s, axis=0, keepdims=True))
        return tuple(tops)

    def softmax_stage(start, size, sbuf, tops, state):
        vt = vt_ref[:, pl.ds(start, size)]
        new = []
        for i in range(2):
            m, l, acc = state[3 * i:3 * i + 3]
            m_new = jnp.maximum(m, tops[i])
            alpha = jnp.exp(m - m_new)
            p = jnp.exp(sbuf[i, 0:size, :] - m_new)
            new += [m_new, alpha * l + jnp.sum(p, axis=0, keepdims=True),
                    alpha * acc + _dot(vt, p.astype(BF16))]
        return tuple(new)

    def latent_start(n):
        return pl.multiple_of(CTX_LEN + jnp.minimum(n, n_latent - 1) * kb, CTX_LEN)

    row = jnp.full((1, tq), NEG_INF, F32)
    zrow = jnp.zeros((1, tq), F32)
    zacc = jnp.zeros((HEAD_W, tq), F32)
    state = softmax_stage(0, CTX_LEN, sa_ref, score_stage(0, CTX_LEN, sa_ref), (row, zrow, zacc, row, zrow, zacc))
    tops = score_stage(latent_start(0), kb, sb_ref)

    def block_pair(t, carry):
        st, tp = carry[:6], carry[6:]
        n = 2 * t
        tp_next = score_stage(latent_start(n + 1), kb, sa_ref)
        st = softmax_stage(latent_start(n), kb, sb_ref, tp, st)
        tp_last = score_stage(latent_start(n + 2), kb, sb_ref)
        st = softmax_stage(latent_start(n + 1), kb, sa_ref, tp_next, st)
        return st + tp_last

    n_pairs = jnp.where(j == 0, 0, n_latent // 2)
    m1, l1, a1, m2, l2, a2 = lax.fori_loop(0, n_pairs, block_pair, state + tops)[:6]
    out = a1 / l1 - lam_ref[0] * (a2 / l2)
    ms = jnp.mean(out * out, axis=0, keepdims=True)
    y = out * lax.rsqrt(ms + SUBLN_EPS) * (g_ref[...] * post_scale)
    o_ref[...] = y.T.astype(o_ref.dtype)


def diff_attention(proj, lam, cos_t, sin_t, subln_g, lam_init):
    b = proj.shape[0]
    kernel = functools.partial(_diff_attn_kernel, post_scale=1.0 - lam_init)
    return pl.pallas_call(
        kernel,
        grid=(b, DIFF_HEADS, S_ALL // ATT_TILE),
        in_specs=[pl.BlockSpec(memory_space=pltpu.SMEM),
                  pl.BlockSpec((None, ATT_TILE, HEAD_W), lambda bi, h, j: (bi, j, h)),
                  pl.BlockSpec((None, S_ALL, HEAD_W), lambda bi, h, j: (bi, 0, DIFF_HEADS + h)),
                  pl.BlockSpec((None, S_ALL, HEAD_W), lambda bi, h, j: (bi, 0, 2 * DIFF_HEADS + h)),
                  pl.BlockSpec((S_ALL, HEAD_W), lambda bi, h, j: (0, 0)),
                  pl.BlockSpec((S_ALL, HEAD_W), lambda bi, h, j: (0, 0)),
                  pl.BlockSpec((HEAD_W, 1), lambda bi, h, j: (0, 0))],
        out_specs=pl.BlockSpec((None, ATT_TILE, HEAD_W), lambda bi, h, j: (bi, j, h)),
        out_shape=jax.ShapeDtypeStruct((b, S_ALL, DIFF_HEADS * HEAD_W), BF16),
        scratch_shapes=[pltpu.VMEM((S_ALL, HEAD_W), BF16), pltpu.VMEM((HEAD_W, S_ALL), BF16),
                        pltpu.VMEM((2, DIFF_KV_BLOCK, ATT_TILE), F32), pltpu.VMEM((2, DIFF_KV_BLOCK, ATT_TILE), F32)],
        compiler_params=_params(("parallel", "parallel", "arbitrary")),
        name="diff_attention",
    )(lam, proj, proj, proj, cos_t, sin_t, subln_g.reshape(HEAD_W, 1))


def rope_tables():
    n_freq = DIFF_QK_DIM // 4
    inv_freq = ROPE_THETA ** (-jnp.arange(n_freq, dtype=F32) / n_freq)
    t = jnp.arange(SEQ, dtype=jnp.int32)
    row = (t // GRID_W).astype(F32)
    col = (t % GRID_W).astype(F32)
    ang = jnp.concatenate([row[:, None] * inv_freq, col[:, None] * inv_freq], axis=-1)
    cos, sin = jnp.cos(ang), jnp.sin(ang)
    cos_l = jnp.concatenate([cos, cos, cos, cos], axis=-1)
    sin_l = jnp.concatenate([-sin, sin, -sin, sin], axis=-1)
    cos_all = jnp.concatenate([jnp.ones((CTX_LEN, HEAD_W), F32), cos_l], axis=0)
    sin_all = jnp.concatenate([jnp.zeros((CTX_LEN, HEAD_W), F32), sin_l], axis=0)
    return cos_all, sin_all


def _na_window_start(j):
    g = j - 1
    return jnp.clip(NA_ROWS_PER_STEP * g - NA_KH // 2, 0, SEQ // GRID_W - NA_WIN_ROWS)


def _na_kernel(q_ref, k_ref, v_ref, bias_ref, o_ref):
    j = pl.program_id(2)
    win = NA_WIN_ROWS * GRID_W
    start = pl.multiple_of(CTX_LEN + _na_window_start(j) * GRID_W, GRID_W)
    scale = NA_DIM ** -0.5
    q = q_ref[...]
    s_c = _dot_nt(q, k_ref[pl.ds(0, CTX_LEN), :]) * scale
    s_w = _dot_nt(q, k_ref[pl.ds(start, win), :]) * scale + bias_ref[...]
    m = jnp.maximum(jnp.max(s_c, axis=-1, keepdims=True), jnp.max(s_w, axis=-1, keepdims=True))
    p_c = jnp.exp(s_c - m)
    p_w = jnp.exp(s_w - m)
    l = jnp.sum(p_c, axis=-1, keepdims=True) + jnp.sum(p_w, axis=-1, keepdims=True)
    o = _dot(p_c.astype(BF16), v_ref[pl.ds(0, CTX_LEN), :]) + _dot(p_w.astype(BF16), v_ref[pl.ds(start, win), :])
    o_ref[...] = (o / l).astype(o_ref.dtype)


def _na_bias_pattern(j):
    n_groups = SEQ // (GRID_W * NA_ROWS_PER_STEP)
    g = j - 1
    return jnp.where(j == 0, 3, jnp.where(g == 0, 0, jnp.where(g == n_groups - 1, 2, 1)))


def na_bias_table(rpb):
    rows = SEQ // GRID_W
    n_groups = rows // NA_ROWS_PER_STEP
    cols = np.arange(GRID_W)
    col_start = np.clip(cols - NA_KW // 2, 0, GRID_W - NA_KW)
    col_mask = (cols[None, :] >= col_start[:, None]) & (cols[None, :] < col_start[:, None] + NA_KW)
    c_idx = np.clip(cols[None, :] - cols[:, None] + NA_KW - 1, 0, 2 * NA_KW - 2)
    pats = []
    for g in (0, 1, n_groups - 1):
        u0 = int(np.clip(NA_ROWS_PER_STEP * g - NA_KH // 2, 0, rows - NA_WIN_ROWS))
        r = NA_ROWS_PER_STEP * g + np.arange(NA_ROWS_PER_STEP)
        r0 = np.clip(r - NA_KH // 2, 0, rows - NA_KH)
        kr = u0 + np.arange(NA_WIN_ROWS)
        valid_r = (kr[None, :] >= r0[:, None]) & (kr[None, :] < r0[:, None] + NA_KH)
        r_idx = np.clip(kr[None, :] - r[:, None] + NA_KH - 1, 0, 2 * NA_KH - 2)
        valid = valid_r[:, None, :, None] & col_mask[None, :, None, :]
        r_sel = np.eye(2 * NA_KH - 1, dtype=np.float32)[r_idx]
        c_sel = np.eye(2 * NA_KW - 1, dtype=np.float32)[c_idx]
        rows_sel = jnp.einsum('qkr,hrc->hqkc', r_sel, rpb.astype(F32), precision=lax.Precision.HIGHEST)
        gathered = jnp.einsum('hqkc,abc->hqakb', rows_sel, c_sel, precision=lax.Precision.HIGHEST)
        pats.append(jnp.where(valid[None], gathered, NEG_INF))
    pats.append(jnp.full_like(pats[0], NEG_INF))
    tab = jnp.stack(pats, axis=1)
    return tab.reshape(NA_HEADS, 4, NA_ROWS_PER_STEP * GRID_W, NA_WIN_ROWS * GRID_W)


def neighbourhood_attention(proj, bias_tab):
    b = proj.shape[0]
    tq = NA_ROWS_PER_STEP * GRID_W
    assert tq == CTX_LEN
    win = NA_WIN_ROWS * GRID_W
    base = 3 * DIFF_HEADS
    return pl.pallas_call(
        _na_kernel,
        grid=(b, NA_HEADS, S_ALL // tq),
        in_specs=[pl.BlockSpec((None, tq, HEAD_W), lambda bi, h, j: (bi, j, base + h)),
                  pl.BlockSpec((None, S_ALL, HEAD_W), lambda bi, h, j: (bi, 0, base + NA_HEADS + h)),
                  pl.BlockSpec((None, S_ALL, HEAD_W), lambda bi, h, j: (bi, 0, base + 2 * NA_HEADS + h)),
                  pl.BlockSpec((None, None, tq, win), lambda bi, h, j: (h, _na_bias_pattern(j), 0, 0))],
        out_specs=pl.BlockSpec((None, tq, HEAD_W), lambda bi, h, j: (bi, j, h)),
        out_shape=jax.ShapeDtypeStruct((b, S_ALL, NA_HEADS * HEAD_W), BF16),
        compiler_params=_params(("parallel", "parallel", "arbitrary")),
        name="neighbourhood_attention",
    )(proj, proj, proj, bias_tab)


def _each(fn, *lists):
    return [fn(*args) for args in zip(*lists)]


def _head_sum(x):
    rows, w = x.shape
    r_i = lax.broadcasted_iota(jnp.int32, (w, w), 0)
    c_i = lax.broadcasted_iota(jnp.int32, (w, w), 1)
    ones_bd = jnp.where((r_i < RWKV_HEAD) == (c_i < RWKV_HEAD), 1.0, 0.0).astype(BF16)
    hi = x.astype(BF16)
    lo = (x - hi.astype(F32)).astype(BF16)
    s = _dot(jnp.concatenate([hi, lo], axis=0), ones_bd)
    return s[:rows] + s[rows:]


def _scan_prepare_kernel(r_ref, k_ref, v_ref, hw_ref, ha_ref, w2w0_ref, w2w1_ref, w2a0_ref, w2a1_ref, par_ref,
                         y0_ref, rm_ref, d0_ref, bonus_ref):
    c = SCAN_CHUNK
    w = SCAN_PAIR_W
    hw = RWKV_HEAD
    t_idx = lax.broadcasted_iota(jnp.int32, (c, w), 0)
    lane = lax.broadcasted_iota(jnp.int32, (c, w), 1)
    s_idx = lane & (hw - 1)
    head0 = lane < hw
    eye = s_idx == t_idx
    tt = lax.broadcasted_iota(jnp.int32, (c, c), 0)
    ss = lax.broadcasted_iota(jnp.int32, (c, c), 1)
    strict = (s_idx < t_idx, s_idx > t_idx)
    incl = (s_idx <= t_idx, s_idx >= t_idx)
    tri = (jnp.where(ss <= tt, 1.0, 0.0).astype(BF16), jnp.where(ss >= tt, 1.0, 0.0).astype(BF16))
    row2 = lax.broadcasted_iota(jnp.int32, (w, w), 0)
    lane2 = lax.broadcasted_iota(jnp.int32, (w, w), 1)
    bdmask = (row2 < hw) == (lane2 < hw)
    eye2 = row2 == lane2

    def bd(y):
        return jnp.where(bdmask, jnp.concatenate([y, y], axis=0), 0.0).astype(BF16)

    def pm(x, ybd):
        return _dot(x.astype(BF16), ybd)

    inst = [(d, sub) for d in range(2) for sub in range(SCAN_SUB)]
    dirs = [d for d, _ in inst]
    rows = [slice(sub * c, (sub + 1) * c) for _, sub in inst]
    par = par_ref[...]
    k_k, k_a, r_k = par[4:5, :], par[5:6, :], par[6:7, :]
    sub_rows = [slice(sub * c, (sub + 1) * c) for sub in range(SCAN_SUB)]
    r_s = [r_ref[rw, :] for rw in sub_rows]
    k_s = [k_ref[rw, :] for rw in sub_rows]
    v_s = [v_ref[rw, :] for rw in sub_rows]
    hid_w = hw_ref[...]
    hid_a = ha_ref[...]
    wl_all = [_dot(hid_w, w2w0_ref[...]), _dot(hid_w, w2w1_ref[...])]
    al_all = [_dot(hid_a, w2a0_ref[...]), _dot(hid_a, w2a1_ref[...])]
    wl_s = [[wl_all[d][rw, :] for rw in sub_rows] for d in range(2)]
    al_s = [[al_all[d][rw, :] for rw in sub_rows] for d in range(2)]

    def unit_key(k_i):
        kk_i = k_i * k_k
        return kk_i * lax.rsqrt(jnp.maximum(_head_sum(kk_i * kk_i), 1e-24))

    def log_decay(wl_i, d):
        z = -(par[d:d + 1, :] + wl_i)
        softplus = jnp.maximum(z, 0.0) + jnp.log(1.0 + jnp.exp(-jnp.abs(z)))
        return -jnp.exp(-softplus - 0.5)

    kk_s = _each(unit_key, k_s)
    a_s = [[jax.nn.sigmoid(par[2 + d:3 + d, :] + x) for x in al_s[d]] for d in range(2)]
    kd_s = [[k_i * (1.0 + (a_i - 1.0) * k_a) for k_i, a_i in zip(k_s, a_s[d])] for d in range(2)]
    bonus = [_head_sum(r_i * r_k * (kd0 + kd1)) * v_i for r_i, kd0, kd1, v_i in zip(r_s, kd_s[0], kd_s[1], v_s)]

    r = [r_s[sub] for _, sub in inst]
    kk = [kk_s[sub] for _, sub in inst]
    v = [v_s[sub] for _, sub in inst]
    ld = [log_decay(wl_s[d][sub], d) for d, sub in inst]
    a = [a_s[d][sub] for d, sub in inst]
    kd = [kd_s[d][sub] for d, sub in inst]

    def cumulative(ld_i, d):
        p_hi = ld_i.astype(BF16)
        rem = ld_i - p_hi.astype(F32)
        p_mid = rem.astype(BF16)
        p_lo = (rem - p_mid.astype(F32)).astype(BF16)
        cs = _dot(tri[d], jnp.concatenate([p_hi, p_mid, p_lo], axis=1))
        return cs[:, :w] + cs[:, w:2 * w] + cs[:, 2 * w:]

    lam = _each(cumulative, ld, dirs)
    lam_end = _each(lambda l, d: l[0:1, :] if d else l[c - 1:c, :], lam, dirs)
    g_cum = _each(jnp.exp, lam)
    g_inv = _each(lambda l: jnp.exp(-l), lam)
    g_prev = _each(lambda l, x: jnp.exp(l - x), lam, ld)
    g_rel = _each(lambda le, l: jnp.exp(le - l), lam_end, lam)
    g_end = _each(jnp.exp, lam_end)

    qk = _each(jnp.multiply, kk, g_prev)
    rt = _each(jnp.multiply, r, g_cum)
    beta = _each(jnp.multiply, kk, a)
    bt = _each(jnp.multiply, beta, g_inv)
    kt = _each(jnp.multiply, kd, g_inv)
    bh = _each(jnp.multiply, beta, g_rel)
    kh = _each(jnp.multiply, kd, g_rel)

    def big_product(qk_i, rt_i, bt_i, kt_i):
        rhs = jnp.concatenate([jnp.where(head0, bt_i, 0.0), jnp.where(head0, 0.0, bt_i),
                               jnp.where(head0, kt_i, 0.0), jnp.where(head0, 0.0, kt_i)], axis=0).astype(BF16)
        return _dot_nt(jnp.concatenate([qk_i, rt_i], axis=0).astype(BF16), rhs)

    big = _each(big_product, qk, rt, bt, kt)
    n_pow = _each(lambda b, d: jnp.where(strict[d], -b[:c, :w], 0.0), big, dirs)
    a_k = _each(lambda b, d: jnp.where(strict[d], b[:c, w:], 0.0), big, dirs)
    g_b = _each(lambda b, d: jnp.where(incl[d], b[c:, :w], 0.0), big, dirs)
    g_k = _each(lambda b, d: jnp.where(incl[d], b[c:, w:], 0.0), big, dirs)

    t_inv = _each(lambda n: jnp.where(eye, 1.0, 0.0) + n, n_pow)
    for _ in range(int(math.log2(c)) - 1):
        n_pow = _each(lambda n: pm(n, bd(n)), n_pow)
        t_inv = _each(lambda t, n: t + pm(t, bd(n)), t_inv, n_pow)

    v_bd = _each(bd, v)
    x1 = _each(pm, a_k, v_bd)
    uw = _each(lambda t, x, q: _dot(t.astype(BF16), jnp.concatenate([bd(x), bd(q)], axis=1)), t_inv, x1, qk)
    u0 = _each(lambda x: -x[:, :w], uw)
    wm = _each(lambda x: x[:, w:], uw)
    y0 = _each(lambda gk, vb, gb, u: pm(gk, vb) + pm(gb, bd(u)), g_k, v_bd, g_b, u0)
    rm = _each(lambda rt_i, gb, wm_i: rt_i - pm(gb, bd(wm_i)), rt, g_b, wm)
    d0 = _each(lambda kh_i, bh_i, v_i, u: jnp.where(bdmask, _dot_tn(
        jnp.concatenate([kh_i, bh_i], axis=0).astype(BF16), jnp.concatenate([v_i, u], axis=0).astype(BF16)), 0.0),
        kh, bh, v, u0)
    mm = _each(lambda bh_i, wm_i, ge: jnp.where(eye2, ge, 0.0) - jnp.where(
        bdmask, _dot_tn(bh_i.astype(BF16), wm_i.astype(BF16)), 0.0), bh, wm, g_end)
    d0 = _each(lambda x: x[:hw] + x[hw:], d0)
    mm = _each(lambda x: x[:hw] + x[hw:], mm)
    mm_hi = _each(lambda m: m.astype(BF16), mm)
    mm_lo = _each(lambda m, mh: (m - mh.astype(F32)).astype(BF16), mm, mm_hi)

    for sub in range(SCAN_SUB):
        bonus_ref[sub_rows[sub], :] = bonus[sub]
    for i, (d, sub) in enumerate(inst):
        y0_ref[d, rows[i], :] = y0[i]
        rm_ref[d, sub, 0, 0:c, :] = rm[i].astype(BF16)
        rm_ref[d, sub, 0, c:c + hw, :] = mm_hi[i]
        rm_ref[d, sub, 0, c + hw:c + 2 * hw, :] = mm_lo[i]
        d0_ref[d, sub, 0, :, :] = d0[i]


def scan_prepare(r, k, v, hid_w, hid_a, w2_w, w2_a, par):
    rows, d = r.shape
    c, w = SCAN_CHUNK, SCAN_PAIR_W
    n_pairs = d // w
    n_chunks = rows // c
    blk = SCAN_SUB * c
    rank = hid_w.shape[1]
    shared = pl.BlockSpec((blk, w), lambda i, p: (i, p))
    hidden = pl.BlockSpec((blk, rank), lambda i, p: (i, 0))
    fwd_w2 = pl.BlockSpec((rank, w), lambda i, p: (0, p))
    rev_w2 = pl.BlockSpec((rank, w), lambda i, p: (0, n_pairs + p))
    return pl.pallas_call(
        _scan_prepare_kernel,
        grid=(rows // blk, n_pairs),
        in_specs=[shared, shared, shared, hidden, hidden, fwd_w2, rev_w2, fwd_w2, rev_w2,
                  pl.BlockSpec((8, w), lambda i, p: (0, p))],
        out_specs=[pl.BlockSpec((2, blk, w), lambda i, p: (0, i, p)),
                   pl.BlockSpec((2, SCAN_SUB, 1, c + w, w), lambda i, p: (0, i, p, 0, 0)),
                   pl.BlockSpec((2, SCAN_SUB, 1, w // 2, w), lambda i, p: (0, i, p, 0, 0)),
                   shared],
        out_shape=[jax.ShapeDtypeStruct((2, rows, d), F32),
                   jax.ShapeDtypeStruct((2, n_chunks, n_pairs, c + w, w), BF16),
                   jax.ShapeDtypeStruct((2, n_chunks, n_pairs, w // 2, w), F32),
                   jax.ShapeDtypeStruct((rows, d), F32)],
        compiler_params=_params(("parallel", "parallel")),
        name="scan_prepare",
    )(r, k, v, hid_w, hid_a, w2_w, w2_w, w2_a, w2_a, par)


def _scan_apply_kernel(y0_ref, rm_ref, d0_ref, y_ref, z_ref, *, n_pairs):
    c, w = SCAN_CHUNK, SCAN_PAIR_W
    i = pl.program_id(2)

    @pl.when(i == 0)
    def _():
        z_ref[...] = jnp.zeros_like(z_ref)

    hw = RWKV_HEAD
    row2 = lax.broadcasted_iota(jnp.int32, (w, w), 0)
    lane2 = lax.broadcasted_iota(jnp.int32, (w, w), 1)
    bdmask = (row2 < hw) == (lane2 < hw)

    def bd(x):
        return jnp.where(bdmask, jnp.concatenate([x, x], axis=0), 0.0)

    z = [bd(z_ref[p]) for p in range(n_pairs)]
    z_hi = _each(lambda x: x.astype(BF16), z)
    z_lo = _each(lambda x, xh: (x - xh.astype(F32)).astype(BF16), z, z_hi)
    res = [_dot(rm_ref[p], jnp.concatenate([z_hi[p], z_lo[p]], axis=1)) for p in range(n_pairs)]
    res = _each(lambda x: x[:, :w] + x[:, w:], res)
    y = [y0_ref[:, p * w:(p + 1) * w] + res[p][:c] for p in range(n_pairs)]
    z_new = [d0_ref[p] + res[p][c:c + hw] + res[p][c + hw:] for p in range(n_pairs)]
    for p in range(n_pairs):
        y_ref[:, p * w:(p + 1) * w] = y[p]
        z_ref[p] = z_new[p]


def _scan_chunk_index(i, d):
    n_ctx = CTX_LEN // SCAN_CHUNK
    n_all = S_ALL // SCAN_CHUNK
    return jnp.where(d == 0, i, jnp.where(i < n_ctx, n_ctx - 1 - i, n_all + n_ctx - 1 - i))


def scan_apply(y0, rm, d0, batch):
    _, rows, d = y0.shape
    c, w = SCAN_CHUNK, SCAN_PAIR_W
    n_pairs = d // w
    per_b = rows // batch // c

    def idx(b, i, dr):
        return b * per_b + _scan_chunk_index(i, dr)

    kernel = functools.partial(_scan_apply_kernel, n_pairs=n_pairs)
    return pl.pallas_call(
        kernel,
        grid=(2, batch, per_b),
        in_specs=[pl.BlockSpec((None, c, d), lambda dr, b, i: (dr, idx(b, i, dr), 0)),
                  pl.BlockSpec((None, None, n_pairs, c + w, w), lambda dr, b, i: (dr, idx(b, i, dr), 0, 0, 0)),
                  pl.BlockSpec((None, None, n_pairs, w // 2, w), lambda dr, b, i: (dr, idx(b, i, dr), 0, 0, 0))],
        out_specs=pl.BlockSpec((None, c, d), lambda dr, b, i: (dr, idx(b, i, dr), 0)),
        out_shape=jax.ShapeDtypeStruct((2, rows, d), F32),
        scratch_shapes=[pltpu.VMEM((n_pairs, w // 2, w), F32)],
        compiler_params=_params(("parallel", "parallel", "arbitrary")),
        name="scan_apply",
    )(y0, rm, d0)


def _shift_mix_kernel(h_ref, hp_ref, hn_ref, mix_ref, *out_refs):
    per_b = S_ALL // ROW_TILE
    pos = pl.program_id(0) % per_b
    h = h_ref[...]
    row = lax.broadcasted_iota(jnp.int32, h.shape, 0)
    starts_seq = (pos == 0) | (pos == 1)
    ends_seq = (pos == 0) | (pos == per_b - 1)
    prev_row = jnp.where(starts_seq, 0.0, hp_ref[7:8, :])
    next_row = jnp.where(ends_seq, 0.0, hn_ref[0:1, :])
    prev = jnp.where(row == 0, prev_row, pltpu.roll(h, 1, 0))
    nxt = jnp.where(row == ROW_TILE - 1, next_row, pltpu.roll(h, ROW_TILE - 1, 0))
    xx = 0.5 * (prev + nxt) - h
    for j, o_ref in enumerate(out_refs):
        o_ref[...] = (h + xx * mix_ref[j:j + 1, :]).astype(o_ref.dtype)


def shift_mix(h, x_mix):
    rows, d = h.shape
    n_mix = x_mix.shape[0]
    sub = ROW_TILE // 8
    tile = pl.BlockSpec((ROW_TILE, d), lambda i: (i, 0))
    return pl.pallas_call(
        _shift_mix_kernel,
        grid=(rows // ROW_TILE,),
        in_specs=[tile,
                  pl.BlockSpec((8, d), lambda i: (jnp.maximum(i * sub - 1, 0), 0)),
                  pl.BlockSpec((8, d), lambda i: (jnp.minimum((i + 1) * sub, rows // 8 - 1), 0)),
                  pl.BlockSpec((n_mix, d), lambda i: (0, 0))],
        out_specs=[tile] * n_mix,
        out_shape=[jax.ShapeDtypeStruct((rows, d), BF16)] * n_mix,
        compiler_params=_params(("parallel",)),
        name="shift_mix",
    )(h, h, h, x_mix)


def _rwkv_out_kernel(y_ref, bonus_ref, g_ref, gn_ref, o_ref):
    w = SCAN_PAIR_W
    for s in range(o_ref.shape[-1] // w):
        cols = slice(s * w, (s + 1) * w)
        y = y_ref[0, :, cols] + y_ref[1, :, cols]
        mu = _head_sum(y) * (1.0 / RWKV_HEAD)
        dev = y - mu
        var = _head_sum(dev * dev) * (1.0 / RWKV_HEAD)
        yn = dev * lax.rsqrt(var + GN_EPS) * gn_ref[0:1, cols] + gn_ref[1:2, cols]
        o_ref[:, cols] = ((yn + bonus_ref[:, cols]) * g_ref[:, cols]).astype(o_ref.dtype)


def rwkv_out(y, bonus, g, gn_g, gn_b, tn=512):
    _, rows, d = y.shape
    tile = pl.BlockSpec((ROW_TILE, tn), lambda i, j: (i, j))
    return pl.pallas_call(
        _rwkv_out_kernel,
        grid=(rows // ROW_TILE, d // tn),
        in_specs=[pl.BlockSpec((2, ROW_TILE, tn), lambda i, j: (0, i, j)), tile, tile,
                  pl.BlockSpec((2, tn), lambda i, j: (0, j))],
        out_specs=tile,
        out_shape=jax.ShapeDtypeStruct((rows, d), BF16),
        compiler_params=_params(("parallel", "parallel")),
        name="rwkv_out",
    )(y, bonus, g, jnp.stack([gn_g, gn_b]))


def _route_topk(logits):
    lane = lax.broadcasted_iota(jnp.int32, logits.shape, 1)
    far = 4 * HEAD_W

    def first_max(vals):
        top = jnp.max(vals, axis=-1, keepdims=True)
        return top, jnp.min(jnp.where(vals == top, lane, far), axis=-1, keepdims=True)

    is_group = lane < N_GROUPS
    g_top, g_sel = first_max(jnp.where(is_group, logits, NEG_INF))
    p_sel = 1.0 / jnp.sum(jnp.where(is_group, jnp.exp(logits - g_top), 0.0), axis=-1, keepdims=True)
    lo = N_GROUPS + g_sel * EXPERTS_PER_GROUP
    le = jnp.where((lane >= lo) & (lane < lo + EXPERTS_PER_GROUP), logits, NEG_INF)
    v1, i1 = first_max(le)
    v2, i2 = first_max(jnp.where(lane == i1, NEG_INF, le))
    e2 = jnp.exp(v2 - v1)
    w1 = p_sel / (1.0 + e2)
    w2 = p_sel * e2 / (1.0 + e2)
    out = jnp.where(lane == 0, (i1 - N_GROUPS).astype(F32), 0.0)
    out = jnp.where(lane == 1, (i2 - N_GROUPS).astype(F32), out)
    out = jnp.where(lane == 2, w1, out)
    return jnp.where(lane == 3, w2, out)


def moe_dispatch(route):
    t = route.shape[0]
    n_pairs = TOP_K * t
    tm = MOE_TILE
    n_slots = n_pairs + N_EXPERTS * tm
    n_tiles = n_slots // tm
    e_flat = route[:, :TOP_K].astype(jnp.int32).reshape(-1)
    gate_bits = lax.bitcast_convert_type(route[:, TOP_K:2 * TOP_K], jnp.int32).reshape(-1)
    onehot = (e_flat[:, None] == jnp.arange(N_EXPERTS, dtype=jnp.int32)[None, :]).astype(jnp.int32)
    csum = jnp.cumsum(onehot, axis=0)
    counts = csum[-1]
    padded = ((counts + tm - 1) // tm) * tm
    pend = jnp.cumsum(padded)
    pstart = pend - padded
    pos = jnp.sum(onehot * (pstart[None, :] + csum - onehot), axis=1)
    pair = jnp.arange(n_pairs, dtype=jnp.int32)
    slots = jnp.full((n_slots, 2), -1, jnp.int32).at[pos].set(jnp.stack([pair, gate_bits], axis=1))
    valid = slots[:, 0] >= 0
    tok = jnp.where(valid, slots[:, 0] // TOP_K, 0)
    dst = jnp.where(valid, (slots[:, 0] % TOP_K) * t + slots[:, 0] // TOP_K,
                    n_pairs + jnp.arange(n_slots, dtype=jnp.int32) % tm)
    gate = jnp.where(valid, lax.bitcast_convert_type(slots[:, 1], F32), 0.0)
    tile_start = jnp.arange(n_tiles, dtype=jnp.int32) * tm
    tile_e = jnp.sum((tile_start[:, None] >= pend[None, :]).astype(jnp.int32), axis=1)
    tile_e = jnp.minimum(tile_e, N_EXPERTS - 1)
    n_live = (pend[-1] // tm).reshape(1)
    return (tok.reshape(n_tiles, 1, tm), dst.reshape(n_tiles, 1, tm), gate.reshape(n_slots, 1),
            tile_e.astype(jnp.int32), n_live.astype(jnp.int32))


def _moe_kernel(te_ref, nl_ref, tok_ref, tokn_ref, dst_ref, dstp_ref, gate_ref, h_hbm, w1_ref, w3_ref, w2_ref,
                y_hbm, xbuf, obuf, w1b, w3b, w2b, sem_in, sem_out):
    i = pl.program_id(0)
    n_live = nl_ref[0]
    tm = MOE_TILE
    slot = i % 2
    other = 1 - slot

    def gather_copy(src_row, r, s):
        return pltpu.make_async_copy(h_hbm.at[pl.ds(src_row, 1), :], xbuf.at[s, pl.ds(r, 1), :], sem_in.at[s])

    def scatter_copy(r, dst_row, s):
        return pltpu.make_async_copy(obuf.at[s, pl.ds(r, 1), :], y_hbm.at[pl.ds(dst_row, 1), :], sem_out.at[s])

    def wait_gather(s):
        pltpu.make_async_copy(h_hbm.at[pl.ds(0, tm), :], xbuf.at[s], sem_in.at[s]).wait()

    def wait_scatter(s):
        pltpu.make_async_copy(obuf.at[s], y_hbm.at[pl.ds(0, tm), :], sem_out.at[s]).wait()

    def per_row(fn):
        def body(r, carry):
            fn(r)
            return carry
        lax.fori_loop(0, tm, body, 0, unroll=MOE_DMA_UNROLL)

    def expert_tile(scatter_previous):
        x = xbuf[slot].astype(BF16)
        for r in range(tm):
            gather_copy(tokn_ref[0, r], r, other).start()
        if scatter_previous:
            for r in range(tm):
                scatter_copy(r, dstp_ref[0, r], other).start()
        up = _dot(x, w1b[...])
        hid = (up * jax.nn.sigmoid(up)) * _dot(x, w3b[...]) * gate_ref[...]
        return _dot(hid.astype(BF16), w2b[...])

    @pl.when(i == 0)
    def _():
        per_row(lambda r: gather_copy(tok_ref[0, r], r, 0).start())

    @pl.when(i < n_live)
    def _():
        wait_gather(slot)
        prev_e = te_ref[jnp.maximum(i - 1, 0)]

        @pl.when((i == 0) | (te_ref[i] != prev_e))
        def _():
            w1b[...] = w1_ref[...].astype(BF16)
            w3b[...] = w3_ref[...].astype(BF16)
            w2b[...] = w2_ref[...].astype(BF16)

        @pl.when(i == 0)
        def _():
            obuf[0] = expert_tile(False)
            first_spare = y_hbm.shape[0] - tm
            per_row(lambda r: scatter_copy(r, first_spare + r, 0).start())
            wait_scatter(0)

        @pl.when(i > 0)
        def _():
            out = expert_tile(True)

            @pl.when(i > 1)
            def _():
                wait_scatter(slot)

            obuf[slot] = out

        @pl.when(i == n_live - 1)
        def _():
            per_row(lambda r: scatter_copy(r, dst_ref[0, r], slot).start())
            wait_gather(other)

            @pl.when(i > 0)
            def _():
                wait_scatter(other)

            wait_scatter(slot)


def moe_experts(h, tok, dst, gate, tile_e, n_live, w1, w3, w2, layer):
    t, d = h.shape
    f = w1.shape[-1]
    tm = MOE_TILE
    n_tiles = tok.shape[0]

    def w_index(i, te, nl):
        return (layer, te[i] // EXPERTS_PER_GROUP, te[i] % EXPERTS_PER_GROUP, 0, 0)

    grid_spec = pltpu.PrefetchScalarGridSpec(
        num_scalar_prefetch=2,
        grid=(n_tiles,),
        in_specs=[pl.BlockSpec((None, 1, tm), lambda i, te, nl: (i, 0, 0), memory_space=pltpu.SMEM),
                  pl.BlockSpec((None, 1, tm), lambda i, te, nl: (jnp.minimum(i + 1, n_tiles - 1), 0, 0),
                               memory_space=pltpu.SMEM),
                  pl.BlockSpec((None, 1, tm), lambda i, te, nl: (i, 0, 0), memory_space=pltpu.SMEM),
                  pl.BlockSpec((None, 1, tm), lambda i, te, nl: (jnp.maximum(i - 1, 0), 0, 0),
                               memory_space=pltpu.SMEM),
                  pl.BlockSpec((tm, 1), lambda i, te, nl: (i, 0)),
                  pl.BlockSpec(memory_space=pl.ANY),
                  pl.BlockSpec((None, None, None, d, f), w_index),
                  pl.BlockSpec((None, None, None, d, f), w_index),
                  pl.BlockSpec((None, None, None, f, d), w_index)],
        out_specs=pl.BlockSpec(memory_space=pl.ANY),
        scratch_shapes=[pltpu.VMEM((2, tm, d), F32), pltpu.VMEM((2, tm, d), F32),
                        pltpu.VMEM((d, f), BF16), pltpu.VMEM((d, f), BF16), pltpu.VMEM((f, d), BF16),
                        pltpu.SemaphoreType.DMA((2,)), pltpu.SemaphoreType.DMA((2,))])
    return pl.pallas_call(
        _moe_kernel,
        grid_spec=grid_spec,
        out_shape=jax.ShapeDtypeStruct((TOP_K * t + tm, d), F32),
        compiler_params=_params(("arbitrary",)),
        name="moe_experts",
    )(tile_e, n_live, tok, tok, dst, dst, gate, h, w1, w3, w2)


def _mod_spec(k):
    per_b = S_ALL // ROW_TILE
    return pl.BlockSpec((None, 1, D_MODEL),
                        lambda i: ((i // per_b * 2 + jnp.minimum(i % per_b, 1)) * N_MOD + k, 0, 0))


def _norm_mod(x, g_ref, sh_ref, sc_ref):
    h = x * lax.rsqrt(jnp.mean(x * x, axis=-1, keepdims=True) + NORM_EPS) * g_ref[...]
    if sh_ref is None:
        return h
    return h * (1.0 + sc_ref[...]) + sh_ref[...]


def _first_norm_kernel(x_ref, g_ref, sh_ref, sc_ref, h_ref):
    h_ref[...] = _norm_mod(x_ref[...], g_ref, sh_ref, sc_ref).astype(h_ref.dtype)


def first_norm(xs, g, mod, h_dtype):
    rows, d = xs.shape
    tile = pl.BlockSpec((ROW_TILE, d), lambda i: (i, 0))
    return pl.pallas_call(
        _first_norm_kernel,
        grid=(rows // ROW_TILE,),
        in_specs=[tile, pl.BlockSpec((1, d), lambda i: (0, 0)), _mod_spec(0), _mod_spec(1)],
        out_specs=tile,
        out_shape=jax.ShapeDtypeStruct((rows, d), h_dtype),
        compiler_params=_params(("parallel",)),
        name="first_norm",
    )(xs, g.reshape(1, d), mod, mod)


def _post_mixer_kernel(x_ref, u_ref, gate_ref, g_ref, sh_ref, sc_ref, wr_ref, br_ref, xo_ref, h_ref, route_ref):
    x = x_ref[...] + gate_ref[...] * u_ref[...]
    xo_ref[...] = x
    h = _norm_mod(x, g_ref, sh_ref, sc_ref)
    h_ref[...] = h
    logits = jnp.dot(h, wr_ref[...], precision=lax.Precision.HIGHEST, preferred_element_type=F32) + br_ref[...]
    route_ref[...] = _route_topk(logits)


def post_mixer(xs, upd, g, mod, router_w, router_b):
    rows, d = xs.shape
    tile = pl.BlockSpec((ROW_TILE, d), lambda i: (i, 0))
    rtile = pl.BlockSpec((ROW_TILE, HEAD_W), lambda i: (i, 0))
    return pl.pallas_call(
        _post_mixer_kernel,
        grid=(rows // ROW_TILE,),
        in_specs=[tile, tile, _mod_spec(2), pl.BlockSpec((1, d), lambda i: (0, 0)), _mod_spec(3), _mod_spec(4),
                  pl.BlockSpec((d, HEAD_W), lambda i: (0, 0)), pl.BlockSpec((1, HEAD_W), lambda i: (0, 0))],
        out_specs=[tile, tile, rtile],
        out_shape=[jax.ShapeDtypeStruct((rows, d), F32), jax.ShapeDtypeStruct((rows, d), F32),
                   jax.ShapeDtypeStruct((rows, HEAD_W), F32)],
        compiler_params=_params(("parallel",)),
        name="post_mixer",
    )(xs, upd, mod, g.reshape(1, d), mod, mod, router_w, router_b)


def _post_moe_kernel(x_ref, ya_ref, yb_ref, gate_ref, g_ref, *rest, modulate):
    if modulate:
        sh_ref, sc_ref, xo_ref, h_ref = rest
    else:
        sh_ref = sc_ref = None
        xo_ref, h_ref = rest
    x = x_ref[...] + gate_ref[...] * (ya_ref[...] + yb_ref[...])
    xo_ref[...] = x
    h_ref[...] = _norm_mod(x, g_ref, sh_ref, sc_ref).astype(h_ref.dtype)


def post_moe(xs, y2, g, mod, next_mod, h_dtype):
    rows, d = xs.shape
    tile = pl.BlockSpec((ROW_TILE, d), lambda i: (i, 0))
    second = pl.BlockSpec((ROW_TILE, d), lambda i: (rows // ROW_TILE + i, 0))
    gspec = pl.BlockSpec((1, d), lambda i: (0, 0))
    modulate = next_mod is not None
    in_specs = [tile, tile, second, _mod_spec(5), gspec] + ([_mod_spec(0), _mod_spec(1)] if modulate else [])
    args = (xs, y2, y2, mod, g.reshape(1, d)) + ((next_mod, next_mod) if modulate else ())
    return pl.pallas_call(
        functools.partial(_post_moe_kernel, modulate=modulate),
        grid=(rows // ROW_TILE,),
        in_specs=in_specs,
        out_specs=[tile, tile],
        out_shape=[jax.ShapeDtypeStruct((rows, d), F32), jax.ShapeDtypeStruct((rows, d), h_dtype)],
        compiler_params=_params(("parallel",)),
        name="post_moe",
    )(*args)


def _pad_cols(w, n):
    return jnp.pad(w, ((0, 0), (0, n - w.shape[1])))


def _two_dir_lora(x, w1, w2):
    r = w1.shape[-1]
    w1c = _pad_cols(jnp.concatenate([w1[0], w1[1]], axis=1), 256).astype(BF16)
    d = w2.shape[-1]
    w2bd = jnp.zeros((256, 2 * d), F32).at[:r, :d].set(w2[0]).at[r:2 * r, d:].set(w2[1]).astype(BF16)
    return matmul(x, w1c, F32, tn=256), w2bd


def rwkv_mixer(h, p, batch):
    mixes = shift_mix(h, p['x_mix'])
    r = matmul(mixes[0], p['w_rkv'][0].astype(BF16), F32)
    k = matmul(mixes[1], p['w_rkv'][1].astype(BF16), F32)
    v = matmul(mixes[2], p['w_rkv'][2].astype(BF16), F32)
    hid_w, w2bd_w = _two_dir_lora(mixes[3], p['dec_w1'], p['dec_w2'])
    hid_a, w2bd_a = _two_dir_lora(mixes[4], p['iclr_a1'], p['iclr_a2'])
    g_hid = matmul(mixes[5], p['gate_g1'].astype(BF16), F32, tn=256)
    g = matmul(jax.nn.sigmoid(g_hid).astype(BF16), p['gate_g2'].astype(BF16), F32)

    par = jnp.concatenate([p['dec_w0'], p['iclr_a0'], p['k_k'][None], p['k_a'][None], p['r_k'].reshape(1, -1),
                           jnp.zeros((1, D_MODEL), F32)], axis=0)
    y0, rm, d0, bonus = scan_prepare(r, k, v, jnp.tanh(hid_w).astype(BF16), hid_a.astype(BF16), w2bd_w, w2bd_a, par)
    y = scan_apply(y0, rm, d0, batch)
    yo = rwkv_out(y, bonus, g, p['gn_g'], p['gn_b'])
    return matmul(yo, p['w_out'].astype(BF16), F32)


def even_mixer(h, p, layer_idx, cos_t, sin_t, batch):
    rows = batch * S_ALL
    lam_init = 0.8 - 0.6 * math.exp(-0.3 * layer_idx)
    proj = matmul(h, p['w_in'].astype(BF16), BF16, tn=1024)
    proj = proj.reshape(batch, S_ALL, -1)
    lf = p['diff_lambda']
    lam = (jnp.exp(jnp.sum(lf[0] * lf[1])) - jnp.exp(jnp.sum(lf[2] * lf[3])) + lam_init).reshape(1)
    a_out = diff_attention(proj, lam, cos_t, sin_t, p['subln_g'], lam_init)
    b_out = neighbourhood_attention(proj, na_bias_table(p['rpb']))
    mixed = jnp.concatenate([a_out, b_out], axis=-1).reshape(rows, -1)
    return matmul(mixed, p['w_out'].astype(BF16), F32)


def router_table(router_g, router_g_b, router_e, router_e_b):
    w = _pad_cols(jnp.concatenate([router_g, router_e], axis=1), HEAD_W)
    b = _pad_cols(jnp.concatenate([router_g_b, router_e_b])[None, :], HEAD_W)
    return w, b


def kernel(x, c, ctx, c_ctx, ada_w, ada_b, norm_g, final_g, even_w_in, even_w_out, diff_lambda, diff_subln_g, na_rpb, rwkv_x_mix, rwkv_w_rkv, rwkv_w_out, rwkv_dec_w0, rwkv_dec_w1, rwkv_dec_w2, rwkv_iclr_a0, rwkv_iclr_a1, rwkv_iclr_a2, rwkv_gate_g1, rwkv_gate_g2, rwkv_k_k, rwkv_k_a, rwkv_r_k, rwkv_gn_g, rwkv_gn_b, moe_router_g, moe_router_g_b, moe_router_e, moe_router_e_b, moe_w1, moe_w3, moe_w2):
    batch = x.shape[0]
    rows = batch * S_ALL
    xs = jnp.concatenate([ctx, x], axis=1).reshape(rows, D_MODEL)
    cos_t, sin_t = rope_tables()

    cvec = jnp.concatenate([c, c_ctx[None], jnp.zeros((8 - batch - 1, D_MODEL), F32)], axis=0)
    mods = ada_modulation(jax.nn.silu(cvec).astype(BF16), ada_w, ada_b)
    mods = mods.reshape(DEPTH, 8, N_MOD, D_MODEL)

    def mod_table(i):
        mod_l = mods[i, :batch]
        mod_c = jnp.broadcast_to(mods[i, batch][None], mod_l.shape)
        return jnp.stack([mod_c, mod_l], axis=1).reshape(batch * 2 * N_MOD, 1, D_MODEL)

    mod = mod_table(0)
    h = first_norm(xs, norm_g[0, 0], mod, BF16)
    for i in range(DEPTH):
        j = i // 2
        if i % 2 == 0:
            p = dict(w_in=even_w_in[j], w_out=even_w_out[j], diff_lambda=diff_lambda[j],
                     subln_g=diff_subln_g[j], rpb=na_rpb[j])
            out = even_mixer(h, p, i, cos_t, sin_t, batch)
        else:
            p = dict(x_mix=rwkv_x_mix[j], w_rkv=rwkv_w_rkv[j], w_out=rwkv_w_out[j], dec_w0=rwkv_dec_w0[j],
                     dec_w1=rwkv_dec_w1[j], dec_w2=rwkv_dec_w2[j], iclr_a0=rwkv_iclr_a0[j],
                     iclr_a1=rwkv_iclr_a1[j], iclr_a2=rwkv_iclr_a2[j], gate_g1=rwkv_gate_g1[j],
                     gate_g2=rwkv_gate_g2[j], k_k=rwkv_k_k[j], k_a=rwkv_k_a[j], r_k=rwkv_r_k[j],
                     gn_g=rwkv_gn_g[j], gn_b=rwkv_gn_b[j])
            out = rwkv_mixer(h, p, batch)
        rw, rb = router_table(moe_router_g[i], moe_router_g_b[i], moe_router_e[i], moe_router_e_b[i])
        xs, h2, route = post_mixer(xs, out, norm_g[i, 1], mod, rw, rb)
        tok, dst, gate, tile_e, n_live = moe_dispatch(route)
        y2 = moe_experts(h2, tok, dst, gate, tile_e, n_live, moe_w1, moe_w3, moe_w2, i)
        if i + 1 < DEPTH:
            next_mod = mod_table(i + 1)
            xs, h = post_moe(xs, y2, norm_g[i + 1, 0], mod, next_mod, F32 if (i + 1) % 2 else BF16)
            mod = next_mod
        else:
            _, h = post_moe(xs, y2, final_g, mod, None, F32)
    return h.reshape(batch, S_ALL, D_MODEL)[:, CTX_LEN:]
```

```python
import functools
import math

import numpy as np
import jax
import jax.numpy as jnp
from jax import lax
from jax.experimental import pallas as pl
from jax.experimental.pallas import tpu as pltpu

F32 = jnp.float32
BF16 = jnp.bfloat16

D_MODEL = 2048
DEPTH = 4
GRID_W = 64
CTX_LEN = 256
SEQ = 4096
S_ALL = CTX_LEN + SEQ
N_MOD = 6
NORM_EPS = 1e-6
NEG_INF = -1e30

DIFF_HEADS = 8
DIFF_QK_DIM = 64
NA_HEADS = 8
NA_DIM = 128
NA_KH = 8
NA_KW = 16
ROPE_THETA = 10000.0
SUBLN_EPS = 1e-5
HEAD_W = 128
ATT_TILE = 256
DIFF_KV_BLOCK = 1024
NA_ROWS_PER_STEP = 4
NA_WIN_ROWS = 12

RWKV_HEAD = 64
RWKV_HEADS = D_MODEL // RWKV_HEAD
GN_EPS = 64e-5
SCAN_CHUNK = 64
SCAN_PAIR_W = 2 * RWKV_HEAD
SCAN_SUB = 8

N_GROUPS = 4
EXPERTS_PER_GROUP = 8
N_EXPERTS = N_GROUPS * EXPERTS_PER_GROUP
TOP_K = 2
EXPERT_FF = 512
MOE_TILE = 256
MOE_DMA_UNROLL = 8
ROW_TILE = 256

VMEM_LIMIT = 52 * 1024 * 1024


def _params(sem):
    return pltpu.CompilerParams(dimension_semantics=sem, vmem_limit_bytes=VMEM_LIMIT)


def _dot(a, b):
    return jnp.dot(a, b, preferred_element_type=F32)


def _dot_nt(a, b):
    return lax.dot_general(a, b, (((1,), (1,)), ((), ())), preferred_element_type=F32)


def _dot_tn(a, b):
    return lax.dot_general(a, b, (((0,), (0,)), ((), ())), preferred_element_type=F32)


def _mm_kernel(a_ref, w_ref, o_ref):
    o_ref[...] = _dot(a_ref[...], w_ref[...]).astype(o_ref.dtype)


def matmul(a, w, out_dtype, tm=1024, tn=512):
    m, k = a.shape
    n = w.shape[1]
    while m % tm:
        tm //= 2
    tn = min(tn, n)
    assert n % tn == 0
    return pl.pallas_call(
        _mm_kernel,
        grid=(m // tm, n // tn),
        in_specs=[pl.BlockSpec((tm, k), lambda i, j: (i, 0)),
                  pl.BlockSpec((k, tn), lambda i, j: (0, j))],
        out_specs=pl.BlockSpec((tm, tn), lambda i, j: (i, j)),
        out_shape=jax.ShapeDtypeStruct((m, n), out_dtype),
        compiler_params=_params(("parallel", "parallel")),
        name="matmul",
    )(a, w)


def _ada_kernel(s_ref, w_ref, b_ref, o_ref):
    o_ref[...] = _dot(s_ref[...], w_ref[...].astype(BF16)) + b_ref[...]


def ada_modulation(svec, ada_w, ada_b, tn=1024):
    nl, d, n = ada_w.shape
    rows = svec.shape[0]
    return pl.pallas_call(
        _ada_kernel,
        grid=(nl, n // tn),
        in_specs=[pl.BlockSpec((rows, d), lambda l, j: (0, 0)),
                  pl.BlockSpec((None, d, tn), lambda l, j: (l, 0, j)),
                  pl.BlockSpec((None, 1, tn), lambda l, j: (l, 0, j))],
        out_specs=pl.BlockSpec((None, rows, tn), lambda l, j: (l, 0, j)),
        out_shape=jax.ShapeDtypeStruct((nl, rows, n), F32),
        compiler_params=_params(("parallel", "parallel")),
        name="ada_modulation",
    )(svec, ada_w, ada_b.reshape(nl, 1, n))


def _rope(x, cos, sin_signed):
    lane = lax.broadcasted_iota(jnp.int32, x.shape, 1)
    first_half = (lane & 63) < 32
    partner = jnp.where(first_half, pltpu.roll(x, HEAD_W - 32, 1), pltpu.roll(x, 32, 1))
    return x * cos + partner * sin_signed


def _diff_attn_kernel(lam_ref, q_ref, k_ref, v_ref, cos_ref, sin_ref, g_ref, o_ref, kr_ref, vt_ref, sa_ref, sb_ref, *,
                      post_scale):
    j = pl.program_id(2)
    tq = ATT_TILE

    @pl.when(j == 0)
    def _():
        def prep_chunk(c, carry):
            rows = pl.ds(pl.multiple_of(c * ATT_TILE, ATT_TILE), ATT_TILE)
            kr_ref[rows, :] = _rope(k_ref[rows, :].astype(F32), cos_ref[rows, :], sin_ref[rows, :]).astype(BF16)
            vt_ref[:, rows] = v_ref[rows, :].astype(F32).T.astype(BF16)
            return carry
        lax.fori_loop(0, S_ALL // ATT_TILE, prep_chunk, 0)

    qrows = pl.ds(pl.multiple_of(j * tq, tq), tq)
    q = _rope(q_ref[...].astype(F32), cos_ref[qrows, :], sin_ref[qrows, :]) * (DIFF_QK_DIM ** -0.5)
    lane = lax.broadcasted_iota(jnp.int32, q.shape, 1)
    q1 = jnp.where(lane < DIFF_QK_DIM, q, 0.0).astype(BF16)
    q2 = jnp.where(lane < DIFF_QK_DIM, 0.0, q).astype(BF16)

    kb = DIFF_KV_BLOCK
    n_latent = SEQ // kb
    maps = (q1, q2)

    def score_stage(start, size, sbuf):
        kc = kr_ref[pl.ds(start, size), :]
        tops = []
        for i, qm in enumerate(maps):
            s = _dot_nt(kc, qm)
            sbuf[i, 0:size, :] = s
            tops.append(jnp.max(s, axis=0, keepdims=True))
        return tuple(tops)

    def softmax_stage(start, size, sbuf, tops, state):
        vt = vt_ref[:, pl.ds(start, size)]
        new = []
        for i in range(2):
            m, l, acc = state[3 * i:3 * i + 3]
            m_new = jnp.maximum(m, tops[i])
            alpha = jnp.exp(m - m_new)
            p = jnp.exp(sbuf[i, 0:size, :] - m_new)
            new += [m_new, alpha * l + jnp.sum(p, axis=0, keepdims=True),
                    alpha * acc + _dot(vt, p.astype(BF16))]
        return tuple(new)

    def latent_start(n):
        return pl.multiple_of(CTX_LEN + jnp.minimum(n, n_latent - 1) * kb, CTX_LEN)

    row = jnp.full((1, tq), NEG_INF, F32)
    zrow = jnp.zeros((1, tq), F32)
    zacc = jnp.zeros((HEAD_W, tq), F32)
    state = softmax_stage(0, CTX_LEN, sa_ref, score_stage(0, CTX_LEN, sa_ref), (row, zrow, zacc, row, zrow, zacc))
    tops = score_stage(latent_start(0), kb, sb_ref)

    def block_pair(t, carry):
        st, tp = carry[:6], carry[6:]
        n = 2 * t
        tp_next = score_stage(latent_start(n + 1), kb, sa_ref)
        st = softmax_stage(latent_start(n), kb, sb_ref, tp, st)
        tp_last = score_stage(latent_start(n + 2), kb, sb_ref)
        st = softmax_stage(latent_start(n + 1), kb, sa_ref, tp_next, st)
        return st + tp_last

    n_pairs = jnp.where(j == 0, 0, n_latent // 2)
    m1, l1, a1, m2, l2, a2 = lax.fori_loop(0, n_pairs, block_pair, state + tops)[:6]
    out = a1 / l1 - lam_ref[0] * (a2 / l2)
    ms = jnp.mean(out * out, axis=0, keepdims=True)
    y = out * lax.rsqrt(ms + SUBLN_EPS) * (g_ref[...] * post_scale)
    o_ref[...] = y.T.astype(o_ref.dtype)


def diff_attention(proj, lam, cos_t, sin_t, subln_g, lam_init):
    b = proj.shape[0]
    kernel = functools.partial(_diff_attn_kernel, post_scale=1.0 - lam_init)
    return pl.pallas_call(
        kernel,
        grid=(b, DIFF_HEADS, S_ALL // ATT_TILE),
        in_specs=[pl.BlockSpec(memory_space=pltpu.SMEM),
                  pl.BlockSpec((None, ATT_TILE, HEAD_W), lambda bi, h, j: (bi, j, h)),
                  pl.BlockSpec((None, S_ALL, HEAD_W), lambda bi, h, j: (bi, 0, DIFF_HEADS + h)),
                  pl.BlockSpec((None, S_ALL, HEAD_W), lambda bi, h, j: (bi, 0, 2 * DIFF_HEADS + h)),
                  pl.BlockSpec((S_ALL, HEAD_W), lambda bi, h, j: (0, 0)),
                  pl.BlockSpec((S_ALL, HEAD_W), lambda bi, h, j: (0, 0)),
                  pl.BlockSpec((HEAD_W, 1), lambda bi, h, j: (0, 0))],
        out_specs=pl.BlockSpec((None, ATT_TILE, HEAD_W), lambda bi, h, j: (bi, j, h)),
        out_shape=jax.ShapeDtypeStruct((b, S_ALL, DIFF_HEADS * HEAD_W), BF16),
        scratch_shapes=[pltpu.VMEM((S_ALL, HEAD_W), BF16), pltpu.VMEM((HEAD_W, S_ALL), BF16),
                        pltpu.VMEM((2, DIFF_KV_BLOCK, ATT_TILE), F32), pltpu.VMEM((2, DIFF_KV_BLOCK, ATT_TILE), F32)],
        compiler_params=_params(("parallel", "parallel", "arbitrary")),
        name="diff_attention",
    )(lam, proj, proj, proj, cos_t, sin_t, subln_g.reshape(HEAD_W, 1))


def rope_tables():
    n_freq = DIFF_QK_DIM // 4
    inv_freq = ROPE_THETA ** (-jnp.arange(n_freq, dtype=F32) / n_freq)
    t = jnp.arange(SEQ, dtype=jnp.int32)
    row = (t // GRID_W).astype(F32)
    col = (t % GRID_W).astype(F32)
    ang = jnp.concatenate([row[:, None] * inv_freq, col[:, None] * inv_freq], axis=-1)
    cos, sin = jnp.cos(ang), jnp.sin(ang)
    cos_l = jnp.concatenate([cos, cos, cos, cos], axis=-1)
    sin_l = jnp.concatenate([-sin, sin, -sin, sin], axis=-1)
    cos_all = jnp.concatenate([jnp.ones((CTX_LEN, HEAD_W), F32), cos_l], axis=0)
    sin_all = jnp.concatenate([jnp.zeros((CTX_LEN, HEAD_W), F32), sin_l], axis=0)
    return cos_all, sin_all


def _na_window_start(j):
    g = j - 1
    return jnp.clip(NA_ROWS_PER_STEP * g - NA_KH // 2, 0, SEQ // GRID_W - NA_WIN_ROWS)


def _na_kernel(q_ref, k_ref, v_ref, bias_ref, o_ref):
    j = pl.program_id(2)
    win = NA_WIN_ROWS * GRID_W
    start = pl.multiple_of(CTX_LEN + _na_window_start(j) * GRID_W, GRID_W)
    scale = NA_DIM ** -0.5
    q = q_ref[...]
    s_c = _dot_nt(q, k_ref[pl.ds(0, CTX_LEN), :]) * scale
    s_w = _dot_nt(q, k_ref[pl.ds(start, win), :]) * scale + bias_ref[...]
    m = jnp.maximum(jnp.max(s_c, axis=-1, keepdims=True), jnp.max(s_w, axis=-1, keepdims=True))
    p_c = jnp.exp(s_c - m)
    p_w = jnp.exp(s_w - m)
    l = jnp.sum(p_c, axis=-1, keepdims=True) + jnp.sum(p_w, axis=-1, keepdims=True)
    o = _dot(p_c.astype(BF16), v_ref[pl.ds(0, CTX_LEN), :]) + _dot(p_w.astype(BF16), v_ref[pl.ds(start, win), :])
    o_ref[...] = (o / l).astype(o_ref.dtype)


def _na_bias_pattern(j):
    n_groups = SEQ // (GRID_W * NA_ROWS_PER_STEP)
    g = j - 1
    return jnp.where(j == 0, 3, jnp.where(g == 0, 0, jnp.where(g == n_groups - 1, 2, 1)))


def na_bias_table(rpb):
    rows = SEQ // GRID_W
    n_groups = rows // NA_ROWS_PER_STEP
    cols = np.arange(GRID_W)
    col_start = np.clip(cols - NA_KW // 2, 0, GRID_W - NA_KW)
    col_mask = (cols[None, :] >= col_start[:, None]) & (cols[None, :] < col_start[:, None] + NA_KW)
    c_idx = np.clip(cols[None, :] - cols[:, None] + NA_KW - 1, 0, 2 * NA_KW - 2)
    pats = []
    for g in (0, 1, n_groups - 1):
        u0 = int(np.clip(NA_ROWS_PER_STEP * g - NA_KH // 2, 0, rows - NA_WIN_ROWS))
        r = NA_ROWS_PER_STEP * g + np.arange(NA_ROWS_PER_STEP)
        r0 = np.clip(r - NA_KH // 2, 0, rows - NA_KH)
        kr = u0 + np.arange(NA_WIN_ROWS)
        valid_r = (kr[None, :] >= r0[:, None]) & (kr[None, :] < r0[:, None] + NA_KH)
        r_idx = np.clip(kr[None, :] - r[:, None] + NA_KH - 1, 0, 2 * NA_KH - 2)
        valid = valid_r[:, None, :, None] & col_mask[None, :, None, :]
        r_sel = np.eye(2 * NA_KH - 1, dtype=np.float32)[r_idx]
        c_sel = np.eye(2 * NA_KW - 1, dtype=np.float32)[c_idx]
        rows_sel = jnp.einsum('qkr,hrc->hqkc', r_sel, rpb.astype(F32), precision=lax.Precision.HIGHEST)
        gathered = jnp.einsum('hqkc,abc->hqakb', rows_sel, c_sel, precision=lax.Precision.HIGHEST)
        pats.append(jnp.where(valid[None], gathered, NEG_INF))
    pats.append(jnp.full_like(pats[0], NEG_INF))
    tab = jnp.stack(pats, axis=1)
    return tab.reshape(NA_HEADS, 4, NA_ROWS_PER_STEP * GRID_W, NA_WIN_ROWS * GRID_W)


def neighbourhood_attention(proj, bias_tab):
    b = proj.shape[0]
    tq = NA_ROWS_PER_STEP * GRID_W
    assert tq == CTX_LEN
    win = NA_WIN_ROWS * GRID_W
    base = 3 * DIFF_HEADS
    return pl.pallas_call(
        _na_kernel,
        grid=(b, NA_HEADS, S_ALL // tq),
        in_specs=[pl.BlockSpec((None, tq, HEAD_W), lambda bi, h, j: (bi, j, base + h)),
                  pl.BlockSpec((None, S_ALL, HEAD_W), lambda bi, h, j: (bi, 0, base + NA_HEADS + h)),
                  pl.BlockSpec((None, S_ALL, HEAD_W), lambda bi, h, j: (bi, 0, base + 2 * NA_HEADS + h)),
                  pl.BlockSpec((None, None, tq, win), lambda bi, h, j: (h, _na_bias_pattern(j), 0, 0))],
        out_specs=pl.BlockSpec((None, tq, HEAD_W), lambda bi, h, j: (bi, j, h)),
        out_shape=jax.ShapeDtypeStruct((b, S_ALL, NA_HEADS * HEAD_W), BF16),
        compiler_params=_params(("parallel", "parallel", "arbitrary")),
        name="neighbourhood_attention",
    )(proj, proj, proj, bias_tab)


def _each(fn, *lists):
    return [fn(*args) for args in zip(*lists)]


def _head_sum(x):
    rows, w = x.shape
    r_i = lax.broadcasted_iota(jnp.int32, (w, w), 0)
    c_i = lax.broadcasted_iota(jnp.int32, (w, w), 1)
    ones_bd = jnp.where((r_i < RWKV_HEAD) == (c_i < RWKV_HEAD), 1.0, 0.0).astype(BF16)
    hi = x.astype(BF16)
    lo = (x - hi.astype(F32)).astype(BF16)
    s = _dot(jnp.concatenate([hi, lo], axis=0), ones_bd)
    return s[:rows] + s[rows:]


def _scan_prepare_kernel(r_ref, k_ref, v_ref, hw_ref, ha_ref, w2w0_ref, w2w1_ref, w2a0_ref, w2a1_ref, par_ref,
                         y0_ref, rm_ref, d0_ref, bonus_ref):
    c = SCAN_CHUNK
    w = SCAN_PAIR_W
    hw = RWKV_HEAD
    t_idx = lax.broadcasted_iota(jnp.int32, (c, w), 0)
    lane = lax.broadcasted_iota(jnp.int32, (c, w), 1)
    s_idx = lane & (hw - 1)
    head0 = lane < hw
    eye = s_idx == t_idx
    tt = lax.broadcasted_iota(jnp.int32, (c, c), 0)
    ss = lax.broadcasted_iota(jnp.int32, (c, c), 1)
    strict = (s_idx < t_idx, s_idx > t_idx)
    incl = (s_idx <= t_idx, s_idx >= t_idx)
    tri = (jnp.where(ss <= tt, 1.0, 0.0).astype(BF16), jnp.where(ss >= tt, 1.0, 0.0).astype(BF16))
    row2 = lax.broadcasted_iota(jnp.int32, (w, w), 0)
    lane2 = lax.broadcasted_iota(jnp.int32, (w, w), 1)
    bdmask = (row2 < hw) == (lane2 < hw)
    eye2 = row2 == lane2

    def bd(y):
        return jnp.where(bdmask, jnp.concatenate([y, y], axis=0), 0.0).astype(BF16)

    def pm(x, ybd):
        return _dot(x.astype(BF16), ybd)

    inst = [(d, sub) for d in range(2) for sub in range(SCAN_SUB)]
    dirs = [d for d, _ in inst]
    rows = [slice(sub * c, (sub + 1) * c) for _, sub in inst]
    par = par_ref[...]
    k_k, k_a, r_k = par[4:5, :], par[5:6, :], par[6:7, :]
    sub_rows = [slice(sub * c, (sub + 1) * c) for sub in range(SCAN_SUB)]
    r_s = [r_ref[rw, :] for rw in sub_rows]
    k_s = [k_ref[rw, :] for rw in sub_rows]
    v_s = [v_ref[rw, :] for rw in sub_rows]
    hid_w = hw_ref[...]
    hid_a = ha_ref[...]
    wl_all = [_dot(hid_w, w2w0_ref[...]), _dot(hid_w, w2w1_ref[...])]
    al_all = [_dot(hid_a, w2a0_ref[...]), _dot(hid_a, w2a1_ref[...])]
    wl_s = [[wl_all[d][rw, :] for rw in sub_rows] for d in range(2)]
    al_s = [[al_all[d][rw, :] for rw in sub_rows] for d in range(2)]

    def unit_key(k_i):
        kk_i = k_i * k_k
        return kk_i * lax.rsqrt(jnp.maximum(_head_sum(kk_i * kk_i), 1e-24))

    def log_decay(wl_i, d):
        z = -(par[d:d + 1, :] + wl_i)
        softplus = jnp.maximum(z, 0.0) + jnp.log(1.0 + jnp.exp(-jnp.abs(z)))
        return -jnp.exp(-softplus - 0.5)

    kk_s = _each(unit_key, k_s)
    a_s = [[jax.nn.sigmoid(par[2 + d:3 + d, :] + x) for x in al_s[d]] for d in range(2)]
    kd_s = [[k_i * (1.0 + (a_i - 1.0) * k_a) for k_i, a_i in zip(k_s, a_s[d])] for d in range(2)]
    bonus = [_head_sum(r_i * r_k * (kd0 + kd1)) * v_i for r_i, kd0, kd1, v_i in zip(r_s, kd_s[0], kd_s[1], v_s)]

    r = [r_s[sub] for _, sub in inst]
    kk = [kk_s[sub] for _, sub in inst]
    v = [v_s[sub] for _, sub in inst]
    ld = [log_decay(wl_s[d][sub], d) for d, sub in inst]
    a = [a_s[d][sub] for d, sub in inst]
    kd = [kd_s[d][sub] for d, sub in inst]

    def cumulative(ld_i, d):
        p_hi = ld_i.astype(BF16)
        rem = ld_i - p_hi.astype(F32)
        p_mid = rem.astype(BF16)
        p_lo = (rem - p_mid.astype(F32)).astype(BF16)
        cs = _dot(tri[d], jnp.concatenate([p_hi, p_mid, p_lo], axis=1))
        return cs[:, :w] + cs[:, w:2 * w] + cs[:, 2 * w:]

    lam = _each(cumulative, ld, dirs)
    lam_end = _each(lambda l, d: l[0:1, :] if d else l[c - 1:c, :], lam, dirs)
    g_cum = _each(jnp.exp, lam)
    g_inv = _each(lambda l: jnp.exp(-l), lam)
    g_prev = _each(lambda l, x: jnp.exp(l - x), lam, ld)
    g_rel = _each(lambda le, l: jnp.exp(le - l), lam_end, lam)
    g_end = _each(jnp.exp, lam_end)

    qk = _each(jnp.multiply, kk, g_prev)
    rt = _each(jnp.multiply, r, g_cum)
    beta = _each(jnp.multiply, kk, a)
    bt = _each(jnp.multiply, beta, g_inv)
    kt = _each(jnp.multiply, kd, g_inv)
    bh = _each(jnp.multiply, beta, g_rel)
    kh = _each(jnp.multiply, kd, g_rel)

    def big_product(qk_i, rt_i, bt_i, kt_i):
        rhs = jnp.concatenate([jnp.where(head0, bt_i, 0.0), jnp.where(head0, 0.0, bt_i),
                               jnp.where(head0, kt_i, 0.0), jnp.where(head0, 0.0, kt_i)], axis=0).astype(BF16)
        return _dot_nt(jnp.concatenate([qk_i, rt_i], axis=0).astype(BF16), rhs)

    big = _each(big_product, qk, rt, bt, kt)
    n_pow = _each(lambda b, d: jnp.where(strict[d], -b[:c, :w], 0.0), big, dirs)
    a_k = _each(lambda b, d: jnp.where(strict[d], b[:c, w:], 0.0), big, dirs)
    g_b = _each(lambda b, d: jnp.where(incl[d], b[c:, :w], 0.0), big, dirs)
    g_k = _each(lambda b, d: jnp.where(incl[d], b[c:, w:], 0.0), big, dirs)

    t_inv = _each(lambda n: jnp.where(eye, 1.0, 0.0) + n, n_pow)
    for _ in range(int(math.log2(c)) - 1):
        n_pow = _each(lambda n: pm(n, bd(n)), n_pow)
        t_inv = _each(lambda t, n: t + pm(t, bd(n)), t_inv, n_pow)

    v_bd = _each(bd, v)
    akv_gkv = _each(lambda ak, gk, vb: pm(jnp.concatenate([ak, gk], axis=0), vb), a_k, g_k, v_bd)
    x1 = _each(lambda x: x[:c], akv_gkv)
    uw = _each(lambda t, x, q: _dot(t.astype(BF16), jnp.concatenate([bd(x), bd(q)], axis=1)), t_inv, x1, qk)
    u0 = _each(lambda x: -x[:, :w], uw)
    wm = _each(lambda x: x[:, w:], uw)
    gb_uw = _each(lambda gb, u, wm_i: _dot(gb.astype(BF16), jnp.concatenate([bd(u), bd(wm_i)], axis=1)), g_b, u0, wm)
    y0 = _each(lambda x, y: x[c:] + y[:, :w], akv_gkv, gb_uw)
    rm = _each(lambda rt_i, y: rt_i - y[:, w:], rt, gb_uw)
    d0 = _each(lambda kh_i, bh_i, v_i, u: jnp.where(bdmask, _dot_tn(
        jnp.concatenate([kh_i, bh_i], axis=0).astype(BF16), jnp.concatenate([v_i, u], axis=0).astype(BF16)), 0.0),
        kh, bh, v, u0)
    mm = _each(lambda bh_i, wm_i, ge: jnp.where(eye2, ge, 0.0) - jnp.where(
        bdmask, _dot_tn(bh_i.astype(BF16), wm_i.astype(BF16)), 0.0), bh, wm, g_end)
    d0 = _each(lambda x: x[:hw] + x[hw:], d0)
    mm = _each(lambda x: x[:hw] + x[hw:], mm)
    mm_hi = _each(lambda m: m.astype(BF16), mm)
    mm_lo = _each(lambda m, mh: (m - mh.astype(F32)).astype(BF16), mm, mm_hi)

    for sub in range(SCAN_SUB):
        bonus_ref[sub_rows[sub], :] = bonus[sub]
    for i, (d, sub) in enumerate(inst):
        y0_ref[d, rows[i], :] = y0[i]
        rm_ref[d, sub, 0, 0:c, :] = rm[i].astype(BF16)
        rm_ref[d, sub, 0, c:c + hw, :] = mm_hi[i]
        rm_ref[d, sub, 0, c + hw:c + 2 * hw, :] = mm_lo[i]
        d0_ref[d, sub, 0, :, :] = d0[i]


def scan_prepare(r, k, v, hid_w, hid_a, w2_w, w2_a, par):
    rows, d = r.shape
    c, w = SCAN_CHUNK, SCAN_PAIR_W
    n_pairs = d // w
    n_chunks = rows // c
    blk = SCAN_SUB * c
    rank = hid_w.shape[1]
    shared = pl.BlockSpec((blk, w), lambda i, p: (i, p))
    hidden = pl.BlockSpec((blk, rank), lambda i, p: (i, 0))
    fwd_w2 = pl.BlockSpec((rank, w), lambda i, p: (0, p))
    rev_w2 = pl.BlockSpec((rank, w), lambda i, p: (0, n_pairs + p))
    return pl.pallas_call(
        _scan_prepare_kernel,
        grid=(rows // blk, n_pairs),
        in_specs=[shared, shared, shared, hidden, hidden, fwd_w2, rev_w2, fwd_w2, rev_w2,
                  pl.BlockSpec((8, w), lambda i, p: (0, p))],
        out_specs=[pl.BlockSpec((2, blk, w), lambda i, p: (0, i, p)),
                   pl.BlockSpec((2, SCAN_SUB, 1, c + w, w), lambda i, p: (0, i, p, 0, 0)),
                   pl.BlockSpec((2, SCAN_SUB, 1, w // 2, w), lambda i, p: (0, i, p, 0, 0)),
                   shared],
        out_shape=[jax.ShapeDtypeStruct((2, rows, d), F32),
                   jax.ShapeDtypeStruct((2, n_chunks, n_pairs, c + w, w), BF16),
                   jax.ShapeDtypeStruct((2, n_chunks, n_pairs, w // 2, w), F32),
                   jax.ShapeDtypeStruct((rows, d), F32)],
        compiler_params=_params(("parallel", "parallel")),
        name="scan_prepare",
    )(r, k, v, hid_w, hid_a, w2_w, w2_w, w2_a, w2_a, par)


def _scan_apply_kernel(y0_ref, rm_ref, d0_ref, y_ref, z_ref, *, n_pairs):
    c, w = SCAN_CHUNK, SCAN_PAIR_W
    i = pl.program_id(2)

    @pl.when(i == 0)
    def _():
        z_ref[...] = jnp.zeros_like(z_ref)

    hw = RWKV_HEAD
    row2 = lax.broadcasted_iota(jnp.int32, (w, w), 0)
    lane2 = lax.broadcasted_iota(jnp.int32, (w, w), 1)
    bdmask = (row2 < hw) == (lane2 < hw)

    def bd(x):
        return jnp.where(bdmask, jnp.concatenate([x, x], axis=0), 0.0)

    z = [bd(z_ref[p]) for p in range(n_pairs)]
    z_hi = _each(lambda x: x.astype(BF16), z)
    z_lo = _each(lambda x, xh: (x - xh.astype(F32)).astype(BF16), z, z_hi)
    res = [_dot(rm_ref[p], jnp.concatenate([z_hi[p], z_lo[p]], axis=1)) for p in range(n_pairs)]
    res = _each(lambda x: x[:, :w] + x[:, w:], res)
    y = [y0_ref[:, p * w:(p + 1) * w] + res[p][:c] for p in range(n_pairs)]
    z_new = [d0_ref[p] + res[p][c:c + hw] + res[p][c + hw:] for p in range(n_pairs)]
    for p in range(n_pairs):
        y_ref[:, p * w:(p + 1) * w] = y[p]
        z_ref[p] = z_new[p]


def _scan_chunk_index(i, d):
    n_ctx = CTX_LEN // SCAN_CHUNK
    n_all = S_ALL // SCAN_CHUNK
    return jnp.where(d == 0, i, jnp.where(i < n_ctx, n_ctx - 1 - i, n_all + n_ctx - 1 - i))


def scan_apply(y0, rm, d0, batch):
    _, rows, d = y0.shape
    c, w = SCAN_CHUNK, SCAN_PAIR_W
    n_pairs = d // w
    per_b = rows // batch // c

    def idx(b, i, dr):
        return b * per_b + _scan_chunk_index(i, dr)

    kernel = functools.partial(_scan_apply_kernel, n_pairs=n_pairs)
    return pl.pallas_call(
        kernel,
        grid=(2, batch, per_b),
        in_specs=[pl.BlockSpec((None, c, d), lambda dr, b, i: (dr, idx(b, i, dr), 0)),
                  pl.BlockSpec((None, None, n_pairs, c + w, w), lambda dr, b, i: (dr, idx(b, i, dr), 0, 0, 0)),
                  pl.BlockSpec((None, None, n_pairs, w // 2, w), lambda dr, b, i: (dr, idx(b, i, dr), 0, 0, 0))],
        out_specs=pl.BlockSpec((None, c, d), lambda dr, b, i: (dr, idx(b, i, dr), 0)),
        out_shape=jax.ShapeDtypeStruct((2, rows, d), F32),
        scratch_shapes=[pltpu.VMEM((n_pairs, w // 2, w), F32)],
        compiler_params=_params(("parallel", "parallel", "arbitrary")),
        name="scan_apply",
    )(y0, rm, d0)


def _shift_mix_kernel(h_ref, hp_ref, hn_ref, mix_ref, *out_refs):
    per_b = S_ALL // ROW_TILE
    pos = pl.program_id(0) % per_b
    h = h_ref[...]
    row = lax.broadcasted_iota(jnp.int32, h.shape, 0)
    starts_seq = (pos == 0) | (pos == 1)
    ends_seq = (pos == 0) | (pos == per_b - 1)
    prev_row = jnp.where(starts_seq, 0.0, hp_ref[7:8, :])
    next_row = jnp.where(ends_seq, 0.0, hn_ref[0:1, :])
    prev = jnp.where(row == 0, prev_row, pltpu.roll(h, 1, 0))
    nxt = jnp.where(row == ROW_TILE - 1, next_row, pltpu.roll(h, ROW_TILE - 1, 0))
    xx = 0.5 * (prev + nxt) - h
    for j, o_ref in enumerate(out_refs):
        o_ref[...] = (h + xx * mix_ref[j:j + 1, :]).astype(o_ref.dtype)


def shift_mix(h, x_mix):
    rows, d = h.shape
    n_mix = x_mix.shape[0]
    sub = ROW_TILE // 8
    tile = pl.BlockSpec((ROW_TILE, d), lambda i: (i, 0))
    return pl.pallas_call(
        _shift_mix_kernel,
        grid=(rows // ROW_TILE,),
        in_specs=[tile,
                  pl.BlockSpec((8, d), lambda i: (jnp.maximum(i * sub - 1, 0), 0)),
                  pl.BlockSpec((8, d), lambda i: (jnp.minimum((i + 1) * sub, rows // 8 - 1), 0)),
                  pl.BlockSpec((n_mix, d), lambda i: (0, 0))],
        out_specs=[tile] * n_mix,
        out_shape=[jax.ShapeDtypeStruct((rows, d), BF16)] * n_mix,
        compiler_params=_params(("parallel",)),
        name="shift_mix",
    )(h, h, h, x_mix)


def _rwkv_out_kernel(y_ref, bonus_ref, g_ref, gn_ref, o_ref):
    w = SCAN_PAIR_W
    for s in range(o_ref.shape[-1] // w):
        cols = slice(s * w, (s + 1) * w)
        y = y_ref[0, :, cols] + y_ref[1, :, cols]
        mu = _head_sum(y) * (1.0 / RWKV_HEAD)
        dev = y - mu
        var = _head_sum(dev * dev) * (1.0 / RWKV_HEAD)
        yn = dev * lax.rsqrt(var + GN_EPS) * gn_ref[0:1, cols] + gn_ref[1:2, cols]
        o_ref[:, cols] = ((yn + bonus_ref[:, cols]) * g_ref[:, cols]).astype(o_ref.dtype)


def rwkv_out(y, bonus, g, gn_g, gn_b, tn=512):
    _, rows, d = y.shape
    tile = pl.BlockSpec((ROW_TILE, tn), lambda i, j: (i, j))
    return pl.pallas_call(
        _rwkv_out_kernel,
        grid=(rows // ROW_TILE, d // tn),
        in_specs=[pl.BlockSpec((2, ROW_TILE, tn), lambda i, j: (0, i, j)), tile, tile,
                  pl.BlockSpec((2, tn), lambda i, j: (0, j))],
        out_specs=tile,
        out_shape=jax.ShapeDtypeStruct((rows, d), BF16),
        compiler_params=_params(("parallel", "parallel")),
        name="rwkv_out",
    )(y, bonus, g, jnp.stack([gn_g, gn_b]))


def _route_topk(logits):
    lane = lax.broadcasted_iota(jnp.int32, logits.shape, 1)
    far = 4 * HEAD_W

    def first_max(vals):
        top = jnp.max(vals, axis=-1, keepdims=True)
        return top, jnp.min(jnp.where(vals == top, lane, far), axis=-1, keepdims=True)

    is_group = lane < N_GROUPS
    g_top, g_sel = first_max(jnp.where(is_group, logits, NEG_INF))
    p_sel = 1.0 / jnp.sum(jnp.where(is_group, jnp.exp(logits - g_top), 0.0), axis=-1, keepdims=True)
    lo = N_GROUPS + g_sel * EXPERTS_PER_GROUP
    le = jnp.where((lane >= lo) & (lane < lo + EXPERTS_PER_GROUP), logits, NEG_INF)
    v1, i1 = first_max(le)
    v2, i2 = first_max(jnp.where(lane == i1, NEG_INF, le))
    e2 = jnp.exp(v2 - v1)
    w1 = p_sel / (1.0 + e2)
    w2 = p_sel * e2 / (1.0 + e2)
    out = jnp.where(lane == 0, (i1 - N_GROUPS).astype(F32), 0.0)
    out = jnp.where(lane == 1, (i2 - N_GROUPS).astype(F32), out)
    out = jnp.where(lane == 2, w1, out)
    return jnp.where(lane == 3, w2, out)


def moe_dispatch(route):
    t = route.shape[0]
    n_pairs = TOP_K * t
    tm = MOE_TILE
    n_slots = n_pairs + N_EXPERTS * tm
    n_tiles = n_slots // tm
    e_flat = route[:, :TOP_K].astype(jnp.int32).reshape(-1)
    gate_bits = lax.bitcast_convert_type(route[:, TOP_K:2 * TOP_K], jnp.int32).reshape(-1)
    onehot = (e_flat[:, None] == jnp.arange(N_EXPERTS, dtype=jnp.int32)[None, :]).astype(jnp.int32)
    csum = jnp.cumsum(onehot, axis=0)
    counts = csum[-1]
    padded = ((counts + tm - 1) // tm) * tm
    pend = jnp.cumsum(padded)
    pstart = pend - padded
    pos = jnp.sum(onehot * (pstart[None, :] + csum - onehot), axis=1)
    pair = jnp.arange(n_pairs, dtype=jnp.int32)
    slots = jnp.full((n_slots, 2), -1, jnp.int32).at[pos].set(jnp.stack([pair, gate_bits], axis=1))
    valid = slots[:, 0] >= 0
    tok = jnp.where(valid, slots[:, 0] // TOP_K, 0)
    dst = jnp.where(valid, (slots[:, 0] % TOP_K) * t + slots[:, 0] // TOP_K,
                    n_pairs + jnp.arange(n_slots, dtype=jnp.int32) % tm)
    gate = jnp.where(valid, lax.bitcast_convert_type(slots[:, 1], F32), 0.0)
    tile_start = jnp.arange(n_tiles, dtype=jnp.int32) * tm
    tile_e = jnp.sum((tile_start[:, None] >= pend[None, :]).astype(jnp.int32), axis=1)
    tile_e = jnp.minimum(tile_e, N_EXPERTS - 1)
    n_live = (pend[-1] // tm).reshape(1)
    return (tok.reshape(n_tiles, 1, tm), dst.reshape(n_tiles, 1, tm), gate.reshape(n_slots, 1),
            tile_e.astype(jnp.int32), n_live.astype(jnp.int32))


def _moe_kernel(te_ref, nl_ref, tok_ref, tokn_ref, dst_ref, dstp_ref, gate_ref, h_hbm, w1_ref, w3_ref, w2_ref,
                y_hbm, xbuf, obuf, w1b, w3b, w2b, sem_in, sem_out):
    i = pl.program_id(0)
    n_live = nl_ref[0]
    tm = MOE_TILE
    slot = i % 2
    other = 1 - slot

    def gather_copy(src_row, r, s):
        return pltpu.make_async_copy(h_hbm.at[pl.ds(src_row, 1), :], xbuf.at[s, pl.ds(r, 1), :], sem_in.at[s])

    def scatter_copy(r, dst_row, s):
        return pltpu.make_async_copy(obuf.at[s, pl.ds(r, 1), :], y_hbm.at[pl.ds(dst_row, 1), :], sem_out.at[s])

    def wait_gather(s):
        pltpu.make_async_copy(h_hbm.at[pl.ds(0, tm), :], xbuf.at[s], sem_in.at[s]).wait()

    def wait_scatter(s):
        pltpu.make_async_copy(obuf.at[s], y_hbm.at[pl.ds(0, tm), :], sem_out.at[s]).wait()

    def per_row(fn):
        def body(r, carry):
            fn(r)
            return carry
        lax.fori_loop(0, tm, body, 0, unroll=MOE_DMA_UNROLL)

    def expert_tile(scatter_previous):
        packed = xbuf[slot]
        half = packed.shape[-1]
        x_lo = lax.bitcast_convert_type(packed << 16, F32).astype(BF16)
        x_hi = lax.bitcast_convert_type(packed & jnp.uint32(0xFFFF0000), F32).astype(BF16)
        for r in range(tm):
            gather_copy(tokn_ref[0, r], r, other).start(priority=r % 2)
        if scatter_previous:
            for r in range(tm):
                scatter_copy(r, dstp_ref[0, r], other).start(priority=r % 2)
        up = _dot(x_lo, w1b[0:half, :]) + _dot(x_hi, w1b[half:, :])
        gate_in = _dot(x_lo, w3b[0:half, :]) + _dot(x_hi, w3b[half:, :])
        hid = (up * jax.nn.sigmoid(up)) * gate_in * gate_ref[...]
        return _dot(hid.astype(BF16), w2b[...])

    @pl.when(i == 0)
    def _():
        per_row(lambda r: gather_copy(tok_ref[0, r], r, 0).start())

    @pl.when(i < n_live)
    def _():
        wait_gather(slot)
        prev_e = te_ref[jnp.maximum(i - 1, 0)]

        @pl.when((i == 0) | (te_ref[i] != prev_e))
        def _():
            w1b[...] = w1_ref[...].astype(BF16)
            w3b[...] = w3_ref[...].astype(BF16)
            w2b[...] = w2_ref[...].astype(BF16)

        @pl.when(i == 0)
        def _():
            obuf[0] = expert_tile(False)
            first_spare = y_hbm.shape[0] - tm
            per_row(lambda r: scatter_copy(r, first_spare + r, 0).start())
            wait_scatter(0)

        @pl.when(i > 0)
        def _():
            out = expert_tile(True)

            @pl.when(i > 1)
            def _():
                wait_scatter(slot)

            obuf[slot] = out

        @pl.when(i == n_live - 1)
        def _():
            per_row(lambda r: scatter_copy(r, dst_ref[0, r], slot).start())
            wait_gather(other)

            @pl.when(i > 0)
            def _():
                wait_scatter(other)

            wait_scatter(slot)


def moe_experts(h, tok, dst, gate, tile_e, n_live, w1, w3, w2, layer):
    t = h.shape[0]
    d = w1.shape[-2]
    f = w1.shape[-1]
    tm = MOE_TILE
    n_tiles = tok.shape[0]

    def w_index(i, te, nl):
        return (layer, te[i] // EXPERTS_PER_GROUP, te[i] % EXPERTS_PER_GROUP, 0, 0)

    grid_spec = pltpu.PrefetchScalarGridSpec(
        num_scalar_prefetch=2,
        grid=(n_tiles,),
        in_specs=[pl.BlockSpec((None, 1, tm), lambda i, te, nl: (i, 0, 0), memory_space=pltpu.SMEM),
                  pl.BlockSpec((None, 1, tm), lambda i, te, nl: (jnp.minimum(i + 1, n_tiles - 1), 0, 0),
                               memory_space=pltpu.SMEM),
                  pl.BlockSpec((None, 1, tm), lambda i, te, nl: (i, 0, 0), memory_space=pltpu.SMEM),
                  pl.BlockSpec((None, 1, tm), lambda i, te, nl: (jnp.maximum(i - 1, 0), 0, 0),
                               memory_space=pltpu.SMEM),
                  pl.BlockSpec((tm, 1), lambda i, te, nl: (i, 0)),
                  pl.BlockSpec(memory_space=pl.ANY),
                  pl.BlockSpec((None, None, None, d, f), w_index),
                  pl.BlockSpec((None, None, None, d, f), w_index),
                  pl.BlockSpec((None, None, None, f, d), w_index)],
        out_specs=pl.BlockSpec(memory_space=pl.ANY),
        scratch_shapes=[pltpu.VMEM((2, tm, d // 2), jnp.uint32), pltpu.VMEM((2, tm, d), F32),
                        pltpu.VMEM((d, f), BF16), pltpu.VMEM((d, f), BF16), pltpu.VMEM((f, d), BF16),
                        pltpu.SemaphoreType.DMA((2,)), pltpu.SemaphoreType.DMA((2,))])
    return pl.pallas_call(
        _moe_kernel,
        grid_spec=grid_spec,
        out_shape=jax.ShapeDtypeStruct((TOP_K * t + tm, d), F32),
        compiler_params=_params(("arbitrary",)),
        name="moe_experts",
    )(tile_e, n_live, tok, tok, dst, dst, gate, h, w1, w3, w2)


def _mod_spec(k):
    per_b = S_ALL // ROW_TILE
    return pl.BlockSpec((None, 1, D_MODEL),
                        lambda i: ((i // per_b * 2 + jnp.minimum(i % per_b, 1)) * N_MOD + k, 0, 0))


def _norm_mod(x, g_ref, sh_ref, sc_ref):
    h = x * lax.rsqrt(jnp.mean(x * x, axis=-1, keepdims=True) + NORM_EPS) * g_ref[...]
    if sh_ref is None:
        return h
    return h * (1.0 + sc_ref[...]) + sh_ref[...]


def _first_norm_kernel(x_ref, g_ref, sh_ref, sc_ref, h_ref):
    h_ref[...] = _norm_mod(x_ref[...], g_ref, sh_ref, sc_ref).astype(h_ref.dtype)


def first_norm(xs, g, mod, h_dtype):
    rows, d = xs.shape
    tile = pl.BlockSpec((ROW_TILE, d), lambda i: (i, 0))
    return pl.pallas_call(
        _first_norm_kernel,
        grid=(rows // ROW_TILE,),
        in_specs=[tile, pl.BlockSpec((1, d), lambda i: (0, 0)), _mod_spec(0), _mod_spec(1)],
        out_specs=tile,
        out_shape=jax.ShapeDtypeStruct((rows, d), h_dtype),
        compiler_params=_params(("parallel",)),
        name="first_norm",
    )(xs, g.reshape(1, d), mod, mod)


def _post_mixer_kernel(x_ref, u_ref, gate_ref, g_ref, sh_ref, sc_ref, wr_ref, br_ref, xo_ref, h_ref, route_ref):
    x = x_ref[...] + gate_ref[...] * u_ref[...]
    xo_ref[...] = x
    h = _norm_mod(x, g_ref, sh_ref, sc_ref)
    half = h.shape[-1] // 2
    bits = lax.bitcast_convert_type(h.astype(BF16).astype(F32), jnp.uint32)
    h_ref[...] = (bits[:, half:] & jnp.uint32(0xFFFF0000)) | (bits[:, :half] >> 16)
    logits = jnp.dot(h, wr_ref[...], precision=lax.Precision.HIGHEST, preferred_element_type=F32) + br_ref[...]
    route_ref[...] = _route_topk(logits)


def post_mixer(xs, upd, g, mod, router_w, router_b):
    rows, d = xs.shape
    tile = pl.BlockSpec((ROW_TILE, d), lambda i: (i, 0))
    ptile = pl.BlockSpec((ROW_TILE, d // 2), lambda i: (i, 0))
    rtile = pl.BlockSpec((ROW_TILE, HEAD_W), lambda i: (i, 0))
    return pl.pallas_call(
        _post_mixer_kernel,
        grid=(rows // ROW_TILE,),
        in_specs=[tile, tile, _mod_spec(2), pl.BlockSpec((1, d), lambda i: (0, 0)), _mod_spec(3), _mod_spec(4),
                  pl.BlockSpec((d, HEAD_W), lambda i: (0, 0)), pl.BlockSpec((1, HEAD_W), lambda i: (0, 0))],
        out_specs=[tile, ptile, rtile],
        out_shape=[jax.ShapeDtypeStruct((rows, d), F32), jax.ShapeDtypeStruct((rows, d // 2), jnp.uint32),
                   jax.ShapeDtypeStruct((rows, HEAD_W), F32)],
        compiler_params=_params(("parallel",)),
        name="post_mixer",
    )(xs, upd, mod, g.reshape(1, d), mod, mod, router_w, router_b)


def _post_moe_kernel(x_ref, ya_ref, yb_ref, gate_ref, g_ref, *rest, modulate):
    if modulate:
        sh_ref, sc_ref, xo_ref, h_ref = rest
    else:
        sh_ref = sc_ref = None
        xo_ref, h_ref = rest
    x = x_ref[...] + gate_ref[...] * (ya_ref[...] + yb_ref[...])
    xo_ref[...] = x
    h_ref[...] = _norm_mod(x, g_ref, sh_ref, sc_ref).astype(h_ref.dtype)


def post_moe(xs, y2, g, mod, next_mod, h_dtype):
    rows, d = xs.shape
    tile = pl.BlockSpec((ROW_TILE, d), lambda i: (i, 0))
    second = pl.BlockSpec((ROW_TILE, d), lambda i: (rows // ROW_TILE + i, 0))
    gspec = pl.BlockSpec((1, d), lambda i: (0, 0))
    modulate = next_mod is not None
    in_specs = [tile, tile, second, _mod_spec(5), gspec] + ([_mod_spec(0), _mod_spec(1)] if modulate else [])
    args = (xs, y2, y2, mod, g.reshape(1, d)) + ((next_mod, next_mod) if modulate else ())
    return pl.pallas_call(
        functools.partial(_post_moe_kernel, modulate=modulate),
        grid=(rows // ROW_TILE,),
        in_specs=in_specs,
        out_specs=[tile, tile],
        out_shape=[jax.ShapeDtypeStruct((rows, d), F32), jax.ShapeDtypeStruct((rows, d), h_dtype)],
        compiler_params=_params(("parallel",)),
        name="post_moe",
    )(*args)


def _pad_cols(w, n):
    return jnp.pad(w, ((0, 0), (0, n - w.shape[1])))


def _two_dir_lora(x, w1, w2):
    r = w1.shape[-1]
    w1c = _pad_cols(jnp.concatenate([w1[0], w1[1]], axis=1), 256).astype(BF16)
    d = w2.shape[-1]
    w2bd = jnp.zeros((256, 2 * d), F32).at[:r, :d].set(w2[0]).at[r:2 * r, d:].set(w2[1]).astype(BF16)
    return matmul(x, w1c, F32, tn=256), w2bd


def rwkv_mixer(h, p, batch):
    mixes = shift_mix(h, p['x_mix'])
    r = matmul(mixes[0], p['w_rkv'][0].astype(BF16), F32)
    k = matmul(mixes[1], p['w_rkv'][1].astype(BF16), F32)
    v = matmul(mixes[2], p['w_rkv'][2].astype(BF16), F32)
    hid_w, w2bd_w = _two_dir_lora(mixes[3], p['dec_w1'], p['dec_w2'])
    hid_a, w2bd_a = _two_dir_lora(mixes[4], p['iclr_a1'], p['iclr_a2'])
    g_hid = matmul(mixes[5], p['gate_g1'].astype(BF16), F32, tn=256)
    g = matmul(jax.nn.sigmoid(g_hid).astype(BF16), p['gate_g2'].astype(BF16), F32)

    par = jnp.concatenate([p['dec_w0'], p['iclr_a0'], p['k_k'][None], p['k_a'][None], p['r_k'].reshape(1, -1),
                           jnp.zeros((1, D_MODEL), F32)], axis=0)
    y0, rm, d0, bonus = scan_prepare(r, k, v, jnp.tanh(hid_w).astype(BF16), hid_a.astype(BF16), w2bd_w, w2bd_a, par)
    y = scan_apply(y0, rm, d0, batch)
    yo = rwkv_out(y, bonus, g, p['gn_g'], p['gn_b'])
    return matmul(yo, p['w_out'].astype(BF16), F32)


def even_mixer(h, p, layer_idx, cos_t, sin_t, batch):
    rows = batch * S_ALL
    lam_init = 0.8 - 0.6 * math.exp(-0.3 * layer_idx)
    proj = matmul(h, p['w_in'].astype(BF16), BF16, tn=1024)
    proj = proj.reshape(batch, S_ALL, -1)
    lf = p['diff_lambda']
    lam = (jnp.exp(jnp.sum(lf[0] * lf[1])) - jnp.exp(jnp.sum(lf[2] * lf[3])) + lam_init).reshape(1)
    a_out = diff_attention(proj, lam, cos_t, sin_t, p['subln_g'], lam_init)
    b_out = neighbourhood_attention(proj, na_bias_table(p['rpb']))
    mixed = jnp.concatenate([a_out, b_out], axis=-1).reshape(rows, -1)
    return matmul(mixed, p['w_out'].astype(BF16), F32)


def router_table(router_g, router_g_b, router_e, router_e_b):
    w = _pad_cols(jnp.concatenate([router_g, router_e], axis=1), HEAD_W)
    b = _pad_cols(jnp.concatenate([router_g_b, router_e_b])[None, :], HEAD_W)
    return w, b


def kernel(x, c, ctx, c_ctx, ada_w, ada_b, norm_g, final_g, even_w_in, even_w_out, diff_lambda, diff_subln_g, na_rpb, rwkv_x_mix, rwkv_w_rkv, rwkv_w_out, rwkv_dec_w0, rwkv_dec_w1, rwkv_dec_w2, rwkv_iclr_a0, rwkv_iclr_a1, rwkv_iclr_a2, rwkv_gate_g1, rwkv_gate_g2, rwkv_k_k, rwkv_k_a, rwkv_r_k, rwkv_gn_g, rwkv_gn_b, moe_router_g, moe_router_g_b, moe_router_e, moe_router_e_b, moe_w1, moe_w3, moe_w2):
    batch = x.shape[0]
    rows = batch * S_ALL
    xs = jnp.concatenate([ctx, x], axis=1).reshape(rows, D_MODEL)
    cos_t, sin_t = rope_tables()

    cvec = jnp.concatenate([c, c_ctx[None], jnp.zeros((8 - batch - 1, D_MODEL), F32)], axis=0)
    mods = ada_modulation(jax.nn.silu(cvec).astype(BF16), ada_w, ada_b)
    mods = mods.reshape(DEPTH, 8, N_MOD, D_MODEL)

    def mod_table(i):
        mod_l = mods[i, :batch]
        mod_c = jnp.broadcast_to(mods[i, batch][None], mod_l.shape)
        return jnp.stack([mod_c, mod_l], axis=1).reshape(batch * 2 * N_MOD, 1, D_MODEL)

    mod = mod_table(0)
    h = first_norm(xs, norm_g[0, 0], mod, BF16)
    for i in range(DEPTH):
        j = i // 2
        if i % 2 == 0:
            p = dict(w_in=even_w_in[j], w_out=even_w_out[j], diff_lambda=diff_lambda[j],
                     subln_g=diff_subln_g[j], rpb=na_rpb[j])
            out = even_mixer(h, p, i, cos_t, sin_t, batch)
        else:
            p = dict(x_mix=rwkv_x_mix[j], w_rkv=rwkv_w_rkv[j], w_out=rwkv_w_out[j], dec_w0=rwkv_dec_w0[j],
                     dec_w1=rwkv_dec_w1[j], dec_w2=rwkv_dec_w2[j], iclr_a0=rwkv_iclr_a0[j],
                     iclr_a1=rwkv_iclr_a1[j], iclr_a2=rwkv_iclr_a2[j], gate_g1=rwkv_gate_g1[j],
                     gate_g2=rwkv_gate_g2[j], k_k=rwkv_k_k[j], k_a=rwkv_k_a[j], r_k=rwkv_r_k[j],
                     gn_g=rwkv_gn_g[j], gn_b=rwkv_gn_b[j])
            out = rwkv_mixer(h, p, batch)
        rw, rb = router_table(moe_router_g[i], moe_router_g_b[i], moe_router_e[i], moe_router_e_b[i])
        xs, h2, route = post_mixer(xs, out, norm_g[i, 1], mod, rw, rb)
        tok, dst, gate, tile_e, n_live = moe_dispatch(route)
        y2 = moe_experts(h2, tok, dst, gate, tile_e, n_live, moe_w1, moe_w3, moe_w2, i)
        if i + 1 < DEPTH:
            next_mod = mod_table(i + 1)
            xs, h = post_moe(xs, y2, norm_g[i + 1, 0], mod, next_mod, F32 if (i + 1) % 2 else BF16)
            mod = next_mod
        else:
            _, h = post_moe(xs, y2, final_g, mod, None, F32)
    return h.reshape(batch, S_ALL, D_MODEL)[:, CTX_LEN:]
```

```python
import functools
import math

import numpy as np
import jax
import jax.numpy as jnp
from jax import lax
from jax.experimental import pallas as pl
from jax.experimental.pallas import tpu as pltpu

F32 = jnp.float32
BF16 = jnp.bfloat16

D_MODEL = 2048
DEPTH = 4
GRID_W = 64
CTX_LEN = 256
SEQ = 4096
S_ALL = CTX_LEN + SEQ
N_MOD = 6
NORM_EPS = 1e-6
NEG_INF = -1e30

DIFF_HEADS = 8
DIFF_QK_DIM = 64
NA_HEADS = 8
NA_DIM = 128
NA_KH = 8
NA_KW = 16
ROPE_THETA = 10000.0
SUBLN_EPS = 1e-5
HEAD_W = 128
ATT_TILE = 256
DIFF_KV_BLOCK = 1024
NA_ROWS_PER_STEP = 4
NA_WIN_ROWS = 12

RWKV_HEAD = 64
RWKV_HEADS = D_MODEL // RWKV_HEAD
GN_EPS = 64e-5
SCAN_CHUNK = 64
SCAN_PAIR_W = 2 * RWKV_HEAD
SCAN_SUB = 8

N_GROUPS = 4
EXPERTS_PER_GROUP = 8
N_EXPERTS = N_GROUPS * EXPERTS_PER_GROUP
TOP_K = 2
EXPERT_FF = 512
MOE_TILE = 256
MOE_DMA_UNROLL = 8
ROW_TILE = 256

VMEM_LIMIT = 52 * 1024 * 1024


def _params(sem):
    return pltpu.CompilerParams(dimension_semantics=sem, vmem_limit_bytes=VMEM_LIMIT)


def _dot(a, b):
    return jnp.dot(a, b, preferred_element_type=F32)


def _dot_nt(a, b):
    return lax.dot_general(a, b, (((1,), (1,)), ((), ())), preferred_element_type=F32)


def _dot_tn(a, b):
    return lax.dot_general(a, b, (((0,), (0,)), ((), ())), preferred_element_type=F32)


def _mm_kernel(a_ref, w_ref, o_ref):
    o_ref[...] = _dot(a_ref[...], w_ref[...]).astype(o_ref.dtype)


def matmul(a, w, out_dtype, tm=1024, tn=512):
    m, k = a.shape
    n = w.shape[1]
    while m % tm:
        tm //= 2
    tn = min(tn, n)
    assert n % tn == 0
    return pl.pallas_call(
        _mm_kernel,
        grid=(m // tm, n // tn),
        in_specs=[pl.BlockSpec((tm, k), lambda i, j: (i, 0)),
                  pl.BlockSpec((k, tn), lambda i, j: (0, j))],
        out_specs=pl.BlockSpec((tm, tn), lambda i, j: (i, j)),
        out_shape=jax.ShapeDtypeStruct((m, n), out_dtype),
        compiler_params=_params(("parallel", "parallel")),
        name="matmul",
    )(a, w)


def _ada_kernel(s_ref, w_ref, b_ref, o_ref):
    o_ref[...] = _dot(s_ref[...], w_ref[...].astype(BF16)) + b_ref[...]


def ada_modulation(svec, ada_w, ada_b, tn=1024):
    nl, d, n = ada_w.shape
    rows = svec.shape[0]
    return pl.pallas_call(
        _ada_kernel,
        grid=(nl, n // tn),
        in_specs=[pl.BlockSpec((rows, d), lambda l, j: (0, 0)),
                  pl.BlockSpec((None, d, tn), lambda l, j: (l, 0, j)),
                  pl.BlockSpec((None, 1, tn), lambda l, j: (l, 0, j))],
        out_specs=pl.BlockSpec((None, rows, tn), lambda l, j: (l, 0, j)),
        out_shape=jax.ShapeDtypeStruct((nl, rows, n), F32),
        compiler_params=_params(("parallel", "parallel")),
        name="ada_modulation",
    )(svec, ada_w, ada_b.reshape(nl, 1, n))


def _rope(x, cos, sin_signed):
    lane = lax.broadcasted_iota(jnp.int32, x.shape, 1)
    first_half = (lane & 63) < 32
    partner = jnp.where(first_half, pltpu.roll(x, HEAD_W - 32, 1), pltpu.roll(x, 32, 1))
    return x * cos + partner * sin_signed


def _diff_attn_kernel(lam_ref, q_ref, k_ref, v_ref, cos_ref, sin_ref, g_ref, o_ref, kr_ref, vt_ref, sa_ref, sb_ref, *,
                      post_scale):
    j = pl.program_id(2)
    tq = ATT_TILE

    @pl.when(j == 0)
    def _():
        def prep_chunk(c, carry):
            rows = pl.ds(pl.multiple_of(c * ATT_TILE, ATT_TILE), ATT_TILE)
            kr_ref[rows, :] = _rope(k_ref[rows, :].astype(F32), cos_ref[rows, :], sin_ref[rows, :]).astype(BF16)
            vt_ref[:, rows] = v_ref[rows, :].astype(F32).T.astype(BF16)
            return carry
        lax.fori_loop(0, S_ALL // ATT_TILE, prep_chunk, 0)

    qrows = pl.ds(pl.multiple_of(j * tq, tq), tq)
    q = _rope(q_ref[...].astype(F32), cos_ref[qrows, :], sin_ref[qrows, :]) * (DIFF_QK_DIM ** -0.5)
    lane = lax.broadcasted_iota(jnp.int32, q.shape, 1)
    q1 = jnp.where(lane < DIFF_QK_DIM, q, 0.0).astype(BF16)
    q2 = jnp.where(lane < DIFF_QK_DIM, 0.0, q).astype(BF16)

    kb = DIFF_KV_BLOCK
    n_latent = SEQ // kb
    maps = (q1, q2)

    def score_stage(start, size, sbuf):
        kc = kr_ref[pl.ds(start, size), :]
        tops = []
        for i, qm in enumerate(maps):
            s = _dot_nt(kc, qm)
            sbuf[i, 0:size, :] = s
            tops.append(jnp.max(s, axis=0, keepdims=True))
        return tuple(tops)

    def softmax_stage(start, size, sbuf, tops, state):
        vt = vt_ref[:, pl.ds(start, size)]
        new = []
        for i in range(2):
            m, l, acc = state[3 * i:3 * i + 3]
            m_new = jnp.maximum(m, tops[i])
            alpha = jnp.exp(m - m_new)
            p = jnp.exp(sbuf[i, 0:size, :] - m_new)
            new += [m_new, alpha * l + jnp.sum(p, axis=0, keepdims=True),
                    alpha * acc + _dot(vt, p.astype(BF16))]
        return tuple(new)

    def latent_start(n):
        return pl.multiple_of(CTX_LEN + jnp.minimum(n, n_latent - 1) * kb, CTX_LEN)

    row = jnp.full((1, tq), NEG_INF, F32)
    zrow = jnp.zeros((1, tq), F32)
    zacc = jnp.zeros((HEAD_W, tq), F32)
    state = softmax_stage(0, CTX_LEN, sa_ref, score_stage(0, CTX_LEN, sa_ref), (row, zrow, zacc, row, zrow, zacc))
    tops = score_stage(latent_start(0), kb, sb_ref)

    def block_pair(n, carry, more_follow):
        st, tp = carry[:6], carry[6:]
        tp_next = score_stage(latent_start(n + 1), kb, sa_ref)
        st = softmax_stage(latent_start(n), kb, sb_ref, tp, st)
        tp_after = score_stage(latent_start(n + 2), kb, sb_ref) if more_follow else tp_next
        st = softmax_stage(latent_start(n + 1), kb, sa_ref, tp_next, st)
        return st + tp_after

    n_pairs = n_latent // 2
    carry = lax.fori_loop(0, jnp.where(j == 0, 0, n_pairs - 1),
                          lambda t, c: block_pair(2 * t, c, True), state + tops)
    carry = lax.fori_loop(0, jnp.where(j == 0, 0, 1),
                          lambda t, c: block_pair(2 * (n_pairs - 1), c, False), carry)
    m1, l1, a1, m2, l2, a2 = carry[:6]
    out = a1 / l1 - lam_ref[0] * (a2 / l2)
    ms = jnp.mean(out * out, axis=0, keepdims=True)
    y = out * lax.rsqrt(ms + SUBLN_EPS) * (g_ref[...] * post_scale)
    o_ref[...] = y.T.astype(o_ref.dtype)


def diff_attention(proj, lam, cos_t, sin_t, subln_g, lam_init):
    b = proj.shape[0]
    kernel = functools.partial(_diff_attn_kernel, post_scale=1.0 - lam_init)
    return pl.pallas_call(
        kernel,
        grid=(b, DIFF_HEADS, S_ALL // ATT_TILE),
        in_specs=[pl.BlockSpec(memory_space=pltpu.SMEM),
                  pl.BlockSpec((None, ATT_TILE, HEAD_W), lambda bi, h, j: (bi, j, h)),
                  pl.BlockSpec((None, S_ALL, HEAD_W), lambda bi, h, j: (bi, 0, DIFF_HEADS + h)),
                  pl.BlockSpec((None, S_ALL, HEAD_W), lambda bi, h, j: (bi, 0, 2 * DIFF_HEADS + h)),
                  pl.BlockSpec((S_ALL, HEAD_W), lambda bi, h, j: (0, 0)),
                  pl.BlockSpec((S_ALL, HEAD_W), lambda bi, h, j: (0, 0)),
                  pl.BlockSpec((HEAD_W, 1), lambda bi, h, j: (0, 0))],
        out_specs=pl.BlockSpec((None, ATT_TILE, HEAD_W), lambda bi, h, j: (bi, j, h)),
        out_shape=jax.ShapeDtypeStruct((b, S_ALL, DIFF_HEADS * HEAD_W), BF16),
        scratch_shapes=[pltpu.VMEM((S_ALL, HEAD_W), BF16), pltpu.VMEM((HEAD_W, S_ALL), BF16),
                        pltpu.VMEM((2, DIFF_KV_BLOCK, ATT_TILE), F32), pltpu.VMEM((2, DIFF_KV_BLOCK, ATT_TILE), F32)],
        compiler_params=_params(("parallel", "parallel", "arbitrary")),
        name="diff_attention",
    )(lam, proj, proj, proj, cos_t, sin_t, subln_g.reshape(HEAD_W, 1))


def rope_tables():
    n_freq = DIFF_QK_DIM // 4
    inv_freq = ROPE_THETA ** (-jnp.arange(n_freq, dtype=F32) / n_freq)
    t = jnp.arange(SEQ, dtype=jnp.int32)
    row = (t // GRID_W).astype(F32)
    col = (t % GRID_W).astype(F32)
    ang = jnp.concatenate([row[:, None] * inv_freq, col[:, None] * inv_freq], axis=-1)
    cos, sin = jnp.cos(ang), jnp.sin(ang)
    cos_l = jnp.concatenate([cos, cos, cos, cos], axis=-1)
    sin_l = jnp.concatenate([-sin, sin, -sin, sin], axis=-1)
    cos_all = jnp.concatenate([jnp.ones((CTX_LEN, HEAD_W), F32), cos_l], axis=0)
    sin_all = jnp.concatenate([jnp.zeros((CTX_LEN, HEAD_W), F32), sin_l], axis=0)
    return cos_all, sin_all


def _na_window_start(j):
    g = j - 1
    return jnp.clip(NA_ROWS_PER_STEP * g - NA_KH // 2, 0, SEQ // GRID_W - NA_WIN_ROWS)


def _na_kernel(q_ref, k_ref, v_ref, bias_ref, o_ref):
    j = pl.program_id(2)
    win = NA_WIN_ROWS * GRID_W
    start = pl.multiple_of(CTX_LEN + _na_window_start(j) * GRID_W, GRID_W)
    scale = NA_DIM ** -0.5
    q = q_ref[...]
    s_c = _dot_nt(q, k_ref[pl.ds(0, CTX_LEN), :]) * scale
    s_w = _dot_nt(q, k_ref[pl.ds(start, win), :]) * scale + bias_ref[...]
    m = jnp.maximum(jnp.max(s_c, axis=-1, keepdims=True), jnp.max(s_w, axis=-1, keepdims=True))
    p_c = jnp.exp(s_c - m)
    p_w = jnp.exp(s_w - m)
    l = jnp.sum(p_c, axis=-1, keepdims=True) + jnp.sum(p_w, axis=-1, keepdims=True)
    o = _dot(p_c.astype(BF16), v_ref[pl.ds(0, CTX_LEN), :]) + _dot(p_w.astype(BF16), v_ref[pl.ds(start, win), :])
    o_ref[...] = (o / l).astype(o_ref.dtype)


def _na_bias_pattern(j):
    n_groups = SEQ // (GRID_W * NA_ROWS_PER_STEP)
    g = j - 1
    return jnp.where(j == 0, 3, jnp.where(g == 0, 0, jnp.where(g == n_groups - 1, 2, 1)))


def na_bias_table(rpb):
    rows = SEQ // GRID_W
    n_groups = rows // NA_ROWS_PER_STEP
    cols = np.arange(GRID_W)
    col_start = np.clip(cols - NA_KW // 2, 0, GRID_W - NA_KW)
    col_mask = (cols[None, :] >= col_start[:, None]) & (cols[None, :] < col_start[:, None] + NA_KW)
    c_idx = np.clip(cols[None, :] - cols[:, None] + NA_KW - 1, 0, 2 * NA_KW - 2)
    pats = []
    for g in (0, 1, n_groups - 1):
        u0 = int(np.clip(NA_ROWS_PER_STEP * g - NA_KH // 2, 0, rows - NA_WIN_ROWS))
        r = NA_ROWS_PER_STEP * g + np.arange(NA_ROWS_PER_STEP)
        r0 = np.clip(r - NA_KH // 2, 0, rows - NA_KH)
        kr = u0 + np.arange(NA_WIN_ROWS)
        valid_r = (kr[None, :] >= r0[:, None]) & (kr[None, :] < r0[:, None] + NA_KH)
        r_idx = np.clip(kr[None, :] - r[:, None] + NA_KH - 1, 0, 2 * NA_KH - 2)
        valid = valid_r[:, None, :, None] & col_mask[None, :, None, :]
        r_sel = np.eye(2 * NA_KH - 1, dtype=np.float32)[r_idx]
        c_sel = np.eye(2 * NA_KW - 1, dtype=np.float32)[c_idx]
        rows_sel = jnp.einsum('qkr,hrc->hqkc', r_sel, rpb.astype(F32), precision=lax.Precision.HIGHEST)
        gathered = jnp.einsum('hqkc,abc->hqakb', rows_sel, c_sel, precision=lax.Precision.HIGHEST)
        pats.append(jnp.where(valid[None], gathered, NEG_INF))
    pats.append(jnp.full_like(pats[0], NEG_INF))
    tab = jnp.stack(pats, axis=1)
    return tab.reshape(NA_HEADS, 4, NA_ROWS_PER_STEP * GRID_W, NA_WIN_ROWS * GRID_W)


def neighbourhood_attention(proj, bias_tab):
    b = proj.shape[0]
    tq = NA_ROWS_PER_STEP * GRID_W
    assert tq == CTX_LEN
    win = NA_WIN_ROWS * GRID_W
    base = 3 * DIFF_HEADS
    return pl.pallas_call(
        _na_kernel,
        grid=(b, NA_HEADS, S_ALL // tq),
        in_specs=[pl.BlockSpec((None, tq, HEAD_W), lambda bi, h, j: (bi, j, base + h)),
                  pl.BlockSpec((None, S_ALL, HEAD_W), lambda bi, h, j: (bi, 0, base + NA_HEADS + h)),
                  pl.BlockSpec((None, S_ALL, HEAD_W), lambda bi, h, j: (bi, 0, base + 2 * NA_HEADS + h)),
                  pl.BlockSpec((None, None, tq, win), lambda bi, h, j: (h, _na_bias_pattern(j), 0, 0))],
        out_specs=pl.BlockSpec((None, tq, HEAD_W), lambda bi, h, j: (bi, j, h)),
        out_shape=jax.ShapeDtypeStruct((b, S_ALL, NA_HEADS * HEAD_W), BF16),
        compiler_params=_params(("parallel", "parallel", "arbitrary")),
        name="neighbourhood_attention",
    )(proj, proj, proj, bias_tab)


def _each(fn, *lists):
    return [fn(*args) for args in zip(*lists)]


def _head_sum(x):
    rows, w = x.shape
    r_i = lax.broadcasted_iota(jnp.int32, (w, w), 0)
    c_i = lax.broadcasted_iota(jnp.int32, (w, w), 1)
    ones_bd = jnp.where((r_i < RWKV_HEAD) == (c_i < RWKV_HEAD), 1.0, 0.0).astype(BF16)
    hi = x.astype(BF16)
    lo = (x - hi.astype(F32)).astype(BF16)
    s = _dot(jnp.concatenate([hi, lo], axis=0), ones_bd)
    return s[:rows] + s[rows:]


def _scan_prepare_kernel(r_ref, k_ref, v_ref, hw_ref, ha_ref, w2w0_ref, w2w1_ref, w2a0_ref, w2a1_ref, par_ref,
                         y0_ref, rm_ref, d0_ref, bonus_ref):
    c = SCAN_CHUNK
    w = SCAN_PAIR_W
    hw = RWKV_HEAD
    t_idx = lax.broadcasted_iota(jnp.int32, (c, w), 0)
    lane = lax.broadcasted_iota(jnp.int32, (c, w), 1)
    s_idx = lane & (hw - 1)
    head0 = lane < hw
    eye = s_idx == t_idx
    tt = lax.broadcasted_iota(jnp.int32, (c, c), 0)
    ss = lax.broadcasted_iota(jnp.int32, (c, c), 1)
    strict = (s_idx < t_idx, s_idx > t_idx)
    incl = (s_idx <= t_idx, s_idx >= t_idx)
    tri = (jnp.where(ss <= tt, 1.0, 0.0).astype(BF16), jnp.where(ss >= tt, 1.0, 0.0).astype(BF16))
    row2 = lax.broadcasted_iota(jnp.int32, (w, w), 0)
    lane2 = lax.broadcasted_iota(jnp.int32, (w, w), 1)
    bdmask = (row2 < hw) == (lane2 < hw)
    eye2 = row2 == lane2

    def bd(y):
        return jnp.where(bdmask, jnp.concatenate([y, y], axis=0), 0.0).astype(BF16)

    def pm(x, ybd):
        return _dot(x.astype(BF16), ybd)

    inst = [(d, sub) for d in range(2) for sub in range(SCAN_SUB)]
    dirs = [d for d, _ in inst]
    rows = [slice(sub * c, (sub + 1) * c) for _, sub in inst]
    par = par_ref[...]
    k_k, k_a, r_k = par[4:5, :], par[5:6, :], par[6:7, :]
    sub_rows = [slice(sub * c, (sub + 1) * c) for sub in range(SCAN_SUB)]
    r_s = [r_ref[rw, :] for rw in sub_rows]
    k_s = [k_ref[rw, :] for rw in sub_rows]
    v_s = [v_ref[rw, :] for rw in sub_rows]
    hid_w = hw_ref[...]
    hid_a = ha_ref[...]
    wl_all = [_dot(hid_w, w2w0_ref[...]), _dot(hid_w, w2w1_ref[...])]
    al_all = [_dot(hid_a, w2a0_ref[...]), _dot(hid_a, w2a1_ref[...])]
    wl_s = [[wl_all[d][rw, :] for rw in sub_rows] for d in range(2)]
    al_s = [[al_all[d][rw, :] for rw in sub_rows] for d in range(2)]

    def unit_key(k_i):
        kk_i = k_i * k_k
        return kk_i * lax.rsqrt(jnp.maximum(_head_sum(kk_i * kk_i), 1e-24))

    def log_decay(wl_i, d):
        z = -(par[d:d + 1, :] + wl_i)
        softplus = jnp.maximum(z, 0.0) + jnp.log(1.0 + jnp.exp(-jnp.abs(z)))
        return -jnp.exp(-softplus - 0.5)

    kk_s = _each(unit_key, k_s)
    a_s = [[jax.nn.sigmoid(par[2 + d:3 + d, :] + x) for x in al_s[d]] for d in range(2)]
    kd_s = [[k_i * (1.0 + (a_i - 1.0) * k_a) for k_i, a_i in zip(k_s, a_s[d])] for d in range(2)]
    bonus = [_head_sum(r_i * r_k * (kd0 + kd1)) * v_i for r_i, kd0, kd1, v_i in zip(r_s, kd_s[0], kd_s[1], v_s)]

    r = [r_s[sub] for _, sub in inst]
    kk = [kk_s[sub] for _, sub in inst]
    v = [v_s[sub] for _, sub in inst]
    ld = [log_decay(wl_s[d][sub], d) for d, sub in inst]
    a = [a_s[d][sub] for d, sub in inst]
    kd = [kd_s[d][sub] for d, sub in inst]

    def cumulative(ld_i, d):
        p_hi = ld_i.astype(BF16)
        rem = ld_i - p_hi.astype(F32)
        p_mid = rem.astype(BF16)
        p_lo = (rem - p_mid.astype(F32)).astype(BF16)
        cs = _dot(tri[d], jnp.concatenate([p_hi, p_mid, p_lo], axis=1))
        return cs[:, :w] + cs[:, w:2 * w] + cs[:, 2 * w:]

    lam = _each(cumulative, ld, dirs)
    lam_end = _each(lambda l, d: l[0:1, :] if d else l[c - 1:c, :], lam, dirs)
    g_cum = _each(jnp.exp, lam)
    g_inv = _each(lambda l: jnp.exp(-l), lam)
    g_prev = _each(lambda l, x: jnp.exp(l - x), lam, ld)
    g_rel = _each(lambda le, l: jnp.exp(le - l), lam_end, lam)
    g_end = _each(jnp.exp, lam_end)

    qk = _each(jnp.multiply, kk, g_prev)
    rt = _each(jnp.multiply, r, g_cum)
    beta = _each(jnp.multiply, kk, a)
    bt = _each(jnp.multiply, beta, g_inv)
    kt = _each(jnp.multiply, kd, g_inv)
    bh = _each(jnp.multiply, beta, g_rel)
    kh = _each(jnp.multiply, kd, g_rel)

    def big_product(qk_i, rt_i, bt_i, kt_i):
        rhs = jnp.concatenate([jnp.where(head0, bt_i, 0.0), jnp.where(head0, 0.0, bt_i),
                               jnp.where(head0, kt_i, 0.0), jnp.where(head0, 0.0, kt_i)], axis=0).astype(BF16)
        return _dot_nt(jnp.concatenate([qk_i, rt_i], axis=0).astype(BF16), rhs)

    big = _each(big_product, qk, rt, bt, kt)
    n_pow = _each(lambda b, d: jnp.where(strict[d], -b[:c, :w], 0.0), big, dirs)
    a_k = _each(lambda b, d: jnp.where(strict[d], b[:c, w:], 0.0), big, dirs)
    g_b = _each(lambda b, d: jnp.where(incl[d], b[c:, :w], 0.0), big, dirs)
    g_k = _each(lambda b, d: jnp.where(incl[d], b[c:, w:], 0.0), big, dirs)

    t_inv = _each(lambda n: jnp.where(eye, 1.0, 0.0) + n, n_pow)
    for _ in range(int(math.log2(c)) - 1):
        n_pow = _each(lambda n: pm(n, bd(n)), n_pow)
        t_inv = _each(lambda t, n: t + pm(t, bd(n)), t_inv, n_pow)

    v_bd = _each(bd, v)
    akv_gkv = _each(lambda ak, gk, vb: pm(jnp.concatenate([ak, gk], axis=0), vb), a_k, g_k, v_bd)
    x1 = _each(lambda x: x[:c], akv_gkv)
    uw = _each(lambda t, x, q: _dot(t.astype(BF16), jnp.concatenate([bd(x), bd(q)], axis=1)), t_inv, x1, qk)
    u0 = _each(lambda x: -x[:, :w], uw)
    wm = _each(lambda x: x[:, w:], uw)
    gb_uw = _each(lambda gb, u, wm_i: _dot(gb.astype(BF16), jnp.concatenate([bd(u), bd(wm_i)], axis=1)), g_b, u0, wm)
    y0 = _each(lambda x, y: x[c:] + y[:, :w], akv_gkv, gb_uw)
    rm = _each(lambda rt_i, y: rt_i - y[:, w:], rt, gb_uw)
    d0 = _each(lambda kh_i, bh_i, v_i, u: jnp.where(bdmask, _dot_tn(
        jnp.concatenate([kh_i, bh_i], axis=0).astype(BF16), jnp.concatenate([v_i, u], axis=0).astype(BF16)), 0.0),
        kh, bh, v, u0)
    mm = _each(lambda bh_i, wm_i, ge: jnp.where(eye2, ge, 0.0) - jnp.where(
        bdmask, _dot_tn(bh_i.astype(BF16), wm_i.astype(BF16)), 0.0), bh, wm, g_end)
    d0 = _each(lambda x: x[:hw] + x[hw:], d0)
    mm = _each(lambda x: x[:hw] + x[hw:], mm)
    mm_hi = _each(lambda m: m.astype(BF16), mm)
    mm_lo = _each(lambda m, mh: (m - mh.astype(F32)).astype(BF16), mm, mm_hi)

    for sub in range(SCAN_SUB):
        bonus_ref[sub_rows[sub], :] = bonus[sub]
    for i, (d, sub) in enumerate(inst):
        y0_ref[d, rows[i], :] = y0[i]
        rm_ref[d, sub, 0, 0:c, :] = rm[i].astype(BF16)
        rm_ref[d, sub, 0, c:c + hw, :] = mm_hi[i]
        rm_ref[d, sub, 0, c + hw:c + 2 * hw, :] = mm_lo[i]
        d0_ref[d, sub, 0, :, :] = d0[i]


def scan_prepare(r, k, v, hid_w, hid_a, w2_w, w2_a, par):
    rows, d = r.shape
    c, w = SCAN_CHUNK, SCAN_PAIR_W
    n_pairs = d // w
    n_chunks = rows // c
    blk = SCAN_SUB * c
    rank = hid_w.shape[1]
    shared = pl.BlockSpec((blk, w), lambda i, p: (i, p))
    hidden = pl.BlockSpec((blk, rank), lambda i, p: (i, 0))
    fwd_w2 = pl.BlockSpec((rank, w), lambda i, p: (0, p))
    rev_w2 = pl.BlockSpec((rank, w), lambda i, p: (0, n_pairs + p))
    return pl.pallas_call(
        _scan_prepare_kernel,
        grid=(rows // blk, n_pairs),
        in_specs=[shared, shared, shared, hidden, hidden, fwd_w2, rev_w2, fwd_w2, rev_w2,
                  pl.BlockSpec((8, w), lambda i, p: (0, p))],
        out_specs=[pl.BlockSpec((2, blk, w), lambda i, p: (0, i, p)),
                   pl.BlockSpec((2, SCAN_SUB, 1, c + w, w), lambda i, p: (0, i, p, 0, 0)),
                   pl.BlockSpec((2, SCAN_SUB, 1, w // 2, w), lambda i, p: (0, i, p, 0, 0)),
                   shared],
        out_shape=[jax.ShapeDtypeStruct((2, rows, d), F32),
                   jax.ShapeDtypeStruct((2, n_chunks, n_pairs, c + w, w), BF16),
                   jax.ShapeDtypeStruct((2, n_chunks, n_pairs, w // 2, w), F32),
                   jax.ShapeDtypeStruct((rows, d), F32)],
        compiler_params=_params(("parallel", "parallel")),
        name="scan_prepare",
    )(r, k, v, hid_w, hid_a, w2_w, w2_w, w2_a, w2_a, par)


def _scan_apply_kernel(y0f_ref, rmf_ref, d0f_ref, y0r_ref, rmr_ref, d0r_ref, yf_ref, yr_ref, z_ref, *, n_pairs):
    c, w = SCAN_CHUNK, SCAN_PAIR_W
    i = pl.program_id(1)

    @pl.when(i == 0)
    def _():
        z_ref[...] = jnp.zeros_like(z_ref)

    hw = RWKV_HEAD
    row2 = lax.broadcasted_iota(jnp.int32, (w, w), 0)
    lane2 = lax.broadcasted_iota(jnp.int32, (w, w), 1)
    bdmask = (row2 < hw) == (lane2 < hw)

    def bd(x):
        return jnp.where(bdmask, jnp.concatenate([x, x], axis=0), 0.0)

    chains = [(d, p) for d in range(2) for p in range(n_pairs)]
    y0_refs, rm_refs, d0_refs, y_refs = (y0f_ref, y0r_ref), (rmf_ref, rmr_ref), (d0f_ref, d0r_ref), (yf_ref, yr_ref)
    z = [bd(z_ref[d, p]) for d, p in chains]
    z_hi = _each(lambda x: x.astype(BF16), z)
    z_lo = _each(lambda x, xh: (x - xh.astype(F32)).astype(BF16), z, z_hi)
    res = [_dot(rm_refs[d][p], jnp.concatenate([zh, zl], axis=1)) for (d, p), zh, zl in zip(chains, z_hi, z_lo)]
    res = _each(lambda x: x[:, :w] + x[:, w:], res)
    y = [y0_refs[d][:, p * w:(p + 1) * w] + x[:c] for (d, p), x in zip(chains, res)]
    z_new = [d0_refs[d][p] + x[c:c + hw] + x[c + hw:] for (d, p), x in zip(chains, res)]
    for (d, p), y_i, z_i in zip(chains, y, z_new):
        y_refs[d][:, p * w:(p + 1) * w] = y_i
        z_ref[d, p] = z_i


def _scan_chunk_index(i, d):
    n_ctx = CTX_LEN // SCAN_CHUNK
    n_all = S_ALL // SCAN_CHUNK
    return jnp.where(d == 0, i, jnp.where(i < n_ctx, n_ctx - 1 - i, n_all + n_ctx - 1 - i))


def scan_apply(y0, rm, d0, batch):
    _, rows, d = y0.shape
    c, w = SCAN_CHUNK, SCAN_PAIR_W
    n_pairs = d // w
    per_b = rows // batch // c

    def specs(dr):
        def idx(b, i):
            return b * per_b + _scan_chunk_index(i, dr)
        return [pl.BlockSpec((None, c, d), lambda b, i: (dr, idx(b, i), 0)),
                pl.BlockSpec((None, None, n_pairs, c + w, w), lambda b, i: (dr, idx(b, i), 0, 0, 0)),
                pl.BlockSpec((None, None, n_pairs, w // 2, w), lambda b, i: (dr, idx(b, i), 0, 0, 0)),
                pl.BlockSpec((c, d), lambda b, i: (idx(b, i), 0))]

    fwd, rev = specs(0), specs(1)
    kernel = functools.partial(_scan_apply_kernel, n_pairs=n_pairs)
    return pl.pallas_call(
        kernel,
        grid=(batch, per_b),
        in_specs=fwd[:3] + rev[:3],
        out_specs=[fwd[3], rev[3]],
        out_shape=[jax.ShapeDtypeStruct((rows, d), F32)] * 2,
        scratch_shapes=[pltpu.VMEM((2, n_pairs, w // 2, w), F32)],
        compiler_params=_params(("parallel", "arbitrary")),
        name="scan_apply",
    )(y0, rm, d0, y0, rm, d0)


def _shift_mix_kernel(h_ref, hp_ref, hn_ref, mix_ref, *out_refs):
    per_b = S_ALL // ROW_TILE
    pos = pl.program_id(0) % per_b
    h = h_ref[...]
    row = lax.broadcasted_iota(jnp.int32, h.shape, 0)
    starts_seq = (pos == 0) | (pos == 1)
    ends_seq = (pos == 0) | (pos == per_b - 1)
    prev_row = jnp.where(starts_seq, 0.0, hp_ref[7:8, :])
    next_row = jnp.where(ends_seq, 0.0, hn_ref[0:1, :])
    prev = jnp.where(row == 0, prev_row, pltpu.roll(h, 1, 0))
    nxt = jnp.where(row == ROW_TILE - 1, next_row, pltpu.roll(h, ROW_TILE - 1, 0))
    xx = 0.5 * (prev + nxt) - h
    for j, o_ref in enumerate(out_refs):
        o_ref[...] = (h + xx * mix_ref[j:j + 1, :]).astype(o_ref.dtype)


def shift_mix(h, x_mix):
    rows, d = h.shape
    n_mix = x_mix.shape[0]
    sub = ROW_TILE // 8
    tile = pl.BlockSpec((ROW_TILE, d), lambda i: (i, 0))
    return pl.pallas_call(
        _shift_mix_kernel,
        grid=(rows // ROW_TILE,),
        in_specs=[tile,
                  pl.BlockSpec((8, d), lambda i: (jnp.maximum(i * sub - 1, 0), 0)),
                  pl.BlockSpec((8, d), lambda i: (jnp.minimum((i + 1) * sub, rows // 8 - 1), 0)),
                  pl.BlockSpec((n_mix, d), lambda i: (0, 0))],
        out_specs=[tile] * n_mix,
        out_shape=[jax.ShapeDtypeStruct((rows, d), BF16)] * n_mix,
        compiler_params=_params(("parallel",)),
        name="shift_mix",
    )(h, h, h, x_mix)


def _rwkv_out_kernel(yf_ref, yr_ref, bonus_ref, g_ref, gn_ref, o_ref):
    w = SCAN_PAIR_W
    for s in range(o_ref.shape[-1] // w):
        cols = slice(s * w, (s + 1) * w)
        y = yf_ref[:, cols] + yr_ref[:, cols]
        mu = _head_sum(y) * (1.0 / RWKV_HEAD)
        dev = y - mu
        var = _head_sum(dev * dev) * (1.0 / RWKV_HEAD)
        yn = dev * lax.rsqrt(var + GN_EPS) * gn_ref[0:1, cols] + gn_ref[1:2, cols]
        o_ref[:, cols] = ((yn + bonus_ref[:, cols]) * g_ref[:, cols]).astype(o_ref.dtype)


def rwkv_out(y_fwd, y_rev, bonus, g, gn_g, gn_b, tn=512):
    rows, d = y_fwd.shape
    tile = pl.BlockSpec((ROW_TILE, tn), lambda i, j: (i, j))
    return pl.pallas_call(
        _rwkv_out_kernel,
        grid=(rows // ROW_TILE, d // tn),
        in_specs=[tile, tile, tile, tile, pl.BlockSpec((2, tn), lambda i, j: (0, j))],
        out_specs=tile,
        out_shape=jax.ShapeDtypeStruct((rows, d), BF16),
        compiler_params=_params(("parallel", "parallel")),
        name="rwkv_out",
    )(y_fwd, y_rev, bonus, g, jnp.stack([gn_g, gn_b]))


def _route_topk(logits):
    lane = lax.broadcasted_iota(jnp.int32, logits.shape, 1)
    far = 4 * HEAD_W

    def first_max(vals):
        top = jnp.max(vals, axis=-1, keepdims=True)
        return top, jnp.min(jnp.where(vals == top, lane, far), axis=-1, keepdims=True)

    is_group = lane < N_GROUPS
    g_top, g_sel = first_max(jnp.where(is_group, logits, NEG_INF))
    p_sel = 1.0 / jnp.sum(jnp.where(is_group, jnp.exp(logits - g_top), 0.0), axis=-1, keepdims=True)
    lo = N_GROUPS + g_sel * EXPERTS_PER_GROUP
    le = jnp.where((lane >= lo) & (lane < lo + EXPERTS_PER_GROUP), logits, NEG_INF)
    v1, i1 = first_max(le)
    v2, i2 = first_max(jnp.where(lane == i1, NEG_INF, le))
    e2 = jnp.exp(v2 - v1)
    w1 = p_sel / (1.0 + e2)
    w2 = p_sel * e2 / (1.0 + e2)
    out = jnp.where(lane == 0, (i1 - N_GROUPS).astype(F32), 0.0)
    out = jnp.where(lane == 1, (i2 - N_GROUPS).astype(F32), out)
    out = jnp.where(lane == 2, w1, out)
    return jnp.where(lane == 3, w2, out)


def moe_dispatch(route):
    t = route.shape[0]
    n_pairs = TOP_K * t
    tm = MOE_TILE
    n_slots = n_pairs + N_EXPERTS * tm
    n_tiles = n_slots // tm
    e_flat = route[:, :TOP_K].astype(jnp.int32).reshape(-1)
    gate_bits = lax.bitcast_convert_type(route[:, TOP_K:2 * TOP_K], jnp.int32).reshape(-1)
    onehot = (e_flat[:, None] == jnp.arange(N_EXPERTS, dtype=jnp.int32)[None, :]).astype(jnp.int32)
    csum = jnp.cumsum(onehot, axis=0)
    counts = csum[-1]
    padded = ((counts + tm - 1) // tm) * tm
    pend = jnp.cumsum(padded)
    pstart = pend - padded
    pos = jnp.sum(onehot * (pstart[None, :] + csum - onehot), axis=1)
    pair = jnp.arange(n_pairs, dtype=jnp.int32)
    slots = jnp.full((n_slots, 2), -1, jnp.int32).at[pos].set(jnp.stack([pair, gate_bits], axis=1))
    valid = slots[:, 0] >= 0
    tok = jnp.where(valid, slots[:, 0] // TOP_K, 0)
    dst = jnp.where(valid, (slots[:, 0] % TOP_K) * t + slots[:, 0] // TOP_K,
                    n_pairs + jnp.arange(n_slots, dtype=jnp.int32) % tm)
    gate = jnp.where(valid, lax.bitcast_convert_type(slots[:, 1], F32), 0.0)
    tile_start = jnp.arange(n_tiles, dtype=jnp.int32) * tm
    tile_e = jnp.sum((tile_start[:, None] >= pend[None, :]).astype(jnp.int32), axis=1)
    tile_e = jnp.minimum(tile_e, N_EXPERTS - 1)
    n_live = (pend[-1] // tm).reshape(1)
    return (tok.reshape(n_tiles, 1, tm), dst.reshape(n_tiles, 1, tm), gate.reshape(n_slots, 1),
            tile_e.astype(jnp.int32), n_live.astype(jnp.int32))


def _moe_kernel(te_ref, nl_ref, tok_ref, tokn_ref, dst_ref, dstp_ref, gate_ref, h_hbm, w1_ref, w3_ref, w2_ref,
                y_hbm, xbuf, obuf, w1b, w3b, w2b, sem_in, sem_out):
    i = pl.program_id(0)
    n_live = nl_ref[0]
    tm = MOE_TILE
    slot = i % 2
    other = 1 - slot

    def gather_copy(src_row, r, s):
        return pltpu.make_async_copy(h_hbm.at[pl.ds(src_row, 1), :], xbuf.at[s, pl.ds(r, 1), :], sem_in.at[s])

    def scatter_copy(r, dst_row, s):
        return pltpu.make_async_copy(obuf.at[s, pl.ds(r, 1), :], y_hbm.at[pl.ds(dst_row, 1), :], sem_out.at[s])

    def wait_gather(s):
        pltpu.make_async_copy(h_hbm.at[pl.ds(0, tm), :], xbuf.at[s], sem_in.at[s]).wait()

    def wait_scatter(s):
        pltpu.make_async_copy(obuf.at[s], y_hbm.at[pl.ds(0, tm), :], sem_out.at[s]).wait()

    def per_row(fn):
        def body(r, carry):
            fn(r)
            return carry
        lax.fori_loop(0, tm, body, 0, unroll=MOE_DMA_UNROLL)

    def expert_tile(scatter_previous):
        packed = xbuf[slot]
        half = packed.shape[-1]
        x_lo = lax.bitcast_convert_type(packed << 16, F32).astype(BF16)
        x_hi = lax.bitcast_convert_type(packed & jnp.uint32(0xFFFF0000), F32).astype(BF16)
        for r in range(tm):
            gather_copy(tokn_ref[0, r], r, other).start(priority=r % 2)
        if scatter_previous:
            for r in range(tm):
                scatter_copy(r, dstp_ref[0, r], other).start(priority=r % 2)
        up = _dot(x_lo, w1b[0:half, :]) + _dot(x_hi, w1b[half:, :])
        gate_in = _dot(x_lo, w3b[0:half, :]) + _dot(x_hi, w3b[half:, :])
        hid = (up * jax.nn.sigmoid(up)) * gate_in * gate_ref[...]
        return _dot(hid.astype(BF16), w2b[...])

    @pl.when(i == 0)
    def _():
        per_row(lambda r: gather_copy(tok_ref[0, r], r, 0).start())

    @pl.when(i < n_live)
    def _():
        wait_gather(slot)
        prev_e = te_ref[jnp.maximum(i - 1, 0)]

        @pl.when((i == 0) | (te_ref[i] != prev_e))
        def _():
            w1b[...] = w1_ref[...].astype(BF16)
            w3b[...] = w3_ref[...].astype(BF16)
            w2b[...] = w2_ref[...].astype(BF16)

        @pl.when(i == 0)
        def _():
            obuf[0] = expert_tile(False)
            first_spare = y_hbm.shape[0] - tm
            per_row(lambda r: scatter_copy(r, first_spare + r, 0).start())
            wait_scatter(0)

        @pl.when(i > 0)
        def _():
            out = expert_tile(True)

            @pl.when(i > 1)
            def _():
                wait_scatter(slot)

            obuf[slot] = out

        @pl.when(i == n_live - 1)
        def _():
            per_row(lambda r: scatter_copy(r, dst_ref[0, r], slot).start())
            wait_gather(other)

            @pl.when(i > 0)
            def _():
                wait_scatter(other)

            wait_scatter(slot)


def moe_experts(h, tok, dst, gate, tile_e, n_live, w1, w3, w2, layer):
    t = h.shape[0]
    d = w1.shape[-2]
    f = w1.shape[-1]
    tm = MOE_TILE
    n_tiles = tok.shape[0]

    def w_index(i, te, nl):
        return (layer, te[i] // EXPERTS_PER_GROUP, te[i] % EXPERTS_PER_GROUP, 0, 0)

    grid_spec = pltpu.PrefetchScalarGridSpec(
        num_scalar_prefetch=2,
        grid=(n_tiles,),
        in_specs=[pl.BlockSpec((None, 1, tm), lambda i, te, nl: (i, 0, 0), memory_space=pltpu.SMEM),
                  pl.BlockSpec((None, 1, tm), lambda i, te, nl: (jnp.minimum(i + 1, n_tiles - 1), 0, 0),
                               memory_space=pltpu.SMEM),
                  pl.BlockSpec((None, 1, tm), lambda i, te, nl: (i, 0, 0), memory_space=pltpu.SMEM),
                  pl.BlockSpec((None, 1, tm), lambda i, te, nl: (jnp.maximum(i - 1, 0), 0, 0),
                               memory_space=pltpu.SMEM),
                  pl.BlockSpec((tm, 1), lambda i, te, nl: (i, 0)),
                  pl.BlockSpec(memory_space=pl.ANY),
                  pl.BlockSpec((None, None, None, d, f), w_index),
                  pl.BlockSpec((None, None, None, d, f), w_index),
                  pl.BlockSpec((None, None, None, f, d), w_index)],
        out_specs=pl.BlockSpec(memory_space=pl.ANY),
        scratch_shapes=[pltpu.VMEM((2, tm, d // 2), jnp.uint32), pltpu.VMEM((2, tm, d), F32),
                        pltpu.VMEM((d, f), BF16), pltpu.VMEM((d, f), BF16), pltpu.VMEM((f, d), BF16),
                        pltpu.SemaphoreType.DMA((2,)), pltpu.SemaphoreType.DMA((2,))])
    return pl.pallas_call(
        _moe_kernel,
        grid_spec=grid_spec,
        out_shape=jax.ShapeDtypeStruct((TOP_K * t + tm, d), F32),
        compiler_params=_params(("arbitrary",)),
        name="moe_experts",
    )(tile_e, n_live, tok, tok, dst, dst, gate, h, w1, w3, w2)


def _mod_spec(k):
    per_b = S_ALL // ROW_TILE
    return pl.BlockSpec((None, 1, D_MODEL),
                        lambda i: ((i // per_b * 2 + jnp.minimum(i % per_b, 1)) * N_MOD + k, 0, 0))


def _norm_mod(x, g_ref, sh_ref, sc_ref):
    h = x * lax.rsqrt(jnp.mean(x * x, axis=-1, keepdims=True) + NORM_EPS) * g_ref[...]
    if sh_ref is None:
        return h
    return h * (1.0 + sc_ref[...]) + sh_ref[...]


def _first_norm_kernel(x_ref, g_ref, sh_ref, sc_ref, h_ref):
    h_ref[...] = _norm_mod(x_ref[...], g_ref, sh_ref, sc_ref).astype(h_ref.dtype)


def first_norm(xs, g, mod, h_dtype):
    rows, d = xs.shape
    tile = pl.BlockSpec((ROW_TILE, d), lambda i: (i, 0))
    return pl.pallas_call(
        _first_norm_kernel,
        grid=(rows // ROW_TILE,),
        in_specs=[tile, pl.BlockSpec((1, d), lambda i: (0, 0)), _mod_spec(0), _mod_spec(1)],
        out_specs=tile,
        out_shape=jax.ShapeDtypeStruct((rows, d), h_dtype),
        compiler_params=_params(("parallel",)),
        name="first_norm",
    )(xs, g.reshape(1, d), mod, mod)


def _post_mixer_kernel(x_ref, u_ref, gate_ref, g_ref, sh_ref, sc_ref, wr_ref, br_ref, xo_ref, h_ref, route_ref):
    x = x_ref[...] + gate_ref[...] * u_ref[...]
    xo_ref[...] = x
    h = _norm_mod(x, g_ref, sh_ref, sc_ref)
    half = h.shape[-1] // 2
    bits = lax.bitcast_convert_type(h.astype(BF16).astype(F32), jnp.uint32)
    h_ref[...] = (bits[:, half:] & jnp.uint32(0xFFFF0000)) | (bits[:, :half] >> 16)
    logits = jnp.dot(h, wr_ref[...], precision=lax.Precision.HIGHEST, preferred_element_type=F32) + br_ref[...]
    route_ref[...] = _route_topk(logits)


def post_mixer(xs, upd, g, mod, router_w, router_b):
    rows, d = xs.shape
    tile = pl.BlockSpec((ROW_TILE, d), lambda i: (i, 0))
    ptile = pl.BlockSpec((ROW_TILE, d // 2), lambda i: (i, 0))
    rtile = pl.BlockSpec((ROW_TILE, HEAD_W), lambda i: (i, 0))
    return pl.pallas_call(
        _post_mixer_kernel,
        grid=(rows // ROW_TILE,),
        in_specs=[tile, tile, _mod_spec(2), pl.BlockSpec((1, d), lambda i: (0, 0)), _mod_spec(3), _mod_spec(4),
                  pl.BlockSpec((d, HEAD_W), lambda i: (0, 0)), pl.BlockSpec((1, HEAD_W), lambda i: (0, 0))],
        out_specs=[tile, ptile, rtile],
        out_shape=[jax.ShapeDtypeStruct((rows, d), F32), jax.ShapeDtypeStruct((rows, d // 2), jnp.uint32),
                   jax.ShapeDtypeStruct((rows, HEAD_W), F32)],
        compiler_params=_params(("parallel",)),
        name="post_mixer",
    )(xs, upd, mod, g.reshape(1, d), mod, mod, router_w, router_b)


def _post_moe_kernel(x_ref, ya_ref, yb_ref, gate_ref, g_ref, *rest, modulate):
    if modulate:
        sh_ref, sc_ref, xo_ref, h_ref = rest
    else:
        sh_ref = sc_ref = None
        xo_ref, h_ref = rest
    x = x_ref[...] + gate_ref[...] * (ya_ref[...] + yb_ref[...])
    xo_ref[...] = x
    h_ref[...] = _norm_mod(x, g_ref, sh_ref, sc_ref).astype(h_ref.dtype)


def post_moe(xs, y2, g, mod, next_mod, h_dtype):
    rows, d = xs.shape
    tile = pl.BlockSpec((ROW_TILE, d), lambda i: (i, 0))
    second = pl.BlockSpec((ROW_TILE, d), lambda i: (rows // ROW_TILE + i, 0))
    gspec = pl.BlockSpec((1, d), lambda i: (0, 0))
    modulate = next_mod is not None
    in_specs = [tile, tile, second, _mod_spec(5), gspec] + ([_mod_spec(0), _mod_spec(1)] if modulate else [])
    args = (xs, y2, y2, mod, g.reshape(1, d)) + ((next_mod, next_mod) if modulate else ())
    return pl.pallas_call(
        functools.partial(_post_moe_kernel, modulate=modulate),
        grid=(rows // ROW_TILE,),
        in_specs=in_specs,
        out_specs=[tile, tile],
        out_shape=[jax.ShapeDtypeStruct((rows, d), F32), jax.ShapeDtypeStruct((rows, d), h_dtype)],
        compiler_params=_params(("parallel",)),
        name="post_moe",
    )(*args)


def _pad_cols(w, n):
    return jnp.pad(w, ((0, 0), (0, n - w.shape[1])))


def _two_dir_lora(x, w1, w2):
    r = w1.shape[-1]
    w1c = _pad_cols(jnp.concatenate([w1[0], w1[1]], axis=1), 256).astype(BF16)
    d = w2.shape[-1]
    w2bd = jnp.zeros((256, 2 * d), F32).at[:r, :d].set(w2[0]).at[r:2 * r, d:].set(w2[1]).astype(BF16)
    return matmul(x, w1c, F32, tn=256), w2bd


def rwkv_mixer(h, p, batch):
    mixes = shift_mix(h, p['x_mix'])
    r = matmul(mixes[0], p['w_rkv'][0].astype(BF16), F32)
    k = matmul(mixes[1], p['w_rkv'][1].astype(BF16), F32)
    v = matmul(mixes[2], p['w_rkv'][2].astype(BF16), F32)
    hid_w, w2bd_w = _two_dir_lora(mixes[3], p['dec_w1'], p['dec_w2'])
    hid_a, w2bd_a = _two_dir_lora(mixes[4], p['iclr_a1'], p['iclr_a2'])
    g_hid = matmul(mixes[5], p['gate_g1'].astype(BF16), F32, tn=256)
    g = matmul(jax.nn.sigmoid(g_hid).astype(BF16), p['gate_g2'].astype(BF16), F32)

    par = jnp.concatenate([p['dec_w0'], p['iclr_a0'], p['k_k'][None], p['k_a'][None], p['r_k'].reshape(1, -1),
                           jnp.zeros((1, D_MODEL), F32)], axis=0)
    y0, rm, d0, bonus = scan_prepare(r, k, v, jnp.tanh(hid_w).astype(BF16), hid_a.astype(BF16), w2bd_w, w2bd_a, par)
    y_fwd, y_rev = scan_apply(y0, rm, d0, batch)
    yo = rwkv_out(y_fwd, y_rev, bonus, g, p['gn_g'], p['gn_b'])
    return matmul(yo, p['w_out'].astype(BF16), F32)


def even_mixer(h, p, layer_idx, cos_t, sin_t, batch):
    rows = batch * S_ALL
    lam_init = 0.8 - 0.6 * math.exp(-0.3 * layer_idx)
    proj = matmul(h, p['w_in'].astype(BF16), BF16, tn=1024)
    proj = proj.reshape(batch, S_ALL, -1)
    lf = p['diff_lambda']
    lam = (jnp.exp(jnp.sum(lf[0] * lf[1])) - jnp.exp(jnp.sum(lf[2] * lf[3])) + lam_init).reshape(1)
    a_out = diff_attention(proj, lam, cos_t, sin_t, p['subln_g'], lam_init)
    b_out = neighbourhood_attention(proj, na_bias_table(p['rpb']))
    mixed = jnp.concatenate([a_out, b_out], axis=-1).reshape(rows, -1)
    return matmul(mixed, p['w_out'].astype(BF16), F32)


def router_table(router_g, router_g_b, router_e, router_e_b):
    w = _pad_cols(jnp.concatenate([router_g, router_e], axis=1), HEAD_W)
    b = _pad_cols(jnp.concatenate([router_g_b, router_e_b])[None, :], HEAD_W)
    return w, b


def kernel(x, c, ctx, c_ctx, ada_w, ada_b, norm_g, final_g, even_w_in, even_w_out, diff_lambda, diff_subln_g, na_rpb, rwkv_x_mix, rwkv_w_rkv, rwkv_w_out, rwkv_dec_w0, rwkv_dec_w1, rwkv_dec_w2, rwkv_iclr_a0, rwkv_iclr_a1, rwkv_iclr_a2, rwkv_gate_g1, rwkv_gate_g2, rwkv_k_k, rwkv_k_a, rwkv_r_k, rwkv_gn_g, rwkv_gn_b, moe_router_g, moe_router_g_b, moe_router_e, moe_router_e_b, moe_w1, moe_w3, moe_w2):
    batch = x.shape[0]
    rows = batch * S_ALL
    xs = jnp.concatenate([ctx, x], axis=1).reshape(rows, D_MODEL)
    cos_t, sin_t = rope_tables()

    cvec = jnp.concatenate([c, c_ctx[None], jnp.zeros((8 - batch - 1, D_MODEL), F32)], axis=0)
    mods = ada_modulation(jax.nn.silu(cvec).astype(BF16), ada_w, ada_b)
    mods = mods.reshape(DEPTH, 8, N_MOD, D_MODEL)

    def mod_table(i):
        mod_l = mods[i, :batch]
        mod_c = jnp.broadcast_to(mods[i, batch][None], mod_l.shape)
        return jnp.stack([mod_c, mod_l], axis=1).reshape(batch * 2 * N_MOD, 1, D_MODEL)

    mod = mod_table(0)
    h = first_norm(xs, norm_g[0, 0], mod, BF16)
    for i in range(DEPTH):
        j = i // 2
        if i % 2 == 0:
            p = dict(w_in=even_w_in[j], w_out=even_w_out[j], diff_lambda=diff_lambda[j],
                     subln_g=diff_subln_g[j], rpb=na_rpb[j])
            out = even_mixer(h, p, i, cos_t, sin_t, batch)
        else:
            p = dict(x_mix=rwkv_x_mix[j], w_rkv=rwkv_w_rkv[j], w_out=rwkv_w_out[j], dec_w0=rwkv_dec_w0[j],
                     dec_w1=rwkv_dec_w1[j], dec_w2=rwkv_dec_w2[j], iclr_a0=rwkv_iclr_a0[j],
                     iclr_a1=rwkv_iclr_a1[j], iclr_a2=rwkv_iclr_a2[j], gate_g1=rwkv_gate_g1[j],
                     gate_g2=rwkv_gate_g2[j], k_k=rwkv_k_k[j], k_a=rwkv_k_a[j], r_k=rwkv_r_k[j],
                     gn_g=rwkv_gn_g[j], gn_b=rwkv_gn_b[j])
            out = rwkv_mixer(h, p, batch)
        rw, rb = router_table(moe_router_g[i], moe_router_g_b[i], moe_router_e[i], moe_router_e_b[i])
        xs, h2, route = post_mixer(xs, out, norm_g[i, 1], mod, rw, rb)
        tok, dst, gate, tile_e, n_live = moe_dispatch(route)
        y2 = moe_experts(h2, tok, dst, gate, tile_e, n_live, moe_w1, moe_w3, moe_w2, i)
        if i + 1 < DEPTH:
            next_mod = mod_table(i + 1)
            xs, h = post_moe(xs, y2, norm_g[i + 1, 0], mod, next_mod, F32 if (i + 1) % 2 else BF16)
            mod = next_mod
        else:
            _, h = post_moe(xs, y2, final_g, mod, None, F32)
    return h.reshape(batch, S_ALL, D_MODEL)[:, CTX_LEN:]
```

```python
import functools
import math

import numpy as np
import jax
import jax.numpy as jnp
from jax import lax
from jax.experimental import pallas as pl
from jax.experimental.pallas import tpu as pltpu

F32 = jnp.float32
BF16 = jnp.bfloat16

D_MODEL = 2048
DEPTH = 4
GRID_W = 64
CTX_LEN = 256
SEQ = 4096
S_ALL = CTX_LEN + SEQ
N_MOD = 6
NORM_EPS = 1e-6
NEG_INF = -1e30

DIFF_HEADS = 8
DIFF_QK_DIM = 64
NA_HEADS = 8
NA_DIM = 128
NA_KH = 8
NA_KW = 16
ROPE_THETA = 10000.0
SUBLN_EPS = 1e-5
HEAD_W = 128
ATT_TILE = 256
DIFF_KV_BLOCK = 2048
NA_ROWS_PER_STEP = 4
NA_WIN_ROWS = 12

RWKV_HEAD = 64
RWKV_HEADS = D_MODEL // RWKV_HEAD
GN_EPS = 64e-5
SCAN_CHUNK = 64
SCAN_PAIR_W = 2 * RWKV_HEAD
SCAN_SUB = 16

N_GROUPS = 4
EXPERTS_PER_GROUP = 8
N_EXPERTS = N_GROUPS * EXPERTS_PER_GROUP
TOP_K = 2
EXPERT_FF = 512
MOE_TILE = 256
MOE_DMA_UNROLL = 8
ROW_TILE = 256

VMEM_LIMIT = 52 * 1024 * 1024


def _params(sem):
    return pltpu.CompilerParams(dimension_semantics=sem, vmem_limit_bytes=VMEM_LIMIT)


def _dot(a, b):
    return jnp.dot(a, b, preferred_element_type=F32)


def _dot_nt(a, b):
    return lax.dot_general(a, b, (((1,), (1,)), ((), ())), preferred_element_type=F32)


def _dot_tn(a, b):
    return lax.dot_general(a, b, (((0,), (0,)), ((), ())), preferred_element_type=F32)


def _mm_kernel(a_ref, w_ref, o_ref):
    o_ref[...] = _dot(a_ref[...], w_ref[...]).astype(o_ref.dtype)


def matmul(a, w, out_dtype, tm=1024, tn=512):
    m, k = a.shape
    n = w.shape[1]
    while m % tm:
        tm //= 2
    tn = min(tn, n)
    assert n % tn == 0
    return pl.pallas_call(
        _mm_kernel,
        grid=(m // tm, n // tn),
        in_specs=[pl.BlockSpec((tm, k), lambda i, j: (i, 0)),
                  pl.BlockSpec((k, tn), lambda i, j: (0, j))],
        out_specs=pl.BlockSpec((tm, tn), lambda i, j: (i, j)),
        out_shape=jax.ShapeDtypeStruct((m, n), out_dtype),
        compiler_params=_params(("parallel", "parallel")),
        name="matmul",
    )(a, w)


def _ada_kernel(s_ref, w_ref, b_ref, o_ref):
    o_ref[...] = _dot(s_ref[...], w_ref[...].astype(BF16)) + b_ref[...]


def ada_modulation(svec, ada_w, ada_b, tn=1024):
    nl, d, n = ada_w.shape
    rows = svec.shape[0]
    return pl.pallas_call(
        _ada_kernel,
        grid=(nl, n // tn),
        in_specs=[pl.BlockSpec((rows, d), lambda l, j: (0, 0)),
                  pl.BlockSpec((None, d, tn), lambda l, j: (l, 0, j)),
                  pl.BlockSpec((None, 1, tn), lambda l, j: (l, 0, j))],
        out_specs=pl.BlockSpec((None, rows, tn), lambda l, j: (l, 0, j)),
        out_shape=jax.ShapeDtypeStruct((nl, rows, n), F32),
        compiler_params=_params(("parallel", "parallel")),
        name="ada_modulation",
    )(svec, ada_w, ada_b.reshape(nl, 1, n))


def _rope(x, cos, sin_signed):
    lane = lax.broadcasted_iota(jnp.int32, x.shape, 1)
    first_half = (lane & 63) < 32
    partner = jnp.where(first_half, pltpu.roll(x, HEAD_W - 32, 1), pltpu.roll(x, 32, 1))
    return x * cos + partner * sin_signed


def _diff_attn_kernel(lam_ref, q_ref, k_ref, v_ref, cos_ref, sin_ref, g_ref, o_ref, kr_ref, vt_ref, sa_ref, sb_ref, *,
                      post_scale):
    j = pl.program_id(2)
    tq = ATT_TILE

    @pl.when(j == 0)
    def _():
        def prep_chunk(c, carry):
            rows = pl.ds(pl.multiple_of(c * ATT_TILE, ATT_TILE), ATT_TILE)
            kr_ref[rows, :] = _rope(k_ref[rows, :].astype(F32), cos_ref[rows, :], sin_ref[rows, :]).astype(BF16)
            vt_ref[:, rows] = v_ref[rows, :].astype(F32).T.astype(BF16)
            return carry
        lax.fori_loop(0, S_ALL // ATT_TILE, prep_chunk, 0)

    qrows = pl.ds(pl.multiple_of(j * tq, tq), tq)
    q = _rope(q_ref[...].astype(F32), cos_ref[qrows, :], sin_ref[qrows, :]) * (DIFF_QK_DIM ** -0.5)
    lane = lax.broadcasted_iota(jnp.int32, q.shape, 1)
    q1 = jnp.where(lane < DIFF_QK_DIM, q, 0.0).astype(BF16)
    q2 = jnp.where(lane < DIFF_QK_DIM, 0.0, q).astype(BF16)

    kb = DIFF_KV_BLOCK
    n_latent = SEQ // kb
    maps = (q1, q2)

    def score_stage(start, size, sbuf):
        kc = kr_ref[pl.ds(start, size), :]
        tops = []
        for i, qm in enumerate(maps):
            s = _dot_nt(kc, qm)
            sbuf[i, 0:size, :] = s
            tops.append(jnp.max(s, axis=0, keepdims=True))
        return tuple(tops)

    def softmax_stage(start, size, sbuf, tops, state):
        vt = vt_ref[:, pl.ds(start, size)]
        new = []
        for i in range(2):
            m, l, acc = state[3 * i:3 * i + 3]
            m_new = jnp.maximum(m, tops[i])
            alpha = jnp.exp(m - m_new)
            p = jnp.exp(sbuf[i, 0:size, :] - m_new)
            new += [m_new, alpha * l + jnp.sum(p, axis=0, keepdims=True),
                    alpha * acc + _dot(vt, p.astype(BF16))]
        return tuple(new)

    def latent_start(n):
        return pl.multiple_of(CTX_LEN + jnp.minimum(n, n_latent - 1) * kb, CTX_LEN)

    row = jnp.full((1, tq), NEG_INF, F32)
    zrow = jnp.zeros((1, tq), F32)
    zacc = jnp.zeros((HEAD_W, tq), F32)
    state = softmax_stage(0, CTX_LEN, sa_ref, score_stage(0, CTX_LEN, sa_ref), (row, zrow, zacc, row, zrow, zacc))
    tops = score_stage(latent_start(0), kb, sb_ref)

    def block_pair(n, carry, more_follow):
        st, tp = carry[:6], carry[6:]
        tp_next = score_stage(latent_start(n + 1), kb, sa_ref)
        st = softmax_stage(latent_start(n), kb, sb_ref, tp, st)
        tp_after = score_stage(latent_start(n + 2), kb, sb_ref) if more_follow else tp_next
        st = softmax_stage(latent_start(n + 1), kb, sa_ref, tp_next, st)
        return st + tp_after

    n_pairs = n_latent // 2
    carry = lax.fori_loop(0, jnp.where(j == 0, 0, n_pairs - 1),
                          lambda t, c: block_pair(2 * t, c, True), state + tops)
    carry = lax.fori_loop(0, jnp.where(j == 0, 0, 1),
                          lambda t, c: block_pair(2 * (n_pairs - 1), c, False), carry)
    m1, l1, a1, m2, l2, a2 = carry[:6]
    out = a1 / l1 - lam_ref[0] * (a2 / l2)
    ms = jnp.mean(out * out, axis=0, keepdims=True)
    y = out * lax.rsqrt(ms + SUBLN_EPS) * (g_ref[...] * post_scale)
    o_ref[...] = y.T.astype(o_ref.dtype)


def diff_attention(proj, lam, cos_t, sin_t, subln_g, lam_init):
    b = proj.shape[0]
    kernel = functools.partial(_diff_attn_kernel, post_scale=1.0 - lam_init)
    return pl.pallas_call(
        kernel,
        grid=(b, DIFF_HEADS, S_ALL // ATT_TILE),
        in_specs=[pl.BlockSpec(memory_space=pltpu.SMEM),
                  pl.BlockSpec((None, ATT_TILE, HEAD_W), lambda bi, h, j: (bi, j, h)),
                  pl.BlockSpec((None, S_ALL, HEAD_W), lambda bi, h, j: (bi, 0, DIFF_HEADS + h)),
                  pl.BlockSpec((None, S_ALL, HEAD_W), lambda bi, h, j: (bi, 0, 2 * DIFF_HEADS + h)),
                  pl.BlockSpec((S_ALL, HEAD_W), lambda bi, h, j: (0, 0)),
                  pl.BlockSpec((S_ALL, HEAD_W), lambda bi, h, j: (0, 0)),
                  pl.BlockSpec((HEAD_W, 1), lambda bi, h, j: (0, 0))],
        out_specs=pl.BlockSpec((None, ATT_TILE, HEAD_W), lambda bi, h, j: (bi, j, h)),
        out_shape=jax.ShapeDtypeStruct((b, S_ALL, DIFF_HEADS * HEAD_W), BF16),
        scratch_shapes=[pltpu.VMEM((S_ALL, HEAD_W), BF16), pltpu.VMEM((HEAD_W, S_ALL), BF16),
                        pltpu.VMEM((2, DIFF_KV_BLOCK, ATT_TILE), F32), pltpu.VMEM((2, DIFF_KV_BLOCK, ATT_TILE), F32)],
        compiler_params=_params(("parallel", "parallel", "arbitrary")),
        name="diff_attention",
    )(lam, proj, proj, proj, cos_t, sin_t, subln_g.reshape(HEAD_W, 1))


def rope_tables():
    n_freq = DIFF_QK_DIM // 4
    inv_freq = ROPE_THETA ** (-jnp.arange(n_freq, dtype=F32) / n_freq)
    t = jnp.arange(SEQ, dtype=jnp.int32)
    row = (t // GRID_W).astype(F32)
    col = (t % GRID_W).astype(F32)
    ang = jnp.concatenate([row[:, None] * inv_freq, col[:, None] * inv_freq], axis=-1)
    cos, sin = jnp.cos(ang), jnp.sin(ang)
    cos_l = jnp.concatenate([cos, cos, cos, cos], axis=-1)
    sin_l = jnp.concatenate([-sin, sin, -sin, sin], axis=-1)
    cos_all = jnp.concatenate([jnp.ones((CTX_LEN, HEAD_W), F32), cos_l], axis=0)
    sin_all = jnp.concatenate([jnp.zeros((CTX_LEN, HEAD_W), F32), sin_l], axis=0)
    return cos_all, sin_all


def _na_window_start(j):
    g = j - 1
    return jnp.clip(NA_ROWS_PER_STEP * g - NA_KH // 2, 0, SEQ // GRID_W - NA_WIN_ROWS)


def _na_kernel(q_ref, k_ref, v_ref, bias_ref, o_ref):
    j = pl.program_id(2)
    win = NA_WIN_ROWS * GRID_W
    start = pl.multiple_of(CTX_LEN + _na_window_start(j) * GRID_W, GRID_W)
    scale = NA_DIM ** -0.5
    q = q_ref[...]
    s_c = _dot_nt(q, k_ref[pl.ds(0, CTX_LEN), :]) * scale
    s_w = _dot_nt(q, k_ref[pl.ds(start, win), :]) * scale + bias_ref[...]
    m = jnp.maximum(jnp.max(s_c, axis=-1, keepdims=True), jnp.max(s_w, axis=-1, keepdims=True))
    p_c = jnp.exp(s_c - m)
    p_w = jnp.exp(s_w - m)
    l = jnp.sum(p_c, axis=-1, keepdims=True) + jnp.sum(p_w, axis=-1, keepdims=True)
    o = _dot(p_c.astype(BF16), v_ref[pl.ds(0, CTX_LEN), :]) + _dot(p_w.astype(BF16), v_ref[pl.ds(start, win), :])
    o_ref[...] = (o / l).astype(o_ref.dtype)


def _na_bias_pattern(j):
    n_groups = SEQ // (GRID_W * NA_ROWS_PER_STEP)
    g = j - 1
    return jnp.where(j == 0, 3, jnp.where(g == 0, 0, jnp.where(g == n_groups - 1, 2, 1)))


def na_bias_table(rpb):
    rows = SEQ // GRID_W
    n_groups = rows // NA_ROWS_PER_STEP
    cols = np.arange(GRID_W)
    col_start = np.clip(cols - NA_KW // 2, 0, GRID_W - NA_KW)
    col_mask = (cols[None, :] >= col_start[:, None]) & (cols[None, :] < col_start[:, None] + NA_KW)
    c_idx = np.clip(cols[None, :] - cols[:, None] + NA_KW - 1, 0, 2 * NA_KW - 2)
    pats = []
    for g in (0, 1, n_groups - 1):
        u0 = int(np.clip(NA_ROWS_PER_STEP * g - NA_KH // 2, 0, rows - NA_WIN_ROWS))
        r = NA_ROWS_PER_STEP * g + np.arange(NA_ROWS_PER_STEP)
        r0 = np.clip(r - NA_KH // 2, 0, rows - NA_KH)
        kr = u0 + np.arange(NA_WIN_ROWS)
        valid_r = (kr[None, :] >= r0[:, None]) & (kr[None, :] < r0[:, None] + NA_KH)
        r_idx = np.clip(kr[None, :] - r[:, None] + NA_KH - 1, 0, 2 * NA_KH - 2)
        valid = valid_r[:, None, :, None] & col_mask[None, :, None, :]
        r_sel = np.eye(2 * NA_KH - 1, dtype=np.float32)[r_idx]
        c_sel = np.eye(2 * NA_KW - 1, dtype=np.float32)[c_idx]
        rows_sel = jnp.einsum('qkr,hrc->hqkc', r_sel, rpb.astype(F32), precision=lax.Precision.HIGHEST)
        gathered = jnp.einsum('hqkc,abc->hqakb', rows_sel, c_sel, precision=lax.Precision.HIGHEST)
        pats.append(jnp.where(valid[None], gathered, NEG_INF))
    pats.append(jnp.full_like(pats[0], NEG_INF))
    tab = jnp.stack(pats, axis=1)
    return tab.reshape(NA_HEADS, 4, NA_ROWS_PER_STEP * GRID_W, NA_WIN_ROWS * GRID_W)


def neighbourhood_attention(proj, bias_tab):
    b = proj.shape[0]
    tq = NA_ROWS_PER_STEP * GRID_W
    assert tq == CTX_LEN
    win = NA_WIN_ROWS * GRID_W
    base = 3 * DIFF_HEADS
    return pl.pallas_call(
        _na_kernel,
        grid=(b, NA_HEADS, S_ALL // tq),
        in_specs=[pl.BlockSpec((None, tq, HEAD_W), lambda bi, h, j: (bi, j, base + h)),
                  pl.BlockSpec((None, S_ALL, HEAD_W), lambda bi, h, j: (bi, 0, base + NA_HEADS + h)),
                  pl.BlockSpec((None, S_ALL, HEAD_W), lambda bi, h, j: (bi, 0, base + 2 * NA_HEADS + h)),
                  pl.BlockSpec((None, None, tq, win), lambda bi, h, j: (h, _na_bias_pattern(j), 0, 0))],
        out_specs=pl.BlockSpec((None, tq, HEAD_W), lambda bi, h, j: (bi, j, h)),
        out_shape=jax.ShapeDtypeStruct((b, S_ALL, NA_HEADS * HEAD_W), BF16),
        compiler_params=_params(("parallel", "parallel", "arbitrary")),
        name="neighbourhood_attention",
    )(proj, proj, proj, bias_tab)


def _each(fn, *lists):
    return [fn(*args) for args in zip(*lists)]


def _head_sum(x):
    rows, w = x.shape
    r_i = lax.broadcasted_iota(jnp.int32, (w, w), 0)
    c_i = lax.broadcasted_iota(jnp.int32, (w, w), 1)
    ones_bd = jnp.where((r_i < RWKV_HEAD) == (c_i < RWKV_HEAD), 1.0, 0.0).astype(BF16)
    hi = x.astype(BF16)
    lo = (x - hi.astype(F32)).astype(BF16)
    s = _dot(jnp.concatenate([hi, lo], axis=0), ones_bd)
    return s[:rows] + s[rows:]


def _scan_prepare_kernel(r_ref, k_ref, v_ref, hw_ref, ha_ref, w2w0_ref, w2w1_ref, w2a0_ref, w2a1_ref, par_ref,
                         y0_ref, rm_ref, d0_ref, bonus_ref):
    c = SCAN_CHUNK
    w = SCAN_PAIR_W
    hw = RWKV_HEAD
    t_idx = lax.broadcasted_iota(jnp.int32, (c, w), 0)
    lane = lax.broadcasted_iota(jnp.int32, (c, w), 1)
    s_idx = lane & (hw - 1)
    head0 = lane < hw
    eye = s_idx == t_idx
    tt = lax.broadcasted_iota(jnp.int32, (c, c), 0)
    ss = lax.broadcasted_iota(jnp.int32, (c, c), 1)
    strict = (s_idx < t_idx, s_idx > t_idx)
    incl = (s_idx <= t_idx, s_idx >= t_idx)
    tri = (jnp.where(ss <= tt, 1.0, 0.0).astype(BF16), jnp.where(ss >= tt, 1.0, 0.0).astype(BF16))
    row2 = lax.broadcasted_iota(jnp.int32, (w, w), 0)
    lane2 = lax.broadcasted_iota(jnp.int32, (w, w), 1)
    bdmask = (row2 < hw) == (lane2 < hw)
    eye2 = row2 == lane2

    def bd(y):
        return jnp.where(bdmask, jnp.concatenate([y, y], axis=0), 0.0).astype(BF16)

    def pm(x, ybd):
        return _dot(x.astype(BF16), ybd)

    inst = [(d, sub) for d in range(2) for sub in range(SCAN_SUB)]
    dirs = [d for d, _ in inst]
    rows = [slice(sub * c, (sub + 1) * c) for _, sub in inst]
    par = par_ref[...]
    k_k, k_a, r_k = par[4:5, :], par[5:6, :], par[6:7, :]
    sub_rows = [slice(sub * c, (sub + 1) * c) for sub in range(SCAN_SUB)]
    r_s = [r_ref[rw, :] for rw in sub_rows]
    k_s = [k_ref[rw, :] for rw in sub_rows]
    v_s = [v_ref[rw, :] for rw in sub_rows]
    hid_w = hw_ref[...]
    hid_a = ha_ref[...]
    wl_all = [_dot(hid_w, w2w0_ref[...]), _dot(hid_w, w2w1_ref[...])]
    al_all = [_dot(hid_a, w2a0_ref[...]), _dot(hid_a, w2a1_ref[...])]
    wl_s = [[wl_all[d][rw, :] for rw in sub_rows] for d in range(2)]
    al_s = [[al_all[d][rw, :] for rw in sub_rows] for d in range(2)]

    def unit_key(k_i):
        kk_i = k_i * k_k
        return kk_i * lax.rsqrt(jnp.maximum(_head_sum(kk_i * kk_i), 1e-24))

    def log_decay(wl_i, d):
        z = -(par[d:d + 1, :] + wl_i)
        softplus = jnp.maximum(z, 0.0) + jnp.log(1.0 + jnp.exp(-jnp.abs(z)))
        return -jnp.exp(-softplus - 0.5)

    kk_s = _each(unit_key, k_s)
    a_s = [[jax.nn.sigmoid(par[2 + d:3 + d, :] + x) for x in al_s[d]] for d in range(2)]
    kd_s = [[k_i * (1.0 + (a_i - 1.0) * k_a) for k_i, a_i in zip(k_s, a_s[d])] for d in range(2)]
    bonus = [_head_sum(r_i * r_k * (kd0 + kd1)) * v_i for r_i, kd0, kd1, v_i in zip(r_s, kd_s[0], kd_s[1], v_s)]

    r = [r_s[sub] for _, sub in inst]
    kk = [kk_s[sub] for _, sub in inst]
    v = [v_s[sub] for _, sub in inst]
    ld = [log_decay(wl_s[d][sub], d) for d, sub in inst]
    a = [a_s[d][sub] for d, sub in inst]
    kd = [kd_s[d][sub] for d, sub in inst]

    def cumulative(ld_i, d):
        p_hi = ld_i.astype(BF16)
        rem = ld_i - p_hi.astype(F32)
        p_mid = rem.astype(BF16)
        p_lo = (rem - p_mid.astype(F32)).astype(BF16)
        cs = _dot(tri[d], jnp.concatenate([p_hi, p_mid, p_lo], axis=1))
        return cs[:, :w] + cs[:, w:2 * w] + cs[:, 2 * w:]

    lam = _each(cumulative, ld, dirs)
    lam_end = _each(lambda l, d: l[0:1, :] if d else l[c - 1:c, :], lam, dirs)
    g_cum = _each(jnp.exp, lam)
    g_inv = _each(lambda l: jnp.exp(-l), lam)
    g_prev = _each(lambda l, x: jnp.exp(l - x), lam, ld)
    g_rel = _each(lambda le, l: jnp.exp(le - l), lam_end, lam)
    g_end = _each(jnp.exp, lam_end)

    qk = _each(jnp.multiply, kk, g_prev)
    rt = _each(jnp.multiply, r, g_cum)
    beta = _each(jnp.multiply, kk, a)
    bt = _each(jnp.multiply, beta, g_inv)
    kt = _each(jnp.multiply, kd, g_inv)
    bh = _each(jnp.multiply, beta, g_rel)
    kh = _each(jnp.multiply, kd, g_rel)

    def big_product(qk_i, rt_i, bt_i, kt_i):
        rhs = jnp.concatenate([jnp.where(head0, bt_i, 0.0), jnp.where(head0, 0.0, bt_i),
                               jnp.where(head0, kt_i, 0.0), jnp.where(head0, 0.0, kt_i)], axis=0).astype(BF16)
        return _dot_nt(jnp.concatenate([qk_i, rt_i], axis=0).astype(BF16), rhs)

    big = _each(big_product, qk, rt, bt, kt)
    n_pow = _each(lambda b, d: jnp.where(strict[d], -b[:c, :w], 0.0), big, dirs)
    a_k = _each(lambda b, d: jnp.where(strict[d], b[:c, w:], 0.0), big, dirs)
    g_b = _each(lambda b, d: jnp.where(incl[d], b[c:, :w], 0.0), big, dirs)
    g_k = _each(lambda b, d: jnp.where(incl[d], b[c:, w:], 0.0), big, dirs)

    t_inv = _each(lambda n: jnp.where(eye, 1.0, 0.0) + n, n_pow)
    for _ in range(int(math.log2(c)) - 1):
        n_pow = _each(lambda n: pm(n, bd(n)), n_pow)
        t_inv = _each(lambda t, n: t + pm(t, bd(n)), t_inv, n_pow)

    v_bd = _each(bd, v)
    akv_gkv = _each(lambda ak, gk, vb: pm(jnp.concatenate([ak, gk], axis=0), vb), a_k, g_k, v_bd)
    x1 = _each(lambda x: x[:c], akv_gkv)
    uw = _each(lambda t, x, q: _dot(t.astype(BF16), jnp.concatenate([bd(x), bd(q)], axis=1)), t_inv, x1, qk)
    u0 = _each(lambda x: -x[:, :w], uw)
    wm = _each(lambda x: x[:, w:], uw)
    gb_uw = _each(lambda gb, u, wm_i: _dot(gb.astype(BF16), jnp.concatenate([bd(u), bd(wm_i)], axis=1)), g_b, u0, wm)
    y0 = _each(lambda x, y: x[c:] + y[:, :w], akv_gkv, gb_uw)
    rm = _each(lambda rt_i, y: rt_i - y[:, w:], rt, gb_uw)
    d0 = _each(lambda kh_i, bh_i, v_i, u: jnp.where(bdmask, _dot_tn(
        jnp.concatenate([kh_i, bh_i], axis=0).astype(BF16), jnp.concatenate([v_i, u], axis=0).astype(BF16)), 0.0),
        kh, bh, v, u0)
    mm = _each(lambda bh_i, wm_i, ge: jnp.where(eye2, ge, 0.0) - jnp.where(
        bdmask, _dot_tn(bh_i.astype(BF16), wm_i.astype(BF16)), 0.0), bh, wm, g_end)
    d0 = _each(lambda x: x[:hw] + x[hw:], d0)
    mm = _each(lambda x: x[:hw] + x[hw:], mm)
    mm_hi = _each(lambda m: m.astype(BF16), mm)
    mm_lo = _each(lambda m, mh: (m - mh.astype(F32)).astype(BF16), mm, mm_hi)

    for sub in range(SCAN_SUB):
        bonus_ref[sub_rows[sub], :] = bonus[sub]
    for i, (d, sub) in enumerate(inst):
        y0_ref[d, rows[i], :] = y0[i]
        rm_ref[d, sub, 0, 0:c, :] = rm[i].astype(BF16)
        rm_ref[d, sub, 0, c:c + hw, :] = mm_hi[i]
        rm_ref[d, sub, 0, c + hw:c + 2 * hw, :] = mm_lo[i]
        d0_ref[d, sub, 0, :, :] = d0[i]


def scan_prepare(r, k, v, hid_w, hid_a, w2_w, w2_a, par):
    rows, d = r.shape
    c, w = SCAN_CHUNK, SCAN_PAIR_W
    n_pairs = d // w
    n_chunks = rows // c
    blk = SCAN_SUB * c
    rank = hid_w.shape[1]
    shared = pl.BlockSpec((blk, w), lambda i, p: (i, p))
    hidden = pl.BlockSpec((blk, rank), lambda i, p: (i, 0))
    fwd_w2 = pl.BlockSpec((rank, w), lambda i, p: (0, p))
    rev_w2 = pl.BlockSpec((rank, w), lambda i, p: (0, n_pairs + p))
    return pl.pallas_call(
        _scan_prepare_kernel,
        grid=(rows // blk, n_pairs),
        in_specs=[shared, shared, shared, hidden, hidden, fwd_w2, rev_w2, fwd_w2, rev_w2,
                  pl.BlockSpec((8, w), lambda i, p: (0, p))],
        out_specs=[pl.BlockSpec((2, blk, w), lambda i, p: (0, i, p)),
                   pl.BlockSpec((2, SCAN_SUB, 1, c + w, w), lambda i, p: (0, i, p, 0, 0)),
                   pl.BlockSpec((2, SCAN_SUB, 1, w // 2, w), lambda i, p: (0, i, p, 0, 0)),
                   shared],
        out_shape=[jax.ShapeDtypeStruct((2, rows, d), F32),
                   jax.ShapeDtypeStruct((2, n_chunks, n_pairs, c + w, w), BF16),
                   jax.ShapeDtypeStruct((2, n_chunks, n_pairs, w // 2, w), F32),
                   jax.ShapeDtypeStruct((rows, d), F32)],
        compiler_params=_params(("parallel", "parallel")),
        name="scan_prepare",
    )(r, k, v, hid_w, hid_a, w2_w, w2_w, w2_a, w2_a, par)


def _scan_apply_kernel(y0f_ref, rmf_ref, d0f_ref, y0r_ref, rmr_ref, d0r_ref, yf_ref, yr_ref, z_ref, *, n_pairs):
    c, w = SCAN_CHUNK, SCAN_PAIR_W
    i = pl.program_id(1)

    @pl.when(i == 0)
    def _():
        z_ref[...] = jnp.zeros_like(z_ref)

    hw = RWKV_HEAD
    row2 = lax.broadcasted_iota(jnp.int32, (w, w), 0)
    lane2 = lax.broadcasted_iota(jnp.int32, (w, w), 1)
    bdmask = (row2 < hw) == (lane2 < hw)

    def bd(x):
        return jnp.where(bdmask, jnp.concatenate([x, x], axis=0), 0.0)

    chains = [(d, p) for d in range(2) for p in range(n_pairs)]
    y0_refs, rm_refs, d0_refs, y_refs = (y0f_ref, y0r_ref), (rmf_ref, rmr_ref), (d0f_ref, d0r_ref), (yf_ref, yr_ref)
    z = [bd(z_ref[d, p]) for d, p in chains]
    z_hi = _each(lambda x: x.astype(BF16), z)
    z_lo = _each(lambda x, xh: (x - xh.astype(F32)).astype(BF16), z, z_hi)
    res = [_dot(rm_refs[d][p], jnp.concatenate([zh, zl], axis=1)) for (d, p), zh, zl in zip(chains, z_hi, z_lo)]
    res = _each(lambda x: x[:, :w] + x[:, w:], res)
    y = [y0_refs[d][:, p * w:(p + 1) * w] + x[:c] for (d, p), x in zip(chains, res)]
    z_new = [d0_refs[d][p] + x[c:c + hw] + x[c + hw:] for (d, p), x in zip(chains, res)]
    for (d, p), y_i, z_i in zip(chains, y, z_new):
        y_refs[d][:, p * w:(p + 1) * w] = y_i
        z_ref[d, p] = z_i


def _scan_chunk_index(i, d):
    n_ctx = CTX_LEN // SCAN_CHUNK
    n_all = S_ALL // SCAN_CHUNK
    return jnp.where(d == 0, i, jnp.where(i < n_ctx, n_ctx - 1 - i, n_all + n_ctx - 1 - i))


def scan_apply(y0, rm, d0, batch):
    _, rows, d = y0.shape
    c, w = SCAN_CHUNK, SCAN_PAIR_W
    n_pairs = d // w
    per_b = rows // batch // c

    def specs(dr):
        def idx(b, i):
            return b * per_b + _scan_chunk_index(i, dr)
        return [pl.BlockSpec((None, c, d), lambda b, i: (dr, idx(b, i), 0)),
                pl.BlockSpec((None, None, n_pairs, c + w, w), lambda b, i: (dr, idx(b, i), 0, 0, 0)),
                pl.BlockSpec((None, None, n_pairs, w // 2, w), lambda b, i: (dr, idx(b, i), 0, 0, 0)),
                pl.BlockSpec((c, d), lambda b, i: (idx(b, i), 0))]

    fwd, rev = specs(0), specs(1)
    kernel = functools.partial(_scan_apply_kernel, n_pairs=n_pairs)
    return pl.pallas_call(
        kernel,
        grid=(batch, per_b),
        in_specs=fwd[:3] + rev[:3],
        out_specs=[fwd[3], rev[3]],
        out_shape=[jax.ShapeDtypeStruct((rows, d), F32)] * 2,
        scratch_shapes=[pltpu.VMEM((2, n_pairs, w // 2, w), F32)],
        compiler_params=_params(("parallel", "arbitrary")),
        name="scan_apply",
    )(y0, rm, d0, y0, rm, d0)


def _shift_mix_kernel(h_ref, hp_ref, hn_ref, mix_ref, *out_refs):
    per_b = S_ALL // ROW_TILE
    pos = pl.program_id(0) % per_b
    h = h_ref[...]
    row = lax.broadcasted_iota(jnp.int32, h.shape, 0)
    starts_seq = (pos == 0) | (pos == 1)
    ends_seq = (pos == 0) | (pos == per_b - 1)
    prev_row = jnp.where(starts_seq, 0.0, hp_ref[7:8, :])
    next_row = jnp.where(ends_seq, 0.0, hn_ref[0:1, :])
    prev = jnp.where(row == 0, prev_row, pltpu.roll(h, 1, 0))
    nxt = jnp.where(row == ROW_TILE - 1, next_row, pltpu.roll(h, ROW_TILE - 1, 0))
    xx = 0.5 * (prev + nxt) - h
    for j, o_ref in enumerate(out_refs):
        o_ref[...] = (h + xx * mix_ref[j:j + 1, :]).astype(o_ref.dtype)


def shift_mix(h, x_mix):
    rows, d = h.shape
    n_mix = x_mix.shape[0]
    sub = ROW_TILE // 8
    tile = pl.BlockSpec((ROW_TILE, d), lambda i: (i, 0))
    return pl.pallas_call(
        _shift_mix_kernel,
        grid=(rows // ROW_TILE,),
        in_specs=[tile,
                  pl.BlockSpec((8, d), lambda i: (jnp.maximum(i * sub - 1, 0), 0)),
                  pl.BlockSpec((8, d), lambda i: (jnp.minimum((i + 1) * sub, rows // 8 - 1), 0)),
                  pl.BlockSpec((n_mix, d), lambda i: (0, 0))],
        out_specs=[tile] * n_mix,
        out_shape=[jax.ShapeDtypeStruct((rows, d), BF16)] * n_mix,
        compiler_params=_params(("parallel",)),
        name="shift_mix",
    )(h, h, h, x_mix)


def _rwkv_out_kernel(yf_ref, yr_ref, bonus_ref, g_ref, gn_ref, o_ref):
    w = SCAN_PAIR_W
    for s in range(o_ref.shape[-1] // w):
        cols = slice(s * w, (s + 1) * w)
        y = yf_ref[:, cols] + yr_ref[:, cols]
        mu = _head_sum(y) * (1.0 / RWKV_HEAD)
        dev = y - mu
        var = _head_sum(dev * dev) * (1.0 / RWKV_HEAD)
        yn = dev * lax.rsqrt(var + GN_EPS) * gn_ref[0:1, cols] + gn_ref[1:2, cols]
        o_ref[:, cols] = ((yn + bonus_ref[:, cols]) * g_ref[:, cols]).astype(o_ref.dtype)


def rwkv_out(y_fwd, y_rev, bonus, g, gn_g, gn_b, tn=512):
    rows, d = y_fwd.shape
    tile = pl.BlockSpec((ROW_TILE, tn), lambda i, j: (i, j))
    return pl.pallas_call(
        _rwkv_out_kernel,
        grid=(rows // ROW_TILE, d // tn),
        in_specs=[tile, tile, tile, tile, pl.BlockSpec((2, tn), lambda i, j: (0, j))],
        out_specs=tile,
        out_shape=jax.ShapeDtypeStruct((rows, d), BF16),
        compiler_params=_params(("parallel", "parallel")),
        name="rwkv_out",
    )(y_fwd, y_rev, bonus, g, jnp.stack([gn_g, gn_b]))


def _route_topk(logits):
    lane = lax.broadcasted_iota(jnp.int32, logits.shape, 1)
    far = 4 * HEAD_W

    def first_max(vals):
        top = jnp.max(vals, axis=-1, keepdims=True)
        return top, jnp.min(jnp.where(vals == top, lane, far), axis=-1, keepdims=True)

    is_group = lane < N_GROUPS
    g_top, g_sel = first_max(jnp.where(is_group, logits, NEG_INF))
    p_sel = 1.0 / jnp.sum(jnp.where(is_group, jnp.exp(logits - g_top), 0.0), axis=-1, keepdims=True)
    lo = N_GROUPS + g_sel * EXPERTS_PER_GROUP
    le = jnp.where((lane >= lo) & (lane < lo + EXPERTS_PER_GROUP), logits, NEG_INF)
    v1, i1 = first_max(le)
    v2, i2 = first_max(jnp.where(lane == i1, NEG_INF, le))
    e2 = jnp.exp(v2 - v1)
    w1 = p_sel / (1.0 + e2)
    w2 = p_sel * e2 / (1.0 + e2)
    out = jnp.where(lane == 0, (i1 - N_GROUPS).astype(F32), 0.0)
    out = jnp.where(lane == 1, (i2 - N_GROUPS).astype(F32), out)
    out = jnp.where(lane == 2, w1, out)
    return jnp.where(lane == 3, w2, out)


def moe_dispatch(route):
    t = route.shape[0]
    n_pairs = TOP_K * t
    tm = MOE_TILE
    n_slots = n_pairs + N_EXPERTS * tm
    n_tiles = n_slots // tm
    e_flat = route[:, :TOP_K].astype(jnp.int32).reshape(-1)
    gate_bits = lax.bitcast_convert_type(route[:, TOP_K:2 * TOP_K], jnp.int32).reshape(-1)
    onehot = (e_flat[:, None] == jnp.arange(N_EXPERTS, dtype=jnp.int32)[None, :]).astype(jnp.int32)
    csum = jnp.cumsum(onehot, axis=0)
    counts = csum[-1]
    padded = ((counts + tm - 1) // tm) * tm
    pend = jnp.cumsum(padded)
    pstart = pend - padded
    pos = jnp.sum(onehot * (pstart[None, :] + csum - onehot), axis=1)
    pair = jnp.arange(n_pairs, dtype=jnp.int32)
    slots = jnp.full((n_slots, 2), -1, jnp.int32).at[pos].set(jnp.stack([pair, gate_bits], axis=1))
    valid = slots[:, 0] >= 0
    tok = jnp.where(valid, slots[:, 0] // TOP_K, 0)
    dst = jnp.where(valid, (slots[:, 0] % TOP_K) * t + slots[:, 0] // TOP_K,
                    n_pairs + jnp.arange(n_slots, dtype=jnp.int32) % tm)
    gate = jnp.where(valid, lax.bitcast_convert_type(slots[:, 1], F32), 0.0)
    tile_start = jnp.arange(n_tiles, dtype=jnp.int32) * tm
    tile_e = jnp.sum((tile_start[:, None] >= pend[None, :]).astype(jnp.int32), axis=1)
    tile_e = jnp.minimum(tile_e, N_EXPERTS - 1)
    n_live = (pend[-1] // tm).reshape(1)
    return (tok.reshape(n_tiles, 1, tm), dst.reshape(n_tiles, 1, tm), gate.reshape(n_slots, 1),
            tile_e.astype(jnp.int32), n_live.astype(jnp.int32))


def _moe_kernel(te_ref, nl_ref, tok_ref, tokn_ref, dst_ref, dstp_ref, gate_ref, h_hbm, w1_ref, w3_ref, w2_ref,
                y_hbm, xbuf, obuf, w1b, w3b, w2b, sem_in, sem_out):
    i = pl.program_id(0)
    n_live = nl_ref[0]
    tm = MOE_TILE
    slot = i % 2
    other = 1 - slot

    def gather_copy(src_row, r, s):
        return pltpu.make_async_copy(h_hbm.at[pl.ds(src_row, 1), :], xbuf.at[s, pl.ds(r, 1), :], sem_in.at[s])

    def scatter_copy(r, dst_row, s):
        return pltpu.make_async_copy(obuf.at[s, pl.ds(r, 1), :], y_hbm.at[pl.ds(dst_row, 1), :], sem_out.at[s])

    def wait_gather(s):
        pltpu.make_async_copy(h_hbm.at[pl.ds(0, tm), :], xbuf.at[s], sem_in.at[s]).wait()

    def wait_scatter(s):
        pltpu.make_async_copy(obuf.at[s], y_hbm.at[pl.ds(0, tm), :], sem_out.at[s]).wait()

    def per_row(fn):
        def body(r, carry):
            fn(r)
            return carry
        lax.fori_loop(0, tm, body, 0, unroll=MOE_DMA_UNROLL)

    def expert_tile(scatter_previous):
        packed = xbuf[slot]
        half = packed.shape[-1]
        x_lo = lax.bitcast_convert_type(packed << 16, F32).astype(BF16)
        x_hi = lax.bitcast_convert_type(packed & jnp.uint32(0xFFFF0000), F32).astype(BF16)
        for r in range(tm):
            gather_copy(tokn_ref[0, r], r, other).start(priority=r % 2)
        if scatter_previous:
            for r in range(tm):
                scatter_copy(r, dstp_ref[0, r], other).start(priority=r % 2)
        up = _dot(x_lo, w1b[0:half, :]) + _dot(x_hi, w1b[half:, :])
        gate_in = _dot(x_lo, w3b[0:half, :]) + _dot(x_hi, w3b[half:, :])
        hid = (up * jax.nn.sigmoid(up)) * gate_in * gate_ref[...]
        return _dot(hid.astype(BF16), w2b[...])

    @pl.when(i == 0)
    def _():
        per_row(lambda r: gather_copy(tok_ref[0, r], r, 0).start())

    @pl.when(i < n_live)
    def _():
        wait_gather(slot)
        prev_e = te_ref[jnp.maximum(i - 1, 0)]

        @pl.when((i == 0) | (te_ref[i] != prev_e))
        def _():
            w1b[...] = w1_ref[...].astype(BF16)
            w3b[...] = w3_ref[...].astype(BF16)
            w2b[...] = w2_ref[...].astype(BF16)

        @pl.when(i == 0)
        def _():
            obuf[0] = expert_tile(False)
            first_spare = y_hbm.shape[0] - tm
            per_row(lambda r: scatter_copy(r, first_spare + r, 0).start())
            wait_scatter(0)

        @pl.when(i > 0)
        def _():
            out = expert_tile(True)

            @pl.when(i > 1)
            def _():
                wait_scatter(slot)

            obuf[slot] = out

        @pl.when(i == n_live - 1)
        def _():
            per_row(lambda r: scatter_copy(r, dst_ref[0, r], slot).start())
            wait_gather(other)

            @pl.when(i > 0)
            def _():
                wait_scatter(other)

            wait_scatter(slot)


def moe_experts(h, tok, dst, gate, tile_e, n_live, w1, w3, w2, layer):
    t = h.shape[0]
    d = w1.shape[-2]
    f = w1.shape[-1]
    tm = MOE_TILE
    n_tiles = tok.shape[0]

    def w_index(i, te, nl):
        return (layer, te[i] // EXPERTS_PER_GROUP, te[i] % EXPERTS_PER_GROUP, 0, 0)

    grid_spec = pltpu.PrefetchScalarGridSpec(
        num_scalar_prefetch=2,
        grid=(n_tiles,),
        in_specs=[pl.BlockSpec((None, 1, tm), lambda i, te, nl: (i, 0, 0), memory_space=pltpu.SMEM),
                  pl.BlockSpec((None, 1, tm), lambda i, te, nl: (jnp.minimum(i + 1, n_tiles - 1), 0, 0),
                               memory_space=pltpu.SMEM),
                  pl.BlockSpec((None, 1, tm), lambda i, te, nl: (i, 0, 0), memory_space=pltpu.SMEM),
                  pl.BlockSpec((None, 1, tm), lambda i, te, nl: (jnp.maximum(i - 1, 0), 0, 0),
                               memory_space=pltpu.SMEM),
                  pl.BlockSpec((tm, 1), lambda i, te, nl: (i, 0)),
                  pl.BlockSpec(memory_space=pl.ANY),
                  pl.BlockSpec((None, None, None, d, f), w_index),
                  pl.BlockSpec((None, None, None, d, f), w_index),
                  pl.BlockSpec((None, None, None, f, d), w_index)],
        out_specs=pl.BlockSpec(memory_space=pl.ANY),
        scratch_shapes=[pltpu.VMEM((2, tm, d // 2), jnp.uint32), pltpu.VMEM((2, tm, d), F32),
                        pltpu.VMEM((d, f), BF16), pltpu.VMEM((d, f), BF16), pltpu.VMEM((f, d), BF16),
                        pltpu.SemaphoreType.DMA((2,)), pltpu.SemaphoreType.DMA((2,))])
    return pl.pallas_call(
        _moe_kernel,
        grid_spec=grid_spec,
        out_shape=jax.ShapeDtypeStruct((TOP_K * t + tm, d), F32),
        compiler_params=_params(("arbitrary",)),
        name="moe_experts",
    )(tile_e, n_live, tok, tok, dst, dst, gate, h, w1, w3, w2)


def _mod_spec(k):
    per_b = S_ALL // ROW_TILE
    return pl.BlockSpec((None, 1, D_MODEL),
                        lambda i: ((i // per_b * 2 + jnp.minimum(i % per_b, 1)) * N_MOD + k, 0, 0))


def _norm_mod(x, g_ref, sh_ref, sc_ref):
    h = x * lax.rsqrt(jnp.mean(x * x, axis=-1, keepdims=True) + NORM_EPS) * g_ref[...]
    if sh_ref is None:
        return h
    return h * (1.0 + sc_ref[...]) + sh_ref[...]


def _first_norm_kernel(x_ref, g_ref, sh_ref, sc_ref, h_ref):
    h_ref[...] = _norm_mod(x_ref[...], g_ref, sh_ref, sc_ref).astype(h_ref.dtype)


def first_norm(xs, g, mod, h_dtype):
    rows, d = xs.shape
    tile = pl.BlockSpec((ROW_TILE, d), lambda i: (i, 0))
    return pl.pallas_call(
        _first_norm_kernel,
        grid=(rows // ROW_TILE,),
        in_specs=[tile, pl.BlockSpec((1, d), lambda i: (0, 0)), _mod_spec(0), _mod_spec(1)],
        out_specs=tile,
        out_shape=jax.ShapeDtypeStruct((rows, d), h_dtype),
        compiler_params=_params(("parallel",)),
        name="first_norm",
    )(xs, g.reshape(1, d), mod, mod)


def _post_mixer_kernel(x_ref, u_ref, gate_ref, g_ref, sh_ref, sc_ref, wr_ref, br_ref, xo_ref, h_ref, route_ref):
    x = x_ref[...] + gate_ref[...] * u_ref[...]
    xo_ref[...] = x
    h = _norm_mod(x, g_ref, sh_ref, sc_ref)
    half = h.shape[-1] // 2
    bits = lax.bitcast_convert_type(h.astype(BF16).astype(F32), jnp.uint32)
    h_ref[...] = (bits[:, half:] & jnp.uint32(0xFFFF0000)) | (bits[:, :half] >> 16)
    logits = jnp.dot(h, wr_ref[...], precision=lax.Precision.HIGHEST, preferred_element_type=F32) + br_ref[...]
    route_ref[...] = _route_topk(logits)


def post_mixer(xs, upd, g, mod, router_w, router_b):
    rows, d = xs.shape
    tile = pl.BlockSpec((ROW_TILE, d), lambda i: (i, 0))
    ptile = pl.BlockSpec((ROW_TILE, d // 2), lambda i: (i, 0))
    rtile = pl.BlockSpec((ROW_TILE, HEAD_W), lambda i: (i, 0))
    return pl.pallas_call(
        _post_mixer_kernel,
        grid=(rows // ROW_TILE,),
        in_specs=[tile, tile, _mod_spec(2), pl.BlockSpec((1, d), lambda i: (0, 0)), _mod_spec(3), _mod_spec(4),
                  pl.BlockSpec((d, HEAD_W), lambda i: (0, 0)), pl.BlockSpec((1, HEAD_W), lambda i: (0, 0))],
        out_specs=[tile, ptile, rtile],
        out_shape=[jax.ShapeDtypeStruct((rows, d), F32), jax.ShapeDtypeStruct((rows, d // 2), jnp.uint32),
                   jax.ShapeDtypeStruct((rows, HEAD_W), F32)],
        compiler_params=_params(("parallel",)),
        name="post_mixer",
    )(xs, upd, mod, g.reshape(1, d), mod, mod, router_w, router_b)


def _post_moe_kernel(x_ref, ya_ref, yb_ref, gate_ref, g_ref, *rest, modulate):
    if modulate:
        sh_ref, sc_ref, xo_ref, h_ref = rest
    else:
        sh_ref = sc_ref = None
        xo_ref, h_ref = rest
    x = x_ref[...] + gate_ref[...] * (ya_ref[...] + yb_ref[...])
    xo_ref[...] = x
    h_ref[...] = _norm_mod(x, g_ref, sh_ref, sc_ref).astype(h_ref.dtype)


def post_moe(xs, y2, g, mod, next_mod, h_dtype):
    rows, d = xs.shape
    tile = pl.BlockSpec((ROW_TILE, d), lambda i: (i, 0))
    second = pl.BlockSpec((ROW_TILE, d), lambda i: (rows // ROW_TILE + i, 0))
    gspec = pl.BlockSpec((1, d), lambda i: (0, 0))
    modulate = next_mod is not None
    in_specs = [tile, tile, second, _mod_spec(5), gspec] + ([_mod_spec(0), _mod_spec(1)] if modulate else [])
    args = (xs, y2, y2, mod, g.reshape(1, d)) + ((next_mod, next_mod) if modulate else ())
    return pl.pallas_call(
        functools.partial(_post_moe_kernel, modulate=modulate),
        grid=(rows // ROW_TILE,),
        in_specs=in_specs,
        out_specs=[tile, tile],
        out_shape=[jax.ShapeDtypeStruct((rows, d), F32), jax.ShapeDtypeStruct((rows, d), h_dtype)],
        compiler_params=_params(("parallel",)),
        name="post_moe",
    )(*args)


def _pad_cols(w, n):
    return jnp.pad(w, ((0, 0), (0, n - w.shape[1])))


def _two_dir_lora(x, w1, w2):
    r = w1.shape[-1]
    w1c = _pad_cols(jnp.concatenate([w1[0], w1[1]], axis=1), 256).astype(BF16)
    d = w2.shape[-1]
    w2bd = jnp.zeros((256, 2 * d), F32).at[:r, :d].set(w2[0]).at[r:2 * r, d:].set(w2[1]).astype(BF16)
    return matmul(x, w1c, F32, tn=256), w2bd


def rwkv_mixer(h, p, batch):
    mixes = shift_mix(h, p['x_mix'])
    r = matmul(mixes[0], p['w_rkv'][0].astype(BF16), F32)
    k = matmul(mixes[1], p['w_rkv'][1].astype(BF16), F32)
    v = matmul(mixes[2], p['w_rkv'][2].astype(BF16), F32)
    hid_w, w2bd_w = _two_dir_lora(mixes[3], p['dec_w1'], p['dec_w2'])
    hid_a, w2bd_a = _two_dir_lora(mixes[4], p['iclr_a1'], p['iclr_a2'])
    g_hid = matmul(mixes[5], p['gate_g1'].astype(BF16), F32, tn=256)
    g = matmul(jax.nn.sigmoid(g_hid).astype(BF16), p['gate_g2'].astype(BF16), F32)

    par = jnp.concatenate([p['dec_w0'], p['iclr_a0'], p['k_k'][None], p['k_a'][None], p['r_k'].reshape(1, -1),
                           jnp.zeros((1, D_MODEL), F32)], axis=0)
    y0, rm, d0, bonus = scan_prepare(r, k, v, jnp.tanh(hid_w).astype(BF16), hid_a.astype(BF16), w2bd_w, w2bd_a, par)
    y_fwd, y_rev = scan_apply(y0, rm, d0, batch)
    yo = rwkv_out(y_fwd, y_rev, bonus, g, p['gn_g'], p['gn_b'])
    return matmul(yo, p['w_out'].astype(BF16), F32)


def even_mixer(h, p, layer_idx, cos_t, sin_t, batch):
    rows = batch * S_ALL
    lam_init = 0.8 - 0.6 * math.exp(-0.3 * layer_idx)
    proj = matmul(h, p['w_in'].astype(BF16), BF16, tn=1024)
    proj = proj.reshape(batch, S_ALL, -1)
    lf = p['diff_lambda']
    lam = (jnp.exp(jnp.sum(lf[0] * lf[1])) - jnp.exp(jnp.sum(lf[2] * lf[3])) + lam_init).reshape(1)
    a_out = diff_attention(proj, lam, cos_t, sin_t, p['subln_g'], lam_init)
    b_out = neighbourhood_attention(proj, na_bias_table(p['rpb']))
    mixed = jnp.concatenate([a_out, b_out], axis=-1).reshape(rows, -1)
    return matmul(mixed, p['w_out'].astype(BF16), F32)


def router_table(router_g, router_g_b, router_e, router_e_b):
    w = _pad_cols(jnp.concatenate([router_g, router_e], axis=1), HEAD_W)
    b = _pad_cols(jnp.concatenate([router_g_b, router_e_b])[None, :], HEAD_W)
    return w, b


def kernel(x, c, ctx, c_ctx, ada_w, ada_b, norm_g, final_g, even_w_in, even_w_out, diff_lambda, diff_subln_g, na_rpb, rwkv_x_mix, rwkv_w_rkv, rwkv_w_out, rwkv_dec_w0, rwkv_dec_w1, rwkv_dec_w2, rwkv_iclr_a0, rwkv_iclr_a1, rwkv_iclr_a2, rwkv_gate_g1, rwkv_gate_g2, rwkv_k_k, rwkv_k_a, rwkv_r_k, rwkv_gn_g, rwkv_gn_b, moe_router_g, moe_router_g_b, moe_router_e, moe_router_e_b, moe_w1, moe_w3, moe_w2):
    batch = x.shape[0]
    rows = batch * S_ALL
    xs = jnp.concatenate([ctx, x], axis=1).reshape(rows, D_MODEL)
    cos_t, sin_t = rope_tables()

    cvec = jnp.concatenate([c, c_ctx[None], jnp.zeros((8 - batch - 1, D_MODEL), F32)], axis=0)
    mods = ada_modulation(jax.nn.silu(cvec).astype(BF16), ada_w, ada_b)
    mods = mods.reshape(DEPTH, 8, N_MOD, D_MODEL)

    def mod_table(i):
        mod_l = mods[i, :batch]
        mod_c = jnp.broadcast_to(mods[i, batch][None], mod_l.shape)
        return jnp.stack([mod_c, mod_l], axis=1).reshape(batch * 2 * N_MOD, 1, D_MODEL)

    mod = mod_table(0)
    h = first_norm(xs, norm_g[0, 0], mod, BF16)
    for i in range(DEPTH):
        j = i // 2
        if i % 2 == 0:
            p = dict(w_in=even_w_in[j], w_out=even_w_out[j], diff_lambda=diff_lambda[j],
                     subln_g=diff_subln_g[j], rpb=na_rpb[j])
            out = even_mixer(h, p, i, cos_t, sin_t, batch)
        else:
            p = dict(x_mix=rwkv_x_mix[j], w_rkv=rwkv_w_rkv[j], w_out=rwkv_w_out[j], dec_w0=rwkv_dec_w0[j],
                     dec_w1=rwkv_dec_w1[j], dec_w2=rwkv_dec_w2[j], iclr_a0=rwkv_iclr_a0[j],
                     iclr_a1=rwkv_iclr_a1[j], iclr_a2=rwkv_iclr_a2[j], gate_g1=rwkv_gate_g1[j],
                     gate_g2=rwkv_gate_g2[j], k_k=rwkv_k_k[j], k_a=rwkv_k_a[j], r_k=rwkv_r_k[j],
                     gn_g=rwkv_gn_g[j], gn_b=rwkv_gn_b[j])
            out = rwkv_mixer(h, p, batch)
        rw, rb = router_table(moe_router_g[i], moe_router_g_b[i], moe_router_e[i], moe_router_e_b[i])
        xs, h2, route = post_mixer(xs, out, norm_g[i, 1], mod, rw, rb)
        tok, dst, gate, tile_e, n_live = moe_dispatch(route)
        y2 = moe_experts(h2, tok, dst, gate, tile_e, n_live, moe_w1, moe_w3, moe_w2, i)
        if i + 1 < DEPTH:
            next_mod = mod_table(i + 1)
            xs, h = post_moe(xs, y2, norm_g[i + 1, 0], mod, next_mod, F32 if (i + 1) % 2 else BF16)
            mod = next_mod
        else:
            _, h = post_moe(xs, y2, final_g, mod, None, F32)
    return h.reshape(batch, S_ALL, D_MODEL)[:, CTX_LEN:]
```

```python
import functools
import math

import numpy as np
import jax
import jax.numpy as jnp
from jax import lax
from jax.experimental import pallas as pl
from jax.experimental.pallas import tpu as pltpu

F32 = jnp.float32
BF16 = jnp.bfloat16

D_MODEL = 2048
DEPTH = 4
GRID_W = 64
CTX_LEN = 256
SEQ = 4096
S_ALL = CTX_LEN + SEQ
N_MOD = 6
NORM_EPS = 1e-6
NEG_INF = -1e30

DIFF_HEADS = 8
DIFF_QK_DIM = 64
NA_HEADS = 8
NA_DIM = 128
NA_KH = 8
NA_KW = 16
ROPE_THETA = 10000.0
SUBLN_EPS = 1e-5
HEAD_W = 128
ATT_TILE = 256
DIFF_KV_BLOCK = 2048
NA_ROWS_PER_STEP = 4
NA_WIN_ROWS = 12

RWKV_HEAD = 64
RWKV_HEADS = D_MODEL // RWKV_HEAD
GN_EPS = 64e-5
SCAN_CHUNK = 64
SCAN_PAIR_W = 2 * RWKV_HEAD
SCAN_SUB = 16

N_GROUPS = 4
EXPERTS_PER_GROUP = 8
N_EXPERTS = N_GROUPS * EXPERTS_PER_GROUP
TOP_K = 2
EXPERT_FF = 512
MOE_TILE = 256
MOE_DMA_UNROLL = 8
ROW_TILE = 256

VMEM_LIMIT = 52 * 1024 * 1024


def _params(sem):
    return pltpu.CompilerParams(dimension_semantics=sem, vmem_limit_bytes=VMEM_LIMIT)


def _dot(a, b):
    return jnp.dot(a, b, preferred_element_type=F32)


def _dot_nt(a, b):
    return lax.dot_general(a, b, (((1,), (1,)), ((), ())), preferred_element_type=F32)


def _dot_tn(a, b):
    return lax.dot_general(a, b, (((0,), (0,)), ((), ())), preferred_element_type=F32)


def _mm_kernel(a_ref, w_ref, o_ref):
    o_ref[...] = _dot(a_ref[...], w_ref[...]).astype(o_ref.dtype)


def matmul(a, w, out_dtype, tm=1024, tn=512):
    m, k = a.shape
    n = w.shape[1]
    while m % tm:
        tm //= 2
    tn = min(tn, n)
    assert n % tn == 0
    return pl.pallas_call(
        _mm_kernel,
        grid=(m // tm, n // tn),
        in_specs=[pl.BlockSpec((tm, k), lambda i, j: (i, 0)),
                  pl.BlockSpec((k, tn), lambda i, j: (0, j))],
        out_specs=pl.BlockSpec((tm, tn), lambda i, j: (i, j)),
        out_shape=jax.ShapeDtypeStruct((m, n), out_dtype),
        compiler_params=_params(("parallel", "parallel")),
        name="matmul",
    )(a, w)


def _ada_kernel(s_ref, w_ref, b_ref, o_ref):
    o_ref[...] = _dot(s_ref[...], w_ref[...].astype(BF16)) + b_ref[...]


def ada_modulation(svec, ada_w, ada_b, tn=1024):
    nl, d, n = ada_w.shape
    rows = svec.shape[0]
    return pl.pallas_call(
        _ada_kernel,
        grid=(nl, n // tn),
        in_specs=[pl.BlockSpec((rows, d), lambda l, j: (0, 0)),
                  pl.BlockSpec((None, d, tn), lambda l, j: (l, 0, j)),
                  pl.BlockSpec((None, 1, tn), lambda l, j: (l, 0, j))],
        out_specs=pl.BlockSpec((None, rows, tn), lambda l, j: (l, 0, j)),
        out_shape=jax.ShapeDtypeStruct((nl, rows, n), F32),
        compiler_params=_params(("parallel", "parallel")),
        name="ada_modulation",
    )(svec, ada_w, ada_b.reshape(nl, 1, n))


def _rope(x, cos, sin_signed):
    lane = lax.broadcasted_iota(jnp.int32, x.shape, 1)
    first_half = (lane & 63) < 32
    partner = jnp.where(first_half, pltpu.roll(x, HEAD_W - 32, 1), pltpu.roll(x, 32, 1))
    return x * cos + partner * sin_signed


def _diff_attn_kernel(lam_ref, q_ref, k_ref, v_ref, cos_ref, sin_ref, g_ref, o_ref, kr_ref, vt_ref, sa_ref, sb_ref, *,
                      post_scale):
    j = pl.program_id(2)
    tq = ATT_TILE

    @pl.when(j == 0)
    def _():
        def prep_chunk(c, carry):
            rows = pl.ds(pl.multiple_of(c * ATT_TILE, ATT_TILE), ATT_TILE)
            kr_ref[rows, :] = _rope(k_ref[rows, :].astype(F32), cos_ref[rows, :], sin_ref[rows, :]).astype(BF16)
            vt_ref[:, rows] = v_ref[rows, :].astype(F32).T.astype(BF16)
            return carry
        lax.fori_loop(0, S_ALL // ATT_TILE, prep_chunk, 0)

    qrows = pl.ds(pl.multiple_of(j * tq, tq), tq)
    q = _rope(q_ref[...].astype(F32), cos_ref[qrows, :], sin_ref[qrows, :]) * (DIFF_QK_DIM ** -0.5)
    lane = lax.broadcasted_iota(jnp.int32, q.shape, 1)
    q1 = jnp.where(lane < DIFF_QK_DIM, q, 0.0).astype(BF16)
    q2 = jnp.where(lane < DIFF_QK_DIM, 0.0, q).astype(BF16)

    kb = DIFF_KV_BLOCK
    n_latent = SEQ // kb
    maps = (q1, q2)

    def score_stage(start, size, sbuf):
        kc = kr_ref[pl.ds(start, size), :]
        tops = []
        for i, qm in enumerate(maps):
            s = _dot_nt(kc, qm)
            sbuf[i, 0:size, :] = s
            tops.append(jnp.max(s, axis=0, keepdims=True))
        return tuple(tops)

    def softmax_stage(start, size, sbuf, tops, state):
        vt = vt_ref[:, pl.ds(start, size)]
        new = []
        for i in range(2):
            m, l, acc = state[3 * i:3 * i + 3]
            m_new = jnp.maximum(m, tops[i])
            alpha = jnp.exp(m - m_new)
            p = jnp.exp(sbuf[i, 0:size, :] - m_new)
            new += [m_new, alpha * l + jnp.sum(p, axis=0, keepdims=True),
                    alpha * acc + _dot(vt, p.astype(BF16))]
        return tuple(new)

    def latent_start(n):
        return pl.multiple_of(CTX_LEN + jnp.minimum(n, n_latent - 1) * kb, CTX_LEN)

    row = jnp.full((1, tq), NEG_INF, F32)
    zrow = jnp.zeros((1, tq), F32)
    zacc = jnp.zeros((HEAD_W, tq), F32)
    state = softmax_stage(0, CTX_LEN, sa_ref, score_stage(0, CTX_LEN, sa_ref), (row, zrow, zacc, row, zrow, zacc))
    tops = score_stage(latent_start(0), kb, sb_ref)

    def block_pair(n, carry, more_follow):
        st, tp = carry[:6], carry[6:]
        tp_next = score_stage(latent_start(n + 1), kb, sa_ref)
        st = softmax_stage(latent_start(n), kb, sb_ref, tp, st)
        tp_after = score_stage(latent_start(n + 2), kb, sb_ref) if more_follow else tp_next
        st = softmax_stage(latent_start(n + 1), kb, sa_ref, tp_next, st)
        return st + tp_after

    n_pairs = n_latent // 2
    carry = lax.fori_loop(0, jnp.where(j == 0, 0, n_pairs - 1),
                          lambda t, c: block_pair(2 * t, c, True), state + tops)
    carry = lax.fori_loop(0, jnp.where(j == 0, 0, 1),
                          lambda t, c: block_pair(2 * (n_pairs - 1), c, False), carry)
    m1, l1, a1, m2, l2, a2 = carry[:6]
    out = a1 / l1 - lam_ref[0] * (a2 / l2)
    ms = jnp.mean(out * out, axis=0, keepdims=True)
    y = out * lax.rsqrt(ms + SUBLN_EPS) * (g_ref[...] * post_scale)
    o_ref[...] = y.T.astype(o_ref.dtype)


def diff_attention(proj, lam, cos_t, sin_t, subln_g, lam_init):
    b = proj.shape[0]
    kernel = functools.partial(_diff_attn_kernel, post_scale=1.0 - lam_init)
    return pl.pallas_call(
        kernel,
        grid=(b, DIFF_HEADS, S_ALL // ATT_TILE),
        in_specs=[pl.BlockSpec(memory_space=pltpu.SMEM),
                  pl.BlockSpec((None, ATT_TILE, HEAD_W), lambda bi, h, j: (bi, j, h)),
                  pl.BlockSpec((None, S_ALL, HEAD_W), lambda bi, h, j: (bi, 0, DIFF_HEADS + h)),
                  pl.BlockSpec((None, S_ALL, HEAD_W), lambda bi, h, j: (bi, 0, 2 * DIFF_HEADS + h)),
                  pl.BlockSpec((S_ALL, HEAD_W), lambda bi, h, j: (0, 0)),
                  pl.BlockSpec((S_ALL, HEAD_W), lambda bi, h, j: (0, 0)),
                  pl.BlockSpec((HEAD_W, 1), lambda bi, h, j: (0, 0))],
        out_specs=pl.BlockSpec((None, ATT_TILE, HEAD_W), lambda bi, h, j: (bi, j, h)),
        out_shape=jax.ShapeDtypeStruct((b, S_ALL, DIFF_HEADS * HEAD_W), BF16),
        scratch_shapes=[pltpu.VMEM((S_ALL, HEAD_W), BF16), pltpu.VMEM((HEAD_W, S_ALL), BF16),
                        pltpu.VMEM((2, DIFF_KV_BLOCK, ATT_TILE), F32), pltpu.VMEM((2, DIFF_KV_BLOCK, ATT_TILE), F32)],
        compiler_params=_params(("parallel", "parallel", "arbitrary")),
        name="diff_attention",
    )(lam, proj, proj, proj, cos_t, sin_t, subln_g.reshape(HEAD_W, 1))


def rope_tables():
    n_freq = DIFF_QK_DIM // 4
    inv_freq = ROPE_THETA ** (-jnp.arange(n_freq, dtype=F32) / n_freq)
    t = jnp.arange(SEQ, dtype=jnp.int32)
    row = (t // GRID_W).astype(F32)
    col = (t % GRID_W).astype(F32)
    ang = jnp.concatenate([row[:, None] * inv_freq, col[:, None] * inv_freq], axis=-1)
    cos, sin = jnp.cos(ang), jnp.sin(ang)
    cos_l = jnp.concatenate([cos, cos, cos, cos], axis=-1)
    sin_l = jnp.concatenate([-sin, sin, -sin, sin], axis=-1)
    cos_all = jnp.concatenate([jnp.ones((CTX_LEN, HEAD_W), F32), cos_l], axis=0)
    sin_all = jnp.concatenate([jnp.zeros((CTX_LEN, HEAD_W), F32), sin_l], axis=0)
    return cos_all, sin_all


def _na_window_start(j):
    g = j - 1
    return jnp.clip(NA_ROWS_PER_STEP * g - NA_KH // 2, 0, SEQ // GRID_W - NA_WIN_ROWS)


def _na_kernel(q_ref, k_ref, v_ref, bias_ref, o_ref):
    j = pl.program_id(2)
    win = NA_WIN_ROWS * GRID_W
    start = pl.multiple_of(CTX_LEN + _na_window_start(j) * GRID_W, GRID_W)
    scale = NA_DIM ** -0.5
    q = q_ref[...]
    s_c = _dot_nt(q, k_ref[pl.ds(0, CTX_LEN), :]) * scale
    s_w = _dot_nt(q, k_ref[pl.ds(start, win), :]) * scale + bias_ref[...]
    m = jnp.maximum(jnp.max(s_c, axis=-1, keepdims=True), jnp.max(s_w, axis=-1, keepdims=True))
    p_c = jnp.exp(s_c - m)
    p_w = jnp.exp(s_w - m)
    l = jnp.sum(p_c, axis=-1, keepdims=True) + jnp.sum(p_w, axis=-1, keepdims=True)
    o = _dot(p_c.astype(BF16), v_ref[pl.ds(0, CTX_LEN), :]) + _dot(p_w.astype(BF16), v_ref[pl.ds(start, win), :])
    o_ref[...] = (o / l).astype(o_ref.dtype)


def _na_bias_pattern(j):
    n_groups = SEQ // (GRID_W * NA_ROWS_PER_STEP)
    g = j - 1
    return jnp.where(j == 0, 3, jnp.where(g == 0, 0, jnp.where(g == n_groups - 1, 2, 1)))


def na_bias_table(rpb):
    rows = SEQ // GRID_W
    n_groups = rows // NA_ROWS_PER_STEP
    cols = np.arange(GRID_W)
    col_start = np.clip(cols - NA_KW // 2, 0, GRID_W - NA_KW)
    col_mask = (cols[None, :] >= col_start[:, None]) & (cols[None, :] < col_start[:, None] + NA_KW)
    c_idx = np.clip(cols[None, :] - cols[:, None] + NA_KW - 1, 0, 2 * NA_KW - 2)
    pats = []
    for g in (0, 1, n_groups - 1):
        u0 = int(np.clip(NA_ROWS_PER_STEP * g - NA_KH // 2, 0, rows - NA_WIN_ROWS))
        r = NA_ROWS_PER_STEP * g + np.arange(NA_ROWS_PER_STEP)
        r0 = np.clip(r - NA_KH // 2, 0, rows - NA_KH)
        kr = u0 + np.arange(NA_WIN_ROWS)
        valid_r = (kr[None, :] >= r0[:, None]) & (kr[None, :] < r0[:, None] + NA_KH)
        r_idx = np.clip(kr[None, :] - r[:, None] + NA_KH - 1, 0, 2 * NA_KH - 2)
        valid = valid_r[:, None, :, None] & col_mask[None, :, None, :]
        r_sel = np.eye(2 * NA_KH - 1, dtype=np.float32)[r_idx]
        c_sel = np.eye(2 * NA_KW - 1, dtype=np.float32)[c_idx]
        rows_sel = jnp.einsum('qkr,hrc->hqkc', r_sel, rpb.astype(F32), precision=lax.Precision.HIGHEST)
        gathered = jnp.einsum('hqkc,abc->hqakb', rows_sel, c_sel, precision=lax.Precision.HIGHEST)
        pats.append(jnp.where(valid[None], gathered, NEG_INF))
    pats.append(jnp.full_like(pats[0], NEG_INF))
    tab = jnp.stack(pats, axis=1)
    return tab.reshape(NA_HEADS, 4, NA_ROWS_PER_STEP * GRID_W, NA_WIN_ROWS * GRID_W)


def neighbourhood_attention(proj, bias_tab):
    b = proj.shape[0]
    tq = NA_ROWS_PER_STEP * GRID_W
    assert tq == CTX_LEN
    win = NA_WIN_ROWS * GRID_W
    base = 3 * DIFF_HEADS
    return pl.pallas_call(
        _na_kernel,
        grid=(b, NA_HEADS, S_ALL // tq),
        in_specs=[pl.BlockSpec((None, tq, HEAD_W), lambda bi, h, j: (bi, j, base + h)),
                  pl.BlockSpec((None, S_ALL, HEAD_W), lambda bi, h, j: (bi, 0, base + NA_HEADS + h)),
                  pl.BlockSpec((None, S_ALL, HEAD_W), lambda bi, h, j: (bi, 0, base + 2 * NA_HEADS + h)),
                  pl.BlockSpec((None, None, tq, win), lambda bi, h, j: (h, _na_bias_pattern(j), 0, 0))],
        out_specs=pl.BlockSpec((None, tq, HEAD_W), lambda bi, h, j: (bi, j, h)),
        out_shape=jax.ShapeDtypeStruct((b, S_ALL, NA_HEADS * HEAD_W), BF16),
        compiler_params=_params(("parallel", "parallel", "arbitrary")),
        name="neighbourhood_attention",
    )(proj, proj, proj, bias_tab)


def _each(fn, *lists):
    return [fn(*args) for args in zip(*lists)]


def _head_sum(x):
    rows, w = x.shape
    r_i = lax.broadcasted_iota(jnp.int32, (w, w), 0)
    c_i = lax.broadcasted_iota(jnp.int32, (w, w), 1)
    ones_bd = jnp.where((r_i < RWKV_HEAD) == (c_i < RWKV_HEAD), 1.0, 0.0).astype(BF16)
    hi = x.astype(BF16)
    lo = (x - hi.astype(F32)).astype(BF16)
    s = _dot(jnp.concatenate([hi, lo], axis=0), ones_bd)
    return s[:rows] + s[rows:]


def _scan_prepare_kernel(r_ref, k_ref, v_ref, hw_ref, ha_ref, w2w0_ref, w2w1_ref, w2a0_ref, w2a1_ref, par_ref,
                         y0_ref, rm_ref, d0_ref, bonus_ref):
    c = SCAN_CHUNK
    w = SCAN_PAIR_W
    hw = RWKV_HEAD
    t_idx = lax.broadcasted_iota(jnp.int32, (c, w), 0)
    lane = lax.broadcasted_iota(jnp.int32, (c, w), 1)
    s_idx = lane & (hw - 1)
    head0 = lane < hw
    eye = s_idx == t_idx
    tt = lax.broadcasted_iota(jnp.int32, (c, c), 0)
    ss = lax.broadcasted_iota(jnp.int32, (c, c), 1)
    strict = (s_idx < t_idx, s_idx > t_idx)
    incl = (s_idx <= t_idx, s_idx >= t_idx)
    tri = (jnp.where(ss <= tt, 1.0, 0.0).astype(BF16), jnp.where(ss >= tt, 1.0, 0.0).astype(BF16))
    row2 = lax.broadcasted_iota(jnp.int32, (w, w), 0)
    lane2 = lax.broadcasted_iota(jnp.int32, (w, w), 1)
    bdmask = (row2 < hw) == (lane2 < hw)
    eye2 = row2 == lane2

    def bd(y):
        return jnp.where(bdmask, jnp.concatenate([y, y], axis=0), 0.0).astype(BF16)

    def pm(x, ybd):
        return _dot(x.astype(BF16), ybd)

    inst = [(d, sub) for d in range(2) for sub in range(SCAN_SUB)]
    dirs = [d for d, _ in inst]
    rows = [slice(sub * c, (sub + 1) * c) for _, sub in inst]
    par = par_ref[...]
    k_k, k_a, r_k = par[4:5, :], par[5:6, :], par[6:7, :]
    sub_rows = [slice(sub * c, (sub + 1) * c) for sub in range(SCAN_SUB)]
    r_s = [r_ref[rw, :] for rw in sub_rows]
    k_s = [k_ref[rw, :] for rw in sub_rows]
    v_s = [v_ref[rw, :] for rw in sub_rows]
    hid_w = hw_ref[...]
    hid_a = ha_ref[...]
    wl_all = [_dot(hid_w, w2w0_ref[...]), _dot(hid_w, w2w1_ref[...])]
    al_all = [_dot(hid_a, w2a0_ref[...]), _dot(hid_a, w2a1_ref[...])]
    wl_s = [[wl_all[d][rw, :] for rw in sub_rows] for d in range(2)]
    al_s = [[al_all[d][rw, :] for rw in sub_rows] for d in range(2)]

    def unit_key(k_i):
        kk_i = k_i * k_k
        return kk_i * lax.rsqrt(jnp.maximum(_head_sum(kk_i * kk_i), 1e-24))

    def log_decay(wl_i, d):
        z = -(par[d:d + 1, :] + wl_i)
        softplus = jnp.maximum(z, 0.0) + jnp.log(1.0 + jnp.exp(-jnp.abs(z)))
        return -jnp.exp(-softplus - 0.5)

    kk_s = _each(unit_key, k_s)
    a_s = [[jax.nn.sigmoid(par[2 + d:3 + d, :] + x) for x in al_s[d]] for d in range(2)]
    kd_s = [[k_i * (1.0 + (a_i - 1.0) * k_a) for k_i, a_i in zip(k_s, a_s[d])] for d in range(2)]
    bonus = [_head_sum(r_i * r_k * (kd0 + kd1)) * v_i for r_i, kd0, kd1, v_i in zip(r_s, kd_s[0], kd_s[1], v_s)]

    r = [r_s[sub] for _, sub in inst]
    kk = [kk_s[sub] for _, sub in inst]
    v = [v_s[sub] for _, sub in inst]
    ld = [log_decay(wl_s[d][sub], d) for d, sub in inst]
    a = [a_s[d][sub] for d, sub in inst]
    kd = [kd_s[d][sub] for d, sub in inst]

    def cumulative(ld_i, d):
        p_hi = ld_i.astype(BF16)
        rem = ld_i - p_hi.astype(F32)
        p_mid = rem.astype(BF16)
        p_lo = (rem - p_mid.astype(F32)).astype(BF16)
        cs = _dot(tri[d], jnp.concatenate([p_hi, p_mid, p_lo], axis=1))
        return cs[:, :w] + cs[:, w:2 * w] + cs[:, 2 * w:]

    lam = _each(cumulative, ld, dirs)
    lam_end = _each(lambda l, d: l[0:1, :] if d else l[c - 1:c, :], lam, dirs)
    g_cum = _each(jnp.exp, lam)
    g_inv = _each(lambda l: jnp.exp(-l), lam)
    g_prev = _each(lambda l, x: jnp.exp(l - x), lam, ld)
    g_rel = _each(lambda le, l: jnp.exp(le - l), lam_end, lam)
    g_end = _each(jnp.exp, lam_end)

    qk = _each(jnp.multiply, kk, g_prev)
    rt = _each(jnp.multiply, r, g_cum)
    beta = _each(jnp.multiply, kk, a)
    bt = _each(jnp.multiply, beta, g_inv)
    kt = _each(jnp.multiply, kd, g_inv)
    bh = _each(jnp.multiply, beta, g_rel)
    kh = _each(jnp.multiply, kd, g_rel)

    def big_product(qk_i, rt_i, bt_i, kt_i):
        rhs = jnp.concatenate([jnp.where(head0, bt_i, 0.0), jnp.where(head0, 0.0, bt_i),
                               jnp.where(head0, kt_i, 0.0), jnp.where(head0, 0.0, kt_i)], axis=0).astype(BF16)
        return _dot_nt(jnp.concatenate([qk_i, rt_i], axis=0).astype(BF16), rhs)

    big = _each(big_product, qk, rt, bt, kt)
    n_pow = _each(lambda b, d: jnp.where(strict[d], -b[:c, :w], 0.0), big, dirs)
    a_k = _each(lambda b, d: jnp.where(strict[d], b[:c, w:], 0.0), big, dirs)
    g_b = _each(lambda b, d: jnp.where(incl[d], b[c:, :w], 0.0), big, dirs)
    g_k = _each(lambda b, d: jnp.where(incl[d], b[c:, w:], 0.0), big, dirs)

    t_inv = _each(lambda n: jnp.where(eye, 1.0, 0.0) + n, n_pow)
    for _ in range(int(math.log2(c)) - 1):
        n_pow = _each(lambda n: pm(n, bd(n)), n_pow)
        t_inv = _each(lambda t, n: t + pm(t, bd(n)), t_inv, n_pow)

    v_bd = _each(bd, v)
    akv_gkv = _each(lambda ak, gk, vb: pm(jnp.concatenate([ak, gk], axis=0), vb), a_k, g_k, v_bd)
    x1 = _each(lambda x: x[:c], akv_gkv)
    uw = _each(lambda t, x, q: _dot(t.astype(BF16), jnp.concatenate([bd(x), bd(q)], axis=1)), t_inv, x1, qk)
    u0 = _each(lambda x: -x[:, :w], uw)
    wm = _each(lambda x: x[:, w:], uw)
    gb_uw = _each(lambda gb, u, wm_i: _dot(gb.astype(BF16), jnp.concatenate([bd(u), bd(wm_i)], axis=1)), g_b, u0, wm)
    y0 = _each(lambda x, y: x[c:] + y[:, :w], akv_gkv, gb_uw)
    rm = _each(lambda rt_i, y: rt_i - y[:, w:], rt, gb_uw)
    d0 = _each(lambda kh_i, bh_i, v_i, u: jnp.where(bdmask, _dot_tn(
        jnp.concatenate([kh_i, bh_i], axis=0).astype(BF16), jnp.concatenate([v_i, u], axis=0).astype(BF16)), 0.0),
        kh, bh, v, u0)
    mm = _each(lambda bh_i, wm_i, ge: jnp.where(eye2, ge, 0.0) - jnp.where(
        bdmask, _dot_tn(bh_i.astype(BF16), wm_i.astype(BF16)), 0.0), bh, wm, g_end)
    d0 = _each(lambda x: x[:hw] + x[hw:], d0)
    mm = _each(lambda x: x[:hw] + x[hw:], mm)
    mm_hi = _each(lambda m: m.astype(BF16), mm)
    mm_lo = _each(lambda m, mh: (m - mh.astype(F32)).astype(BF16), mm, mm_hi)

    for sub in range(SCAN_SUB):
        bonus_ref[sub_rows[sub], :] = bonus[sub]
    for i, (d, sub) in enumerate(inst):
        y0_ref[d, rows[i], :] = y0[i]
        rm_ref[d, sub, 0, 0:c, :] = rm[i].astype(BF16)
        rm_ref[d, sub, 0, c:c + hw, :] = mm_hi[i]
        rm_ref[d, sub, 0, c + hw:c + 2 * hw, :] = mm_lo[i]
        d0_ref[d, sub, 0, :, :] = d0[i]


def scan_prepare(r, k, v, hid_w, hid_a, w2_w, w2_a, par):
    rows, d = r.shape
    c, w = SCAN_CHUNK, SCAN_PAIR_W
    n_pairs = d // w
    n_chunks = rows // c
    blk = SCAN_SUB * c
    rank = hid_w.shape[1]
    shared = pl.BlockSpec((blk, w), lambda i, p: (i, p))
    hidden = pl.BlockSpec((blk, rank), lambda i, p: (i, 0))
    fwd_w2 = pl.BlockSpec((rank, w), lambda i, p: (0, p))
    rev_w2 = pl.BlockSpec((rank, w), lambda i, p: (0, n_pairs + p))
    return pl.pallas_call(
        _scan_prepare_kernel,
        grid=(rows // blk, n_pairs),
        in_specs=[shared, shared, shared, hidden, hidden, fwd_w2, rev_w2, fwd_w2, rev_w2,
                  pl.BlockSpec((8, w), lambda i, p: (0, p))],
        out_specs=[pl.BlockSpec((2, blk, w), lambda i, p: (0, i, p)),
                   pl.BlockSpec((2, SCAN_SUB, 1, c + w, w), lambda i, p: (0, i, p, 0, 0)),
                   pl.BlockSpec((2, SCAN_SUB, 1, w // 2, w), lambda i, p: (0, i, p, 0, 0)),
                   shared],
        out_shape=[jax.ShapeDtypeStruct((2, rows, d), F32),
                   jax.ShapeDtypeStruct((2, n_chunks, n_pairs, c + w, w), BF16),
                   jax.ShapeDtypeStruct((2, n_chunks, n_pairs, w // 2, w), F32),
                   jax.ShapeDtypeStruct((rows, d), F32)],
        compiler_params=_params(("parallel", "parallel")),
        name="scan_prepare",
    )(r, k, v, hid_w, hid_a, w2_w, w2_w, w2_a, w2_a, par)


def _scan_apply_kernel(y0f_ref, rmf_ref, d0f_ref, y0r_ref, rmr_ref, d0r_ref, yf_ref, yr_ref, z_ref, *, n_pairs):
    c, w = SCAN_CHUNK, SCAN_PAIR_W
    i = pl.program_id(1)

    @pl.when(i == 0)
    def _():
        z_ref[...] = jnp.zeros_like(z_ref)

    hw = RWKV_HEAD
    row2 = lax.broadcasted_iota(jnp.int32, (w, w), 0)
    lane2 = lax.broadcasted_iota(jnp.int32, (w, w), 1)
    bdmask = (row2 < hw) == (lane2 < hw)

    def bd(x):
        return jnp.where(bdmask, jnp.concatenate([x, x], axis=0), 0.0)

    chains = [(d, p) for d in range(2) for p in range(n_pairs)]
    y0_refs, rm_refs, d0_refs, y_refs = (y0f_ref, y0r_ref), (rmf_ref, rmr_ref), (d0f_ref, d0r_ref), (yf_ref, yr_ref)
    z = [bd(z_ref[d, p]) for d, p in chains]
    z_hi = _each(lambda x: x.astype(BF16), z)
    z_lo = _each(lambda x, xh: (x - xh.astype(F32)).astype(BF16), z, z_hi)
    res = [_dot(rm_refs[d][p], jnp.concatenate([zh, zl], axis=1)) for (d, p), zh, zl in zip(chains, z_hi, z_lo)]
    res = _each(lambda x: x[:, :w] + x[:, w:], res)
    y = [y0_refs[d][:, p * w:(p + 1) * w] + x[:c] for (d, p), x in zip(chains, res)]
    z_new = [d0_refs[d][p] + x[c:c + hw] + x[c + hw:] for (d, p), x in zip(chains, res)]
    for (d, p), y_i, z_i in zip(chains, y, z_new):
        y_refs[d][:, p * w:(p + 1) * w] = y_i
        z_ref[d, p] = z_i


def _scan_chunk_index(i, d):
    n_ctx = CTX_LEN // SCAN_CHUNK
    n_all = S_ALL // SCAN_CHUNK
    return jnp.where(d == 0, i, jnp.where(i < n_ctx, n_ctx - 1 - i, n_all + n_ctx - 1 - i))


def scan_apply(y0, rm, d0, batch):
    _, rows, d = y0.shape
    c, w = SCAN_CHUNK, SCAN_PAIR_W
    n_pairs = d // w
    per_b = rows // batch // c

    def specs(dr):
        def idx(b, i):
            return b * per_b + _scan_chunk_index(i, dr)
        return [pl.BlockSpec((None, c, d), lambda b, i: (dr, idx(b, i), 0)),
                pl.BlockSpec((None, None, n_pairs, c + w, w), lambda b, i: (dr, idx(b, i), 0, 0, 0)),
                pl.BlockSpec((None, None, n_pairs, w // 2, w), lambda b, i: (dr, idx(b, i), 0, 0, 0)),
                pl.BlockSpec((c, d), lambda b, i: (idx(b, i), 0))]

    fwd, rev = specs(0), specs(1)
    kernel = functools.partial(_scan_apply_kernel, n_pairs=n_pairs)
    return pl.pallas_call(
        kernel,
        grid=(batch, per_b),
        in_specs=fwd[:3] + rev[:3],
        out_specs=[fwd[3], rev[3]],
        out_shape=[jax.ShapeDtypeStruct((rows, d), F32)] * 2,
        scratch_shapes=[pltpu.VMEM((2, n_pairs, w // 2, w), F32)],
        compiler_params=_params(("parallel", "arbitrary")),
        name="scan_apply",
    )(y0, rm, d0, y0, rm, d0)


def _shift_mix_kernel(h_ref, hp_ref, hn_ref, mix_ref, *out_refs):
    per_b = S_ALL // ROW_TILE
    pos = pl.program_id(0) % per_b
    h = h_ref[...]
    row = lax.broadcasted_iota(jnp.int32, h.shape, 0)
    starts_seq = (pos == 0) | (pos == 1)
    ends_seq = (pos == 0) | (pos == per_b - 1)
    prev_row = jnp.where(starts_seq, 0.0, hp_ref[7:8, :])
    next_row = jnp.where(ends_seq, 0.0, hn_ref[0:1, :])
    prev = jnp.where(row == 0, prev_row, pltpu.roll(h, 1, 0))
    nxt = jnp.where(row == ROW_TILE - 1, next_row, pltpu.roll(h, ROW_TILE - 1, 0))
    xx = 0.5 * (prev + nxt) - h
    for j, o_ref in enumerate(out_refs):
        o_ref[...] = (h + xx * mix_ref[j:j + 1, :]).astype(o_ref.dtype)


def shift_mix(h, x_mix):
    rows, d = h.shape
    n_mix = x_mix.shape[0]
    sub = ROW_TILE // 8
    tile = pl.BlockSpec((ROW_TILE, d), lambda i: (i, 0))
    return pl.pallas_call(
        _shift_mix_kernel,
        grid=(rows // ROW_TILE,),
        in_specs=[tile,
                  pl.BlockSpec((8, d), lambda i: (jnp.maximum(i * sub - 1, 0), 0)),
                  pl.BlockSpec((8, d), lambda i: (jnp.minimum((i + 1) * sub, rows // 8 - 1), 0)),
                  pl.BlockSpec((n_mix, d), lambda i: (0, 0))],
        out_specs=[tile] * n_mix,
        out_shape=[jax.ShapeDtypeStruct((rows, d), BF16)] * n_mix,
        compiler_params=_params(("parallel",)),
        name="shift_mix",
    )(h, h, h, x_mix)


def _rwkv_out_kernel(yf_ref, yr_ref, bonus_ref, g_ref, gn_ref, o_ref):
    w = SCAN_PAIR_W
    for s in range(o_ref.shape[-1] // w):
        cols = slice(s * w, (s + 1) * w)
        y = yf_ref[:, cols] + yr_ref[:, cols]
        mu = _head_sum(y) * (1.0 / RWKV_HEAD)
        dev = y - mu
        var = _head_sum(dev * dev) * (1.0 / RWKV_HEAD)
        yn = dev * lax.rsqrt(var + GN_EPS) * gn_ref[0:1, cols] + gn_ref[1:2, cols]
        o_ref[:, cols] = ((yn + bonus_ref[:, cols]) * g_ref[:, cols]).astype(o_ref.dtype)


def rwkv_out(y_fwd, y_rev, bonus, g, gn_g, gn_b, tn=512):
    rows, d = y_fwd.shape
    tile = pl.BlockSpec((ROW_TILE, tn), lambda i, j: (i, j))
    return pl.pallas_call(
        _rwkv_out_kernel,
        grid=(rows // ROW_TILE, d // tn),
        in_specs=[tile, tile, tile, tile, pl.BlockSpec((2, tn), lambda i, j: (0, j))],
        out_specs=tile,
        out_shape=jax.ShapeDtypeStruct((rows, d), BF16),
        compiler_params=_params(("parallel", "parallel")),
        name="rwkv_out",
    )(y_fwd, y_rev, bonus, g, jnp.stack([gn_g, gn_b]))


def _route_topk(logits):
    lane = lax.broadcasted_iota(jnp.int32, logits.shape, 1)
    far = 4 * HEAD_W

    def first_max(vals):
        top = jnp.max(vals, axis=-1, keepdims=True)
        return top, jnp.min(jnp.where(vals == top, lane, far), axis=-1, keepdims=True)

    is_group = lane < N_GROUPS
    g_top, g_sel = first_max(jnp.where(is_group, logits, NEG_INF))
    p_sel = 1.0 / jnp.sum(jnp.where(is_group, jnp.exp(logits - g_top), 0.0), axis=-1, keepdims=True)
    lo = N_GROUPS + g_sel * EXPERTS_PER_GROUP
    le = jnp.where((lane >= lo) & (lane < lo + EXPERTS_PER_GROUP), logits, NEG_INF)
    v1, i1 = first_max(le)
    v2, i2 = first_max(jnp.where(lane == i1, NEG_INF, le))
    e2 = jnp.exp(v2 - v1)
    w1 = p_sel / (1.0 + e2)
    w2 = p_sel * e2 / (1.0 + e2)
    out = jnp.where(lane == 0, (i1 - N_GROUPS).astype(F32), 0.0)
    out = jnp.where(lane == 1, (i2 - N_GROUPS).astype(F32), out)
    out = jnp.where(lane == 2, w1, out)
    return jnp.where(lane == 3, w2, out)


def moe_dispatch(route):
    t = route.shape[0]
    n_pairs = TOP_K * t
    tm = MOE_TILE
    n_slots = n_pairs + N_EXPERTS * tm
    n_tiles = n_slots // tm
    e_flat = route[:, :TOP_K].astype(jnp.int32).reshape(-1)
    gate_bits = lax.bitcast_convert_type(route[:, TOP_K:2 * TOP_K], jnp.int32).reshape(-1)
    onehot = (e_flat[:, None] == jnp.arange(N_EXPERTS, dtype=jnp.int32)[None, :]).astype(jnp.int32)
    csum = jnp.cumsum(onehot, axis=0)
    counts = csum[-1]
    padded = ((counts + tm - 1) // tm) * tm
    pend = jnp.cumsum(padded)
    pstart = pend - padded
    pos = jnp.sum(onehot * (pstart[None, :] + csum - onehot), axis=1)
    pair = jnp.arange(n_pairs, dtype=jnp.int32)
    slots = jnp.full((n_slots, 2), -1, jnp.int32).at[pos].set(jnp.stack([pair, gate_bits], axis=1))
    valid = slots[:, 0] >= 0
    tok = jnp.where(valid, slots[:, 0] // TOP_K, 0)
    dst = jnp.where(valid, (slots[:, 0] % TOP_K) * t + slots[:, 0] // TOP_K,
                    n_pairs + jnp.arange(n_slots, dtype=jnp.int32) % tm)
    gate = jnp.where(valid, lax.bitcast_convert_type(slots[:, 1], F32), 0.0)
    tile_start = jnp.arange(n_tiles, dtype=jnp.int32) * tm
    tile_e = jnp.sum((tile_start[:, None] >= pend[None, :]).astype(jnp.int32), axis=1)
    tile_e = jnp.minimum(tile_e, N_EXPERTS - 1)
    n_live = (pend[-1] // tm).reshape(1)
    tok = tok.reshape(n_tiles, tm)
    dst = dst.reshape(n_tiles, tm)
    idx = jnp.stack([tok, jnp.roll(tok, -1, axis=0), dst, jnp.roll(dst, 1, axis=0)], axis=1)
    return idx, gate.reshape(n_slots, 1), tile_e.astype(jnp.int32), n_live.astype(jnp.int32)


def _moe_kernel(te_ref, nl_ref, idx_ref, gate_ref, h_hbm, w1_ref, w3_ref, w2_ref,
                y_hbm, xbuf, obuf, w1b, w3b, w2b, sem_in, sem_out):
    i = pl.program_id(0)
    n_live = nl_ref[0]
    tm = MOE_TILE
    slot = i % 2
    other = 1 - slot

    def gather_copy(src_row, r, s):
        return pltpu.make_async_copy(h_hbm.at[pl.ds(src_row, 1), :], xbuf.at[s, pl.ds(r, 1), :], sem_in.at[s])

    def scatter_copy(r, dst_row, s):
        return pltpu.make_async_copy(obuf.at[s, pl.ds(r, 1), :], y_hbm.at[pl.ds(dst_row, 1), :], sem_out.at[s])

    def wait_gather(s):
        pltpu.make_async_copy(h_hbm.at[pl.ds(0, tm), :], xbuf.at[s], sem_in.at[s]).wait()

    def wait_scatter(s):
        pltpu.make_async_copy(obuf.at[s], y_hbm.at[pl.ds(0, tm), :], sem_out.at[s]).wait()

    def per_row(fn):
        def body(r, carry):
            fn(r)
            return carry
        lax.fori_loop(0, tm, body, 0, unroll=MOE_DMA_UNROLL)

    def expert_tile(scatter_previous):
        packed = xbuf[slot]
        half = packed.shape[-1]
        x_lo = lax.bitcast_convert_type(packed << 16, F32).astype(BF16)
        x_hi = lax.bitcast_convert_type(packed & jnp.uint32(0xFFFF0000), F32).astype(BF16)
        for r in range(tm):
            gather_copy(idx_ref[1, r], r, other).start(priority=r % 2)
        if scatter_previous:
            for r in range(tm):
                scatter_copy(r, idx_ref[3, r], other).start(priority=r % 2)
        up = _dot(x_lo, w1b[0:half, :]) + _dot(x_hi, w1b[half:, :])
        gate_in = _dot(x_lo, w3b[0:half, :]) + _dot(x_hi, w3b[half:, :])
        hid = (up * jax.nn.sigmoid(up)) * gate_in * gate_ref[...]
        return _dot(hid.astype(BF16), w2b[...])

    @pl.when(i == 0)
    def _():
        per_row(lambda r: gather_copy(idx_ref[0, r], r, 0).start())

    @pl.when(i < n_live)
    def _():
        wait_gather(slot)
        prev_e = te_ref[jnp.maximum(i - 1, 0)]

        @pl.when((i == 0) | (te_ref[i] != prev_e))
        def _():
            w1b[...] = w1_ref[...].astype(BF16)
            w3b[...] = w3_ref[...].astype(BF16)
            w2b[...] = w2_ref[...].astype(BF16)

        @pl.when(i == 0)
        def _():
            obuf[0] = expert_tile(False)
            first_spare = y_hbm.shape[0] - tm
            per_row(lambda r: scatter_copy(r, first_spare + r, 0).start())
            wait_scatter(0)

        @pl.when(i > 0)
        def _():
            out = expert_tile(True)

            @pl.when(i > 1)
            def _():
                wait_scatter(slot)

            obuf[slot] = out

        @pl.when(i == n_live - 1)
        def _():
            per_row(lambda r: scatter_copy(r, idx_ref[2, r], slot).start())
            wait_gather(other)

            @pl.when(i > 0)
            def _():
                wait_scatter(other)

            wait_scatter(slot)


def moe_experts(h, idx, gate, tile_e, n_live, w1, w3, w2, layer):
    t = h.shape[0]
    d = w1.shape[-2]
    f = w1.shape[-1]
    tm = MOE_TILE
    n_tiles = idx.shape[0]

    def w_index(i, te, nl):
        return (layer, te[i] // EXPERTS_PER_GROUP, te[i] % EXPERTS_PER_GROUP, 0, 0)

    grid_spec = pltpu.PrefetchScalarGridSpec(
        num_scalar_prefetch=2,
        grid=(n_tiles,),
        in_specs=[pl.BlockSpec((None, 4, tm), lambda i, te, nl: (i, 0, 0), memory_space=pltpu.SMEM),
                  pl.BlockSpec((tm, 1), lambda i, te, nl: (i, 0)),
                  pl.BlockSpec(memory_space=pl.ANY),
                  pl.BlockSpec((None, None, None, d, f), w_index),
                  pl.BlockSpec((None, None, None, d, f), w_index),
                  pl.BlockSpec((None, None, None, f, d), w_index)],
        out_specs=pl.BlockSpec(memory_space=pl.ANY),
        scratch_shapes=[pltpu.VMEM((2, tm, d // 2), jnp.uint32), pltpu.VMEM((2, tm, d), F32),
                        pltpu.VMEM((d, f), BF16), pltpu.VMEM((d, f), BF16), pltpu.VMEM((f, d), BF16),
                        pltpu.SemaphoreType.DMA((2,)), pltpu.SemaphoreType.DMA((2,))])
    return pl.pallas_call(
        _moe_kernel,
        grid_spec=grid_spec,
        out_shape=jax.ShapeDtypeStruct((TOP_K * t + tm, d), F32),
        compiler_params=_params(("arbitrary",)),
        name="moe_experts",
    )(tile_e, n_live, idx, gate, h, w1, w3, w2)


def _mod_spec(k):
    per_b = S_ALL // ROW_TILE
    return pl.BlockSpec((None, 1, D_MODEL),
                        lambda i: ((i // per_b * 2 + jnp.minimum(i % per_b, 1)) * N_MOD + k, 0, 0))


def _norm_mod(x, g_ref, sh_ref, sc_ref):
    h = x * lax.rsqrt(jnp.mean(x * x, axis=-1, keepdims=True) + NORM_EPS) * g_ref[...]
    if sh_ref is None:
        return h
    return h * (1.0 + sc_ref[...]) + sh_ref[...]


def _first_norm_kernel(x_ref, g_ref, sh_ref, sc_ref, h_ref):
    h_ref[...] = _norm_mod(x_ref[...], g_ref, sh_ref, sc_ref).astype(h_ref.dtype)


def first_norm(xs, g, mod, h_dtype):
    rows, d = xs.shape
    tile = pl.BlockSpec((ROW_TILE, d), lambda i: (i, 0))
    return pl.pallas_call(
        _first_norm_kernel,
        grid=(rows // ROW_TILE,),
        in_specs=[tile, pl.BlockSpec((1, d), lambda i: (0, 0)), _mod_spec(0), _mod_spec(1)],
        out_specs=tile,
        out_shape=jax.ShapeDtypeStruct((rows, d), h_dtype),
        compiler_params=_params(("parallel",)),
        name="first_norm",
    )(xs, g.reshape(1, d), mod, mod)


def _post_mixer_kernel(x_ref, u_ref, gate_ref, g_ref, sh_ref, sc_ref, wr_ref, br_ref, xo_ref, h_ref, route_ref):
    x = x_ref[...] + gate_ref[...] * u_ref[...]
    xo_ref[...] = x
    h = _norm_mod(x, g_ref, sh_ref, sc_ref)
    half = h.shape[-1] // 2
    bits = lax.bitcast_convert_type(h.astype(BF16).astype(F32), jnp.uint32)
    h_ref[...] = (bits[:, half:] & jnp.uint32(0xFFFF0000)) | (bits[:, :half] >> 16)
    logits = jnp.dot(h, wr_ref[...], precision=lax.Precision.HIGHEST, preferred_element_type=F32) + br_ref[...]
    route_ref[...] = _route_topk(logits)


def post_mixer(xs, upd, g, mod, router_w, router_b):
    rows, d = xs.shape
    tile = pl.BlockSpec((ROW_TILE, d), lambda i: (i, 0))
    ptile = pl.BlockSpec((ROW_TILE, d // 2), lambda i: (i, 0))
    rtile = pl.BlockSpec((ROW_TILE, HEAD_W), lambda i: (i, 0))
    return pl.pallas_call(
        _post_mixer_kernel,
        grid=(rows // ROW_TILE,),
        in_specs=[tile, tile, _mod_spec(2), pl.BlockSpec((1, d), lambda i: (0, 0)), _mod_spec(3), _mod_spec(4),
                  pl.BlockSpec((d, HEAD_W), lambda i: (0, 0)), pl.BlockSpec((1, HEAD_W), lambda i: (0, 0))],
        out_specs=[tile, ptile, rtile],
        out_shape=[jax.ShapeDtypeStruct((rows, d), F32), jax.ShapeDtypeStruct((rows, d // 2), jnp.uint32),
                   jax.ShapeDtypeStruct((rows, HEAD_W), F32)],
        compiler_params=_params(("parallel",)),
        name="post_mixer",
    )(xs, upd, mod, g.reshape(1, d), mod, mod, router_w, router_b)


def _post_moe_kernel(x_ref, ya_ref, yb_ref, gate_ref, g_ref, *rest, modulate):
    if modulate:
        sh_ref, sc_ref, xo_ref, h_ref = rest
    else:
        sh_ref = sc_ref = None
        xo_ref, h_ref = rest
    x = x_ref[...] + gate_ref[...] * (ya_ref[...] + yb_ref[...])
    xo_ref[...] = x
    h_ref[...] = _norm_mod(x, g_ref, sh_ref, sc_ref).astype(h_ref.dtype)


def post_moe(xs, y2, g, mod, next_mod, h_dtype):
    rows, d = xs.shape
    tile = pl.BlockSpec((ROW_TILE, d), lambda i: (i, 0))
    second = pl.BlockSpec((ROW_TILE, d), lambda i: (rows // ROW_TILE + i, 0))
    gspec = pl.BlockSpec((1, d), lambda i: (0, 0))
    modulate = next_mod is not None
    in_specs = [tile, tile, second, _mod_spec(5), gspec] + ([_mod_spec(0), _mod_spec(1)] if modulate else [])
    args = (xs, y2, y2, mod, g.reshape(1, d)) + ((next_mod, next_mod) if modulate else ())
    return pl.pallas_call(
        functools.partial(_post_moe_kernel, modulate=modulate),
        grid=(rows // ROW_TILE,),
        in_specs=in_specs,
        out_specs=[tile, tile],
        out_shape=[jax.ShapeDtypeStruct((rows, d), F32), jax.ShapeDtypeStruct((rows, d), h_dtype)],
        compiler_params=_params(("parallel",)),
        name="post_moe",
    )(*args)


def _pad_cols(w, n):
    return jnp.pad(w, ((0, 0), (0, n - w.shape[1])))


def _two_dir_lora(x, w1, w2):
    r = w1.shape[-1]
    w1c = _pad_cols(jnp.concatenate([w1[0], w1[1]], axis=1), 256).astype(BF16)
    d = w2.shape[-1]
    w2bd = jnp.zeros((256, 2 * d), F32).at[:r, :d].set(w2[0]).at[r:2 * r, d:].set(w2[1]).astype(BF16)
    return matmul(x, w1c, F32, tn=256), w2bd


def rwkv_mixer(h, p, batch):
    mixes = shift_mix(h, p['x_mix'])
    r = matmul(mixes[0], p['w_rkv'][0].astype(BF16), F32)
    k = matmul(mixes[1], p['w_rkv'][1].astype(BF16), F32)
    v = matmul(mixes[2], p['w_rkv'][2].astype(BF16), F32)
    hid_w, w2bd_w = _two_dir_lora(mixes[3], p['dec_w1'], p['dec_w2'])
    hid_a, w2bd_a = _two_dir_lora(mixes[4], p['iclr_a1'], p['iclr_a2'])
    g_hid = matmul(mixes[5], p['gate_g1'].astype(BF16), F32, tn=256)
    g = matmul(jax.nn.sigmoid(g_hid).astype(BF16), p['gate_g2'].astype(BF16), F32)

    par = jnp.concatenate([p['dec_w0'], p['iclr_a0'], p['k_k'][None], p['k_a'][None], p['r_k'].reshape(1, -1),
                           jnp.zeros((1, D_MODEL), F32)], axis=0)
    y0, rm, d0, bonus = scan_prepare(r, k, v, jnp.tanh(hid_w).astype(BF16), hid_a.astype(BF16), w2bd_w, w2bd_a, par)
    y_fwd, y_rev = scan_apply(y0, rm, d0, batch)
    yo = rwkv_out(y_fwd, y_rev, bonus, g, p['gn_g'], p['gn_b'])
    return matmul(yo, p['w_out'].astype(BF16), F32)


def even_mixer(h, p, layer_idx, cos_t, sin_t, batch):
    rows = batch * S_ALL
    lam_init = 0.8 - 0.6 * math.exp(-0.3 * layer_idx)
    proj = matmul(h, p['w_in'].astype(BF16), BF16, tn=1024)
    proj = proj.reshape(batch, S_ALL, -1)
    lf = p['diff_lambda']
    lam = (jnp.exp(jnp.sum(lf[0] * lf[1])) - jnp.exp(jnp.sum(lf[2] * lf[3])) + lam_init).reshape(1)
    a_out = diff_attention(proj, lam, cos_t, sin_t, p['subln_g'], lam_init)
    b_out = neighbourhood_attention(proj, na_bias_table(p['rpb']))
    mixed = jnp.concatenate([a_out, b_out], axis=-1).reshape(rows, -1)
    return matmul(mixed, p['w_out'].astype(BF16), F32)


def router_table(router_g, router_g_b, router_e, router_e_b):
    w = _pad_cols(jnp.concatenate([router_g, router_e], axis=1), HEAD_W)
    b = _pad_cols(jnp.concatenate([router_g_b, router_e_b])[None, :], HEAD_W)
    return w, b


def kernel(x, c, ctx, c_ctx, ada_w, ada_b, norm_g, final_g, even_w_in, even_w_out, diff_lambda, diff_subln_g, na_rpb, rwkv_x_mix, rwkv_w_rkv, rwkv_w_out, rwkv_dec_w0, rwkv_dec_w1, rwkv_dec_w2, rwkv_iclr_a0, rwkv_iclr_a1, rwkv_iclr_a2, rwkv_gate_g1, rwkv_gate_g2, rwkv_k_k, rwkv_k_a, rwkv_r_k, rwkv_gn_g, rwkv_gn_b, moe_router_g, moe_router_g_b, moe_router_e, moe_router_e_b, moe_w1, moe_w3, moe_w2):
    batch = x.shape[0]
    rows = batch * S_ALL
    xs = jnp.concatenate([ctx, x], axis=1).reshape(rows, D_MODEL)
    cos_t, sin_t = rope_tables()

    cvec = jnp.concatenate([c, c_ctx[None], jnp.zeros((8 - batch - 1, D_MODEL), F32)], axis=0)
    mods = ada_modulation(jax.nn.silu(cvec).astype(BF16), ada_w, ada_b)
    mods = mods.reshape(DEPTH, 8, N_MOD, D_MODEL)

    def mod_table(i):
        mod_l = mods[i, :batch]
        mod_c = jnp.broadcast_to(mods[i, batch][None], mod_l.shape)
        return jnp.stack([mod_c, mod_l], axis=1).reshape(batch * 2 * N_MOD, 1, D_MODEL)

    mod = mod_table(0)
    h = first_norm(xs, norm_g[0, 0], mod, BF16)
    for i in range(DEPTH):
        j = i // 2
        if i % 2 == 0:
            p = dict(w_in=even_w_in[j], w_out=even_w_out[j], diff_lambda=diff_lambda[j],
                     subln_g=diff_subln_g[j], rpb=na_rpb[j])
            out = even_mixer(h, p, i, cos_t, sin_t, batch)
        else:
            p = dict(x_mix=rwkv_x_mix[j], w_rkv=rwkv_w_rkv[j], w_out=rwkv_w_out[j], dec_w0=rwkv_dec_w0[j],
                     dec_w1=rwkv_dec_w1[j], dec_w2=rwkv_dec_w2[j], iclr_a0=rwkv_iclr_a0[j],
                     iclr_a1=rwkv_iclr_a1[j], iclr_a2=rwkv_iclr_a2[j], gate_g1=rwkv_gate_g1[j],
                     gate_g2=rwkv_gate_g2[j], k_k=rwkv_k_k[j], k_a=rwkv_k_a[j], r_k=rwkv_r_k[j],
                     gn_g=rwkv_gn_g[j], gn_b=rwkv_gn_b[j])
            out = rwkv_mixer(h, p, batch)
        rw, rb = router_table(moe_router_g[i], moe_router_g_b[i], moe_router_e[i], moe_router_e_b[i])
        xs, h2, route = post_mixer(xs, out, norm_g[i, 1], mod, rw, rb)
        idx, gate, tile_e, n_live = moe_dispatch(route)
        y2 = moe_experts(h2, idx, gate, tile_e, n_live, moe_w1, moe_w3, moe_w2, i)
        if i + 1 < DEPTH:
            next_mod = mod_table(i + 1)
            xs, h = post_moe(xs, y2, norm_g[i + 1, 0], mod, next_mod, F32 if (i + 1) % 2 else BF16)
            mod = next_mod
        else:
            _, h = post_moe(xs, y2, final_g, mod, None, F32)
    return h.reshape(batch, S_ALL, D_MODEL)[:, CTX_LEN:]
```

```python
import functools
import math

import numpy as np
import jax
import jax.numpy as jnp
from jax import lax
from jax.experimental import pallas as pl
from jax.experimental.pallas import tpu as pltpu

F32 = jnp.float32
BF16 = jnp.bfloat16

D_MODEL = 2048
DEPTH = 4
GRID_W = 64
CTX_LEN = 256
SEQ = 4096
S_ALL = CTX_LEN + SEQ
N_MOD = 6
NORM_EPS = 1e-6
NEG_INF = -1e30

DIFF_HEADS = 8
DIFF_QK_DIM = 64
NA_HEADS = 8
NA_DIM = 128
NA_KH = 8
NA_KW = 16
ROPE_THETA = 10000.0
SUBLN_EPS = 1e-5
HEAD_W = 128
ATT_TILE = 256
DIFF_KV_BLOCK = 2048
NA_ROWS_PER_STEP = 4
NA_WIN_ROWS = 12

RWKV_HEAD = 64
RWKV_HEADS = D_MODEL // RWKV_HEAD
GN_EPS = 64e-5
SCAN_CHUNK = 64
SCAN_PAIR_W = 2 * RWKV_HEAD
SCAN_SUB = 16

N_GROUPS = 4
EXPERTS_PER_GROUP = 8
N_EXPERTS = N_GROUPS * EXPERTS_PER_GROUP
TOP_K = 2
EXPERT_FF = 512
MOE_TILE = 256
MOE_DMA_UNROLL = 8
ROW_TILE = 256

VMEM_LIMIT = 52 * 1024 * 1024


def _params(sem):
    return pltpu.CompilerParams(dimension_semantics=sem, vmem_limit_bytes=VMEM_LIMIT)


def _dot(a, b):
    return jnp.dot(a, b, preferred_element_type=F32)


def _dot_nt(a, b):
    return lax.dot_general(a, b, (((1,), (1,)), ((), ())), preferred_element_type=F32)


def _dot_tn(a, b):
    return lax.dot_general(a, b, (((0,), (0,)), ((), ())), preferred_element_type=F32)


def _mm_kernel(a_ref, w_ref, o_ref):
    o_ref[...] = _dot(a_ref[...], w_ref[...]).astype(o_ref.dtype)


def matmul(a, w, out_dtype, tm=1024, tn=512):
    m, k = a.shape
    n = w.shape[1]
    while m % tm:
        tm //= 2
    tn = min(tn, n)
    assert n % tn == 0
    return pl.pallas_call(
        _mm_kernel,
        grid=(m // tm, n // tn),
        in_specs=[pl.BlockSpec((tm, k), lambda i, j: (i, 0)),
                  pl.BlockSpec((k, tn), lambda i, j: (0, j))],
        out_specs=pl.BlockSpec((tm, tn), lambda i, j: (i, j)),
        out_shape=jax.ShapeDtypeStruct((m, n), out_dtype),
        compiler_params=_params(("parallel", "parallel")),
        name="matmul",
    )(a, w)


def _ada_kernel(s_ref, w_ref, b_ref, o_ref):
    o_ref[...] = _dot(s_ref[...], w_ref[...].astype(BF16)) + b_ref[...]


def ada_modulation(svec, ada_w, ada_b, tn=1024):
    nl, d, n = ada_w.shape
    rows = svec.shape[0]
    return pl.pallas_call(
        _ada_kernel,
        grid=(nl, n // tn),
        in_specs=[pl.BlockSpec((rows, d), lambda l, j: (0, 0)),
                  pl.BlockSpec((None, d, tn), lambda l, j: (l, 0, j)),
                  pl.BlockSpec((None, 1, tn), lambda l, j: (l, 0, j))],
        out_specs=pl.BlockSpec((None, rows, tn), lambda l, j: (l, 0, j)),
        out_shape=jax.ShapeDtypeStruct((nl, rows, n), F32),
        compiler_params=_params(("parallel", "parallel")),
        name="ada_modulation",
    )(svec, ada_w, ada_b.reshape(nl, 1, n))


def _rope(x, cos, sin_signed):
    lane = lax.broadcasted_iota(jnp.int32, x.shape, 1)
    first_half = (lane & 63) < 32
    partner = jnp.where(first_half, pltpu.roll(x, HEAD_W - 32, 1), pltpu.roll(x, 32, 1))
    return x * cos + partner * sin_signed


def _diff_attn_kernel(lam_ref, q_ref, k_ref, v_ref, cos_ref, sin_ref, g_ref, o_ref, kr_ref, vt_ref, sa_ref, sb_ref, *,
                      post_scale):
    j = pl.program_id(2)
    tq = ATT_TILE

    @pl.when(j == 0)
    def _():
        def prep_chunk(c, carry):
            rows = pl.ds(pl.multiple_of(c * ATT_TILE, ATT_TILE), ATT_TILE)
            kr_ref[rows, :] = _rope(k_ref[rows, :].astype(F32), cos_ref[rows, :], sin_ref[rows, :]).astype(BF16)
            vt_ref[:, rows] = v_ref[rows, :].astype(F32).T.astype(BF16)
            return carry
        lax.fori_loop(0, S_ALL // ATT_TILE, prep_chunk, 0)

    qrows = pl.ds(pl.multiple_of(j * tq, tq), tq)
    q = _rope(q_ref[...].astype(F32), cos_ref[qrows, :], sin_ref[qrows, :]) * (DIFF_QK_DIM ** -0.5)
    lane = lax.broadcasted_iota(jnp.int32, q.shape, 1)
    q1 = jnp.where(lane < DIFF_QK_DIM, q, 0.0).astype(BF16)
    q2 = jnp.where(lane < DIFF_QK_DIM, 0.0, q).astype(BF16)

    kb = DIFF_KV_BLOCK
    n_latent = SEQ // kb
    maps = (q1, q2)

    def score_stage(start, size, sbuf):
        kc = kr_ref[pl.ds(start, size), :]
        tops = []
        for i, qm in enumerate(maps):
            s = _dot_nt(kc, qm)
            sbuf[i, 0:size, :] = s
            tops.append(jnp.max(s, axis=0, keepdims=True))
        return tuple(tops)

    def softmax_stage(start, size, sbuf, tops, state):
        vt = vt_ref[:, pl.ds(start, size)]
        new = []
        for i in range(2):
            m, l, acc = state[3 * i:3 * i + 3]
            m_new = jnp.maximum(m, tops[i])
            alpha = jnp.exp(m - m_new)
            p = jnp.exp(sbuf[i, 0:size, :] - m_new)
            new += [m_new, alpha * l + jnp.sum(p, axis=0, keepdims=True),
                    alpha * acc + _dot(vt, p.astype(BF16))]
        return tuple(new)

    def latent_start(n):
        return pl.multiple_of(CTX_LEN + jnp.minimum(n, n_latent - 1) * kb, CTX_LEN)

    row = jnp.full((1, tq), NEG_INF, F32)
    zrow = jnp.zeros((1, tq), F32)
    zacc = jnp.zeros((HEAD_W, tq), F32)
    state = softmax_stage(0, CTX_LEN, sa_ref, score_stage(0, CTX_LEN, sa_ref), (row, zrow, zacc, row, zrow, zacc))
    tops = score_stage(latent_start(0), kb, sb_ref)

    def block_pair(n, carry, more_follow):
        st, tp = carry[:6], carry[6:]
        tp_next = score_stage(latent_start(n + 1), kb, sa_ref)
        st = softmax_stage(latent_start(n), kb, sb_ref, tp, st)
        tp_after = score_stage(latent_start(n + 2), kb, sb_ref) if more_follow else tp_next
        st = softmax_stage(latent_start(n + 1), kb, sa_ref, tp_next, st)
        return st + tp_after

    n_pairs = n_latent // 2
    carry = lax.fori_loop(0, jnp.where(j == 0, 0, n_pairs - 1),
                          lambda t, c: block_pair(2 * t, c, True), state + tops)
    carry = lax.fori_loop(0, jnp.where(j == 0, 0, 1),
                          lambda t, c: block_pair(2 * (n_pairs - 1), c, False), carry)
    m1, l1, a1, m2, l2, a2 = carry[:6]
    out = a1 / l1 - lam_ref[0] * (a2 / l2)
    ms = jnp.mean(out * out, axis=0, keepdims=True)
    y = out * lax.rsqrt(ms + SUBLN_EPS) * (g_ref[...] * post_scale)
    o_ref[...] = y.T.astype(o_ref.dtype)


def diff_attention(proj, lam, cos_t, sin_t, subln_g, lam_init):
    b = proj.shape[0]
    kernel = functools.partial(_diff_attn_kernel, post_scale=1.0 - lam_init)
    return pl.pallas_call(
        kernel,
        grid=(b, DIFF_HEADS, S_ALL // ATT_TILE),
        in_specs=[pl.BlockSpec(memory_space=pltpu.SMEM),
                  pl.BlockSpec((None, ATT_TILE, HEAD_W), lambda bi, h, j: (bi, j, h)),
                  pl.BlockSpec((None, S_ALL, HEAD_W), lambda bi, h, j: (bi, 0, DIFF_HEADS + h)),
                  pl.BlockSpec((None, S_ALL, HEAD_W), lambda bi, h, j: (bi, 0, 2 * DIFF_HEADS + h)),
                  pl.BlockSpec((S_ALL, HEAD_W), lambda bi, h, j: (0, 0)),
                  pl.BlockSpec((S_ALL, HEAD_W), lambda bi, h, j: (0, 0)),
                  pl.BlockSpec((HEAD_W, 1), lambda bi, h, j: (0, 0))],
        out_specs=pl.BlockSpec((None, ATT_TILE, HEAD_W), lambda bi, h, j: (bi, j, h)),
        out_shape=jax.ShapeDtypeStruct((b, S_ALL, DIFF_HEADS * HEAD_W), BF16),
        scratch_shapes=[pltpu.VMEM((S_ALL, HEAD_W), BF16), pltpu.VMEM((HEAD_W, S_ALL), BF16),
                        pltpu.VMEM((2, DIFF_KV_BLOCK, ATT_TILE), F32), pltpu.VMEM((2, DIFF_KV_BLOCK, ATT_TILE), F32)],
        compiler_params=_params(("parallel", "parallel", "arbitrary")),
        name="diff_attention",
    )(lam, proj, proj, proj, cos_t, sin_t, subln_g.reshape(HEAD_W, 1))


def rope_tables():
    n_freq = DIFF_QK_DIM // 4
    inv_freq = ROPE_THETA ** (-jnp.arange(n_freq, dtype=F32) / n_freq)
    t = jnp.arange(SEQ, dtype=jnp.int32)
    row = (t // GRID_W).astype(F32)
    col = (t % GRID_W).astype(F32)
    ang = jnp.concatenate([row[:, None] * inv_freq, col[:, None] * inv_freq], axis=-1)
    cos, sin = jnp.cos(ang), jnp.sin(ang)
    cos_l = jnp.concatenate([cos, cos, cos, cos], axis=-1)
    sin_l = jnp.concatenate([-sin, sin, -sin, sin], axis=-1)
    cos_all = jnp.concatenate([jnp.ones((CTX_LEN, HEAD_W), F32), cos_l], axis=0)
    sin_all = jnp.concatenate([jnp.zeros((CTX_LEN, HEAD_W), F32), sin_l], axis=0)
    return cos_all, sin_all


def _na_window_start(j):
    g = j - 1
    return jnp.clip(NA_ROWS_PER_STEP * g - NA_KH // 2, 0, SEQ // GRID_W - NA_WIN_ROWS)


def _na_kernel(q_ref, k_ref, v_ref, bias_ref, o_ref):
    j = pl.program_id(2)
    win = NA_WIN_ROWS * GRID_W
    start = pl.multiple_of(CTX_LEN + _na_window_start(j) * GRID_W, GRID_W)
    scale = NA_DIM ** -0.5
    q = q_ref[...]
    s_c = _dot_nt(q, k_ref[pl.ds(0, CTX_LEN), :]) * scale
    s_w = _dot_nt(q, k_ref[pl.ds(start, win), :]) * scale + bias_ref[...]
    m = jnp.maximum(jnp.max(s_c, axis=-1, keepdims=True), jnp.max(s_w, axis=-1, keepdims=True))
    p_c = jnp.exp(s_c - m)
    p_w = jnp.exp(s_w - m)
    l = jnp.sum(p_c, axis=-1, keepdims=True) + jnp.sum(p_w, axis=-1, keepdims=True)
    o = _dot(p_c.astype(BF16), v_ref[pl.ds(0, CTX_LEN), :]) + _dot(p_w.astype(BF16), v_ref[pl.ds(start, win), :])
    o_ref[...] = (o / l).astype(o_ref.dtype)


def _na_bias_pattern(j):
    n_groups = SEQ // (GRID_W * NA_ROWS_PER_STEP)
    g = j - 1
    return jnp.where(j == 0, 3, jnp.where(g == 0, 0, jnp.where(g == n_groups - 1, 2, 1)))


def na_bias_table(rpb):
    rows = SEQ // GRID_W
    n_groups = rows // NA_ROWS_PER_STEP
    cols = np.arange(GRID_W)
    col_start = np.clip(cols - NA_KW // 2, 0, GRID_W - NA_KW)
    col_mask = (cols[None, :] >= col_start[:, None]) & (cols[None, :] < col_start[:, None] + NA_KW)
    c_idx = np.clip(cols[None, :] - cols[:, None] + NA_KW - 1, 0, 2 * NA_KW - 2)
    pats = []
    for g in (0, 1, n_groups - 1):
        u0 = int(np.clip(NA_ROWS_PER_STEP * g - NA_KH // 2, 0, rows - NA_WIN_ROWS))
        r = NA_ROWS_PER_STEP * g + np.arange(NA_ROWS_PER_STEP)
        r0 = np.clip(r - NA_KH // 2, 0, rows - NA_KH)
        kr = u0 + np.arange(NA_WIN_ROWS)
        valid_r = (kr[None, :] >= r0[:, None]) & (kr[None, :] < r0[:, None] + NA_KH)
        r_idx = np.clip(kr[None, :] - r[:, None] + NA_KH - 1, 0, 2 * NA_KH - 2)
        valid = valid_r[:, None, :, None] & col_mask[None, :, None, :]
        r_sel = np.eye(2 * NA_KH - 1, dtype=np.float32)[r_idx]
        c_sel = np.eye(2 * NA_KW - 1, dtype=np.float32)[c_idx]
        rows_sel = jnp.einsum('qkr,hrc->hqkc', r_sel, rpb.astype(F32), precision=lax.Precision.HIGHEST)
        gathered = jnp.einsum('hqkc,abc->hqakb', rows_sel, c_sel, precision=lax.Precision.HIGHEST)
        pats.append(jnp.where(valid[None], gathered, NEG_INF))
    pats.append(jnp.full_like(pats[0], NEG_INF))
    tab = jnp.stack(pats, axis=1)
    return tab.reshape(NA_HEADS, 4, NA_ROWS_PER_STEP * GRID_W, NA_WIN_ROWS * GRID_W)


def neighbourhood_attention(proj, bias_tab):
    b = proj.shape[0]
    tq = NA_ROWS_PER_STEP * GRID_W
    assert tq == CTX_LEN
    win = NA_WIN_ROWS * GRID_W
    base = 3 * DIFF_HEADS
    return pl.pallas_call(
        _na_kernel,
        grid=(b, NA_HEADS, S_ALL // tq),
        in_specs=[pl.BlockSpec((None, tq, HEAD_W), lambda bi, h, j: (bi, j, base + h)),
                  pl.BlockSpec((None, S_ALL, HEAD_W), lambda bi, h, j: (bi, 0, base + NA_HEADS + h)),
                  pl.BlockSpec((None, S_ALL, HEAD_W), lambda bi, h, j: (bi, 0, base + 2 * NA_HEADS + h)),
                  pl.BlockSpec((None, None, tq, win), lambda bi, h, j: (h, _na_bias_pattern(j), 0, 0))],
        out_specs=pl.BlockSpec((None, tq, HEAD_W), lambda bi, h, j: (bi, j, h)),
        out_shape=jax.ShapeDtypeStruct((b, S_ALL, NA_HEADS * HEAD_W), BF16),
        compiler_params=_params(("parallel", "parallel", "arbitrary")),
        name="neighbourhood_attention",
    )(proj, proj, proj, bias_tab)


def _each(fn, *lists):
    return [fn(*args) for args in zip(*lists)]


def _head_sum(x):
    rows, w = x.shape
    r_i = lax.broadcasted_iota(jnp.int32, (w, w), 0)
    c_i = lax.broadcasted_iota(jnp.int32, (w, w), 1)
    ones_bd = jnp.where((r_i < RWKV_HEAD) == (c_i < RWKV_HEAD), 1.0, 0.0).astype(BF16)
    hi = x.astype(BF16)
    lo = (x - hi.astype(F32)).astype(BF16)
    s = _dot(jnp.concatenate([hi, lo], axis=0), ones_bd)
    return s[:rows] + s[rows:]


def _scan_prepare_kernel(r_ref, k_ref, v_ref, hw_ref, ha_ref, w2w0_ref, w2w1_ref, w2a0_ref, w2a1_ref, par_ref,
                         y0_ref, rm_ref, d0_ref, bonus_ref):
    c = SCAN_CHUNK
    w = SCAN_PAIR_W
    hw = RWKV_HEAD
    t_idx = lax.broadcasted_iota(jnp.int32, (c, w), 0)
    lane = lax.broadcasted_iota(jnp.int32, (c, w), 1)
    s_idx = lane & (hw - 1)
    head0 = lane < hw
    eye = s_idx == t_idx
    tt = lax.broadcasted_iota(jnp.int32, (c, c), 0)
    ss = lax.broadcasted_iota(jnp.int32, (c, c), 1)
    strict = (s_idx < t_idx, s_idx > t_idx)
    incl = (s_idx <= t_idx, s_idx >= t_idx)
    tri = (jnp.where(ss <= tt, 1.0, 0.0).astype(BF16), jnp.where(ss >= tt, 1.0, 0.0).astype(BF16))
    row2 = lax.broadcasted_iota(jnp.int32, (w, w), 0)
    lane2 = lax.broadcasted_iota(jnp.int32, (w, w), 1)
    bdmask = (row2 < hw) == (lane2 < hw)
    eye2 = row2 == lane2

    def bd(y):
        return jnp.where(bdmask, jnp.concatenate([y, y], axis=0), 0.0).astype(BF16)

    def pm(x, ybd):
        return _dot(x.astype(BF16), ybd)

    inst = [(d, sub) for d in range(2) for sub in range(SCAN_SUB)]
    dirs = [d for d, _ in inst]
    rows = [slice(sub * c, (sub + 1) * c) for _, sub in inst]
    par = par_ref[...]
    k_k, k_a, r_k = par[4:5, :], par[5:6, :], par[6:7, :]
    sub_rows = [slice(sub * c, (sub + 1) * c) for sub in range(SCAN_SUB)]
    r_s = [r_ref[rw, :] for rw in sub_rows]
    k_s = [k_ref[rw, :] for rw in sub_rows]
    v_s = [v_ref[rw, :] for rw in sub_rows]
    hid_w = hw_ref[...]
    hid_a = ha_ref[...]
    wl_all = [_dot(hid_w, w2w0_ref[...]), _dot(hid_w, w2w1_ref[...])]
    al_all = [_dot(hid_a, w2a0_ref[...]), _dot(hid_a, w2a1_ref[...])]
    wl_s = [[wl_all[d][rw, :] for rw in sub_rows] for d in range(2)]
    al_s = [[al_all[d][rw, :] for rw in sub_rows] for d in range(2)]

    def unit_key(k_i):
        kk_i = k_i * k_k
        return kk_i * lax.rsqrt(jnp.maximum(_head_sum(kk_i * kk_i), 1e-24))

    def log_decay(wl_i, d):
        z = -(par[d:d + 1, :] + wl_i)
        softplus = jnp.maximum(z, 0.0) + jnp.log(1.0 + jnp.exp(-jnp.abs(z)))
        return -jnp.exp(-softplus - 0.5)

    kk_s = _each(unit_key, k_s)
    a_s = [[jax.nn.sigmoid(par[2 + d:3 + d, :] + x) for x in al_s[d]] for d in range(2)]
    kd_s = [[k_i * (1.0 + (a_i - 1.0) * k_a) for k_i, a_i in zip(k_s, a_s[d])] for d in range(2)]
    bonus = [_head_sum(r_i * r_k * (kd0 + kd1)) * v_i for r_i, kd0, kd1, v_i in zip(r_s, kd_s[0], kd_s[1], v_s)]

    r = [r_s[sub] for _, sub in inst]
    kk = [kk_s[sub] for _, sub in inst]
    v = [v_s[sub] for _, sub in inst]
    ld = [log_decay(wl_s[d][sub], d) for d, sub in inst]
    a = [a_s[d][sub] for d, sub in inst]
    kd = [kd_s[d][sub] for d, sub in inst]

    def cumulative(ld_i, d):
        p_hi = ld_i.astype(BF16)
        rem = ld_i - p_hi.astype(F32)
        p_mid = rem.astype(BF16)
        p_lo = (rem - p_mid.astype(F32)).astype(BF16)
        cs = _dot(tri[d], jnp.concatenate([p_hi, p_mid, p_lo], axis=1))
        return cs[:, :w] + cs[:, w:2 * w] + cs[:, 2 * w:]

    lam = _each(cumulative, ld, dirs)
    lam_end = _each(lambda l, d: l[0:1, :] if d else l[c - 1:c, :], lam, dirs)
    g_cum = _each(jnp.exp, lam)
    g_inv = _each(lambda l: jnp.exp(-l), lam)
    g_prev = _each(lambda l, x: jnp.exp(l - x), lam, ld)
    g_rel = _each(lambda le, l: jnp.exp(le - l), lam_end, lam)
    g_end = _each(jnp.exp, lam_end)

    qk = _each(jnp.multiply, kk, g_prev)
    rt = _each(jnp.multiply, r, g_cum)
    beta = _each(jnp.multiply, kk, a)
    bt = _each(jnp.multiply, beta, g_inv)
    kt = _each(jnp.multiply, kd, g_inv)
    bh = _each(jnp.multiply, beta, g_rel)
    kh = _each(jnp.multiply, kd, g_rel)

    def big_product(qk_i, rt_i, bt_i, kt_i):
        rhs = jnp.concatenate([jnp.where(head0, bt_i, 0.0), jnp.where(head0, 0.0, bt_i),
                               jnp.where(head0, kt_i, 0.0), jnp.where(head0, 0.0, kt_i)], axis=0).astype(BF16)
        return _dot_nt(jnp.concatenate([qk_i, rt_i], axis=0).astype(BF16), rhs)

    big = _each(big_product, qk, rt, bt, kt)
    n_pow = _each(lambda b, d: jnp.where(strict[d], -b[:c, :w], 0.0), big, dirs)
    a_k = _each(lambda b, d: jnp.where(strict[d], b[:c, w:], 0.0), big, dirs)
    g_b = _each(lambda b, d: jnp.where(incl[d], b[c:, :w], 0.0), big, dirs)
    g_k = _each(lambda b, d: jnp.where(incl[d], b[c:, w:], 0.0), big, dirs)

    t_inv = _each(lambda n: jnp.where(eye, 1.0, 0.0) + n, n_pow)
    for _ in range(int(math.log2(c)) - 1):
        n_pow = _each(lambda n: pm(n, bd(n)), n_pow)
        t_inv = _each(lambda t, n: t + pm(t, bd(n)), t_inv, n_pow)

    v_bd = _each(bd, v)
    akv_gkv = _each(lambda ak, gk, vb: pm(jnp.concatenate([ak, gk], axis=0), vb), a_k, g_k, v_bd)
    x1 = _each(lambda x: x[:c], akv_gkv)
    uw = _each(lambda t, x, q: _dot(t.astype(BF16), jnp.concatenate([bd(x), bd(q)], axis=1)), t_inv, x1, qk)
    u0 = _each(lambda x: -x[:, :w], uw)
    wm = _each(lambda x: x[:, w:], uw)
    gb_uw = _each(lambda gb, u, wm_i: _dot(gb.astype(BF16), jnp.concatenate([bd(u), bd(wm_i)], axis=1)), g_b, u0, wm)
    y0 = _each(lambda x, y: x[c:] + y[:, :w], akv_gkv, gb_uw)
    rm = _each(lambda rt_i, y: rt_i - y[:, w:], rt, gb_uw)
    d0 = _each(lambda kh_i, bh_i, v_i, u: jnp.where(bdmask, _dot_tn(
        jnp.concatenate([kh_i, bh_i], axis=0).astype(BF16), jnp.concatenate([v_i, u], axis=0).astype(BF16)), 0.0),
        kh, bh, v, u0)
    mm = _each(lambda bh_i, wm_i, ge: jnp.where(eye2, ge, 0.0) - jnp.where(
        bdmask, _dot_tn(bh_i.astype(BF16), wm_i.astype(BF16)), 0.0), bh, wm, g_end)
    d0 = _each(lambda x: x[:hw] + x[hw:], d0)
    mm = _each(lambda x: x[:hw] + x[hw:], mm)
    mm_hi = _each(lambda m: m.astype(BF16), mm)
    mm_lo = _each(lambda m, mh: (m - mh.astype(F32)).astype(BF16), mm, mm_hi)

    for sub in range(SCAN_SUB):
        bonus_ref[sub_rows[sub], :] = bonus[sub]
    for i, (d, sub) in enumerate(inst):
        y0_ref[d, rows[i], :] = y0[i]
        rm_ref[d, sub, 0, 0:c, :] = rm[i].astype(BF16)
        rm_ref[d, sub, 0, c:c + hw, :] = mm_hi[i]
        rm_ref[d, sub, 0, c + hw:c + 2 * hw, :] = mm_lo[i]
        d0_ref[d, sub, 0, :, :] = d0[i]


def scan_prepare(r, k, v, hid_w, hid_a, w2_w, w2_a, par):
    rows, d = r.shape
    c, w = SCAN_CHUNK, SCAN_PAIR_W
    n_pairs = d // w
    n_chunks = rows // c
    blk = SCAN_SUB * c
    rank = hid_w.shape[1]
    shared = pl.BlockSpec((blk, w), lambda i, p: (i, p))
    hidden = pl.BlockSpec((blk, rank), lambda i, p: (i, 0))
    fwd_w2 = pl.BlockSpec((rank, w), lambda i, p: (0, p))
    rev_w2 = pl.BlockSpec((rank, w), lambda i, p: (0, n_pairs + p))
    return pl.pallas_call(
        _scan_prepare_kernel,
        grid=(rows // blk, n_pairs),
        in_specs=[shared, shared, shared, hidden, hidden, fwd_w2, rev_w2, fwd_w2, rev_w2,
                  pl.BlockSpec((8, w), lambda i, p: (0, p))],
        out_specs=[pl.BlockSpec((2, blk, w), lambda i, p: (0, i, p)),
                   pl.BlockSpec((2, SCAN_SUB, 1, c + w, w), lambda i, p: (0, i, p, 0, 0)),
                   pl.BlockSpec((2, SCAN_SUB, 1, w // 2, w), lambda i, p: (0, i, p, 0, 0)),
                   shared],
        out_shape=[jax.ShapeDtypeStruct((2, rows, d), F32),
                   jax.ShapeDtypeStruct((2, n_chunks, n_pairs, c + w, w), BF16),
                   jax.ShapeDtypeStruct((2, n_chunks, n_pairs, w // 2, w), F32),
                   jax.ShapeDtypeStruct((rows, d), F32)],
        compiler_params=_params(("parallel", "parallel")),
        name="scan_prepare",
    )(r, k, v, hid_w, hid_a, w2_w, w2_w, w2_a, w2_a, par)


def _scan_apply_kernel(y0f_ref, rmf_ref, d0f_ref, y0r_ref, rmr_ref, d0r_ref, yf_ref, yr_ref, z_ref, *, n_pairs):
    c, w = SCAN_CHUNK, SCAN_PAIR_W
    i = pl.program_id(1)

    @pl.when(i == 0)
    def _():
        z_ref[...] = jnp.zeros_like(z_ref)

    hw = RWKV_HEAD
    row2 = lax.broadcasted_iota(jnp.int32, (w, w), 0)
    lane2 = lax.broadcasted_iota(jnp.int32, (w, w), 1)
    bdmask = (row2 < hw) == (lane2 < hw)

    def bd(x):
        return jnp.where(bdmask, jnp.concatenate([x, x], axis=0), 0.0)

    chains = [(d, p) for d in range(2) for p in range(n_pairs)]
    y0_refs, rm_refs, d0_refs, y_refs = (y0f_ref, y0r_ref), (rmf_ref, rmr_ref), (d0f_ref, d0r_ref), (yf_ref, yr_ref)
    z = [bd(z_ref[d, p]) for d, p in chains]
    z_hi = _each(lambda x: x.astype(BF16), z)
    z_lo = _each(lambda x, xh: (x - xh.astype(F32)).astype(BF16), z, z_hi)
    res = [_dot(rm_refs[d][p], jnp.concatenate([zh, zl], axis=1)) for (d, p), zh, zl in zip(chains, z_hi, z_lo)]
    res = _each(lambda x: x[:, :w] + x[:, w:], res)
    y = [y0_refs[d][:, p * w:(p + 1) * w] + x[:c] for (d, p), x in zip(chains, res)]
    z_new = [d0_refs[d][p] + x[c:c + hw] + x[c + hw:] for (d, p), x in zip(chains, res)]
    for (d, p), y_i, z_i in zip(chains, y, z_new):
        y_refs[d][:, p * w:(p + 1) * w] = y_i
        z_ref[d, p] = z_i


def _scan_chunk_index(i, d):
    n_ctx = CTX_LEN // SCAN_CHUNK
    n_all = S_ALL // SCAN_CHUNK
    return jnp.where(d == 0, i, jnp.where(i < n_ctx, n_ctx - 1 - i, n_all + n_ctx - 1 - i))


def scan_apply(y0, rm, d0, batch):
    _, rows, d = y0.shape
    c, w = SCAN_CHUNK, SCAN_PAIR_W
    n_pairs = d // w
    per_b = rows // batch // c

    def specs(dr):
        def idx(b, i):
            return b * per_b + _scan_chunk_index(i, dr)
        return [pl.BlockSpec((None, c, d), lambda b, i: (dr, idx(b, i), 0)),
                pl.BlockSpec((None, None, n_pairs, c + w, w), lambda b, i: (dr, idx(b, i), 0, 0, 0)),
                pl.BlockSpec((None, None, n_pairs, w // 2, w), lambda b, i: (dr, idx(b, i), 0, 0, 0)),
                pl.BlockSpec((c, d), lambda b, i: (idx(b, i), 0))]

    fwd, rev = specs(0), specs(1)
    kernel = functools.partial(_scan_apply_kernel, n_pairs=n_pairs)
    return pl.pallas_call(
        kernel,
        grid=(batch, per_b),
        in_specs=fwd[:3] + rev[:3],
        out_specs=[fwd[3], rev[3]],
        out_shape=[jax.ShapeDtypeStruct((rows, d), F32)] * 2,
        scratch_shapes=[pltpu.VMEM((2, n_pairs, w // 2, w), F32)],
        compiler_params=_params(("parallel", "arbitrary")),
        name="scan_apply",
    )(y0, rm, d0, y0, rm, d0)


def _shift_mix_kernel(h_ref, hp_ref, hn_ref, mix_ref, *out_refs):
    per_b = S_ALL // ROW_TILE
    pos = pl.program_id(0) % per_b
    h = h_ref[...]
    row = lax.broadcasted_iota(jnp.int32, h.shape, 0)
    starts_seq = (pos == 0) | (pos == 1)
    ends_seq = (pos == 0) | (pos == per_b - 1)
    prev_row = jnp.where(starts_seq, 0.0, hp_ref[7:8, :])
    next_row = jnp.where(ends_seq, 0.0, hn_ref[0:1, :])
    prev = jnp.where(row == 0, prev_row, pltpu.roll(h, 1, 0))
    nxt = jnp.where(row == ROW_TILE - 1, next_row, pltpu.roll(h, ROW_TILE - 1, 0))
    xx = 0.5 * (prev + nxt) - h
    for j, o_ref in enumerate(out_refs):
        o_ref[...] = (h + xx * mix_ref[j:j + 1, :]).astype(o_ref.dtype)


def shift_mix(h, x_mix):
    rows, d = h.shape
    n_mix = x_mix.shape[0]
    sub = ROW_TILE // 8
    tile = pl.BlockSpec((ROW_TILE, d), lambda i: (i, 0))
    return pl.pallas_call(
        _shift_mix_kernel,
        grid=(rows // ROW_TILE,),
        in_specs=[tile,
                  pl.BlockSpec((8, d), lambda i: (jnp.maximum(i * sub - 1, 0), 0)),
                  pl.BlockSpec((8, d), lambda i: (jnp.minimum((i + 1) * sub, rows // 8 - 1), 0)),
                  pl.BlockSpec((n_mix, d), lambda i: (0, 0))],
        out_specs=[tile] * n_mix,
        out_shape=[jax.ShapeDtypeStruct((rows, d), BF16)] * n_mix,
        compiler_params=_params(("parallel",)),
        name="shift_mix",
    )(h, h, h, x_mix)


def _rwkv_out_kernel(yf_ref, yr_ref, bonus_ref, g_ref, gn_ref, o_ref):
    w = SCAN_PAIR_W
    for s in range(o_ref.shape[-1] // w):
        cols = slice(s * w, (s + 1) * w)
        y = yf_ref[:, cols] + yr_ref[:, cols]
        mu = _head_sum(y) * (1.0 / RWKV_HEAD)
        dev = y - mu
        var = _head_sum(dev * dev) * (1.0 / RWKV_HEAD)
        yn = dev * lax.rsqrt(var + GN_EPS) * gn_ref[0:1, cols] + gn_ref[1:2, cols]
        o_ref[:, cols] = ((yn + bonus_ref[:, cols]) * g_ref[:, cols]).astype(o_ref.dtype)


def rwkv_out(y_fwd, y_rev, bonus, g, gn_g, gn_b, tn=512):
    rows, d = y_fwd.shape
    tile = pl.BlockSpec((ROW_TILE, tn), lambda i, j: (i, j))
    return pl.pallas_call(
        _rwkv_out_kernel,
        grid=(rows // ROW_TILE, d // tn),
        in_specs=[tile, tile, tile, tile, pl.BlockSpec((2, tn), lambda i, j: (0, j))],
        out_specs=tile,
        out_shape=jax.ShapeDtypeStruct((rows, d), BF16),
        compiler_params=_params(("parallel", "parallel")),
        name="rwkv_out",
    )(y_fwd, y_rev, bonus, g, jnp.stack([gn_g, gn_b]))


def _route_topk(logits):
    lane = lax.broadcasted_iota(jnp.int32, logits.shape, 1)
    far = 4 * HEAD_W

    def first_max(vals):
        top = jnp.max(vals, axis=-1, keepdims=True)
        return top, jnp.min(jnp.where(vals == top, lane, far), axis=-1, keepdims=True)

    is_group = lane < N_GROUPS
    g_top, g_sel = first_max(jnp.where(is_group, logits, NEG_INF))
    p_sel = 1.0 / jnp.sum(jnp.where(is_group, jnp.exp(logits - g_top), 0.0), axis=-1, keepdims=True)
    lo = N_GROUPS + g_sel * EXPERTS_PER_GROUP
    le = jnp.where((lane >= lo) & (lane < lo + EXPERTS_PER_GROUP), logits, NEG_INF)
    v1, i1 = first_max(le)
    v2, i2 = first_max(jnp.where(lane == i1, NEG_INF, le))
    e2 = jnp.exp(v2 - v1)
    w1 = p_sel / (1.0 + e2)
    w2 = p_sel * e2 / (1.0 + e2)
    out = jnp.where(lane == 0, (i1 - N_GROUPS).astype(F32), 0.0)
    out = jnp.where(lane == 1, (i2 - N_GROUPS).astype(F32), out)
    out = jnp.where(lane == 2, w1, out)
    return jnp.where(lane == 3, w2, out)


def moe_dispatch(route):
    t = route.shape[0]
    n_pairs = TOP_K * t
    tm = MOE_TILE
    n_slots = n_pairs + N_EXPERTS * tm
    n_tiles = n_slots // tm
    e_flat = route[:, :TOP_K].astype(jnp.int32).reshape(-1)
    gate_bits = lax.bitcast_convert_type(route[:, TOP_K:2 * TOP_K], jnp.int32).reshape(-1)
    onehot = (e_flat[:, None] == jnp.arange(N_EXPERTS, dtype=jnp.int32)[None, :]).astype(jnp.int32)
    csum = jnp.cumsum(onehot, axis=0)
    counts = csum[-1]
    padded = ((counts + tm - 1) // tm) * tm
    pend = jnp.cumsum(padded)
    pstart = pend - padded
    pos = jnp.sum(onehot * (pstart[None, :] + csum - onehot), axis=1)
    pair = jnp.arange(n_pairs, dtype=jnp.int32)
    slots = jnp.full((n_slots, 2), -1, jnp.int32).at[pos].set(jnp.stack([pair, gate_bits], axis=1))
    valid = slots[:, 0] >= 0
    tok = jnp.where(valid, slots[:, 0] // TOP_K, 0)
    dst = jnp.where(valid, (slots[:, 0] % TOP_K) * t + slots[:, 0] // TOP_K,
                    n_pairs + jnp.arange(n_slots, dtype=jnp.int32) % tm)
    gate = jnp.where(valid, lax.bitcast_convert_type(slots[:, 1], F32), 0.0)
    tile_start = jnp.arange(n_tiles, dtype=jnp.int32) * tm
    tile_e = jnp.sum((tile_start[:, None] >= pend[None, :]).astype(jnp.int32), axis=1)
    tile_e = jnp.minimum(tile_e, N_EXPERTS - 1)
    n_live = (pend[-1] // tm).reshape(1)
    tok = tok.reshape(n_tiles, tm)
    dst = dst.reshape(n_tiles, tm)
    idx = jnp.stack([tok, jnp.roll(tok, -1, axis=0), dst, jnp.roll(dst, 1, axis=0)], axis=1)
    gate = jnp.broadcast_to(gate[:, None], (n_slots, HEAD_W))
    return idx, gate, tile_e.astype(jnp.int32), n_live.astype(jnp.int32)


def _moe_kernel(te_ref, nl_ref, idx_ref, gate_ref, h_hbm, w1_ref, w3_ref, w2_ref,
                y_hbm, xbuf, obuf, w1b, w3b, w2b, sem_in, sem_out):
    i = pl.program_id(0)
    n_live = nl_ref[0]
    tm = MOE_TILE
    slot = i % 2
    other = 1 - slot

    def gather_copy(src_row, r, s):
        return pltpu.make_async_copy(h_hbm.at[pl.ds(src_row, 1), :], xbuf.at[s, pl.ds(r, 1), :], sem_in.at[s])

    def scatter_copy(r, dst_row, s):
        return pltpu.make_async_copy(obuf.at[s, pl.ds(r, 1), :], y_hbm.at[pl.ds(dst_row, 1), :], sem_out.at[s])

    def wait_gather(s):
        pltpu.make_async_copy(h_hbm.at[pl.ds(0, tm), :], xbuf.at[s], sem_in.at[s]).wait()

    def wait_scatter(s):
        pltpu.make_async_copy(obuf.at[s], y_hbm.at[pl.ds(0, tm), :], sem_out.at[s]).wait()

    def per_row(fn):
        def body(r, carry):
            fn(r)
            return carry
        lax.fori_loop(0, tm, body, 0, unroll=MOE_DMA_UNROLL)

    def expert_tile(scatter_previous):
        packed = xbuf[slot]
        half = packed.shape[-1]
        x_lo = lax.bitcast_convert_type(packed << 16, F32).astype(BF16)
        x_hi = lax.bitcast_convert_type(packed & jnp.uint32(0xFFFF0000), F32).astype(BF16)
        for r in range(tm):
            gather_copy(idx_ref[1, r], r, other).start(priority=r % 2)
        if scatter_previous:
            for r in range(tm):
                scatter_copy(r, idx_ref[3, r], other).start(priority=r % 2)
        up = _dot(x_lo, w1b[0:half, :]) + _dot(x_hi, w1b[half:, :])
        gate_in = _dot(x_lo, w3b[0:half, :]) + _dot(x_hi, w3b[half:, :])
        hid = (up * jax.nn.sigmoid(up)) * gate_in * gate_ref[:, 0:1]
        return _dot(hid.astype(BF16), w2b[...])

    @pl.when(i == 0)
    def _():
        per_row(lambda r: gather_copy(idx_ref[0, r], r, 0).start())

    @pl.when(i < n_live)
    def _():
        wait_gather(slot)
        prev_e = te_ref[jnp.maximum(i - 1, 0)]

        @pl.when((i == 0) | (te_ref[i] != prev_e))
        def _():
            w1b[...] = w1_ref[...].astype(BF16)
            w3b[...] = w3_ref[...].astype(BF16)
            w2b[...] = w2_ref[...].astype(BF16)

        @pl.when(i == 0)
        def _():
            obuf[0] = expert_tile(False)
            first_spare = y_hbm.shape[0] - tm
            per_row(lambda r: scatter_copy(r, first_spare + r, 0).start())
            wait_scatter(0)

        @pl.when(i > 0)
        def _():
            out = expert_tile(True)

            @pl.when(i > 1)
            def _():
                wait_scatter(slot)

            obuf[slot] = out

        @pl.when(i == n_live - 1)
        def _():
            per_row(lambda r: scatter_copy(r, idx_ref[2, r], slot).start())
            wait_gather(other)

            @pl.when(i > 0)
            def _():
                wait_scatter(other)

            wait_scatter(slot)


def moe_experts(h, idx, gate, tile_e, n_live, w1, w3, w2, layer):
    t = h.shape[0]
    d = w1.shape[-2]
    f = w1.shape[-1]
    tm = MOE_TILE
    n_tiles = idx.shape[0]

    def w_index(i, te, nl):
        return (layer, te[i] // EXPERTS_PER_GROUP, te[i] % EXPERTS_PER_GROUP, 0, 0)

    grid_spec = pltpu.PrefetchScalarGridSpec(
        num_scalar_prefetch=2,
        grid=(n_tiles,),
        in_specs=[pl.BlockSpec((None, 4, tm), lambda i, te, nl: (i, 0, 0), memory_space=pltpu.SMEM),
                  pl.BlockSpec((tm, HEAD_W), lambda i, te, nl: (i, 0)),
                  pl.BlockSpec(memory_space=pl.ANY),
                  pl.BlockSpec((None, None, None, d, f), w_index),
                  pl.BlockSpec((None, None, None, d, f), w_index),
                  pl.BlockSpec((None, None, None, f, d), w_index)],
        out_specs=pl.BlockSpec(memory_space=pl.ANY),
        scratch_shapes=[pltpu.VMEM((2, tm, d // 2), jnp.uint32), pltpu.VMEM((2, tm, d), F32),
                        pltpu.VMEM((d, f), BF16), pltpu.VMEM((d, f), BF16), pltpu.VMEM((f, d), BF16),
                        pltpu.SemaphoreType.DMA((2,)), pltpu.SemaphoreType.DMA((2,))])
    return pl.pallas_call(
        _moe_kernel,
        grid_spec=grid_spec,
        out_shape=jax.ShapeDtypeStruct((TOP_K * t + tm, d), F32),
        compiler_params=_params(("arbitrary",)),
        name="moe_experts",
    )(tile_e, n_live, idx, gate, h, w1, w3, w2)


def _mod_spec(k):
    per_b = S_ALL // ROW_TILE
    return pl.BlockSpec((None, 1, D_MODEL),
                        lambda i: ((i // per_b * 2 + jnp.minimum(i % per_b, 1)) * N_MOD + k, 0, 0))


def _norm_mod(x, g_ref, sh_ref, sc_ref):
    h = x * lax.rsqrt(jnp.mean(x * x, axis=-1, keepdims=True) + NORM_EPS) * g_ref[...]
    if sh_ref is None:
        return h
    return h * (1.0 + sc_ref[...]) + sh_ref[...]


def _first_norm_kernel(x_ref, g_ref, sh_ref, sc_ref, h_ref):
    h_ref[...] = _norm_mod(x_ref[...], g_ref, sh_ref, sc_ref).astype(h_ref.dtype)


def first_norm(xs, g, mod, h_dtype):
    rows, d = xs.shape
    tile = pl.BlockSpec((ROW_TILE, d), lambda i: (i, 0))
    return pl.pallas_call(
        _first_norm_kernel,
        grid=(rows // ROW_TILE,),
        in_specs=[tile, pl.BlockSpec((1, d), lambda i: (0, 0)), _mod_spec(0), _mod_spec(1)],
        out_specs=tile,
        out_shape=jax.ShapeDtypeStruct((rows, d), h_dtype),
        compiler_params=_params(("parallel",)),
        name="first_norm",
    )(xs, g.reshape(1, d), mod, mod)


def _post_mixer_kernel(x_ref, u_ref, gate_ref, g_ref, sh_ref, sc_ref, wr_ref, br_ref, xo_ref, h_ref, route_ref):
    x = x_ref[...] + gate_ref[...] * u_ref[...]
    xo_ref[...] = x
    h = _norm_mod(x, g_ref, sh_ref, sc_ref)
    half = h.shape[-1] // 2
    bits = lax.bitcast_convert_type(h.astype(BF16).astype(F32), jnp.uint32)
    h_ref[...] = (bits[:, half:] & jnp.uint32(0xFFFF0000)) | (bits[:, :half] >> 16)
    logits = jnp.dot(h, wr_ref[...], precision=lax.Precision.HIGHEST, preferred_element_type=F32) + br_ref[...]
    route_ref[...] = _route_topk(logits)


def post_mixer(xs, upd, g, mod, router_w, router_b):
    rows, d = xs.shape
    tile = pl.BlockSpec((ROW_TILE, d), lambda i: (i, 0))
    ptile = pl.BlockSpec((ROW_TILE, d // 2), lambda i: (i, 0))
    rtile = pl.BlockSpec((ROW_TILE, HEAD_W), lambda i: (i, 0))
    return pl.pallas_call(
        _post_mixer_kernel,
        grid=(rows // ROW_TILE,),
        in_specs=[tile, tile, _mod_spec(2), pl.BlockSpec((1, d), lambda i: (0, 0)), _mod_spec(3), _mod_spec(4),
                  pl.BlockSpec((d, HEAD_W), lambda i: (0, 0)), pl.BlockSpec((1, HEAD_W), lambda i: (0, 0))],
        out_specs=[tile, ptile, rtile],
        out_shape=[jax.ShapeDtypeStruct((rows, d), F32), jax.ShapeDtypeStruct((rows, d // 2), jnp.uint32),
                   jax.ShapeDtypeStruct((rows, HEAD_W), F32)],
        compiler_params=_params(("parallel",)),
        name="post_mixer",
    )(xs, upd, mod, g.reshape(1, d), mod, mod, router_w, router_b)


def _post_moe_kernel(x_ref, ya_ref, yb_ref, gate_ref, g_ref, *rest, modulate):
    if modulate:
        sh_ref, sc_ref, xo_ref, h_ref = rest
    else:
        sh_ref = sc_ref = None
        xo_ref, h_ref = rest
    x = x_ref[...] + gate_ref[...] * (ya_ref[...] + yb_ref[...])
    xo_ref[...] = x
    h_ref[...] = _norm_mod(x, g_ref, sh_ref, sc_ref).astype(h_ref.dtype)


def post_moe(xs, y2, g, mod, next_mod, h_dtype):
    rows, d = xs.shape
    tile = pl.BlockSpec((ROW_TILE, d), lambda i: (i, 0))
    second = pl.BlockSpec((ROW_TILE, d), lambda i: (rows // ROW_TILE + i, 0))
    gspec = pl.BlockSpec((1, d), lambda i: (0, 0))
    modulate = next_mod is not None
    in_specs = [tile, tile, second, _mod_spec(5), gspec] + ([_mod_spec(0), _mod_spec(1)] if modulate else [])
    args = (xs, y2, y2, mod, g.reshape(1, d)) + ((next_mod, next_mod) if modulate else ())
    return pl.pallas_call(
        functools.partial(_post_moe_kernel, modulate=modulate),
        grid=(rows // ROW_TILE,),
        in_specs=in_specs,
        out_specs=[tile, tile],
        out_shape=[jax.ShapeDtypeStruct((rows, d), F32), jax.ShapeDtypeStruct((rows, d), h_dtype)],
        compiler_params=_params(("parallel",)),
        name="post_moe",
    )(*args)


def _pad_cols(w, n):
    return jnp.pad(w, ((0, 0), (0, n - w.shape[1])))


def _two_dir_lora(x, w1, w2):
    r = w1.shape[-1]
    w1c = _pad_cols(jnp.concatenate([w1[0], w1[1]], axis=1), 256).astype(BF16)
    d = w2.shape[-1]
    w2bd = jnp.zeros((256, 2 * d), F32).at[:r, :d].set(w2[0]).at[r:2 * r, d:].set(w2[1]).astype(BF16)
    return matmul(x, w1c, F32, tn=256), w2bd


def rwkv_mixer(h, p, batch):
    mixes = shift_mix(h, p['x_mix'])
    r = matmul(mixes[0], p['w_rkv'][0].astype(BF16), F32)
    k = matmul(mixes[1], p['w_rkv'][1].astype(BF16), F32)
    v = matmul(mixes[2], p['w_rkv'][2].astype(BF16), F32)
    hid_w, w2bd_w = _two_dir_lora(mixes[3], p['dec_w1'], p['dec_w2'])
    hid_a, w2bd_a = _two_dir_lora(mixes[4], p['iclr_a1'], p['iclr_a2'])
    g_hid = matmul(mixes[5], p['gate_g1'].astype(BF16), F32, tn=256)
    g = matmul(jax.nn.sigmoid(g_hid).astype(BF16), p['gate_g2'].astype(BF16), F32)

    par = jnp.concatenate([p['dec_w0'], p['iclr_a0'], p['k_k'][None], p['k_a'][None], p['r_k'].reshape(1, -1),
                           jnp.zeros((1, D_MODEL), F32)], axis=0)
    y0, rm, d0, bonus = scan_prepare(r, k, v, jnp.tanh(hid_w).astype(BF16), hid_a.astype(BF16), w2bd_w, w2bd_a, par)
    y_fwd, y_rev = scan_apply(y0, rm, d0, batch)
    yo = rwkv_out(y_fwd, y_rev, bonus, g, p['gn_g'], p['gn_b'])
    return matmul(yo, p['w_out'].astype(BF16), F32)


def even_mixer(h, p, layer_idx, cos_t, sin_t, batch):
    rows = batch * S_ALL
    lam_init = 0.8 - 0.6 * math.exp(-0.3 * layer_idx)
    proj = matmul(h, p['w_in'].astype(BF16), BF16, tn=1024)
    proj = proj.reshape(batch, S_ALL, -1)
    lf = p['diff_lambda']
    lam = (jnp.exp(jnp.sum(lf[0] * lf[1])) - jnp.exp(jnp.sum(lf[2] * lf[3])) + lam_init).reshape(1)
    a_out = diff_attention(proj, lam, cos_t, sin_t, p['subln_g'], lam_init)
    b_out = neighbourhood_attention(proj, na_bias_table(p['rpb']))
    mixed = jnp.concatenate([a_out, b_out], axis=-1).reshape(rows, -1)
    return matmul(mixed, p['w_out'].astype(BF16), F32)


def router_table(router_g, router_g_b, router_e, router_e_b):
    w = _pad_cols(jnp.concatenate([router_g, router_e], axis=1), HEAD_W)
    b = _pad_cols(jnp.concatenate([router_g_b, router_e_b])[None, :], HEAD_W)
    return w, b


def kernel(x, c, ctx, c_ctx, ada_w, ada_b, norm_g, final_g, even_w_in, even_w_out, diff_lambda, diff_subln_g, na_rpb, rwkv_x_mix, rwkv_w_rkv, rwkv_w_out, rwkv_dec_w0, rwkv_dec_w1, rwkv_dec_w2, rwkv_iclr_a0, rwkv_iclr_a1, rwkv_iclr_a2, rwkv_gate_g1, rwkv_gate_g2, rwkv_k_k, rwkv_k_a, rwkv_r_k, rwkv_gn_g, rwkv_gn_b, moe_router_g, moe_router_g_b, moe_router_e, moe_router_e_b, moe_w1, moe_w3, moe_w2):
    batch = x.shape[0]
    rows = batch * S_ALL
    xs = jnp.concatenate([ctx, x], axis=1).reshape(rows, D_MODEL)
    cos_t, sin_t = rope_tables()

    cvec = jnp.concatenate([c, c_ctx[None], jnp.zeros((8 - batch - 1, D_MODEL), F32)], axis=0)
    mods = ada_modulation(jax.nn.silu(cvec).astype(BF16), ada_w, ada_b)
    mods = mods.reshape(DEPTH, 8, N_MOD, D_MODEL)

    def mod_table(i):
        mod_l = mods[i, :batch]
        mod_c = jnp.broadcast_to(mods[i, batch][None], mod_l.shape)
        return jnp.stack([mod_c, mod_l], axis=1).reshape(batch * 2 * N_MOD, 1, D_MODEL)

    mod = mod_table(0)
    h = first_norm(xs, norm_g[0, 0], mod, BF16)
    for i in range(DEPTH):
        j = i // 2
        if i % 2 == 0:
            p = dict(w_in=even_w_in[j], w_out=even_w_out[j], diff_lambda=diff_lambda[j],
                     subln_g=diff_subln_g[j], rpb=na_rpb[j])
            out = even_mixer(h, p, i, cos_t, sin_t, batch)
        else:
            p = dict(x_mix=rwkv_x_mix[j], w_rkv=rwkv_w_rkv[j], w_out=rwkv_w_out[j], dec_w0=rwkv_dec_w0[j],
                     dec_w1=rwkv_dec_w1[j], dec_w2=rwkv_dec_w2[j], iclr_a0=rwkv_iclr_a0[j],
                     iclr_a1=rwkv_iclr_a1[j], iclr_a2=rwkv_iclr_a2[j], gate_g1=rwkv_gate_g1[j],
                     gate_g2=rwkv_gate_g2[j], k_k=rwkv_k_k[j], k_a=rwkv_k_a[j], r_k=rwkv_r_k[j],
                     gn_g=rwkv_gn_g[j], gn_b=rwkv_gn_b[j])
            out = rwkv_mixer(h, p, batch)
        rw, rb = router_table(moe_router_g[i], moe_router_g_b[i], moe_router_e[i], moe_router_e_b[i])
        xs, h2, route = post_mixer(xs, out, norm_g[i, 1], mod, rw, rb)
        idx, gate, tile_e, n_live = moe_dispatch(route)
        y2 = moe_experts(h2, idx, gate, tile_e, n_live, moe_w1, moe_w3, moe_w2, i)
        if i + 1 < DEPTH:
            next_mod = mod_table(i + 1)
            xs, h = post_moe(xs, y2, norm_g[i + 1, 0], mod, next_mod, F32 if (i + 1) % 2 else BF16)
            mod = next_mod
        else:
            _, h = post_moe(xs, y2, final_g, mod, None, F32)
    return h.reshape(batch, S_ALL, D_MODEL)[:, CTX_LEN:]
```

```python
import functools
import math

import numpy as np
import jax
import jax.numpy as jnp
from jax import lax
from jax.experimental import pallas as pl
from jax.experimental.pallas import tpu as pltpu

F32 = jnp.float32
BF16 = jnp.bfloat16

D_MODEL = 2048
DEPTH = 4
GRID_W = 64
CTX_LEN = 256
SEQ = 4096
S_ALL = CTX_LEN + SEQ
N_MOD = 6
NORM_EPS = 1e-6
NEG_INF = -1e30

DIFF_HEADS = 8
DIFF_QK_DIM = 64
NA_HEADS = 8
NA_DIM = 128
NA_KH = 8
NA_KW = 16
ROPE_THETA = 10000.0
SUBLN_EPS = 1e-5
HEAD_W = 128
ATT_TILE = 256
DIFF_KV_BLOCK = 2048
NA_ROWS_PER_STEP = 4
NA_WIN_ROWS = 12

RWKV_HEAD = 64
RWKV_HEADS = D_MODEL // RWKV_HEAD
GN_EPS = 64e-5
SCAN_CHUNK = 64
SCAN_PAIR_W = 2 * RWKV_HEAD
SCAN_SUB = 16

N_GROUPS = 4
EXPERTS_PER_GROUP = 8
N_EXPERTS = N_GROUPS * EXPERTS_PER_GROUP
TOP_K = 2
EXPERT_FF = 512
MOE_TILE = 256
MOE_DMA_UNROLL = 8
ROW_TILE = 256

VMEM_LIMIT = 52 * 1024 * 1024


def _params(sem):
    return pltpu.CompilerParams(dimension_semantics=sem, vmem_limit_bytes=VMEM_LIMIT)


def _dot(a, b):
    return jnp.dot(a, b, preferred_element_type=F32)


def _dot_nt(a, b):
    return lax.dot_general(a, b, (((1,), (1,)), ((), ())), preferred_element_type=F32)


def _dot_tn(a, b):
    return lax.dot_general(a, b, (((0,), (0,)), ((), ())), preferred_element_type=F32)


def _mm_kernel(a_ref, w_ref, o_ref):
    o_ref[...] = _dot(a_ref[...], w_ref[...]).astype(o_ref.dtype)


def matmul(a, w, out_dtype, tm=1024, tn=1024):
    m, k = a.shape
    n = w.shape[1]
    while m % tm:
        tm //= 2
    tn = min(tn, n)
    assert n % tn == 0
    return pl.pallas_call(
        _mm_kernel,
        grid=(m // tm, n // tn),
        in_specs=[pl.BlockSpec((tm, k), lambda i, j: (i, 0)),
                  pl.BlockSpec((k, tn), lambda i, j: (0, j))],
        out_specs=pl.BlockSpec((tm, tn), lambda i, j: (i, j)),
        out_shape=jax.ShapeDtypeStruct((m, n), out_dtype),
        compiler_params=_params(("parallel", "parallel")),
        name="matmul",
    )(a, w)


def _ada_kernel(s_ref, w_ref, b_ref, o_ref):
    o_ref[...] = _dot(s_ref[...], w_ref[...].astype(BF16)) + b_ref[...]


def ada_modulation(svec, ada_w, ada_b, tn=1024):
    nl, d, n = ada_w.shape
    rows = svec.shape[0]
    return pl.pallas_call(
        _ada_kernel,
        grid=(nl, n // tn),
        in_specs=[pl.BlockSpec((rows, d), lambda l, j: (0, 0)),
                  pl.BlockSpec((None, d, tn), lambda l, j: (l, 0, j)),
                  pl.BlockSpec((None, 1, tn), lambda l, j: (l, 0, j))],
        out_specs=pl.BlockSpec((None, rows, tn), lambda l, j: (l, 0, j)),
        out_shape=jax.ShapeDtypeStruct((nl, rows, n), F32),
        compiler_params=_params(("parallel", "parallel")),
        name="ada_modulation",
    )(svec, ada_w, ada_b.reshape(nl, 1, n))


def _rope(x, cos, sin_signed):
    lane = lax.broadcasted_iota(jnp.int32, x.shape, 1)
    first_half = (lane & 63) < 32
    partner = jnp.where(first_half, pltpu.roll(x, HEAD_W - 32, 1), pltpu.roll(x, 32, 1))
    return x * cos + partner * sin_signed


def _diff_attn_kernel(lam_ref, q_ref, k_ref, v_ref, cos_ref, sin_ref, g_ref, o_ref, kr_ref, vt_ref, sa_ref, sb_ref, *,
                      post_scale):
    j = pl.program_id(2)
    tq = ATT_TILE

    @pl.when(j == 0)
    def _():
        def prep_chunk(c, carry):
            rows = pl.ds(pl.multiple_of(c * ATT_TILE, ATT_TILE), ATT_TILE)
            kr_ref[rows, :] = _rope(k_ref[rows, :].astype(F32), cos_ref[rows, :], sin_ref[rows, :]).astype(BF16)
            vt_ref[:, rows] = v_ref[rows, :].astype(F32).T.astype(BF16)
            return carry
        lax.fori_loop(0, S_ALL // ATT_TILE, prep_chunk, 0)

    qrows = pl.ds(pl.multiple_of(j * tq, tq), tq)
    q = _rope(q_ref[...].astype(F32), cos_ref[qrows, :], sin_ref[qrows, :]) * (DIFF_QK_DIM ** -0.5)
    lane = lax.broadcasted_iota(jnp.int32, q.shape, 1)
    q1 = jnp.where(lane < DIFF_QK_DIM, q, 0.0).astype(BF16)
    q2 = jnp.where(lane < DIFF_QK_DIM, 0.0, q).astype(BF16)

    kb = DIFF_KV_BLOCK
    n_latent = SEQ // kb
    maps = (q1, q2)

    def score_stage(start, size, sbuf):
        kc = kr_ref[pl.ds(start, size), :]
        tops = []
        for i, qm in enumerate(maps):
            s = _dot_nt(kc, qm)
            sbuf[i, 0:size, :] = s
            tops.append(jnp.max(s, axis=0, keepdims=True))
        return tuple(tops)

    def softmax_stage(start, size, sbuf, tops, state):
        vt = vt_ref[:, pl.ds(start, size)]
        new = []
        for i in range(2):
            m, l, acc = state[3 * i:3 * i + 3]
            m_new = jnp.maximum(m, tops[i])
            alpha = jnp.exp(m - m_new)
            p = jnp.exp(sbuf[i, 0:size, :] - m_new)
            new += [m_new, alpha * l + jnp.sum(p, axis=0, keepdims=True),
                    alpha * acc + _dot(vt, p.astype(BF16))]
        return tuple(new)

    def latent_start(n):
        return pl.multiple_of(CTX_LEN + jnp.minimum(n, n_latent - 1) * kb, CTX_LEN)

    row = jnp.full((1, tq), NEG_INF, F32)
    zrow = jnp.zeros((1, tq), F32)
    zacc = jnp.zeros((HEAD_W, tq), F32)
    state = softmax_stage(0, CTX_LEN, sa_ref, score_stage(0, CTX_LEN, sa_ref), (row, zrow, zacc, row, zrow, zacc))
    tops = score_stage(latent_start(0), kb, sb_ref)

    def block_pair(n, carry, more_follow):
        st, tp = carry[:6], carry[6:]
        tp_next = score_stage(latent_start(n + 1), kb, sa_ref)
        st = softmax_stage(latent_start(n), kb, sb_ref, tp, st)
        tp_after = score_stage(latent_start(n + 2), kb, sb_ref) if more_follow else tp_next
        st = softmax_stage(latent_start(n + 1), kb, sa_ref, tp_next, st)
        return st + tp_after

    n_pairs = n_latent // 2
    carry = lax.fori_loop(0, jnp.where(j == 0, 0, n_pairs - 1),
                          lambda t, c: block_pair(2 * t, c, True), state + tops)
    carry = lax.fori_loop(0, jnp.where(j == 0, 0, 1),
                          lambda t, c: block_pair(2 * (n_pairs - 1), c, False), carry)
    m1, l1, a1, m2, l2, a2 = carry[:6]
    out = a1 / l1 - lam_ref[0] * (a2 / l2)
    ms = jnp.mean(out * out, axis=0, keepdims=True)
    y = out * lax.rsqrt(ms + SUBLN_EPS) * (g_ref[...] * post_scale)
    o_ref[...] = y.T.astype(o_ref.dtype)


def diff_attention(proj, lam, cos_t, sin_t, subln_g, lam_init):
    b = proj.shape[0]
    kernel = functools.partial(_diff_attn_kernel, post_scale=1.0 - lam_init)
    return pl.pallas_call(
        kernel,
        grid=(b, DIFF_HEADS, S_ALL // ATT_TILE),
        in_specs=[pl.BlockSpec(memory_space=pltpu.SMEM),
                  pl.BlockSpec((None, ATT_TILE, HEAD_W), lambda bi, h, j: (bi, j, h)),
                  pl.BlockSpec((None, S_ALL, HEAD_W), lambda bi, h, j: (bi, 0, DIFF_HEADS + h)),
                  pl.BlockSpec((None, S_ALL, HEAD_W), lambda bi, h, j: (bi, 0, 2 * DIFF_HEADS + h)),
                  pl.BlockSpec((S_ALL, HEAD_W), lambda bi, h, j: (0, 0)),
                  pl.BlockSpec((S_ALL, HEAD_W), lambda bi, h, j: (0, 0)),
                  pl.BlockSpec((HEAD_W, 1), lambda bi, h, j: (0, 0))],
        out_specs=pl.BlockSpec((None, ATT_TILE, HEAD_W), lambda bi, h, j: (bi, j, h)),
        out_shape=jax.ShapeDtypeStruct((b, S_ALL, DIFF_HEADS * HEAD_W), BF16),
        scratch_shapes=[pltpu.VMEM((S_ALL, HEAD_W), BF16), pltpu.VMEM((HEAD_W, S_ALL), BF16),
                        pltpu.VMEM((2, DIFF_KV_BLOCK, ATT_TILE), F32), pltpu.VMEM((2, DIFF_KV_BLOCK, ATT_TILE), F32)],
        compiler_params=_params(("parallel", "parallel", "arbitrary")),
        name="diff_attention",
    )(lam, proj, proj, proj, cos_t, sin_t, subln_g.reshape(HEAD_W, 1))


def rope_tables():
    n_freq = DIFF_QK_DIM // 4
    inv_freq = ROPE_THETA ** (-jnp.arange(n_freq, dtype=F32) / n_freq)
    t = jnp.arange(SEQ, dtype=jnp.int32)
    row = (t // GRID_W).astype(F32)
    col = (t % GRID_W).astype(F32)
    ang = jnp.concatenate([row[:, None] * inv_freq, col[:, None] * inv_freq], axis=-1)
    cos, sin = jnp.cos(ang), jnp.sin(ang)
    cos_l = jnp.concatenate([cos, cos, cos, cos], axis=-1)
    sin_l = jnp.concatenate([-sin, sin, -sin, sin], axis=-1)
    cos_all = jnp.concatenate([jnp.ones((CTX_LEN, HEAD_W), F32), cos_l], axis=0)
    sin_all = jnp.concatenate([jnp.zeros((CTX_LEN, HEAD_W), F32), sin_l], axis=0)
    return cos_all, sin_all


def _na_window_start(j):
    g = j - 1
    return jnp.clip(NA_ROWS_PER_STEP * g - NA_KH // 2, 0, SEQ // GRID_W - NA_WIN_ROWS)


def _na_kernel(q_ref, k_ref, v_ref, bias_ref, o_ref):
    j = pl.program_id(2)
    win = NA_WIN_ROWS * GRID_W
    start = pl.multiple_of(CTX_LEN + _na_window_start(j) * GRID_W, GRID_W)
    scale = NA_DIM ** -0.5
    q = q_ref[...]
    s_c = _dot_nt(q, k_ref[pl.ds(0, CTX_LEN), :]) * scale
    s_w = _dot_nt(q, k_ref[pl.ds(start, win), :]) * scale + bias_ref[...]
    m = jnp.maximum(jnp.max(s_c, axis=-1, keepdims=True), jnp.max(s_w, axis=-1, keepdims=True))
    p_c = jnp.exp(s_c - m)
    p_w = jnp.exp(s_w - m)
    l = jnp.sum(p_c, axis=-1, keepdims=True) + jnp.sum(p_w, axis=-1, keepdims=True)
    o = _dot(p_c.astype(BF16), v_ref[pl.ds(0, CTX_LEN), :]) + _dot(p_w.astype(BF16), v_ref[pl.ds(start, win), :])
    o_ref[...] = (o / l).astype(o_ref.dtype)


def _na_bias_pattern(j):
    n_groups = SEQ // (GRID_W * NA_ROWS_PER_STEP)
    g = j - 1
    return jnp.where(j == 0, 3, jnp.where(g == 0, 0, jnp.where(g == n_groups - 1, 2, 1)))


def na_bias_table(rpb):
    rows = SEQ // GRID_W
    n_groups = rows // NA_ROWS_PER_STEP
    cols = np.arange(GRID_W)
    col_start = np.clip(cols - NA_KW // 2, 0, GRID_W - NA_KW)
    col_mask = (cols[None, :] >= col_start[:, None]) & (cols[None, :] < col_start[:, None] + NA_KW)
    c_idx = np.clip(cols[None, :] - cols[:, None] + NA_KW - 1, 0, 2 * NA_KW - 2)
    pats = []
    for g in (0, 1, n_groups - 1):
        u0 = int(np.clip(NA_ROWS_PER_STEP * g - NA_KH // 2, 0, rows - NA_WIN_ROWS))
        r = NA_ROWS_PER_STEP * g + np.arange(NA_ROWS_PER_STEP)
        r0 = np.clip(r - NA_KH // 2, 0, rows - NA_KH)
        kr = u0 + np.arange(NA_WIN_ROWS)
        valid_r = (kr[None, :] >= r0[:, None]) & (kr[None, :] < r0[:, None] + NA_KH)
        r_idx = np.clip(kr[None, :] - r[:, None] + NA_KH - 1, 0, 2 * NA_KH - 2)
        valid = valid_r[:, None, :, None] & col_mask[None, :, None, :]
        r_sel = np.eye(2 * NA_KH - 1, dtype=np.float32)[r_idx]
        c_sel = np.eye(2 * NA_KW - 1, dtype=np.float32)[c_idx]
        rows_sel = jnp.einsum('qkr,hrc->hqkc', r_sel, rpb.astype(F32), precision=lax.Precision.HIGHEST)
        gathered = jnp.einsum('hqkc,abc->hqakb', rows_sel, c_sel, precision=lax.Precision.HIGHEST)
        pats.append(jnp.where(valid[None], gathered, NEG_INF))
    pats.append(jnp.full_like(pats[0], NEG_INF))
    tab = jnp.stack(pats, axis=1)
    return tab.reshape(NA_HEADS, 4, NA_ROWS_PER_STEP * GRID_W, NA_WIN_ROWS * GRID_W)


def neighbourhood_attention(proj, bias_tab):
    b = proj.shape[0]
    tq = NA_ROWS_PER_STEP * GRID_W
    assert tq == CTX_LEN
    win = NA_WIN_ROWS * GRID_W
    base = 3 * DIFF_HEADS
    return pl.pallas_call(
        _na_kernel,
        grid=(b, NA_HEADS, S_ALL // tq),
        in_specs=[pl.BlockSpec((None, tq, HEAD_W), lambda bi, h, j: (bi, j, base + h)),
                  pl.BlockSpec((None, S_ALL, HEAD_W), lambda bi, h, j: (bi, 0, base + NA_HEADS + h)),
                  pl.BlockSpec((None, S_ALL, HEAD_W), lambda bi, h, j: (bi, 0, base + 2 * NA_HEADS + h)),
                  pl.BlockSpec((None, None, tq, win), lambda bi, h, j: (h, _na_bias_pattern(j), 0, 0))],
        out_specs=pl.BlockSpec((None, tq, HEAD_W), lambda bi, h, j: (bi, j, h)),
        out_shape=jax.ShapeDtypeStruct((b, S_ALL, NA_HEADS * HEAD_W), BF16),
        compiler_params=_params(("parallel", "parallel", "arbitrary")),
        name="neighbourhood_attention",
    )(proj, proj, proj, bias_tab)


def _each(fn, *lists):
    return [fn(*args) for args in zip(*lists)]


def _head_sum(x):
    rows, w = x.shape
    r_i = lax.broadcasted_iota(jnp.int32, (w, w), 0)
    c_i = lax.broadcasted_iota(jnp.int32, (w, w), 1)
    ones_bd = jnp.where((r_i < RWKV_HEAD) == (c_i < RWKV_HEAD), 1.0, 0.0).astype(BF16)
    hi = x.astype(BF16)
    lo = (x - hi.astype(F32)).astype(BF16)
    s = _dot(jnp.concatenate([hi, lo], axis=0), ones_bd)
    return s[:rows] + s[rows:]


def _scan_prepare_kernel(r_ref, k_ref, v_ref, hw_ref, ha_ref, w2w0_ref, w2w1_ref, w2a0_ref, w2a1_ref, par_ref,
                         y0_ref, rm_ref, d0_ref, bonus_ref):
    c = SCAN_CHUNK
    w = SCAN_PAIR_W
    hw = RWKV_HEAD
    t_idx = lax.broadcasted_iota(jnp.int32, (c, w), 0)
    lane = lax.broadcasted_iota(jnp.int32, (c, w), 1)
    s_idx = lane & (hw - 1)
    head0 = lane < hw
    eye = s_idx == t_idx
    tt = lax.broadcasted_iota(jnp.int32, (c, c), 0)
    ss = lax.broadcasted_iota(jnp.int32, (c, c), 1)
    strict = (s_idx < t_idx, s_idx > t_idx)
    incl = (s_idx <= t_idx, s_idx >= t_idx)
    tri = (jnp.where(ss <= tt, 1.0, 0.0).astype(BF16), jnp.where(ss >= tt, 1.0, 0.0).astype(BF16))
    row2 = lax.broadcasted_iota(jnp.int32, (w, w), 0)
    lane2 = lax.broadcasted_iota(jnp.int32, (w, w), 1)
    bdmask = (row2 < hw) == (lane2 < hw)
    eye2 = row2 == lane2

    def bd(y):
        return jnp.where(bdmask, jnp.concatenate([y, y], axis=0), 0.0).astype(BF16)

    def pm(x, ybd):
        return _dot(x.astype(BF16), ybd)

    inst = [(d, sub) for d in range(2) for sub in range(SCAN_SUB)]
    dirs = [d for d, _ in inst]
    rows = [slice(sub * c, (sub + 1) * c) for _, sub in inst]
    par = par_ref[...]
    k_k, k_a, r_k = par[4:5, :], par[5:6, :], par[6:7, :]
    sub_rows = [slice(sub * c, (sub + 1) * c) for sub in range(SCAN_SUB)]
    r_s = [r_ref[rw, :] for rw in sub_rows]
    k_s = [k_ref[rw, :] for rw in sub_rows]
    v_s = [v_ref[rw, :] for rw in sub_rows]
    hid_w = hw_ref[...]
    hid_a = ha_ref[...]
    wl_all = [_dot(hid_w, w2w0_ref[...]), _dot(hid_w, w2w1_ref[...])]
    al_all = [_dot(hid_a, w2a0_ref[...]), _dot(hid_a, w2a1_ref[...])]
    wl_s = [[wl_all[d][rw, :] for rw in sub_rows] for d in range(2)]
    al_s = [[al_all[d][rw, :] for rw in sub_rows] for d in range(2)]

    def unit_key(k_i):
        kk_i = k_i * k_k
        return kk_i * lax.rsqrt(jnp.maximum(_head_sum(kk_i * kk_i), 1e-24))

    def log_decay(wl_i, d):
        z = -(par[d:d + 1, :] + wl_i)
        softplus = jnp.maximum(z, 0.0) + jnp.log(1.0 + jnp.exp(-jnp.abs(z)))
        return -jnp.exp(-softplus - 0.5)

    kk_s = _each(unit_key, k_s)
    a_s = [[jax.nn.sigmoid(par[2 + d:3 + d, :] + x) for x in al_s[d]] for d in range(2)]
    kd_s = [[k_i * (1.0 + (a_i - 1.0) * k_a) for k_i, a_i in zip(k_s, a_s[d])] for d in range(2)]
    bonus = [_head_sum(r_i * r_k * (kd0 + kd1)) * v_i for r_i, kd0, kd1, v_i in zip(r_s, kd_s[0], kd_s[1], v_s)]

    r = [r_s[sub] for _, sub in inst]
    kk = [kk_s[sub] for _, sub in inst]
    v = [v_s[sub] for _, sub in inst]
    ld = [log_decay(wl_s[d][sub], d) for d, sub in inst]
    a = [a_s[d][sub] for d, sub in inst]
    kd = [kd_s[d][sub] for d, sub in inst]

    def cumulative(ld_i, d):
        p_hi = ld_i.astype(BF16)
        rem = ld_i - p_hi.astype(F32)
        p_mid = rem.astype(BF16)
        p_lo = (rem - p_mid.astype(F32)).astype(BF16)
        cs = _dot(tri[d], jnp.concatenate([p_hi, p_mid, p_lo], axis=1))
        return cs[:, :w] + cs[:, w:2 * w] + cs[:, 2 * w:]

    lam = _each(cumulative, ld, dirs)
    lam_end = _each(lambda l, d: l[0:1, :] if d else l[c - 1:c, :], lam, dirs)
    g_cum = _each(jnp.exp, lam)
    g_inv = _each(lambda l: jnp.exp(-l), lam)
    g_prev = _each(lambda l, x: jnp.exp(l - x), lam, ld)
    g_rel = _each(lambda le, l: jnp.exp(le - l), lam_end, lam)
    g_end = _each(jnp.exp, lam_end)

    qk = _each(jnp.multiply, kk, g_prev)
    rt = _each(jnp.multiply, r, g_cum)
    beta = _each(jnp.multiply, kk, a)
    bt = _each(jnp.multiply, beta, g_inv)
    kt = _each(jnp.multiply, kd, g_inv)
    bh = _each(jnp.multiply, beta, g_rel)
    kh = _each(jnp.multiply, kd, g_rel)

    def big_product(qk_i, rt_i, bt_i, kt_i):
        rhs = jnp.concatenate([jnp.where(head0, bt_i, 0.0), jnp.where(head0, 0.0, bt_i),
                               jnp.where(head0, kt_i, 0.0), jnp.where(head0, 0.0, kt_i)], axis=0).astype(BF16)
        return _dot_nt(jnp.concatenate([qk_i, rt_i], axis=0).astype(BF16), rhs)

    big = _each(big_product, qk, rt, bt, kt)
    n_pow = _each(lambda b, d: jnp.where(strict[d], -b[:c, :w], 0.0), big, dirs)
    a_k = _each(lambda b, d: jnp.where(strict[d], b[:c, w:], 0.0), big, dirs)
    g_b = _each(lambda b, d: jnp.where(incl[d], b[c:, :w], 0.0), big, dirs)
    g_k = _each(lambda b, d: jnp.where(incl[d], b[c:, w:], 0.0), big, dirs)

    t_inv = _each(lambda n: jnp.where(eye, 1.0, 0.0) + n, n_pow)
    for _ in range(int(math.log2(c)) - 1):
        n_pow = _each(lambda n: pm(n, bd(n)), n_pow)
        t_inv = _each(lambda t, n: t + pm(t, bd(n)), t_inv, n_pow)

    v_bd = _each(bd, v)
    akv_gkv = _each(lambda ak, gk, vb: pm(jnp.concatenate([ak, gk], axis=0), vb), a_k, g_k, v_bd)
    x1 = _each(lambda x: x[:c], akv_gkv)
    uw = _each(lambda t, x, q: _dot(t.astype(BF16), jnp.concatenate([bd(x), bd(q)], axis=1)), t_inv, x1, qk)
    u0 = _each(lambda x: -x[:, :w], uw)
    wm = _each(lambda x: x[:, w:], uw)
    gb_uw = _each(lambda gb, u, wm_i: _dot(gb.astype(BF16), jnp.concatenate([bd(u), bd(wm_i)], axis=1)), g_b, u0, wm)
    y0 = _each(lambda x, y: x[c:] + y[:, :w], akv_gkv, gb_uw)
    rm = _each(lambda rt_i, y: rt_i - y[:, w:], rt, gb_uw)
    d0 = _each(lambda kh_i, bh_i, v_i, u: jnp.where(bdmask, _dot_tn(
        jnp.concatenate([kh_i, bh_i], axis=0).astype(BF16), jnp.concatenate([v_i, u], axis=0).astype(BF16)), 0.0),
        kh, bh, v, u0)
    mm = _each(lambda bh_i, wm_i, ge: jnp.where(eye2, ge, 0.0) - jnp.where(
        bdmask, _dot_tn(bh_i.astype(BF16), wm_i.astype(BF16)), 0.0), bh, wm, g_end)
    d0 = _each(lambda x: x[:hw] + x[hw:], d0)
    mm = _each(lambda x: x[:hw] + x[hw:], mm)
    mm_hi = _each(lambda m: m.astype(BF16), mm)
    mm_lo = _each(lambda m, mh: (m - mh.astype(F32)).astype(BF16), mm, mm_hi)

    for sub in range(SCAN_SUB):
        bonus_ref[sub_rows[sub], :] = bonus[sub]
    for i, (d, sub) in enumerate(inst):
        y0_ref[d, rows[i], :] = y0[i]
        rm_ref[d, sub, 0, 0:c, :] = rm[i].astype(BF16)
        rm_ref[d, sub, 0, c:c + hw, :] = mm_hi[i]
        rm_ref[d, sub, 0, c + hw:c + 2 * hw, :] = mm_lo[i]
        d0_ref[d, sub, 0, :, :] = d0[i]


def scan_prepare(r, k, v, hid_w, hid_a, w2_w, w2_a, par):
    rows, d = r.shape
    c, w = SCAN_CHUNK, SCAN_PAIR_W
    n_pairs = d // w
    n_chunks = rows // c
    blk = SCAN_SUB * c
    rank = hid_w.shape[1]
    shared = pl.BlockSpec((blk, w), lambda i, p: (i, p))
    hidden = pl.BlockSpec((blk, rank), lambda i, p: (i, 0))
    fwd_w2 = pl.BlockSpec((rank, w), lambda i, p: (0, p))
    rev_w2 = pl.BlockSpec((rank, w), lambda i, p: (0, n_pairs + p))
    return pl.pallas_call(
        _scan_prepare_kernel,
        grid=(rows // blk, n_pairs),
        in_specs=[shared, shared, shared, hidden, hidden, fwd_w2, rev_w2, fwd_w2, rev_w2,
                  pl.BlockSpec((8, w), lambda i, p: (0, p))],
        out_specs=[pl.BlockSpec((2, blk, w), lambda i, p: (0, i, p)),
                   pl.BlockSpec((2, SCAN_SUB, 1, c + w, w), lambda i, p: (0, i, p, 0, 0)),
                   pl.BlockSpec((2, SCAN_SUB, 1, w // 2, w), lambda i, p: (0, i, p, 0, 0)),
                   shared],
        out_shape=[jax.ShapeDtypeStruct((2, rows, d), F32),
                   jax.ShapeDtypeStruct((2, n_chunks, n_pairs, c + w, w), BF16),
                   jax.ShapeDtypeStruct((2, n_chunks, n_pairs, w // 2, w), F32),
                   jax.ShapeDtypeStruct((rows, d), F32)],
        compiler_params=_params(("parallel", "parallel")),
        name="scan_prepare",
    )(r, k, v, hid_w, hid_a, w2_w, w2_w, w2_a, w2_a, par)


def _scan_apply_kernel(y0f_ref, rmf_ref, d0f_ref, y0r_ref, rmr_ref, d0r_ref, yf_ref, yr_ref, z_ref, *, n_pairs):
    c, w = SCAN_CHUNK, SCAN_PAIR_W
    i = pl.program_id(1)

    @pl.when(i == 0)
    def _():
        z_ref[...] = jnp.zeros_like(z_ref)

    hw = RWKV_HEAD
    row2 = lax.broadcasted_iota(jnp.int32, (w, w), 0)
    lane2 = lax.broadcasted_iota(jnp.int32, (w, w), 1)
    bdmask = (row2 < hw) == (lane2 < hw)

    def bd(x):
        return jnp.where(bdmask, jnp.concatenate([x, x], axis=0), 0.0)

    chains = [(d, p) for d in range(2) for p in range(n_pairs)]
    y0_refs, rm_refs, d0_refs, y_refs = (y0f_ref, y0r_ref), (rmf_ref, rmr_ref), (d0f_ref, d0r_ref), (yf_ref, yr_ref)
    z = [bd(z_ref[d, p]) for d, p in chains]
    z_hi = _each(lambda x: x.astype(BF16), z)
    z_lo = _each(lambda x, xh: (x - xh.astype(F32)).astype(BF16), z, z_hi)
    res = [_dot(rm_refs[d][p], jnp.concatenate([zh, zl], axis=1)) for (d, p), zh, zl in zip(chains, z_hi, z_lo)]
    res = _each(lambda x: x[:, :w] + x[:, w:], res)
    y = [y0_refs[d][:, p * w:(p + 1) * w] + x[:c] for (d, p), x in zip(chains, res)]
    z_new = [d0_refs[d][p] + x[c:c + hw] + x[c + hw:] for (d, p), x in zip(chains, res)]
    for (d, p), y_i, z_i in zip(chains, y, z_new):
        y_refs[d][:, p * w:(p + 1) * w] = y_i
        z_ref[d, p] = z_i


def _scan_chunk_index(i, d):
    n_ctx = CTX_LEN // SCAN_CHUNK
    n_all = S_ALL // SCAN_CHUNK
    return jnp.where(d == 0, i, jnp.where(i < n_ctx, n_ctx - 1 - i, n_all + n_ctx - 1 - i))


def scan_apply(y0, rm, d0, batch):
    _, rows, d = y0.shape
    c, w = SCAN_CHUNK, SCAN_PAIR_W
    n_pairs = d // w
    per_b = rows // batch // c

    def specs(dr):
        def idx(b, i):
            return b * per_b + _scan_chunk_index(i, dr)
        return [pl.BlockSpec((None, c, d), lambda b, i: (dr, idx(b, i), 0)),
                pl.BlockSpec((None, None, n_pairs, c + w, w), lambda b, i: (dr, idx(b, i), 0, 0, 0)),
                pl.BlockSpec((None, None, n_pairs, w // 2, w), lambda b, i: (dr, idx(b, i), 0, 0, 0)),
                pl.BlockSpec((c, d), lambda b, i: (idx(b, i), 0))]

    fwd, rev = specs(0), specs(1)
    kernel = functools.partial(_scan_apply_kernel, n_pairs=n_pairs)
    return pl.pallas_call(
        kernel,
        grid=(batch, per_b),
        in_specs=fwd[:3] + rev[:3],
        out_specs=[fwd[3], rev[3]],
        out_shape=[jax.ShapeDtypeStruct((rows, d), F32)] * 2,
        scratch_shapes=[pltpu.VMEM((2, n_pairs, w // 2, w), F32)],
        compiler_params=_params(("parallel", "arbitrary")),
        name="scan_apply",
    )(y0, rm, d0, y0, rm, d0)


def _shift_mix_kernel(h_ref, hp_ref, hn_ref, mix_ref, *out_refs):
    per_b = S_ALL // ROW_TILE
    pos = pl.program_id(0) % per_b
    h = h_ref[...]
    row = lax.broadcasted_iota(jnp.int32, h.shape, 0)
    starts_seq = (pos == 0) | (pos == 1)
    ends_seq = (pos == 0) | (pos == per_b - 1)
    prev_row = jnp.where(starts_seq, 0.0, hp_ref[7:8, :])
    next_row = jnp.where(ends_seq, 0.0, hn_ref[0:1, :])
    prev = jnp.where(row == 0, prev_row, pltpu.roll(h, 1, 0))
    nxt = jnp.where(row == ROW_TILE - 1, next_row, pltpu.roll(h, ROW_TILE - 1, 0))
    xx = 0.5 * (prev + nxt) - h
    for j, o_ref in enumerate(out_refs):
        o_ref[...] = (h + xx * mix_ref[j:j + 1, :]).astype(o_ref.dtype)


def shift_mix(h, x_mix):
    rows, d = h.shape
    n_mix = x_mix.shape[0]
    sub = ROW_TILE // 8
    tile = pl.BlockSpec((ROW_TILE, d), lambda i: (i, 0))
    return pl.pallas_call(
        _shift_mix_kernel,
        grid=(rows // ROW_TILE,),
        in_specs=[tile,
                  pl.BlockSpec((8, d), lambda i: (jnp.maximum(i * sub - 1, 0), 0)),
                  pl.BlockSpec((8, d), lambda i: (jnp.minimum((i + 1) * sub, rows // 8 - 1), 0)),
                  pl.BlockSpec((n_mix, d), lambda i: (0, 0))],
        out_specs=[tile] * n_mix,
        out_shape=[jax.ShapeDtypeStruct((rows, d), BF16)] * n_mix,
        compiler_params=_params(("parallel",)),
        name="shift_mix",
    )(h, h, h, x_mix)


def _rwkv_out_kernel(yf_ref, yr_ref, bonus_ref, g_ref, gn_ref, o_ref):
    w = SCAN_PAIR_W
    for s in range(o_ref.shape[-1] // w):
        cols = slice(s * w, (s + 1) * w)
        y = yf_ref[:, cols] + yr_ref[:, cols]
        mu = _head_sum(y) * (1.0 / RWKV_HEAD)
        dev = y - mu
        var = _head_sum(dev * dev) * (1.0 / RWKV_HEAD)
        yn = dev * lax.rsqrt(var + GN_EPS) * gn_ref[0:1, cols] + gn_ref[1:2, cols]
        o_ref[:, cols] = ((yn + bonus_ref[:, cols]) * g_ref[:, cols]).astype(o_ref.dtype)


def rwkv_out(y_fwd, y_rev, bonus, g, gn_g, gn_b, tn=512):
    rows, d = y_fwd.shape
    tile = pl.BlockSpec((ROW_TILE, tn), lambda i, j: (i, j))
    return pl.pallas_call(
        _rwkv_out_kernel,
        grid=(rows // ROW_TILE, d // tn),
        in_specs=[tile, tile, tile, tile, pl.BlockSpec((2, tn), lambda i, j: (0, j))],
        out_specs=tile,
        out_shape=jax.ShapeDtypeStruct((rows, d), BF16),
        compiler_params=_params(("parallel", "parallel")),
        name="rwkv_out",
    )(y_fwd, y_rev, bonus, g, jnp.stack([gn_g, gn_b]))


def _route_topk(logits):
    lane = lax.broadcasted_iota(jnp.int32, logits.shape, 1)
    far = 4 * HEAD_W

    def first_max(vals):
        top = jnp.max(vals, axis=-1, keepdims=True)
        return top, jnp.min(jnp.where(vals == top, lane, far), axis=-1, keepdims=True)

    is_group = lane < N_GROUPS
    g_top, g_sel = first_max(jnp.where(is_group, logits, NEG_INF))
    p_sel = 1.0 / jnp.sum(jnp.where(is_group, jnp.exp(logits - g_top), 0.0), axis=-1, keepdims=True)
    lo = N_GROUPS + g_sel * EXPERTS_PER_GROUP
    le = jnp.where((lane >= lo) & (lane < lo + EXPERTS_PER_GROUP), logits, NEG_INF)
    v1, i1 = first_max(le)
    v2, i2 = first_max(jnp.where(lane == i1, NEG_INF, le))
    e2 = jnp.exp(v2 - v1)
    w1 = p_sel / (1.0 + e2)
    w2 = p_sel * e2 / (1.0 + e2)
    out = jnp.where(lane == 0, (i1 - N_GROUPS).astype(F32), 0.0)
    out = jnp.where(lane == 1, (i2 - N_GROUPS).astype(F32), out)
    out = jnp.where(lane == 2, w1, out)
    return jnp.where(lane == 3, w2, out)


def moe_dispatch(route):
    t = route.shape[0]
    n_pairs = TOP_K * t
    tm = MOE_TILE
    n_slots = n_pairs + N_EXPERTS * tm
    n_tiles = n_slots // tm
    e_flat = route[:, :TOP_K].astype(jnp.int32).reshape(-1)
    gate_bits = lax.bitcast_convert_type(route[:, TOP_K:2 * TOP_K], jnp.int32).reshape(-1)
    onehot = (e_flat[:, None] == jnp.arange(N_EXPERTS, dtype=jnp.int32)[None, :]).astype(jnp.int32)
    csum = jnp.cumsum(onehot, axis=0)
    counts = csum[-1]
    padded = ((counts + tm - 1) // tm) * tm
    pend = jnp.cumsum(padded)
    pstart = pend - padded
    pos = jnp.sum(onehot * (pstart[None, :] + csum - onehot), axis=1)
    pair = jnp.arange(n_pairs, dtype=jnp.int32)
    slots = jnp.full((n_slots, 2), -1, jnp.int32).at[pos].set(jnp.stack([pair, gate_bits], axis=1))
    valid = slots[:, 0] >= 0
    tok = jnp.where(valid, slots[:, 0] // TOP_K, 0)
    dst = jnp.where(valid, (slots[:, 0] % TOP_K) * t + slots[:, 0] // TOP_K,
                    n_pairs + jnp.arange(n_slots, dtype=jnp.int32) % tm)
    gate = jnp.where(valid, lax.bitcast_convert_type(slots[:, 1], F32), 0.0)
    tile_start = jnp.arange(n_tiles, dtype=jnp.int32) * tm
    tile_e = jnp.sum((tile_start[:, None] >= pend[None, :]).astype(jnp.int32), axis=1)
    tile_e = jnp.minimum(tile_e, N_EXPERTS - 1)
    n_live = (pend[-1] // tm).reshape(1)
    return (tok.reshape(n_tiles, 1, tm), dst.reshape(n_tiles, 1, tm), gate.reshape(n_slots, 1),
            tile_e.astype(jnp.int32), n_live.astype(jnp.int32))


def _moe_kernel(te_ref, nl_ref, tok_ref, tokn_ref, dst_ref, dstp_ref, gate_ref, h_hbm, w1_ref, w3_ref, w2_ref,
                y_hbm, xbuf, obuf, w1b, w3b, w2b, sem_in, sem_out):
    i = pl.program_id(0)
    n_live = nl_ref[0]
    tm = MOE_TILE
    slot = i % 2
    other = 1 - slot

    def gather_copy(src_row, r, s):
        return pltpu.make_async_copy(h_hbm.at[pl.ds(src_row, 1), :], xbuf.at[s, pl.ds(r, 1), :], sem_in.at[s])

    def scatter_copy(r, dst_row, s):
        return pltpu.make_async_copy(obuf.at[s, pl.ds(r, 1), :], y_hbm.at[pl.ds(dst_row, 1), :], sem_out.at[s])

    def wait_gather(s):
        pltpu.make_async_copy(h_hbm.at[pl.ds(0, tm), :], xbuf.at[s], sem_in.at[s]).wait()

    def wait_scatter(s):
        pltpu.make_async_copy(obuf.at[s], y_hbm.at[pl.ds(0, tm), :], sem_out.at[s]).wait()

    def per_row(fn):
        def body(r, carry):
            fn(r)
            return carry
        lax.fori_loop(0, tm, body, 0, unroll=MOE_DMA_UNROLL)

    def expert_tile(scatter_previous):
        packed = xbuf[slot]
        half = packed.shape[-1]
        x_lo = lax.bitcast_convert_type(packed << 16, F32).astype(BF16)
        x_hi = lax.bitcast_convert_type(packed & jnp.uint32(0xFFFF0000), F32).astype(BF16)
        for r in range(tm):
            gather_copy(tokn_ref[0, r], r, other).start(priority=r % 2)
        if scatter_previous:
            for r in range(tm):
                scatter_copy(r, dstp_ref[0, r], other).start(priority=r % 2)
        up = _dot(x_lo, w1b[0:half, :]) + _dot(x_hi, w1b[half:, :])
        gate_in = _dot(x_lo, w3b[0:half, :]) + _dot(x_hi, w3b[half:, :])
        hid = (up * jax.nn.sigmoid(up)) * gate_in * gate_ref[...]
        return _dot(hid.astype(BF16), w2b[...])

    @pl.when(i == 0)
    def _():
        per_row(lambda r: gather_copy(tok_ref[0, r], r, 0).start())

    @pl.when(i < n_live)
    def _():
        wait_gather(slot)
        prev_e = te_ref[jnp.maximum(i - 1, 0)]

        @pl.when((i == 0) | (te_ref[i] != prev_e))
        def _():
            w1b[...] = w1_ref[...].astype(BF16)
            w3b[...] = w3_ref[...].astype(BF16)
            w2b[...] = w2_ref[...].astype(BF16)

        @pl.when(i == 0)
        def _():
            obuf[0] = expert_tile(False)
            first_spare = y_hbm.shape[0] - tm
            per_row(lambda r: scatter_copy(r, first_spare + r, 0).start())
            wait_scatter(0)

        @pl.when(i > 0)
        def _():
            out = expert_tile(True)

            @pl.when(i > 1)
            def _():
                wait_scatter(slot)

            obuf[slot] = out

        @pl.when(i == n_live - 1)
        def _():
            per_row(lambda r: scatter_copy(r, dst_ref[0, r], slot).start())
            wait_gather(other)

            @pl.when(i > 0)
            def _():
                wait_scatter(other)

            wait_scatter(slot)


def moe_experts(h, tok, dst, gate, tile_e, n_live, w1, w3, w2, layer):
    t = h.shape[0]
    d = w1.shape[-2]
    f = w1.shape[-1]
    tm = MOE_TILE
    n_tiles = tok.shape[0]

    def w_index(i, te, nl):
        return (layer, te[i] // EXPERTS_PER_GROUP, te[i] % EXPERTS_PER_GROUP, 0, 0)

    grid_spec = pltpu.PrefetchScalarGridSpec(
        num_scalar_prefetch=2,
        grid=(n_tiles,),
        in_specs=[pl.BlockSpec((None, 1, tm), lambda i, te, nl: (i, 0, 0), memory_space=pltpu.SMEM),
                  pl.BlockSpec((None, 1, tm), lambda i, te, nl: (jnp.minimum(i + 1, n_tiles - 1), 0, 0),
                               memory_space=pltpu.SMEM),
                  pl.BlockSpec((None, 1, tm), lambda i, te, nl: (i, 0, 0), memory_space=pltpu.SMEM),
                  pl.BlockSpec((None, 1, tm), lambda i, te, nl: (jnp.maximum(i - 1, 0), 0, 0),
                               memory_space=pltpu.SMEM),
                  pl.BlockSpec((tm, 1), lambda i, te, nl: (i, 0)),
                  pl.BlockSpec(memory_space=pl.ANY),
                  pl.BlockSpec((None, None, None, d, f), w_index),
                  pl.BlockSpec((None, None, None, d, f), w_index),
                  pl.BlockSpec((None, None, None, f, d), w_index)],
        out_specs=pl.BlockSpec(memory_space=pl.ANY),
        scratch_shapes=[pltpu.VMEM((2, tm, d // 2), jnp.uint32), pltpu.VMEM((2, tm, d), F32),
                        pltpu.VMEM((d, f), BF16), pltpu.VMEM((d, f), BF16), pltpu.VMEM((f, d), BF16),
                        pltpu.SemaphoreType.DMA((2,)), pltpu.SemaphoreType.DMA((2,))])
    return pl.pallas_call(
        _moe_kernel,
        grid_spec=grid_spec,
        out_shape=jax.ShapeDtypeStruct((TOP_K * t + tm, d), F32),
        compiler_params=_params(("arbitrary",)),
        name="moe_experts",
    )(tile_e, n_live, tok, tok, dst, dst, gate, h, w1, w3, w2)


def _mod_spec(k):
    per_b = S_ALL // ROW_TILE
    return pl.BlockSpec((None, 1, D_MODEL),
                        lambda i: ((i // per_b * 2 + jnp.minimum(i % per_b, 1)) * N_MOD + k, 0, 0))


def _norm_mod(x, g_ref, sh_ref, sc_ref):
    h = x * lax.rsqrt(jnp.mean(x * x, axis=-1, keepdims=True) + NORM_EPS) * g_ref[...]
    if sh_ref is None:
        return h
    return h * (1.0 + sc_ref[...]) + sh_ref[...]


def _first_norm_kernel(x_ref, g_ref, sh_ref, sc_ref, h_ref):
    h_ref[...] = _norm_mod(x_ref[...], g_ref, sh_ref, sc_ref).astype(h_ref.dtype)


def first_norm(xs, g, mod, h_dtype):
    rows, d = xs.shape
    tile = pl.BlockSpec((ROW_TILE, d), lambda i: (i, 0))
    return pl.pallas_call(
        _first_norm_kernel,
        grid=(rows // ROW_TILE,),
        in_specs=[tile, pl.BlockSpec((1, d), lambda i: (0, 0)), _mod_spec(0), _mod_spec(1)],
        out_specs=tile,
        out_shape=jax.ShapeDtypeStruct((rows, d), h_dtype),
        compiler_params=_params(("parallel",)),
        name="first_norm",
    )(xs, g.reshape(1, d), mod, mod)


def _post_mixer_kernel(x_ref, u_ref, gate_ref, g_ref, sh_ref, sc_ref, wr_ref, br_ref, xo_ref, h_ref, route_ref):
    x = x_ref[...] + gate_ref[...] * u_ref[...]
    xo_ref[...] = x
    h = _norm_mod(x, g_ref, sh_ref, sc_ref)
    half = h.shape[-1] // 2
    bits = lax.bitcast_convert_type(h.astype(BF16).astype(F32), jnp.uint32)
    h_ref[...] = (bits[:, half:] & jnp.uint32(0xFFFF0000)) | (bits[:, :half] >> 16)
    logits = jnp.dot(h, wr_ref[...], precision=lax.Precision.HIGHEST, preferred_element_type=F32) + br_ref[...]
    route_ref[...] = _route_topk(logits)


def post_mixer(xs, upd, g, mod, router_w, router_b):
    rows, d = xs.shape
    tile = pl.BlockSpec((ROW_TILE, d), lambda i: (i, 0))
    ptile = pl.BlockSpec((ROW_TILE, d // 2), lambda i: (i, 0))
    rtile = pl.BlockSpec((ROW_TILE, HEAD_W), lambda i: (i, 0))
    return pl.pallas_call(
        _post_mixer_kernel,
        grid=(rows // ROW_TILE,),
        in_specs=[tile, tile, _mod_spec(2), pl.BlockSpec((1, d), lambda i: (0, 0)), _mod_spec(3), _mod_spec(4),
                  pl.BlockSpec((d, HEAD_W), lambda i: (0, 0)), pl.BlockSpec((1, HEAD_W), lambda i: (0, 0))],
        out_specs=[tile, ptile, rtile],
        out_shape=[jax.ShapeDtypeStruct((rows, d), F32), jax.ShapeDtypeStruct((rows, d // 2), jnp.uint32),
                   jax.ShapeDtypeStruct((rows, HEAD_W), F32)],
        compiler_params=_params(("parallel",)),
        name="post_mixer",
    )(xs, upd, mod, g.reshape(1, d), mod, mod, router_w, router_b)


def _post_moe_kernel(x_ref, ya_ref, yb_ref, gate_ref, g_ref, *rest, modulate):
    if modulate:
        sh_ref, sc_ref, xo_ref, h_ref = rest
    else:
        sh_ref = sc_ref = None
        xo_ref, h_ref = rest
    x = x_ref[...] + gate_ref[...] * (ya_ref[...] + yb_ref[...])
    xo_ref[...] = x
    h_ref[...] = _norm_mod(x, g_ref, sh_ref, sc_ref).astype(h_ref.dtype)


def post_moe(xs, y2, g, mod, next_mod, h_dtype):
    rows, d = xs.shape
    tile = pl.BlockSpec((ROW_TILE, d), lambda i: (i, 0))
    second = pl.BlockSpec((ROW_TILE, d), lambda i: (rows // ROW_TILE + i, 0))
    gspec = pl.BlockSpec((1, d), lambda i: (0, 0))
    modulate = next_mod is not None
    in_specs = [tile, tile, second, _mod_spec(5), gspec] + ([_mod_spec(0), _mod_spec(1)] if modulate else [])
    args = (xs, y2, y2, mod, g.reshape(1, d)) + ((next_mod, next_mod) if modulate else ())
    return pl.pallas_call(
        functools.partial(_post_moe_kernel, modulate=modulate),
        grid=(rows // ROW_TILE,),
        in_specs=in_specs,
        out_specs=[tile, tile],
        out_shape=[jax.ShapeDtypeStruct((rows, d), F32), jax.ShapeDtypeStruct((rows, d), h_dtype)],
        compiler_params=_params(("parallel",)),
        name="post_moe",
    )(*args)


def _pad_cols(w, n):
    return jnp.pad(w, ((0, 0), (0, n - w.shape[1])))


def _two_dir_lora(x, w1, w2):
    r = w1.shape[-1]
    w1c = _pad_cols(jnp.concatenate([w1[0], w1[1]], axis=1), 256).astype(BF16)
    d = w2.shape[-1]
    w2bd = jnp.zeros((256, 2 * d), F32).at[:r, :d].set(w2[0]).at[r:2 * r, d:].set(w2[1]).astype(BF16)
    return matmul(x, w1c, F32, tn=256), w2bd


def rwkv_mixer(h, p, batch):
    mixes = shift_mix(h, p['x_mix'])
    r = matmul(mixes[0], p['w_rkv'][0].astype(BF16), F32)
    k = matmul(mixes[1], p['w_rkv'][1].astype(BF16), F32)
    v = matmul(mixes[2], p['w_rkv'][2].astype(BF16), F32)
    hid_w, w2bd_w = _two_dir_lora(mixes[3], p['dec_w1'], p['dec_w2'])
    hid_a, w2bd_a = _two_dir_lora(mixes[4], p['iclr_a1'], p['iclr_a2'])
    g_hid = matmul(mixes[5], p['gate_g1'].astype(BF16), F32, tn=256)
    g = matmul(jax.nn.sigmoid(g_hid).astype(BF16), p['gate_g2'].astype(BF16), F32)

    par = jnp.concatenate([p['dec_w0'], p['iclr_a0'], p['k_k'][None], p['k_a'][None], p['r_k'].reshape(1, -1),
                           jnp.zeros((1, D_MODEL), F32)], axis=0)
    y0, rm, d0, bonus = scan_prepare(r, k, v, jnp.tanh(hid_w).astype(BF16), hid_a.astype(BF16), w2bd_w, w2bd_a, par)
    y_fwd, y_rev = scan_apply(y0, rm, d0, batch)
    yo = rwkv_out(y_fwd, y_rev, bonus, g, p['gn_g'], p['gn_b'])
    return matmul(yo, p['w_out'].astype(BF16), F32)


def even_mixer(h, p, layer_idx, cos_t, sin_t, batch):
    rows = batch * S_ALL
    lam_init = 0.8 - 0.6 * math.exp(-0.3 * layer_idx)
    proj = matmul(h, p['w_in'].astype(BF16), BF16, tn=1024)
    proj = proj.reshape(batch, S_ALL, -1)
    lf = p['diff_lambda']
    lam = (jnp.exp(jnp.sum(lf[0] * lf[1])) - jnp.exp(jnp.sum(lf[2] * lf[3])) + lam_init).reshape(1)
    a_out = diff_attention(proj, lam, cos_t, sin_t, p['subln_g'], lam_init)
    b_out = neighbourhood_attention(proj, na_bias_table(p['rpb']))
    mixed = jnp.concatenate([a_out, b_out], axis=-1).reshape(rows, -1)
    return matmul(mixed, p['w_out'].astype(BF16), F32)


def router_table(router_g, router_g_b, router_e, router_e_b):
    w = _pad_cols(jnp.concatenate([router_g, router_e], axis=1), HEAD_W)
    b = _pad_cols(jnp.concatenate([router_g_b, router_e_b])[None, :], HEAD_W)
    return w, b


def kernel(x, c, ctx, c_ctx, ada_w, ada_b, norm_g, final_g, even_w_in, even_w_out, diff_lambda, diff_subln_g, na_rpb, rwkv_x_mix, rwkv_w_rkv, rwkv_w_out, rwkv_dec_w0, rwkv_dec_w1, rwkv_dec_w2, rwkv_iclr_a0, rwkv_iclr_a1, rwkv_iclr_a2, rwkv_gate_g1, rwkv_gate_g2, rwkv_k_k, rwkv_k_a, rwkv_r_k, rwkv_gn_g, rwkv_gn_b, moe_router_g, moe_router_g_b, moe_router_e, moe_router_e_b, moe_w1, moe_w3, moe_w2):
    batch = x.shape[0]
    rows = batch * S_ALL
    xs = jnp.concatenate([ctx, x], axis=1).reshape(rows, D_MODEL)
    cos_t, sin_t = rope_tables()

    cvec = jnp.concatenate([c, c_ctx[None], jnp.zeros((8 - batch - 1, D_MODEL), F32)], axis=0)
    mods = ada_modulation(jax.nn.silu(cvec).astype(BF16), ada_w, ada_b)
    mods = mods.reshape(DEPTH, 8, N_MOD, D_MODEL)

    def mod_table(i):
        mod_l = mods[i, :batch]
        mod_c = jnp.broadcast_to(mods[i, batch][None], mod_l.shape)
        return jnp.stack([mod_c, mod_l], axis=1).reshape(batch * 2 * N_MOD, 1, D_MODEL)

    mod = mod_table(0)
    h = first_norm(xs, norm_g[0, 0], mod, BF16)
    for i in range(DEPTH):
        j = i // 2
        if i % 2 == 0:
            p = dict(w_in=even_w_in[j], w_out=even_w_out[j], diff_lambda=diff_lambda[j],
                     subln_g=diff_subln_g[j], rpb=na_rpb[j])
            out = even_mixer(h, p, i, cos_t, sin_t, batch)
        else:
            p = dict(x_mix=rwkv_x_mix[j], w_rkv=rwkv_w_rkv[j], w_out=rwkv_w_out[j], dec_w0=rwkv_dec_w0[j],
                     dec_w1=rwkv_dec_w1[j], dec_w2=rwkv_dec_w2[j], iclr_a0=rwkv_iclr_a0[j],
                     iclr_a1=rwkv_iclr_a1[j], iclr_a2=rwkv_iclr_a2[j], gate_g1=rwkv_gate_g1[j],
                     gate_g2=rwkv_gate_g2[j], k_k=rwkv_k_k[j], k_a=rwkv_k_a[j], r_k=rwkv_r_k[j],
                     gn_g=rwkv_gn_g[j], gn_b=rwkv_gn_b[j])
            out = rwkv_mixer(h, p, batch)
        rw, rb = router_table(moe_router_g[i], moe_router_g_b[i], moe_router_e[i], moe_router_e_b[i])
        xs, h2, route = post_mixer(xs, out, norm_g[i, 1], mod, rw, rb)
        tok, dst, gate, tile_e, n_live = moe_dispatch(route)
        y2 = moe_experts(h2, tok, dst, gate, tile_e, n_live, moe_w1, moe_w3, moe_w2, i)
        if i + 1 < DEPTH:
            next_mod = mod_table(i + 1)
            xs, h = post_moe(xs, y2, norm_g[i + 1, 0], mod, next_mod, F32 if (i + 1) % 2 else BF16)
            mod = next_mod
        else:
            _, h = post_moe(xs, y2, final_g, mod, None, F32)
    return h.reshape(batch, S_ALL, D_MODEL)[:, CTX_LEN:]
```
